```python
import jax, jax.numpy as jnp
from jax import lax
import numpy as np

D_MODEL = 1024
BATCH = 8
SEQ = 4096
DEPTH = 4

N_META = 16
BLOCK = 128
N_PAD = (-N_META) % BLOCK

SB_HEADS = 8
SB_HEAD_DIM = 64
MLA_HEADS = 8
MLA_NOPE = 64
MLA_ROPE = 32
MLA_V = 64
MLA_Q_LORA = 384
MLA_KV_LORA = 256
RET_HEADS = 4
RET_QK = 64
RET_V = 128

D_SB = SB_HEADS * SB_HEAD_DIM
D_MLA = MLA_HEADS * MLA_V
D_RET = RET_HEADS * RET_V
D_MIX = D_SB + D_MLA + D_RET
D_FF = 4 * D_MODEL

IN_SIZES = (D_SB, D_SB, D_SB,
            MLA_Q_LORA, MLA_KV_LORA, MLA_ROPE,
            RET_HEADS * RET_QK, RET_HEADS * RET_QK, D_RET, D_RET)
IN_SPLITS = tuple(int(s) for s in np.cumsum(IN_SIZES)[:-1])
N_IN = int(sum(IN_SIZES))

ROPE_THETA = 10000.0
LN_EPS = 1e-5
DN_ALPHA = (2 * DEPTH) ** 0.25
DN_BETA = (8 * DEPTH) ** -0.25
RET_GAMMA = tuple(1.0 - 2.0 ** (-5 - h) for h in range(RET_HEADS))

kernel_name = "hybrid_sb_mla_retention_deepnorm"


def layer_norm(x, g, b):
    x32 = x.astype(jnp.float32)
    mu = jnp.mean(x32, -1, keepdims=True)
    var = jnp.mean(jnp.square(x32 - mu), -1, keepdims=True)
    y = (x32 - mu) * lax.rsqrt(var + LN_EPS)
    return (y * g.astype(jnp.float32) + b.astype(jnp.float32)).astype(x.dtype)


def rms_norm(x, g):
    x32 = x.astype(jnp.float32)
    y = x32 * lax.rsqrt(jnp.mean(jnp.square(x32), -1, keepdims=True) + LN_EPS)
    return (y * g.astype(jnp.float32)).astype(x.dtype)


def head_norm(y):
    y32 = y.astype(jnp.float32)
    mu = jnp.mean(y32, -1, keepdims=True)
    var = jnp.mean(jnp.square(y32 - mu), -1, keepdims=True)
    return (y32 - mu) * lax.rsqrt(var + LN_EPS)


def apply_rope(x, pos):
    half = x.shape[-1] // 2
    inv = ROPE_THETA ** (-jnp.arange(half, dtype=jnp.float32) / half)
    ang = pos[:, None] * inv[None, :]
    cos = jnp.cos(ang)[:, None, :]
    sin = jnp.sin(ang)[:, None, :]
    x1, x2 = x[..., :half], x[..., half:]
    return jnp.concatenate([x1 * cos - x2 * sin, x1 * sin + x2 * cos], -1).astype(x.dtype)


def stick_breaking_attention(q, k, v, valid):
    L, d = q.shape[1], q.shape[-1]
    scale = d ** -0.5
    outs = []
    for i in range(L // BLOCK):
        q0, q1 = i * BLOCK, (i + 1) * BLOCK
        z = jnp.einsum("bqhd,bkhd->bhqk", q[:, q0:q1], k[:, :q1]).astype(jnp.float32) * scale
        t_idx = jnp.arange(q0, q1)[:, None]
        s_idx = jnp.arange(q1)[None, :]
        mask = (s_idx < t_idx) & valid[None, :q1]
        log_beta = jnp.where(mask, jax.nn.log_sigmoid(z), -jnp.inf)
        log_keep = jnp.where(mask, jax.nn.log_sigmoid(-z), 0.0)
        incl = lax.cumsum(log_keep, axis=3, reverse=True)
        excl = jnp.concatenate([incl[..., 1:], jnp.zeros_like(incl[..., :1])], axis=-1)
        w = jnp.exp(log_beta + excl)
        outs.append(jnp.einsum("bhqk,bkhd->bqhd", w.astype(v.dtype), v[:, :q1]))
    return jnp.concatenate(outs, axis=1)


def mla_attention(q_nope, q_rope, k_nope, k_rope, v, valid):
    L = q_nope.shape[1]
    scale = (MLA_NOPE + MLA_ROPE) ** -0.5
    outs = []
    for i in range(L // BLOCK):
        q0, q1 = i * BLOCK, (i + 1) * BLOCK
        s = (jnp.einsum("bqhd,bkhd->bhqk", q_nope[:, q0:q1], k_nope[:, :q1])
             + jnp.einsum("bqhd,bkd->bhqk", q_rope[:, q0:q1], k_rope[:, :q1])).astype(jnp.float32) * scale
        t_idx = jnp.arange(q0, q1)[:, None]
        s_idx = jnp.arange(q1)[None, :]
        mask = (s_idx <= t_idx) & (valid[None, :q1] | (s_idx == t_idx))
        p = jax.nn.softmax(jnp.where(mask, s, -jnp.inf), axis=-1)
        outs.append(jnp.einsum("bhqk,bkhd->bqhd", p.astype(v.dtype), v[:, :q1]))
    return jnp.concatenate(outs, axis=1)


def multiscale_retention(q, k, v):
    B, L, H, dk = q.shape
    dv = v.shape[-1]
    n = L // BLOCK
    log_g = jnp.log(jnp.array(RET_GAMMA, jnp.float32))
    idx = jnp.arange(BLOCK, dtype=jnp.float32)
    diff = idx[:, None] - idx[None, :]
    d_in = jnp.where(diff[None] >= 0, jnp.exp(jnp.maximum(diff, 0.0)[None] * log_g[:, None, None]), 0.0)
    q_decay = jnp.exp((idx[:, None] + 1.0) * log_g[None, :])
    k_decay = jnp.exp((BLOCK - 1.0 - idx[:, None]) * log_g[None, :])
    c_decay = jnp.exp(BLOCK * log_g)

    def to_chunks(a):
        return jnp.moveaxis(a.astype(jnp.float32).reshape(B, n, BLOCK, H, a.shape[-1]), 1, 0)

    def step(state, inp):
        qc, kc, vc = inp
        inner = jnp.einsum("bqhd,bkhd->bhqk", qc, kc) * d_in[None]
        y = (jnp.einsum("bhqk,bkhe->bqhe", inner, vc)
             + jnp.einsum("bqhd,bhde->bqhe", qc, state) * q_decay[None, :, :, None])
        state = (state * c_decay[None, :, None, None]
                 + jnp.einsum("bkhd,bkhe->bhde", kc * k_decay[None, :, :, None], vc))
        return state, y

    state0 = jnp.zeros((B, H, dk, dv), jnp.float32)
    _, ys = lax.scan(step, state0, (to_chunks(q), to_chunks(k), to_chunks(v)))
    return jnp.moveaxis(ys, 0, 1).reshape(B, L, H, dv)


def hybrid_mixer(h, w_in, q_norm_g, kv_norm_g, w_uq, w_ukv, w_out, pos, valid):
    B, L, _ = h.shape
    proj = h @ w_in
    sb_q, sb_k, sb_v, c_q, c_kv, k_r, r_q, r_k, r_v, r_g = jnp.split(proj, IN_SPLITS, axis=-1)

    hs = lambda a, nh: a.reshape(B, L, nh, -1)
    out_a = stick_breaking_attention(hs(sb_q, SB_HEADS), hs(sb_k, SB_HEADS), hs(sb_v, SB_HEADS), valid)

    q = (rms_norm(c_q, q_norm_g) @ w_uq).reshape(B, L, MLA_HEADS, MLA_NOPE + MLA_ROPE)
    q_nope, q_rope = q[..., :MLA_NOPE], apply_rope(q[..., MLA_NOPE:], pos)
    kv = (rms_norm(c_kv, kv_norm_g) @ w_ukv).reshape(B, L, MLA_HEADS, MLA_NOPE + MLA_V)
    k_nope, v_b = kv[..., :MLA_NOPE], kv[..., MLA_NOPE:]
    k_rope = apply_rope(k_r[:, :, None, :], pos)[:, :, 0]
    out_b = mla_attention(q_nope, q_rope, k_nope, k_rope, v_b, valid)

    rq = apply_rope(hs(r_q, RET_HEADS), pos)
    rk = apply_rope(hs(r_k, RET_HEADS), pos) * (RET_QK ** -0.5)
    rk = jnp.where(valid[None, :, None, None], rk, jnp.zeros_like(rk))
    y_c = head_norm(multiscale_retention(rq, rk, hs(r_v, RET_HEADS))).reshape(B, L, D_RET)
    out_c = (jax.nn.silu(r_g.astype(jnp.float32)) * y_c).astype(h.dtype)

    mixed = jnp.concatenate([out_a.reshape(B, L, D_SB), out_b.reshape(B, L, D_MLA), out_c], axis=-1)
    return mixed @ w_out


def squared_relu_mlp(h, w1, w2):
    return jnp.square(jax.nn.relu(h @ w1)) @ w2


def _fwd_setup_inputs(seed: int = 0) -> dict:
    key = jax.random.key(seed)
    ks = jax.random.split(key, 16)
    f32 = jnp.float32

    def nrm(k, shape, std):
        return jax.random.normal(k, shape, f32) * std

    return {
        "x": nrm(ks[0], (BATCH, SEQ, D_MODEL), 1.0),
        "meta_tokens": nrm(ks[1], (N_META, D_MODEL), 1.0),
        "ln_emb_g": 1.0 + nrm(ks[2], (D_MODEL,), 0.02),
        "ln_emb_b": nrm(ks[3], (D_MODEL,), 0.02),
        "w_in": nrm(ks[4], (DEPTH, D_MODEL, N_IN), D_MODEL ** -0.5),
        "mla_q_norm": 1.0 + nrm(ks[5], (DEPTH, MLA_Q_LORA), 0.02),
        "mla_kv_norm": 1.0 + nrm(ks[6], (DEPTH, MLA_KV_LORA), 0.02),
        "w_uq": nrm(ks[7], (DEPTH, MLA_Q_LORA, MLA_HEADS * (MLA_NOPE + MLA_ROPE)), MLA_Q_LORA ** -0.5),
        "w_ukv": nrm(ks[8], (DEPTH, MLA_KV_LORA, MLA_HEADS * (MLA_NOPE + MLA_V)), MLA_KV_LORA ** -0.5),
        "w_out": nrm(ks[9], (DEPTH, D_MIX, D_MODEL), DN_BETA * D_MIX ** -0.5),
        "ln1_g": 1.0 + nrm(ks[10], (DEPTH, D_MODEL), 0.02),
        "ln1_b": nrm(ks[11], (DEPTH, D_MODEL), 0.02),
        "w_ff1": nrm(ks[12], (DEPTH, D_MODEL, D_FF), D_MODEL ** -0.5),
        "w_ff2": nrm(ks[13], (DEPTH, D_FF, D_MODEL), DN_BETA * D_FF ** -0.5),
        "ln2_g": 1.0 + nrm(ks[14], (DEPTH, D_MODEL), 0.02),
        "ln2_b": nrm(ks[15], (DEPTH, D_MODEL), 0.02),
    }


def _fwd_reference(x, meta_tokens, ln_emb_g, ln_emb_b, w_in, mla_q_norm, mla_kv_norm, w_uq, w_ukv, w_out,
              ln1_g, ln1_b, w_ff1, w_ff2, ln2_g, ln2_b):
    B, S, _ = x.shape
    meta = jnp.broadcast_to(meta_tokens[None].astype(x.dtype), (B, N_META, D_MODEL))
    pad = jnp.zeros((B, N_PAD, D_MODEL), x.dtype)
    h = jnp.concatenate([pad, meta, x], axis=1)
    L = h.shape[1]
    pos_i = jnp.arange(L) - N_PAD
    valid = pos_i >= 0
    pos = pos_i.astype(jnp.float32)
    h = layer_norm(h, ln_emb_g, ln_emb_b)
    for l in range(DEPTH):
        mix = hybrid_mixer(h, w_in[l], mla_q_norm[l], mla_kv_norm[l], w_uq[l], w_ukv[l], w_out[l], pos, valid)
        h = layer_norm(DN_ALPHA * h + mix, ln1_g[l], ln1_b[l])
        h = layer_norm(DN_ALPHA * h + squared_relu_mlp(h, w_ff1[l], w_ff2[l]), ln2_g[l], ln2_b[l])
    return h[:, N_PAD + N_META:]


import jax as _jax
import jax.numpy as _jnp

TWIN_FORMAT = 'train_step'
FWD_PARAMS = ['x', 'meta_tokens', 'ln_emb_g', 'ln_emb_b', 'w_in', 'mla_q_norm', 'mla_kv_norm', 'w_uq', 'w_ukv', 'w_out', 'ln1_g', 'ln1_b', 'w_ff1', 'w_ff2', 'ln2_g', 'ln2_b']
TWIN_WEIGHTS = ['meta_tokens', 'ln_emb_g', 'ln_emb_b', 'w_in', 'mla_q_norm', 'mla_kv_norm', 'w_uq', 'w_ukv', 'w_out', 'ln1_g', 'ln1_b', 'w_ff1', 'w_ff2', 'ln2_g', 'ln2_b']
TWIN_DIFF_INPUT = 'x'
TWIN_INPUTS = ['x', 'meta_tokens', 'ln_emb_g', 'ln_emb_b', 'w_in', 'mla_q_norm', 'mla_kv_norm', 'w_uq', 'w_ukv', 'w_out', 'ln1_g', 'ln1_b', 'w_ff1', 'w_ff2', 'ln2_g', 'ln2_b', 'loss_target', 'm_meta_tokens', 'm_ln_emb_g', 'm_ln_emb_b', 'm_w_in', 'm_mla_q_norm', 'm_mla_kv_norm', 'm_w_uq', 'm_w_ukv', 'm_w_out', 'm_ln1_g', 'm_ln1_b', 'm_w_ff1', 'm_w_ff2', 'm_ln2_g', 'm_ln2_b', 'v_meta_tokens', 'v_ln_emb_g', 'v_ln_emb_b', 'v_w_in', 'v_mla_q_norm', 'v_mla_kv_norm', 'v_w_uq', 'v_w_ukv', 'v_w_out', 'v_ln1_g', 'v_ln1_b', 'v_w_ff1', 'v_w_ff2', 'v_ln2_g', 'v_ln2_b']
TWIN_OUTPUTS = ['loss', 'grad_x', 'grad_meta_tokens', 'grad_ln_emb_g', 'grad_ln_emb_b', 'grad_w_in', 'grad_mla_q_norm', 'grad_mla_kv_norm', 'grad_w_uq', 'grad_w_ukv', 'grad_w_out', 'grad_ln1_g', 'grad_ln1_b', 'grad_w_ff1', 'grad_w_ff2', 'grad_ln2_g', 'grad_ln2_b', 'delta_meta_tokens', 'delta_ln_emb_g', 'delta_ln_emb_b', 'delta_w_in', 'delta_mla_q_norm', 'delta_mla_kv_norm', 'delta_w_uq', 'delta_w_ukv', 'delta_w_out', 'delta_ln1_g', 'delta_ln1_b', 'delta_w_ff1', 'delta_w_ff2', 'delta_ln2_g', 'delta_ln2_b', 'new_m_meta_tokens', 'new_m_ln_emb_g', 'new_m_ln_emb_b', 'new_m_w_in', 'new_m_mla_q_norm', 'new_m_mla_kv_norm', 'new_m_w_uq', 'new_m_w_ukv', 'new_m_w_out', 'new_m_ln1_g', 'new_m_ln1_b', 'new_m_w_ff1', 'new_m_w_ff2', 'new_m_ln2_g', 'new_m_ln2_b', 'new_v_meta_tokens', 'new_v_ln_emb_g', 'new_v_ln_emb_b', 'new_v_w_in', 'new_v_mla_q_norm', 'new_v_mla_kv_norm', 'new_v_w_uq', 'new_v_w_ukv', 'new_v_w_out', 'new_v_ln1_g', 'new_v_ln1_b', 'new_v_w_ff1', 'new_v_w_ff2', 'new_v_ln2_g', 'new_v_ln2_b']
TWIN_LEAF_KINDS = {'loss': 'loss', 'grad_x': 'grad_x', 'grad_meta_tokens': 'grad_w', 'grad_ln_emb_g': 'grad_w', 'grad_ln_emb_b': 'grad_w', 'grad_w_in': 'grad_w', 'grad_mla_q_norm': 'grad_w', 'grad_mla_kv_norm': 'grad_w', 'grad_w_uq': 'grad_w', 'grad_w_ukv': 'grad_w', 'grad_w_out': 'grad_w', 'grad_ln1_g': 'grad_w', 'grad_ln1_b': 'grad_w', 'grad_w_ff1': 'grad_w', 'grad_w_ff2': 'grad_w', 'grad_ln2_g': 'grad_w', 'grad_ln2_b': 'grad_w', 'delta_meta_tokens': 'delta_w', 'delta_ln_emb_g': 'delta_w', 'delta_ln_emb_b': 'delta_w', 'delta_w_in': 'delta_w', 'delta_mla_q_norm': 'delta_w', 'delta_mla_kv_norm': 'delta_w', 'delta_w_uq': 'delta_w', 'delta_w_ukv': 'delta_w', 'delta_w_out': 'delta_w', 'delta_ln1_g': 'delta_w', 'delta_ln1_b': 'delta_w', 'delta_w_ff1': 'delta_w', 'delta_w_ff2': 'delta_w', 'delta_ln2_g': 'delta_w', 'delta_ln2_b': 'delta_w', 'new_m_meta_tokens': 'new_m', 'new_m_ln_emb_g': 'new_m', 'new_m_ln_emb_b': 'new_m', 'new_m_w_in': 'new_m', 'new_m_mla_q_norm': 'new_m', 'new_m_mla_kv_norm': 'new_m', 'new_m_w_uq': 'new_m', 'new_m_w_ukv': 'new_m', 'new_m_w_out': 'new_m', 'new_m_ln1_g': 'new_m', 'new_m_ln1_b': 'new_m', 'new_m_w_ff1': 'new_m', 'new_m_w_ff2': 'new_m', 'new_m_ln2_g': 'new_m', 'new_m_ln2_b': 'new_m', 'new_v_meta_tokens': 'new_v', 'new_v_ln_emb_g': 'new_v', 'new_v_ln_emb_b': 'new_v', 'new_v_w_in': 'new_v', 'new_v_mla_q_norm': 'new_v', 'new_v_mla_kv_norm': 'new_v', 'new_v_w_uq': 'new_v', 'new_v_w_ukv': 'new_v', 'new_v_w_out': 'new_v', 'new_v_ln1_g': 'new_v', 'new_v_ln1_b': 'new_v', 'new_v_w_ff1': 'new_v', 'new_v_w_ff2': 'new_v', 'new_v_ln2_g': 'new_v', 'new_v_ln2_b': 'new_v'}


def _forward(args):
    return _fwd_reference(*[args[k] for k in FWD_PARAMS])


def _output_shape():
    out = _jax.eval_shape(lambda: _forward(_fwd_setup_inputs(0)))
    return out.shape, out.dtype

N_MICROBATCH = 1
ADAM_LR = 0.001
ADAM_B1 = 0.9
ADAM_B2 = 0.999
ADAM_EPS = 1e-08
ADAM_WD = 0.01
ADAM_STEP = 10
PER_EXAMPLE_BATCH_AXIS = {'x': 0, 'loss_target': 0}
SHARED_INPUTS = []
_WEIGHT_DTYPES = {'meta_tokens': _jnp.float32, 'ln_emb_g': _jnp.float32, 'ln_emb_b': _jnp.float32, 'w_in': _jnp.float32, 'mla_q_norm': _jnp.float32, 'mla_kv_norm': _jnp.float32, 'w_uq': _jnp.float32, 'w_ukv': _jnp.float32, 'w_out': _jnp.float32, 'ln1_g': _jnp.float32, 'ln1_b': _jnp.float32, 'w_ff1': _jnp.float32, 'w_ff2': _jnp.float32, 'ln2_g': _jnp.float32, 'ln2_b': _jnp.float32}
MOMENT_SCALE = {'meta_tokens': 3.943832e-03, 'ln_emb_g': 5.965606e-01, 'ln_emb_b': 4.816215e-01, 'w_in': 2.075241e-02, 'mla_q_norm': 6.998134e-03, 'mla_kv_norm': 1.581542e-02, 'w_uq': 4.941564e-03, 'w_ukv': 7.989299e-03, 'w_out': 6.220676e-02, 'ln1_g': 7.842432e-01, 'ln1_b': 5.047638e-01, 'w_ff1': 3.163500e-02, 'w_ff2': 1.734944e-01, 'ln2_g': 1.612731e+01, 'ln2_b': 3.780086e+00}


def _to_microbatches(a, axis):
    t = _jnp.moveaxis(a, axis, 0)
    t = t.reshape((N_MICROBATCH, t.shape[0] // N_MICROBATCH) + t.shape[1:])
    return _jnp.moveaxis(t, 1, axis + 1)


def setup_inputs(seed: int = 0) -> dict:
    inp = _fwd_setup_inputs(seed)
    key = _jax.random.fold_in(_jax.random.key(seed), 7919)
    shape, _ = _output_shape()
    out = dict(inp)
    out["loss_target"] = _jax.random.normal(_jax.random.fold_in(key, 0), shape, _jnp.float32)
    for i, name in enumerate(TWIN_WEIGHTS):
        w = inp[name].astype(_jnp.float32)
        if MOMENT_SCALE is None:
            s = _jnp.sqrt(_jnp.mean(_jnp.square(w)) + 1e-30)
        else:
            s = MOMENT_SCALE[name]
        km, kv = _jax.random.split(_jax.random.fold_in(key, i + 1))
        out[name] = w
        out["m_" + name] = s * _jax.random.normal(km, w.shape, _jnp.float32)
        out["v_" + name] = (s * s) * _jax.random.uniform(kv, w.shape, _jnp.float32, 0.5, 1.5)
    if N_MICROBATCH > 1:
        for name, axis in PER_EXAMPLE_BATCH_AXIS.items():
            out[name] = _to_microbatches(out[name], axis)
    return {'x': out['x'], 'meta_tokens': out['meta_tokens'], 'ln_emb_g': out['ln_emb_g'], 'ln_emb_b': out['ln_emb_b'], 'w_in': out['w_in'], 'mla_q_norm': out['mla_q_norm'], 'mla_kv_norm': out['mla_kv_norm'], 'w_uq': out['w_uq'], 'w_ukv': out['w_ukv'], 'w_out': out['w_out'], 'ln1_g': out['ln1_g'], 'ln1_b': out['ln1_b'], 'w_ff1': out['w_ff1'], 'w_ff2': out['w_ff2'], 'ln2_g': out['ln2_g'], 'ln2_b': out['ln2_b'], 'loss_target': out['loss_target'], 'm_meta_tokens': out['m_meta_tokens'], 'm_ln_emb_g': out['m_ln_emb_g'], 'm_ln_emb_b': out['m_ln_emb_b'], 'm_w_in': out['m_w_in'], 'm_mla_q_norm': out['m_mla_q_norm'], 'm_mla_kv_norm': out['m_mla_kv_norm'], 'm_w_uq': out['m_w_uq'], 'm_w_ukv': out['m_w_ukv'], 'm_w_out': out['m_w_out'], 'm_ln1_g': out['m_ln1_g'], 'm_ln1_b': out['m_ln1_b'], 'm_w_ff1': out['m_w_ff1'], 'm_w_ff2': out['m_w_ff2'], 'm_ln2_g': out['m_ln2_g'], 'm_ln2_b': out['m_ln2_b'], 'v_meta_tokens': out['v_meta_tokens'], 'v_ln_emb_g': out['v_ln_emb_g'], 'v_ln_emb_b': out['v_ln_emb_b'], 'v_w_in': out['v_w_in'], 'v_mla_q_norm': out['v_mla_q_norm'], 'v_mla_kv_norm': out['v_mla_kv_norm'], 'v_w_uq': out['v_w_uq'], 'v_w_ukv': out['v_w_ukv'], 'v_w_out': out['v_w_out'], 'v_ln1_g': out['v_ln1_g'], 'v_ln1_b': out['v_ln1_b'], 'v_w_ff1': out['v_w_ff1'], 'v_w_ff2': out['v_w_ff2'], 'v_ln2_g': out['v_ln2_g'], 'v_ln2_b': out['v_ln2_b']}


def _loss(weights, diff, rest, loss_target):
    with _jax.named_scope("forward"):
        args = {**rest, TWIN_DIFF_INPUT: diff, **{k: w.astype(_WEIGHT_DTYPES[k]) for k, w in weights.items()}}
        y = _forward(args)
    with _jax.named_scope("loss_head"):
        err = _jnp.square(y.astype(_jnp.float32) - loss_target)
        return 0.5 * _jnp.sum(_jnp.mean(err, axis=-1)) if err.ndim else 0.5 * err


def _adamw(w, g, m, v):
    m = ADAM_B1 * m + (1.0 - ADAM_B1) * g
    v = ADAM_B2 * v + (1.0 - ADAM_B2) * _jnp.square(g)
    m_hat = m / (1.0 - ADAM_B1 ** ADAM_STEP)
    v_hat = v / (1.0 - ADAM_B2 ** ADAM_STEP)
    delta = -ADAM_LR * (m_hat / (_jnp.sqrt(v_hat) + ADAM_EPS) + ADAM_WD * w)
    return delta, m, v


def reference(x, meta_tokens, ln_emb_g, ln_emb_b, w_in, mla_q_norm, mla_kv_norm, w_uq, w_ukv, w_out, ln1_g, ln1_b, w_ff1, w_ff2, ln2_g, ln2_b, loss_target, m_meta_tokens, m_ln_emb_g, m_ln_emb_b, m_w_in, m_mla_q_norm, m_mla_kv_norm, m_w_uq, m_w_ukv, m_w_out, m_ln1_g, m_ln1_b, m_w_ff1, m_w_ff2, m_ln2_g, m_ln2_b, v_meta_tokens, v_ln_emb_g, v_ln_emb_b, v_w_in, v_mla_q_norm, v_mla_kv_norm, v_w_uq, v_w_ukv, v_w_out, v_ln1_g, v_ln1_b, v_w_ff1, v_w_ff2, v_ln2_g, v_ln2_b):
    given = dict(x=x, meta_tokens=meta_tokens, ln_emb_g=ln_emb_g, ln_emb_b=ln_emb_b, w_in=w_in, mla_q_norm=mla_q_norm, mla_kv_norm=mla_kv_norm, w_uq=w_uq, w_ukv=w_ukv, w_out=w_out, ln1_g=ln1_g, ln1_b=ln1_b, w_ff1=w_ff1, w_ff2=w_ff2, ln2_g=ln2_g, ln2_b=ln2_b, loss_target=loss_target, m_meta_tokens=m_meta_tokens, m_ln_emb_g=m_ln_emb_g, m_ln_emb_b=m_ln_emb_b, m_w_in=m_w_in, m_mla_q_norm=m_mla_q_norm, m_mla_kv_norm=m_mla_kv_norm, m_w_uq=m_w_uq, m_w_ukv=m_w_ukv, m_w_out=m_w_out, m_ln1_g=m_ln1_g, m_ln1_b=m_ln1_b, m_w_ff1=m_w_ff1, m_w_ff2=m_w_ff2, m_ln2_g=m_ln2_g, m_ln2_b=m_ln2_b, v_meta_tokens=v_meta_tokens, v_ln_emb_g=v_ln_emb_g, v_ln_emb_b=v_ln_emb_b, v_w_in=v_w_in, v_mla_q_norm=v_mla_q_norm, v_mla_kv_norm=v_mla_kv_norm, v_w_uq=v_w_uq, v_w_ukv=v_w_ukv, v_w_out=v_w_out, v_ln1_g=v_ln1_g, v_ln1_b=v_ln1_b, v_w_ff1=v_w_ff1, v_w_ff2=v_w_ff2, v_ln2_g=v_ln2_g, v_ln2_b=v_ln2_b)
    weights = {n: given[n] for n in TWIN_WEIGHTS}
    shared = {n: given[n] for n in SHARED_INPUTS}
    per_example = {n: given[n] for n in ['x']}
    grad_fn = _jax.value_and_grad(_loss, argnums=(0, 1))

    def one_microbatch(ex, loss_target):
        ex = dict(ex)
        diff = ex.pop(TWIN_DIFF_INPUT)
        return grad_fn(weights, diff, {**shared, **ex}, loss_target)

    if N_MICROBATCH == 1:
        loss, (grad_w, grad_x) = one_microbatch(per_example, given["loss_target"])
    else:
        def body(carry, xs):
            loss_sum, grad_sum = carry
            l_k, (gw_k, gx_k) = one_microbatch(xs[0], xs[1])
            with _jax.named_scope("update"):
                return (loss_sum + l_k, _jax.tree.map(_jnp.add, grad_sum, gw_k)), gx_k

        init = (_jnp.zeros((), _jnp.float32), _jax.tree.map(_jnp.zeros_like, weights))
        (loss, grad_w), grad_x = _jax.lax.scan(body, init, (per_example, given["loss_target"]))
    with _jax.named_scope("update"):
        delta_w, new_m, new_v = {}, {}, {}
        for n in TWIN_WEIGHTS:
            delta_w[n], new_m[n], new_v[n] = _adamw(weights[n], grad_w[n], given["m_" + n], given["v_" + n])
    return (loss, grad_x, *[grad_w[n] for n in TWIN_WEIGHTS], *[delta_w[n] for n in TWIN_WEIGHTS],
            *[new_m[n] for n in TWIN_WEIGHTS], *[new_v[n] for n in TWIN_WEIGHTS])
```

```python
import functools
import math

import numpy as np
import jax
import jax.numpy as jnp
from jax import lax
from jax.experimental import pallas as pl
from jax.experimental.pallas import tpu as pltpu

f32 = jnp.float32
bf16 = jnp.bfloat16
_MXU = jnp.bfloat16

BLK = 128
N_META = 16
N_PAD = 112
D = 1024
N_DEV = 8
LN_EPS = 1e-5
DEPTH = 4
DN_ALPHA = (2 * DEPTH) ** 0.25
ROPE_THETA = 10000.0
MLA_SCALE = (64 + 32) ** -0.5
SB_SCALE = 0.125
RET_SCALE = 0.125
RET_GAMMA = tuple(1.0 - 2.0 ** (-5 - h) for h in range(4))

ADAM_LR, ADAM_B1, ADAM_B2, ADAM_EPS, ADAM_WD, ADAM_STEP = 0.001, 0.9, 0.999, 1e-08, 0.01, 10

C_SBQ, C_SBK, C_SBV, C_CKV, C_RQ, C_RK, C_RV, C_RG, C_CQ, C_KR, N_INP = (
    0, 512, 1024, 1536, 1792, 2048, 2304, 2816, 3328, 3712, 3840)

VMEM_LIMIT = 56 * 1024 * 1024


def _cp(sem):
    return pltpu.CompilerParams(dimension_semantics=sem, vmem_limit_bytes=VMEM_LIMIT)


def _pick(n, cands):
    for c in cands:
        if n % c == 0:
            return c
    return n


def _dot(a, b, dims=(((1,), (0,)), ((), ()))):
    return lax.dot_general(a.astype(_MXU), b.astype(_MXU), dims, preferred_element_type=f32)


NT = (((1,), (1,)), ((), ()))
TN = (((0,), (0,)), ((), ()))


def _dot3(x, u):
    hi = x.astype(_MXU)
    r1 = x - hi.astype(f32)
    mid = r1.astype(_MXU)
    lo = (r1 - mid.astype(f32)).astype(_MXU)
    return (jnp.dot(hi, u, preferred_element_type=f32) + jnp.dot(mid, u, preferred_element_type=f32)
            + jnp.dot(lo, u, preferred_element_type=f32))


def _rot(x, half):
    lane = lax.broadcasted_iota(jnp.int32, x.shape, 1)
    first = (lane % 64) < half
    return jnp.where(first, -pltpu.roll(x, 128 - half, 1), pltpu.roll(x, half, 1))


def _mm(a, b, mode, name, epi=None, extra=None, alpha=1.0, out_dtype=f32):
    if mode == "nn":
        (M, K), N = a.shape, b.shape[1]
    elif mode == "nt":
        (M, K), N = a.shape, b.shape[0]
    else:
        (K, M), N = a.shape, b.shape[1]
    tm = _pick(M, (512, 384, 256, 128))
    tn = _pick(N, (1024, 768, 512, 384, 256, 128))
    tk = _pick(K, (1024, 768, 512, 384, 256, 128))
    nk = K // tk
    if mode == "nn":
        a_spec = pl.BlockSpec((tm, tk), lambda i, j, k: (i, k))
        b_spec = pl.BlockSpec((tk, tn), lambda i, j, k: (k, j))
        dims = (((1,), (0,)), ((), ()))
    elif mode == "nt":
        a_spec = pl.BlockSpec((tm, tk), lambda i, j, k: (i, k))
        b_spec = pl.BlockSpec((tn, tk), lambda i, j, k: (j, k))
        dims = NT
    else:
        a_spec = pl.BlockSpec((tk, tm), lambda i, j, k: (k, i))
        b_spec = pl.BlockSpec((tk, tn), lambda i, j, k: (k, j))
        dims = TN
    o_spec = pl.BlockSpec((tm, tn), lambda i, j, k: (i, j))
    in_specs, args = [a_spec, b_spec], [a, b]
    if extra is not None:
        in_specs.append(o_spec)
        args.append(extra)
    if epi == "relu2":
        out_shape = (jax.ShapeDtypeStruct((M, N), f32), jax.ShapeDtypeStruct((M, N), bf16))
        out_specs = (o_spec, o_spec)
    else:
        out_shape = jax.ShapeDtypeStruct((M, N), out_dtype)
        out_specs = o_spec

    def body(*refs):
        a_ref, b_ref = refs[0], refs[1]
        acc = refs[-1]
        k = pl.program_id(2)

        @pl.when(k == 0)
        def _():
            acc[...] = jnp.zeros_like(acc)

        acc[...] += _dot(a_ref[...], b_ref[...], dims)

        @pl.when(k == nk - 1)
        def _():
            r = acc[...]
            if epi == "relu2":
                refs[2][...] = r
                refs[3][...] = jnp.square(jnp.maximum(r, 0.0)).astype(bf16)
            elif epi == "mul_relu":
                refs[3][...] = r * (2.0 * jnp.maximum(refs[2][...], 0.0))
            elif epi == "add":
                refs[3][...] = r + alpha * refs[2][...]
            else:
                refs[2][...] = r.astype(out_dtype)

    return pl.pallas_call(
        body, name=name, grid=(M // tm, N // tn, nk), in_specs=in_specs, out_specs=out_specs,
        out_shape=out_shape, scratch_shapes=[pltpu.VMEM((tm, tn), f32)],
        compiler_params=_cp(("parallel", "parallel", "arbitrary")))(*args)


def _ln_fwd(h, m, g, b, alpha, name):
    L = h.shape[0]
    tm = _pick(L, (384, 256, 128))
    row = pl.BlockSpec((tm, D), lambda i: (i, 0))
    vec = pl.BlockSpec((1, D), lambda i: (0, 0))

    def body(*refs):
        if m is None:
            h_ref, g_ref, b_ref, o_ref = refs
            z = h_ref[...]
        else:
            h_ref, m_ref, g_ref, b_ref, o_ref = refs
            z = alpha * h_ref[...] + m_ref[...]
        mu = jnp.mean(z, -1, keepdims=True)
        var = jnp.mean(jnp.square(z - mu), -1, keepdims=True)
        o_ref[...] = (z - mu) * lax.rsqrt(var + LN_EPS) * g_ref[...] + b_ref[...]

    args = [h] + ([] if m is None else [m]) + [g.reshape(1, D), b.reshape(1, D)]
    specs = [row] + ([] if m is None else [row]) + [vec, vec]
    return pl.pallas_call(body, name=name, grid=(L // tm,), in_specs=specs, out_specs=row,
                          out_shape=jax.ShapeDtypeStruct((L, D), f32), compiler_params=_cp(("parallel",)))(*args)


def _ln_bwd(dy, h, m, g, alpha, name):
    L = h.shape[0]
    tm = _pick(L, (384, 256, 128))
    row = pl.BlockSpec((tm, D), lambda i: (i, 0))
    vec = pl.BlockSpec((1, D), lambda i: (0, 0))
    acc = pl.BlockSpec((8, D), lambda i: (0, 0))

    def body(*refs):
        if m is None:
            dy_ref, h_ref, g_ref, dz_ref, dg_ref, db_ref = refs
            z = h_ref[...]
        else:
            dy_ref, h_ref, m_ref, g_ref, dz_ref, dg_ref, db_ref = refs
            z = alpha * h_ref[...] + m_ref[...]

        @pl.when(pl.program_id(0) == 0)
        def _():
            dg_ref[...] = jnp.zeros_like(dg_ref)
            db_ref[...] = jnp.zeros_like(db_ref)

        dyv = dy_ref[...]
        mu = jnp.mean(z, -1, keepdims=True)
        zc = z - mu
        rstd = lax.rsqrt(jnp.mean(jnp.square(zc), -1, keepdims=True) + LN_EPS)
        xh = zc * rstd
        dxh = dyv * g_ref[...]
        dz_ref[...] = rstd * (dxh - jnp.mean(dxh, -1, keepdims=True) - xh * jnp.mean(dxh * xh, -1, keepdims=True))
        dg_ref[0:1, :] += jnp.sum(dyv * xh, 0, keepdims=True)
        db_ref[0:1, :] += jnp.sum(dyv, 0, keepdims=True)

    args = [dy, h] + ([] if m is None else [m]) + [g.reshape(1, D)]
    specs = [row, row] + ([] if m is None else [row]) + [vec]
    dz, dg, db = pl.pallas_call(
        body, name=name, grid=(L // tm,), in_specs=specs, out_specs=(row, acc, acc),
        out_shape=(jax.ShapeDtypeStruct((L, D), f32), jax.ShapeDtypeStruct((8, D), f32),
                   jax.ShapeDtypeStruct((8, D), f32)),
        compiler_params=_cp(("arbitrary",)))(*args)
    return dz, dg[0], db[0]


def _rms(x, g):
    r = lax.rsqrt(jnp.mean(jnp.square(x), -1, keepdims=True) + LN_EPS)
    return x * r * g


def _prep_fwd(proj, gq, gkv, tabs):
    L = proj.shape[0]
    tm = BLK
    rc, rs, mc, ms = tabs

    def body(p_ref, gq_ref, gkv_ref, rc_ref, rs_ref, mc_ref, ms_ref, sb_ref, cq_ref, ckv_ref, rqk_ref, rv_ref, kr_ref):
        i = pl.program_id(0)
        sb_ref[:, 0:512] = (p_ref[:, C_SBQ:C_SBQ + 512] * SB_SCALE).astype(bf16)
        sb_ref[:, 512:1536] = p_ref[:, C_SBK:C_SBK + 1024].astype(bf16)
        cq_ref[...] = _rms(p_ref[:, C_CQ:C_CQ + 384], gq_ref[...]).astype(bf16)
        ckv_ref[...] = _rms(p_ref[:, C_CKV:C_CKV + 256], gkv_ref[...]).astype(bf16)
        valid = (i * tm + lax.broadcasted_iota(jnp.int32, (tm, 128), 0)) >= N_PAD
        for c in range(2):
            sl = slice(c * 128, (c + 1) * 128)
            x = p_ref[:, C_RQ + c * 128:C_RQ + (c + 1) * 128]
            rqk_ref[:, sl] = (x * rc_ref[:, sl] + _rot(x, 32) * rs_ref[:, sl]).astype(bf16)
            x = p_ref[:, C_RK + c * 128:C_RK + (c + 1) * 128]
            kk = (x * rc_ref[:, sl] + _rot(x, 32) * rs_ref[:, sl]) * RET_SCALE
            rqk_ref[:, 256 + c * 128:256 + (c + 1) * 128] = jnp.where(valid, kk, 0.0).astype(bf16)
        rv_ref[...] = p_ref[:, C_RV:C_RV + 512].astype(bf16)
        x = p_ref[:, C_KR:C_KR + 128]
        kr_ref[...] = (x * mc_ref[...] + _rot(x, 16) * ms_ref[...]).astype(bf16)

    def row(w):
        return pl.BlockSpec((tm, w), lambda i: (i, 0))

    def vec(w):
        return pl.BlockSpec((1, w), lambda i: (0, 0))

    widths = (1536, 384, 256, 512, 512, 128)
    return pl.pallas_call(
        body, name="prep_fwd", grid=(L // tm,),
        in_specs=[row(N_INP), vec(384), vec(256), row(256), row(256), row(128), row(128)],
        out_specs=tuple(row(w) for w in widths),
        out_shape=tuple(jax.ShapeDtypeStruct((L, w), bf16) for w in widths),
        compiler_params=_cp(("parallel",)))(proj, gq.reshape(1, 384), gkv.reshape(1, 256), rc, rs, mc, ms)


def _rms_bwd(x, g, dy):
    r = lax.rsqrt(jnp.mean(jnp.square(x), -1, keepdims=True) + LN_EPS)
    u = dy * g
    dx = r * u - x * (r * r * r) * jnp.mean(x * u, -1, keepdims=True)
    return dx, jnp.sum(dy * x * r, 0, keepdims=True)


def _prep_bwd(proj, gq, gkv, tabs, dcqn, dckvn, drq_r, drk_r, dkr_r):
    L = proj.shape[0]
    tm = BLK
    rc, rs, mc, ms = tabs

    def body(p_ref, gq_ref, gkv_ref, rc_ref, rs_ref, mc_ref, ms_ref, dcqn_ref, dckvn_ref, drq_ref, drk_ref,
             dkr_ref, ocq_ref, ockv_ref, orq_ref, ork_ref, okr_ref, dgq_ref, dgkv_ref):
        i = pl.program_id(0)

        @pl.when(i == 0)
        def _():
            dgq_ref[...] = jnp.zeros_like(dgq_ref)
            dgkv_ref[...] = jnp.zeros_like(dgkv_ref)

        dx, dg = _rms_bwd(p_ref[:, C_CQ:C_CQ + 384], gq_ref[...], dcqn_ref[...])
        ocq_ref[...] = dx
        dgq_ref[0:1, :] += dg
        dx, dg = _rms_bwd(p_ref[:, C_CKV:C_CKV + 256], gkv_ref[...], dckvn_ref[...])
        ockv_ref[...] = dx
        dgkv_ref[0:1, :] += dg
        valid = (i * tm + lax.broadcasted_iota(jnp.int32, (tm, 128), 0)) >= N_PAD
        for c in range(2):
            sl = slice(c * 128, (c + 1) * 128)
            dy = drq_ref[:, sl]
            orq_ref[:, sl] = dy * rc_ref[:, sl] - _rot(dy * rs_ref[:, sl], 32)
            dy = jnp.where(valid, drk_ref[:, sl], 0.0) * RET_SCALE
            ork_ref[:, sl] = dy * rc_ref[:, sl] - _rot(dy * rs_ref[:, sl], 32)
        dy = dkr_ref[...]
        okr_ref[...] = dy * mc_ref[...] - _rot(dy * ms_ref[...], 16)

    def row(w):
        return pl.BlockSpec((tm, w), lambda i: (i, 0))

    def vec(w):
        return pl.BlockSpec((1, w), lambda i: (0, 0))

    def acc(w):
        return pl.BlockSpec((8, w), lambda i: (0, 0))

    widths = (384, 256, 256, 256, 128)
    outs = pl.pallas_call(
        body, name="prep_bwd", grid=(L // tm,),
        in_specs=[row(N_INP), vec(384), vec(256), row(256), row(256), row(128), row(128),
                  row(384), row(256), row(256), row(256), row(128)],
        out_specs=tuple(row(w) for w in widths) + (acc(384), acc(256)),
        out_shape=tuple(jax.ShapeDtypeStruct((L, w), f32) for w in widths)
        + (jax.ShapeDtypeStruct((8, 384), f32), jax.ShapeDtypeStruct((8, 256), f32)),
        compiler_params=_cp(("arbitrary",)))(
            proj, gq.reshape(1, 384), gkv.reshape(1, 256), rc, rs, mc, ms, dcqn, dckvn, drq_r, drk_r, dkr_r)
    return outs[:5] + (outs[5][0], outs[6][0])


def _sb_tile(qx, k, mask, carry, u_gt):
    z = _dot(qx, k, NT)
    t = jnp.log1p(jnp.exp(-jnp.abs(z)))
    lb = jnp.minimum(z, 0.0) - t
    lk = jnp.where(mask, -jnp.maximum(z, 0.0) - t, 0.0)
    excl = _dot3(lk, u_gt)
    w = jnp.where(mask, jnp.exp(lb + excl + carry), 0.0)
    return w, lb, lk


def _sb_masks(i, j):
    row = i * BLK + lax.broadcasted_iota(jnp.int32, (BLK, BLK), 0)
    col = j * BLK + lax.broadcasted_iota(jnp.int32, (BLK, BLK), 1)
    return (col < row) & (col >= N_PAD)


def _tri(strict):
    r = lax.broadcasted_iota(jnp.int32, (BLK, BLK), 0)
    c = lax.broadcasted_iota(jnp.int32, (BLK, BLK), 1)
    return ((r > c) if strict else (r >= c)).astype(_MXU)


def _sb_fwd(sb):
    L = sb.shape[0]
    nq = L // BLK

    def body(q_ref, k_ref, v_ref, o_ref):
        i = pl.program_id(1)
        m_a = lax.broadcasted_iota(jnp.int32, (1, BLK), 1) < 64
        q = q_ref[...]
        zq = jnp.zeros_like(q)
        qs = (jnp.where(m_a, q, zq), jnp.where(m_a, zq, q))
        u_gt = _tri(True)

        def step(jj, c):
            j = i - jj
            off = pl.multiple_of(j * BLK, BLK)
            k = k_ref[pl.ds(off, BLK), :]
            v = v_ref[pl.ds(off, BLK), :]
            mask = _sb_masks(i, j)
            out = []
            for x in range(2):
                w, _, lk = _sb_tile(qs[x], k, mask, c[2 + x], u_gt)
                out.append(c[x] + _dot(w, v))
                out.append(c[2 + x] + jnp.sum(lk, 1, keepdims=True))
            return (out[0], out[2], out[1], out[3])

        z0 = jnp.zeros((BLK, BLK), f32)
        c0 = jnp.zeros((BLK, 1), f32)
        acc_a, acc_b, _, _ = lax.fori_loop(0, i + 1, step, (z0, z0, c0, c0))
        o_ref[...] = jnp.where(m_a, acc_a, acc_b)

    return pl.pallas_call(
        body, name="sb_fwd", grid=(4, nq),
        in_specs=[pl.BlockSpec((BLK, 128), lambda p, i: (i, p)),
                  pl.BlockSpec((L, 128), lambda p, i: (0, 4 + p)),
                  pl.BlockSpec((L, 128), lambda p, i: (0, 8 + p))],
        out_specs=pl.BlockSpec((BLK, 128), lambda p, i: (i, p)),
        out_shape=jax.ShapeDtypeStruct((L, 512), f32),
        compiler_params=_cp(("parallel", "arbitrary")))(sb, sb, sb)


def _sb_bwd(dmixed, sb, out_a):
    L = sb.shape[0]
    nq = L // BLK

    def body(do_ref, o_ref, q_ref, k_ref, v_ref, dq_ref, dk_ref, dv_ref):
        i = pl.program_id(1)

        @pl.when(i == 0)
        def _():
            dk_ref[...] = jnp.zeros_like(dk_ref)
            dv_ref[...] = jnp.zeros_like(dv_ref)

        m_a = lax.broadcasted_iota(jnp.int32, (1, BLK), 1) < 64
        q = q_ref[...]
        zq = jnp.zeros_like(q)
        qs = (jnp.where(m_a, q, zq), jnp.where(m_a, zq, q))
        do = do_ref[...]
        zd = jnp.zeros_like(do)
        dos = (jnp.where(m_a, do, zd), jnp.where(m_a, zd, do))
        prod = do.astype(_MXU).astype(f32) * o_ref[...]
        dsum = (jnp.sum(jnp.where(m_a, prod, 0.0), 1, keepdims=True),
                jnp.sum(jnp.where(m_a, 0.0, prod), 1, keepdims=True))
        u_gt = _tri(True)
        u_ge = _tri(False)

        def step(jj, c):
            j = i - jj
            off = pl.multiple_of(j * BLK, BLK)
            k = k_ref[pl.ds(off, BLK), :]
            v = v_ref[pl.ds(off, BLK), :]
            mask = _sb_masks(i, j)
            dk_t = jnp.zeros((BLK, BLK), f32)
            dv_t = jnp.zeros((BLK, BLK), f32)
            out = []
            for x in range(2):
                w, lb, lk = _sb_tile(qs[x], k, mask, c[2 + x], u_gt)
                gr = w.astype(_MXU).astype(f32) * _dot(dos[x], v, NT)
                suffix = _dot3(gr, u_ge) + c[4 + x]
                sig = jnp.exp(lb)
                dz = jnp.where(mask, gr - sig * (gr + dsum[x] - suffix), 0.0)
                out.append(c[x] + _dot(dz, k))
                out.append(c[2 + x] + jnp.sum(lk, 1, keepdims=True))
                out.append(c[4 + x] + jnp.sum(gr, 1, keepdims=True))
                dk_t += _dot(dz, qs[x], TN)
                dv_t += _dot(w, dos[x], TN)
            dk_ref[pl.ds(off, BLK), :] += dk_t
            dv_ref[pl.ds(off, BLK), :] += dv_t
            return (out[0], out[3], out[1], out[4], out[2], out[5])

        z0 = jnp.zeros((BLK, BLK), f32)
        c0 = jnp.zeros((BLK, 1), f32)
        res = lax.fori_loop(0, i + 1, step, (z0, z0, c0, c0, c0, c0))
        dq_ref[...] = jnp.where(m_a, res[0], res[1]) * SB_SCALE

    blk = pl.BlockSpec((BLK, 128), lambda p, i: (i, p))
    return pl.pallas_call(
        body, name="sb_bwd", grid=(4, nq),
        in_specs=[blk, blk, blk,
                  pl.BlockSpec((L, 128), lambda p, i: (0, 4 + p)),
                  pl.BlockSpec((L, 128), lambda p, i: (0, 8 + p))],
        out_specs=(blk, pl.BlockSpec((L, 128), lambda p, i: (0, p)), pl.BlockSpec((L, 128), lambda p, i: (0, p))),
        out_shape=tuple(jax.ShapeDtypeStruct((L, 512), f32) for _ in range(3)),
        compiler_params=_cp(("parallel", "arbitrary")))(dmixed, out_a, sb, sb, sb)


def _mla_masks(i, j):
    row = i * BLK + lax.broadcasted_iota(jnp.int32, (BLK, BLK), 0)
    col = j * BLK + lax.broadcasted_iota(jnp.int32, (BLK, BLK), 1)
    return (col <= row) & ((col >= N_PAD) | (col == row))


def _pair_mask2():
    return (lax.broadcasted_iota(jnp.int32, (1, 256), 1) % 128) < 64


def _mla_q2(qn_ref, qr_ref, mc_ref, ms_ref):
    qr = qr_ref[...]
    qr = qr * mc_ref[...] + _rot(qr, 16) * ms_ref[...]
    q2 = jnp.concatenate([qn_ref[...], qr], axis=1)
    m2 = _pair_mask2()
    z2 = jnp.zeros_like(q2)
    return (jnp.where(m2, q2, z2).astype(_MXU), jnp.where(m2, z2, q2).astype(_MXU))


def _mla_fwd(q, kv, kr2, mc, ms):
    L = q.shape[0]
    nq = L // BLK

    def body(qn_ref, qr_ref, mc_ref, ms_ref, kn_ref, v_ref, kr_ref, o_ref, lse_ref):
        i = pl.program_id(1)
        m_a = lax.broadcasted_iota(jnp.int32, (1, BLK), 1) < 64
        qs = _mla_q2(qn_ref, qr_ref, mc_ref, ms_ref)

        def step(jj, c):
            j = i - jj
            off = pl.multiple_of(j * BLK, BLK)
            k2 = jnp.concatenate([kn_ref[pl.ds(off, BLK), :], kr_ref[pl.ds(off, BLK), :]], axis=1)
            v = v_ref[pl.ds(off, BLK), :]
            mask = _mla_masks(i, j)
            out = []
            for x in range(2):
                acc, m_run, l_run = c[3 * x], c[3 * x + 1], c[3 * x + 2]
                s = jnp.where(mask, _dot(qs[x], k2, NT) * MLA_SCALE, -1e30)
                m_new = jnp.maximum(m_run, jnp.max(s, 1, keepdims=True))
                a = jnp.exp(m_run - m_new)
                p = jnp.where(mask, jnp.exp(s - m_new), 0.0)
                out += [a * acc + _dot(p, v), m_new, a * l_run + jnp.sum(p, 1, keepdims=True)]
            return tuple(out)

        z0 = jnp.zeros((BLK, BLK), f32)
        c0 = jnp.zeros((BLK, 1), f32)
        mi = jnp.full((BLK, 1), -1e30, f32)
        r = lax.fori_loop(0, i + 1, step, (z0, mi, c0, z0, mi, c0))
        o_ref[...] = jnp.where(m_a, r[0] / r[2], r[3] / r[5])
        lse_ref[0] = jnp.where(m_a, r[1] + jnp.log(r[2]), r[4] + jnp.log(r[5]))

    blk = lambda cb: pl.BlockSpec((BLK, 128), lambda p, i: (i, cb + p))
    full = lambda cb: pl.BlockSpec((L, 128), lambda p, i: (0, cb + p))
    tab = pl.BlockSpec((BLK, 128), lambda p, i: (i, 0))
    return pl.pallas_call(
        body, name="mla_fwd", grid=(4, nq),
        in_specs=[blk(0), blk(4), tab, tab, full(0), full(4), pl.BlockSpec((L, 128), lambda p, i: (0, 0))],
        out_specs=(blk(0), pl.BlockSpec((1, BLK, 128), lambda p, i: (p, i, 0))),
        out_shape=(jax.ShapeDtypeStruct((L, 512), f32), jax.ShapeDtypeStruct((4, L, 128), f32)),
        compiler_params=_cp(("parallel", "arbitrary")))(q, q, mc, ms, kv, kv, kr2)


def _mla_bwd(dmixed, q, kv, kr2, out_b, lse, mc, ms):
    L = q.shape[0]
    nq = L // BLK

    def body(do_ref, o_ref, lse_ref, qn_ref, qr_ref, mc_ref, ms_ref, kn_ref, v_ref, kr_ref,
             dqn_ref, dqr_ref, dkn_ref, dv_ref, dkr_ref):
        i = pl.program_id(1)

        @pl.when(i == 0)
        def _():
            dkn_ref[...] = jnp.zeros_like(dkn_ref)
            dv_ref[...] = jnp.zeros_like(dv_ref)
            dkr_ref[...] = jnp.zeros_like(dkr_ref)

        m_a = lax.broadcasted_iota(jnp.int32, (1, BLK), 1) < 64
        qs = _mla_q2(qn_ref, qr_ref, mc_ref, ms_ref)
        do = do_ref[...]
        zd = jnp.zeros_like(do)
        dos = (jnp.where(m_a, do, zd), jnp.where(m_a, zd, do))
        prod = do * o_ref[...]
        dsum = (jnp.sum(jnp.where(m_a, prod, 0.0), 1, keepdims=True),
                jnp.sum(jnp.where(m_a, 0.0, prod), 1, keepdims=True))
        lse_t = lse_ref[0]
        lane = lax.broadcasted_iota(jnp.int32, (BLK, BLK), 1)
        lses = (jnp.sum(jnp.where(lane == 0, lse_t, 0.0), 1, keepdims=True),
                jnp.sum(jnp.where(lane == 64, lse_t, 0.0), 1, keepdims=True))

        def step(jj, c):
            j = i - jj
            off = pl.multiple_of(j * BLK, BLK)
            k2 = jnp.concatenate([kn_ref[pl.ds(off, BLK), :], kr_ref[pl.ds(off, BLK), :]], axis=1)
            v = v_ref[pl.ds(off, BLK), :]
            mask = _mla_masks(i, j)
            dk_t = jnp.zeros((BLK, 256), f32)
            dv_t = jnp.zeros((BLK, BLK), f32)
            out = []
            for x in range(2):
                s = _dot(qs[x], k2, NT) * MLA_SCALE
                p = jnp.where(mask, jnp.exp(jnp.where(mask, s, 0.0) - lses[x]), 0.0)
                ds = p * (_dot(dos[x], v, NT) - dsum[x]) * MLA_SCALE
                out.append(c[x] + _dot(ds, k2))
                dk_t += _dot(ds, qs[x], TN)
                dv_t += _dot(p, dos[x], TN)
            dkn_ref[pl.ds(off, BLK), :] += dk_t[:, 0:128]
            dkr_ref[0, pl.ds(off, BLK), :] += dk_t[:, 128:256]
            dv_ref[pl.ds(off, BLK), :] += dv_t
            return tuple(out)

        z0 = jnp.zeros((BLK, 256), f32)
        r = lax.fori_loop(0, i + 1, step, (z0, z0))
        dq2 = jnp.where(_pair_mask2(), r[0], r[1])
        dqn_ref[...] = dq2[:, 0:128]
        dy = dq2[:, 128:256]
        dqr_ref[...] = dy * mc_ref[...] - _rot(dy * ms_ref[...], 16)

    blk = lambda cb: pl.BlockSpec((BLK, 128), lambda p, i: (i, cb + p))
    full = lambda cb: pl.BlockSpec((L, 128), lambda p, i: (0, cb + p))
    tab = pl.BlockSpec((BLK, 128), lambda p, i: (i, 0))
    o512 = jax.ShapeDtypeStruct((L, 512), f32)
    return pl.pallas_call(
        body, name="mla_bwd", grid=(4, nq),
        in_specs=[blk(4), blk(0), pl.BlockSpec((1, BLK, 128), lambda p, i: (p, i, 0)), blk(0), blk(4), tab, tab,
                  full(0), full(4), pl.BlockSpec((L, 128), lambda p, i: (0, 0))],
        out_specs=(blk(0), blk(0), full(0), full(0), pl.BlockSpec((1, L, 128), lambda p, i: (p, 0, 0))),
        out_shape=(o512, o512, o512, o512, jax.ShapeDtypeStruct((4, L, 128), f32)),
        compiler_params=_cp(("parallel", "arbitrary")))(dmixed, out_b, lse, q, q, mc, ms, kv, kv, kr2)


def _ret_tables():
    log_g = jnp.log(jnp.array(RET_GAMMA, f32))
    idx = jnp.arange(BLK, dtype=f32)
    diff = idx[:, None] - idx[None, :]
    d_in = jnp.where(diff[None] >= 0, jnp.exp(jnp.maximum(diff, 0.0)[None] * log_g[:, None, None]), 0.0)
    q_dec = jnp.exp((idx[None, :] + 1.0) * log_g[:, None])
    k_dec = jnp.exp((BLK - 1.0 - idx[None, :]) * log_g[:, None])
    c_dec = jnp.exp(BLK * log_g)
    bc = lambda a: jnp.broadcast_to(a[:, :, None], (4, BLK, BLK))
    return d_in, bc(q_dec), bc(k_dec), jnp.broadcast_to(c_dec[:, None, None], (4, 8, BLK))


def _head_mask(x):
    lane = lax.broadcasted_iota(jnp.int32, (1, BLK), 1)
    return (lane < 64) if x == 0 else (lane >= 64)


def _ret_fwd(rqk, rv, proj, rtabs):
    L = rqk.shape[0]
    n = L // BLK
    d_in, q_dec, k_dec, c_dec = rtabs

    def body(q_ref, k_ref, v_ref, g_ref, din_ref, qd_ref, kd_ref, cd_ref, y_ref, o_ref, st_ref, s_scr):
        @pl.when(pl.program_id(1) == 0)
        def _():
            s_scr[...] = jnp.zeros_like(s_scr)

        q = q_ref[...]
        k = k_ref[...]
        zq = jnp.zeros_like(q)
        for x in range(2):
            hm = _head_mask(x)
            sl = slice(x * 128, (x + 1) * 128)
            qm = jnp.where(hm, q, zq)
            km = jnp.where(hm, k, zq)
            v = v_ref[:, sl]
            s_in = s_scr[x]
            st_ref[0, 0, x] = s_in
            inner = _dot(qm, km, NT) * din_ref[x]
            y = _dot(inner, v) + _dot(qm, s_in) * qd_ref[x]
            s_scr[x] = s_in * cd_ref[x, 0:1, :] + _dot(km.astype(f32) * kd_ref[x], v, TN)
            y_ref[:, sl] = y
            mu = jnp.mean(y, -1, keepdims=True)
            yc = y - mu
            yn = yc * lax.rsqrt(jnp.mean(jnp.square(yc), -1, keepdims=True) + LN_EPS)
            g = g_ref[:, sl]
            o_ref[:, sl] = g * jax.nn.sigmoid(g) * yn

    tab = pl.BlockSpec((2, BLK, BLK), lambda p, i: (p, 0, 0))
    return pl.pallas_call(
        body, name="ret_fwd", grid=(2, n),
        in_specs=[pl.BlockSpec((BLK, 128), lambda p, i: (i, p)), pl.BlockSpec((BLK, 128), lambda p, i: (i, 2 + p)),
                  pl.BlockSpec((BLK, 256), lambda p, i: (i, p)),
                  pl.BlockSpec((BLK, 256), lambda p, i: (i, C_RG // 256 + p)),
                  tab, tab, tab, pl.BlockSpec((2, 8, BLK), lambda p, i: (p, 0, 0))],
        out_specs=(pl.BlockSpec((BLK, 256), lambda p, i: (i, p)), pl.BlockSpec((BLK, 256), lambda p, i: (i, p)),
                   pl.BlockSpec((1, 1, 2, BLK, BLK), lambda p, i: (p, i, 0, 0, 0))),
        out_shape=(jax.ShapeDtypeStruct((L, 512), f32), jax.ShapeDtypeStruct((L, 512), f32),
                   jax.ShapeDtypeStruct((2, n, 2, BLK, BLK), f32)),
        scratch_shapes=[pltpu.VMEM((2, BLK, BLK), f32)],
        compiler_params=_cp(("parallel", "arbitrary")))(rqk, rqk, rv, proj, d_in, q_dec, k_dec, c_dec)


def _ret_bwd(dmixed, rqk, rv, proj, y, states, rtabs):
    L = rqk.shape[0]
    n = L // BLK
    d_in, q_dec, k_dec, c_dec = rtabs

    def body(do_ref, q_ref, k_ref, v_ref, g_ref, y_ref, st_ref, din_ref, qd_ref, kd_ref, cd_ref,
             dq_ref, dk_ref, dv_ref, dg_ref, ds_scr):
        @pl.when(pl.program_id(1) == 0)
        def _():
            ds_scr[...] = jnp.zeros_like(ds_scr)

        q = q_ref[...]
        k = k_ref[...]
        zq = jnp.zeros_like(q)
        dq_acc = jnp.zeros((BLK, BLK), f32)
        dk_acc = jnp.zeros((BLK, BLK), f32)
        for x in range(2):
            hm = _head_mask(x)
            sl = slice(x * 128, (x + 1) * 128)
            qm = jnp.where(hm, q, zq)
            km = jnp.where(hm, k, zq)
            v = v_ref[:, sl]
            yv = y_ref[:, sl]
            g = g_ref[:, sl]
            do = do_ref[:, sl]
            mu = jnp.mean(yv, -1, keepdims=True)
            yc = yv - mu
            rstd = lax.rsqrt(jnp.mean(jnp.square(yc), -1, keepdims=True) + LN_EPS)
            yn = yc * rstd
            sg = jax.nn.sigmoid(g)
            dg_ref[:, sl] = do * yn * sg * (1.0 + g * (1.0 - sg))
            dyn = do * g * sg
            dy = rstd * (dyn - jnp.mean(dyn, -1, keepdims=True) - yn * jnp.mean(dyn * yn, -1, keepdims=True))
            s_in = st_ref[0, 0, x]
            ds_out = ds_scr[x]
            kd = km.astype(f32) * kd_ref[x]
            a = _dot(qm, km, NT) * din_ref[x]
            da = _dot(dy, v, NT) * din_ref[x]
            dyq = dy * qd_ref[x]
            dq_acc += _dot(da, km) + _dot(dyq, s_in, NT)
            dk_acc += _dot(da, qm, TN) + _dot(v, ds_out, NT) * kd_ref[x]
            dv_ref[:, sl] = _dot(a, dy, TN) + _dot(kd, ds_out)
            ds_scr[x] = ds_out * cd_ref[x, 0:1, :] + _dot(qm, dyq, TN)
        dq_ref[...] = dq_acc
        dk_ref[...] = dk_acc

    rev = lambda w, cb: pl.BlockSpec((BLK, w), lambda p, i: (n - 1 - i, cb + p))
    tab = pl.BlockSpec((2, BLK, BLK), lambda p, i: (p, 0, 0))
    return pl.pallas_call(
        body, name="ret_bwd", grid=(2, n),
        in_specs=[rev(256, 4), rev(128, 0), rev(128, 2), rev(256, 0), rev(256, C_RG // 256), rev(256, 0),
                  pl.BlockSpec((1, 1, 2, BLK, BLK), lambda p, i: (p, n - 1 - i, 0, 0, 0)),
                  tab, tab, tab, pl.BlockSpec((2, 8, BLK), lambda p, i: (p, 0, 0))],
        out_specs=(rev(128, 0), rev(128, 0), rev(256, 0), rev(256, 0)),
        out_shape=(jax.ShapeDtypeStruct((L, 256), f32), jax.ShapeDtypeStruct((L, 256), f32),
                   jax.ShapeDtypeStruct((L, 512), f32), jax.ShapeDtypeStruct((L, 512), f32)),
        scratch_shapes=[pltpu.VMEM((2, BLK, BLK), f32)],
        compiler_params=_cp(("parallel", "arbitrary")))(dmixed, rqk, rqk, rv, proj, y, states, d_in, q_dec, k_dec, c_dec)


def _loss_head(h, target):
    L = h.shape[0]
    n = L // BLK

    def body(h_ref, t_ref, dy_ref, l_ref):
        i = pl.program_id(0)

        @pl.when(i == 0)
        def _():
            dy_ref[...] = jnp.zeros_like(dy_ref)
            l_ref[...] = jnp.zeros_like(l_ref)

        @pl.when(i > 0)
        def _():
            err = h_ref[...] - t_ref[...]
            dy_ref[...] = err * (1.0 / D)
            sq = jnp.sum(jnp.sum(jnp.square(err), 1, keepdims=True), 0, keepdims=True)
            l_ref[...] += (0.5 / D) * sq

    return pl.pallas_call(
        body, name="loss_head", grid=(n,),
        in_specs=[pl.BlockSpec((BLK, D), lambda i: (i, 0)),
                  pl.BlockSpec((BLK, D), lambda i: (jnp.maximum(i - 1, 0), 0))],
        out_specs=(pl.BlockSpec((BLK, D), lambda i: (i, 0)), pl.BlockSpec((8, 128), lambda i: (0, 0))),
        out_shape=(jax.ShapeDtypeStruct((L, D), f32), jax.ShapeDtypeStruct((8, 128), f32)),
        compiler_params=_cp(("arbitrary",)))(h, target)


def _adam_math(w, g, m, v):
    m = ADAM_B1 * m + (1.0 - ADAM_B1) * g
    v = ADAM_B2 * v + (1.0 - ADAM_B2) * jnp.square(g)
    m_hat = m / (1.0 - ADAM_B1 ** ADAM_STEP)
    v_hat = v / (1.0 - ADAM_B2 ** ADAM_STEP)
    delta = -ADAM_LR * (m_hat / (jnp.sqrt(v_hat) + ADAM_EPS) + ADAM_WD * w)
    return delta, m, v


def _adamw(parts, w, m, v, name):
    R = w.shape[0]
    tr = _pick(R, (96, 64, 48, 32, 16, 8))
    row = pl.BlockSpec((tr, D), lambda i: (i, 0))

    def body(p_ref, w_ref, m_ref, v_ref, g_ref, d_ref, nm_ref, nv_ref):
        g = p_ref[0].astype(f32)
        for k in range(1, N_DEV):
            g = g + p_ref[k].astype(f32)
        d, nm, nv = _adam_math(w_ref[...], g, m_ref[...], v_ref[...])
        g_ref[...] = g
        d_ref[...] = d
        nm_ref[...] = nm
        nv_ref[...] = nv

    o = jax.ShapeDtypeStruct((R, D), f32)
    return pl.pallas_call(
        body, name=name, grid=(R // tr,),
        in_specs=[pl.BlockSpec((N_DEV, tr, D), lambda i: (0, i, 0)), row, row, row],
        out_specs=(row, row, row, row), out_shape=(o, o, o, o),
        compiler_params=_cp(("parallel",)))(parts, w, m, v)


def _exchange(x, all_to_all, name):
    shape = x.shape[1:] if all_to_all else x.shape
    n_peer = N_DEV - 1

    def body(x_ref, o_ref, send_sems, recv_sems, local_sem):
        mx, my, mc = lax.axis_index("x"), lax.axis_index("y"), lax.axis_index("c")
        me = 4 * mx + 2 * my + mc

        def peer(k):
            px = (1 - mx) if k & 4 else mx
            py = (1 - my) if k & 2 else my
            pc = (1 - mc) if k & 1 else mc
            return (px, py, pc), 4 * px + 2 * py + pc

        def copy(k):
            dev, idx = peer(k)
            src = x_ref.at[idx] if all_to_all else x_ref
            return pltpu.make_async_remote_copy(
                src_ref=src, dst_ref=o_ref.at[me], send_sem=send_sems.at[k - 1], recv_sem=recv_sems.at[k - 1],
                device_id=dev, device_id_type=pl.DeviceIdType.MESH)

        mine = pltpu.make_async_copy(x_ref.at[me] if all_to_all else x_ref, o_ref.at[me], local_sem)
        mine.start()
        copies = [copy(k) for k in range(1, N_DEV)]
        for cp in copies:
            cp.start()
        for k in range(1, N_DEV):
            _, idx = peer(k)
            pltpu.make_async_remote_copy(
                src_ref=o_ref.at[idx], dst_ref=o_ref.at[idx], send_sem=send_sems.at[k - 1],
                recv_sem=recv_sems.at[k - 1], device_id=peer(k)[0], device_id_type=pl.DeviceIdType.MESH).wait_recv()
        for cp in copies:
            cp.wait_send()
        mine.wait()

    hbm = pl.BlockSpec(memory_space=pltpu.HBM)
    return pl.pallas_call(
        body, name=name, in_specs=[hbm], out_specs=hbm,
        out_shape=jax.ShapeDtypeStruct((N_DEV,) + tuple(shape), x.dtype),
        scratch_shapes=[pltpu.SemaphoreType.DMA((n_peer,)), pltpu.SemaphoreType.DMA((n_peer,)),
                        pltpu.SemaphoreType.DMA])(x)


_BIG = ("w_in", "w_uq", "w_ukv", "w_out", "w_ff1", "w_ff2")


def _pack_shards(ws):
    return jnp.concatenate([w.reshape(-1, D) for w in ws], axis=0)


def _big_rows(depth):
    sizes = [depth * 1024 * 468, depth * 384 * 96, depth * 256 * 128, depth * 192 * 1024, depth * 1024 * 512,
             depth * 512 * 1024]
    return [s // D for s in sizes]


def _unpack_full(gathered, depth):
    rows = _big_rows(depth)
    offs = np.cumsum([0] + rows)
    part = lambda k: gathered[:, offs[k]:offs[k + 1]]
    w_in = part(0).reshape(8, depth, 1024, 468).transpose(1, 2, 0, 3).reshape(depth, 1024, 3744)
    z32 = jnp.zeros((depth, 1024, 32), w_in.dtype)
    kr = w_in[:, :, 2176:2208]
    w_in = jnp.concatenate([w_in[:, :, 0:1536], w_in[:, :, 1920:2176], w_in[:, :, 2208:3744], w_in[:, :, 1536:1920],
                            kr, z32, kr, z32], axis=2)
    w_uq = part(1).reshape(8, depth, 384, 96).transpose(1, 2, 0, 3).reshape(depth, 384, 8, 96)
    rope = jnp.concatenate([w_uq[..., 64:96], jnp.zeros((depth, 384, 8, 32), w_uq.dtype)], axis=-1)
    w_uq = jnp.concatenate([w_uq[..., 0:64].reshape(depth, 384, 512), rope.reshape(depth, 384, 512)], axis=2)
    w_ukv = part(2).reshape(8, depth, 256, 128).transpose(1, 2, 0, 3).reshape(depth, 256, 8, 128)
    w_ukv = jnp.concatenate([w_ukv[..., 0:64].reshape(depth, 256, 512), w_ukv[..., 64:128].reshape(depth, 256, 512)],
                            axis=2)
    w_out = part(3).reshape(8, depth, 192, 1024).transpose(1, 0, 2, 3).reshape(depth, 1536, 1024)
    w_ff1 = part(4).reshape(8, depth, 1024, 512).transpose(1, 2, 0, 3).reshape(depth, 1024, 4096)
    w_ff2 = part(5).reshape(8, depth, 512, 1024).transpose(1, 0, 2, 3).reshape(depth, 4096, 1024)
    return w_in, w_uq, w_ukv, w_out, w_ff1, w_ff2


def _pack_grads(g_in, g_uq, g_ukv, g_out, g_ff1, g_ff2):
    depth = g_in.shape[0]
    kr = g_in[:, :, C_KR:C_KR + 32] + g_in[:, :, C_KR + 64:C_KR + 96]
    g_in = jnp.concatenate([g_in[:, :, 0:1536], g_in[:, :, C_CQ:C_CQ + 384], g_in[:, :, C_CKV:C_CKV + 256], kr,
                            g_in[:, :, C_RQ:C_CQ]], axis=2)
    g_in = g_in.reshape(depth, 1024, 8, 468).transpose(2, 0, 1, 3)
    g_uq = jnp.concatenate([g_uq[:, :, 0:512].reshape(depth, 384, 8, 64),
                            g_uq[:, :, 512:1024].reshape(depth, 384, 8, 64)[..., 0:32]], axis=-1)
    g_uq = g_uq.reshape(depth, 384, 8, 96).transpose(2, 0, 1, 3)
    g_ukv = jnp.concatenate([g_ukv[:, :, 0:512].reshape(depth, 256, 8, 64),
                             g_ukv[:, :, 512:1024].reshape(depth, 256, 8, 64)], axis=-1)
    g_ukv = g_ukv.transpose(2, 0, 1, 3)
    g_out = g_out.reshape(depth, 8, 192, 1024).transpose(1, 0, 2, 3)
    g_ff1 = g_ff1.reshape(depth, 1024, 8, 512).transpose(2, 0, 1, 3)
    g_ff2 = g_ff2.reshape(depth, 8, 512, 1024).transpose(1, 0, 2, 3)
    return jnp.concatenate([g.reshape(8, -1, D) for g in (g_in, g_uq, g_ukv, g_out, g_ff1, g_ff2)], axis=1)


def _rope_tables(L):
    pos = (jnp.arange(L) - N_PAD).astype(f32)

    def cs(half):
        inv = ROPE_THETA ** (-jnp.arange(half, dtype=f32) / half)
        ang = pos[:, None] * inv[None, :]
        return jnp.cos(ang), jnp.sin(ang)

    c, s = cs(32)
    rc, rs = jnp.tile(c, (1, 8)), jnp.tile(s, (1, 8))
    c, s = cs(16)
    z = jnp.zeros((L, 32), f32)
    mc, ms = jnp.concatenate([c, c, z, c, c, z], 1), jnp.concatenate([s, s, z, s, s, z], 1)
    return rc, rs, mc, ms


def _layer_fwd(h, wl, gq, gkv, g1, b1, g2, b2, tabs, rtabs):
    w_in, w_uq, w_ukv, w_out, w_ff1, w_ff2 = wl
    proj = _mm(h, w_in, "nn", "mm_in")
    sb, cqn, ckvn, rqk, rv, kr2 = _prep_fwd(proj, gq, gkv, tabs)
    q = _mm(cqn, w_uq, "nn", "mm_uq")
    kv = _mm(ckvn, w_ukv, "nn", "mm_ukv", out_dtype=bf16)
    out_a = _sb_fwd(sb)
    out_b, lse = _mla_fwd(q, kv, kr2, tabs[2], tabs[3])
    y, out_c, states = _ret_fwd(rqk, rv, proj, rtabs)
    mixed = jnp.concatenate([out_a, out_b, out_c], axis=1)
    mix = _mm(mixed, w_out, "nn", "mm_out")
    h1 = _ln_fwd(h, mix, g1, b1, DN_ALPHA, "ln_fwd")
    u, a = _mm(h1, w_ff1, "nn", "mm_ff1", epi="relu2")
    ff = _mm(a, w_ff2, "nn", "mm_ff2")
    h2 = _ln_fwd(h1, ff, g2, b2, DN_ALPHA, "ln_fwd")
    saved = (h, proj, sb, cqn, ckvn, rqk, rv, kr2, q, kv, out_a, out_b, lse, y, states, mixed, mix, h1, u, a, ff)
    return h2, saved


def _layer_bwd(dh2, saved, wl, gq, gkv, g1, g2, tabs, rtabs):
    w_in, w_uq, w_ukv, w_out, w_ff1, w_ff2 = wl
    h, proj, sb, cqn, ckvn, rqk, rv, kr2, q, kv, out_a, out_b, lse, y, states, mixed, mix, h1, u, a, ff = saved
    dz2, dg2, db2 = _ln_bwd(dh2, h1, ff, g2, DN_ALPHA, "ln_bwd")
    du = _mm(dz2, w_ff2, "nt", "mm_dff2", epi="mul_relu", extra=u)
    gw_ff2 = _mm(a, dz2, "tn", "mm_gff2")
    dh1 = _mm(du, w_ff1, "nt", "mm_dff1", epi="add", extra=dz2, alpha=DN_ALPHA)
    gw_ff1 = _mm(h1, du, "tn", "mm_gff1")
    dz1, dg1, db1 = _ln_bwd(dh1, h, mix, g1, DN_ALPHA, "ln_bwd")
    dmixed = _mm(dz1, w_out, "nt", "mm_dout")
    gw_out = _mm(mixed, dz1, "tn", "mm_gout")
    dsq, dsk, dsv = _sb_bwd(dmixed, sb, out_a)
    dqn, dqr, dkn, dv, dkr_p = _mla_bwd(dmixed, q, kv, kr2, out_b, lse, tabs[2], tabs[3])
    dq = jnp.concatenate([dqn, dqr], axis=1)
    dkv = jnp.concatenate([dkn, dv], axis=1)
    dcqn = _mm(dq, w_uq, "nt", "mm_duq")
    gw_uq = _mm(cqn, dq, "tn", "mm_guq")
    dckvn = _mm(dkv, w_ukv, "nt", "mm_dukv")
    gw_ukv = _mm(ckvn, dkv, "tn", "mm_gukv")
    drq_r, drk_r, drv, drg = _ret_bwd(dmixed, rqk, rv, proj, y, states, rtabs)
    dkr_r = dkr_p[0] + dkr_p[1] + dkr_p[2] + dkr_p[3]
    dcq, dckv, drq, drk, dkr2, dgq, dgkv = _prep_bwd(proj, gq, gkv, tabs, dcqn, dckvn, drq_r, drk_r, dkr_r)
    dproj = jnp.concatenate([dsq, dsk, dsv, dckv, drq, drk, drv, drg, dcq, dkr2], axis=1)
    dh = _mm(dproj, w_in, "nt", "mm_din", epi="add", extra=dz1, alpha=DN_ALPHA)
    gw_in = _mm(h, dproj, "tn", "mm_gin")
    return dh, (gw_in, gw_uq, gw_ukv, gw_out, gw_ff1, gw_ff2), (dgq, dgkv, dg1, db1, dg2, db2)


def kernel(x, meta_tokens, ln_emb_g, ln_emb_b, w_in, mla_q_norm, mla_kv_norm, w_uq, w_ukv, w_out, ln1_g, ln1_b, w_ff1, w_ff2, ln2_g, ln2_b, loss_target, m_meta_tokens, m_ln_emb_g, m_ln_emb_b, m_w_in, m_mla_q_norm, m_mla_kv_norm, m_w_uq, m_w_ukv, m_w_out, m_ln1_g, m_ln1_b, m_w_ff1, m_w_ff2, m_ln2_g, m_ln2_b, v_meta_tokens, v_ln_emb_g, v_ln_emb_b, v_w_in, v_mla_q_norm, v_mla_kv_norm, v_w_uq, v_w_ukv, v_w_out, v_ln1_g, v_ln1_b, v_w_ff1, v_w_ff2, v_ln2_g, v_ln2_b):
    depth = w_in.shape[0]
    S = x.shape[1]
    L = S + BLK
    me = 4 * lax.axis_index("x") + 2 * lax.axis_index("y") + lax.axis_index("c")
    big = (w_in, w_uq, w_ukv, w_out, w_ff1, w_ff2)
    big_m = (m_w_in, m_w_uq, m_w_ukv, m_w_out, m_w_ff1, m_w_ff2)
    big_v = (v_w_in, v_w_uq, v_w_ukv, v_w_out, v_w_ff1, v_w_ff2)
    small = (ln_emb_g, ln_emb_b, mla_q_norm, mla_kv_norm, ln1_g, ln1_b, ln2_g, ln2_b)
    small_m = (m_ln_emb_g, m_ln_emb_b, m_mla_q_norm, m_mla_kv_norm, m_ln1_g, m_ln1_b, m_ln2_g, m_ln2_b)
    small_v = (v_ln_emb_g, v_ln_emb_b, v_mla_q_norm, v_mla_kv_norm, v_ln1_g, v_ln1_b, v_ln2_g, v_ln2_b)

    w_shard = _pack_shards(big)
    gathered = _exchange(w_shard.astype(bf16), False, "gather_weights")
    full = _unpack_full(gathered, depth)
    meta_all = _exchange(meta_tokens, False, "gather_meta")
    meta_full = meta_all.transpose(1, 0, 2).reshape(N_META, D)

    tabs = _rope_tables(L)
    rtabs = _ret_tables()

    hcat = jnp.concatenate([jnp.zeros((N_PAD, D), f32), meta_full, x[0]], axis=0)
    h = _ln_fwd(hcat, None, ln_emb_g, ln_emb_b, 1.0, "ln_emb_fwd")
    saved = []
    for l in range(depth):
        wl = tuple(w[l] for w in full)
        h, sv = _layer_fwd(h, wl, mla_q_norm[l], mla_kv_norm[l], ln1_g[l], ln1_b[l], ln2_g[l], ln2_b[l], tabs, rtabs)
        saved.append(sv)

    dh, loss_part = _loss_head(h, loss_target[0])
    gbig, gsmall = [None] * depth, [None] * depth
    for l in reversed(range(depth)):
        wl = tuple(w[l] for w in full)
        dh, gbig[l], gsmall[l] = _layer_bwd(dh, saved[l], wl, mla_q_norm[l], mla_kv_norm[l], ln1_g[l], ln2_g[l],
                                            tabs, rtabs)
    dz0, dg_emb, db_emb = _ln_bwd(dh, hcat, None, ln_emb_g, 1.0, "ln_emb_bwd")
    grad_x = dz0[BLK:][None]
    dmeta = dz0[N_PAD:BLK]

    gfull = [jnp.stack([gbig[l][k] for l in range(depth)]) for k in range(6)]
    parts = _exchange(_pack_grads(*gfull).astype(bf16), True, "scatter_grads")
    g_sh, d_sh, m_sh, v_sh = _adamw(parts, w_shard, _pack_shards(big_m), _pack_shards(big_v), "adamw_big")

    st = lambda k: jnp.stack([gsmall[l][k] for l in range(depth)])
    g_small = (dg_emb, db_emb, st(0), st(1), st(2), st(3), st(4), st(5))
    n_small = sum(int(np.prod(a.shape)) for a in small)
    flat = jnp.concatenate([a.reshape(-1) for a in g_small] + [dmeta.reshape(-1), loss_part[0, 0:1]])
    rows = -(-(flat.shape[0]) // (8 * D)) * 8
    pad = rows * D - flat.shape[0]
    flat = jnp.concatenate([flat, jnp.zeros((pad,), f32)]).reshape(rows, D)
    parts_s = _exchange(flat, False, "gather_small")

    def pack_small(arrs, meta_shard):
        col = jnp.zeros((N_META, D), f32)
        col = lax.dynamic_update_slice(col, meta_shard, (0, me * 128))
        fl = jnp.concatenate([a.reshape(-1) for a in arrs] + [col.reshape(-1), jnp.zeros((1 + pad,), f32)])
        return fl.reshape(rows, D)

    g_s, d_s, m_s, v_s = _adamw(parts_s, pack_small(small, meta_tokens), pack_small(small_m, m_meta_tokens),
                                pack_small(small_v, v_meta_tokens), "adamw_small")
    loss = g_s.reshape(-1)[n_small + N_META * D]

    def unpack_big(flat_rows):
        outs, off = [], 0
        for w, r in zip(big, _big_rows(depth)):
            outs.append(flat_rows[off:off + r].reshape(w.shape))
            off += r
        return outs

    def unpack_small(flat_rows):
        fl = flat_rows.reshape(-1)
        outs, off = [], 0
        for a in small:
            n = int(np.prod(a.shape))
            outs.append(fl[off:off + n].reshape(a.shape))
            off += n
        meta = lax.dynamic_slice(fl[off:off + N_META * D].reshape(N_META, D), (0, me * 128), (N_META, 128))
        return meta, outs

    def assemble(big_rows_arr, small_rows_arr):
        b = unpack_big(big_rows_arr)
        meta, s = unpack_small(small_rows_arr)
        return [meta, s[0], s[1], b[0], s[2], s[3], b[1], b[2], b[3], s[4], s[5], b[4], b[5], s[6], s[7]]

    return (loss, grad_x, *assemble(g_sh, g_s), *assemble(d_sh, d_s), *assemble(m_sh, m_s), *assemble(v_sh, v_s))
```

```python
import functools
import math

import numpy as np
import jax
import jax.numpy as jnp
from jax import lax
from jax.experimental import pallas as pl
from jax.experimental.pallas import tpu as pltpu

f32 = jnp.float32
bf16 = jnp.bfloat16
_MXU = jnp.bfloat16

BLK = 128
N_META = 16
N_PAD = 112
D = 1024
N_DEV = 8
LN_EPS = 1e-5
DEPTH = 4
DN_ALPHA = (2 * DEPTH) ** 0.25
ROPE_THETA = 10000.0
MLA_SCALE = (64 + 32) ** -0.5
SB_SCALE = 0.125
RET_SCALE = 0.125
RET_GAMMA = tuple(1.0 - 2.0 ** (-5 - h) for h in range(4))

ADAM_LR, ADAM_B1, ADAM_B2, ADAM_EPS, ADAM_WD, ADAM_STEP = 0.001, 0.9, 0.999, 1e-08, 0.01, 10

C_SBQ, C_SBK, C_SBV, C_CKV, C_RQ, C_RK, C_RV, C_RG, C_CQ, C_KR, N_INP = (
    0, 512, 1024, 1536, 1792, 2048, 2304, 2816, 3328, 3712, 3840)

VMEM_LIMIT = 56 * 1024 * 1024


def _cp(sem):
    return pltpu.CompilerParams(dimension_semantics=sem, vmem_limit_bytes=VMEM_LIMIT)


def _pick(n, cands):
    for c in cands:
        if n % c == 0:
            return c
    return n


def _dot(a, b, dims=(((1,), (0,)), ((), ()))):
    return lax.dot_general(a.astype(_MXU), b.astype(_MXU), dims, preferred_element_type=f32)


NT = (((1,), (1,)), ((), ()))
TN = (((0,), (0,)), ((), ()))


def _dot3(x, u):
    hi = x.astype(_MXU)
    r1 = x - hi.astype(f32)
    mid = r1.astype(_MXU)
    lo = (r1 - mid.astype(f32)).astype(_MXU)
    return (jnp.dot(hi, u, preferred_element_type=f32) + jnp.dot(mid, u, preferred_element_type=f32)
            + jnp.dot(lo, u, preferred_element_type=f32))


def _rot(x, half):
    lane = lax.broadcasted_iota(jnp.int32, x.shape, 1)
    first = (lane % 64) < half
    return jnp.where(first, -pltpu.roll(x, 128 - half, 1), pltpu.roll(x, half, 1))


def _mm(a, b, mode, name, epi=None, extra=None, alpha=1.0, out_dtype=f32):
    if mode == "nn":
        (M, K), N = a.shape, b.shape[1]
    elif mode == "nt":
        (M, K), N = a.shape, b.shape[0]
    else:
        (K, M), N = a.shape, b.shape[1]
    tm = _pick(M, (512, 384, 256, 128))
    tn = _pick(N, (1024, 768, 512, 384, 256, 128))
    tk = _pick(K, (1024, 768, 512, 384, 256, 128))
    nk = K // tk
    if mode == "nn":
        a_spec = pl.BlockSpec((tm, tk), lambda i, j, k: (i, k))
        b_spec = pl.BlockSpec((tk, tn), lambda i, j, k: (k, j))
        dims = (((1,), (0,)), ((), ()))
    elif mode == "nt":
        a_spec = pl.BlockSpec((tm, tk), lambda i, j, k: (i, k))
        b_spec = pl.BlockSpec((tn, tk), lambda i, j, k: (j, k))
        dims = NT
    else:
        a_spec = pl.BlockSpec((tk, tm), lambda i, j, k: (k, i))
        b_spec = pl.BlockSpec((tk, tn), lambda i, j, k: (k, j))
        dims = TN
    o_spec = pl.BlockSpec((tm, tn), lambda i, j, k: (i, j))
    in_specs, args = [a_spec, b_spec], [a, b]
    if extra is not None:
        in_specs.append(o_spec)
        args.append(extra)
    if epi == "relu2":
        out_shape = (jax.ShapeDtypeStruct((M, N), f32), jax.ShapeDtypeStruct((M, N), bf16))
        out_specs = (o_spec, o_spec)
    else:
        out_shape = jax.ShapeDtypeStruct((M, N), out_dtype)
        out_specs = o_spec

    def body(*refs):
        a_ref, b_ref = refs[0], refs[1]
        acc = refs[-1]
        k = pl.program_id(2)

        @pl.when(k == 0)
        def _():
            acc[...] = jnp.zeros_like(acc)

        acc[...] += _dot(a_ref[...], b_ref[...], dims)

        @pl.when(k == nk - 1)
        def _():
            r = acc[...]
            if epi == "relu2":
                refs[2][...] = r
                refs[3][...] = jnp.square(jnp.maximum(r, 0.0)).astype(bf16)
            elif epi == "mul_relu":
                refs[3][...] = r * (2.0 * jnp.maximum(refs[2][...], 0.0))
            elif epi == "add":
                refs[3][...] = r + alpha * refs[2][...]
            else:
                refs[2][...] = r.astype(out_dtype)

    return pl.pallas_call(
        body, name=name, grid=(M // tm, N // tn, nk), in_specs=in_specs, out_specs=out_specs,
        out_shape=out_shape, scratch_shapes=[pltpu.VMEM((tm, tn), f32)],
        compiler_params=_cp(("parallel", "parallel", "arbitrary")))(*args)


def _ln_fwd(h, m, g, b, alpha, name):
    L = h.shape[0]
    tm = _pick(L, (384, 256, 128))
    row = pl.BlockSpec((tm, D), lambda i: (i, 0))
    vec = pl.BlockSpec((1, D), lambda i: (0, 0))

    def body(*refs):
        if m is None:
            h_ref, g_ref, b_ref, o_ref = refs
            z = h_ref[...]
        else:
            h_ref, m_ref, g_ref, b_ref, o_ref = refs
            z = alpha * h_ref[...] + m_ref[...]
        mu = jnp.mean(z, -1, keepdims=True)
        var = jnp.mean(jnp.square(z - mu), -1, keepdims=True)
        o_ref[...] = (z - mu) * lax.rsqrt(var + LN_EPS) * g_ref[...] + b_ref[...]

    args = [h] + ([] if m is None else [m]) + [g.reshape(1, D), b.reshape(1, D)]
    specs = [row] + ([] if m is None else [row]) + [vec, vec]
    return pl.pallas_call(body, name=name, grid=(L // tm,), in_specs=specs, out_specs=row,
                          out_shape=jax.ShapeDtypeStruct((L, D), f32), compiler_params=_cp(("parallel",)))(*args)


def _ln_bwd(dy, h, m, g, alpha, name):
    L = h.shape[0]
    tm = _pick(L, (384, 256, 128))
    row = pl.BlockSpec((tm, D), lambda i: (i, 0))
    vec = pl.BlockSpec((1, D), lambda i: (0, 0))
    acc = pl.BlockSpec((8, D), lambda i: (0, 0))

    def body(*refs):
        if m is None:
            dy_ref, h_ref, g_ref, dz_ref, dg_ref, db_ref = refs
            z = h_ref[...]
        else:
            dy_ref, h_ref, m_ref, g_ref, dz_ref, dg_ref, db_ref = refs
            z = alpha * h_ref[...] + m_ref[...]

        @pl.when(pl.program_id(0) == 0)
        def _():
            dg_ref[...] = jnp.zeros_like(dg_ref)
            db_ref[...] = jnp.zeros_like(db_ref)

        dyv = dy_ref[...]
        mu = jnp.mean(z, -1, keepdims=True)
        zc = z - mu
        rstd = lax.rsqrt(jnp.mean(jnp.square(zc), -1, keepdims=True) + LN_EPS)
        xh = zc * rstd
        dxh = dyv * g_ref[...]
        dz_ref[...] = rstd * (dxh - jnp.mean(dxh, -1, keepdims=True) - xh * jnp.mean(dxh * xh, -1, keepdims=True))
        dg_ref[0:1, :] += jnp.sum(dyv * xh, 0, keepdims=True)
        db_ref[0:1, :] += jnp.sum(dyv, 0, keepdims=True)

    args = [dy, h] + ([] if m is None else [m]) + [g.reshape(1, D)]
    specs = [row, row] + ([] if m is None else [row]) + [vec]
    dz, dg, db = pl.pallas_call(
        body, name=name, grid=(L // tm,), in_specs=specs, out_specs=(row, acc, acc),
        out_shape=(jax.ShapeDtypeStruct((L, D), f32), jax.ShapeDtypeStruct((8, D), f32),
                   jax.ShapeDtypeStruct((8, D), f32)),
        compiler_params=_cp(("arbitrary",)))(*args)
    return dz, dg[0], db[0]


def _rms(x, g):
    r = lax.rsqrt(jnp.mean(jnp.square(x), -1, keepdims=True) + LN_EPS)
    return x * r * g


def _prep_fwd(proj, gq, gkv, tabs):
    L = proj.shape[0]
    tm = BLK
    rc, rs, mc, ms = tabs

    def body(p_ref, gq_ref, gkv_ref, rc_ref, rs_ref, mc_ref, ms_ref, sb_ref, cq_ref, ckv_ref, rqk_ref, rv_ref, kr_ref):
        i = pl.program_id(0)
        sb_ref[:, 0:512] = (p_ref[:, C_SBQ:C_SBQ + 512] * SB_SCALE).astype(bf16)
        sb_ref[:, 512:1536] = p_ref[:, C_SBK:C_SBK + 1024].astype(bf16)
        cq_ref[...] = _rms(p_ref[:, C_CQ:C_CQ + 384], gq_ref[...]).astype(bf16)
        ckv_ref[...] = _rms(p_ref[:, C_CKV:C_CKV + 256], gkv_ref[...]).astype(bf16)
        valid = (i * tm + lax.broadcasted_iota(jnp.int32, (tm, 128), 0)) >= N_PAD
        for c in range(2):
            sl = slice(c * 128, (c + 1) * 128)
            x = p_ref[:, C_RQ + c * 128:C_RQ + (c + 1) * 128]
            rqk_ref[:, sl] = (x * rc_ref[:, sl] + _rot(x, 32) * rs_ref[:, sl]).astype(bf16)
            x = p_ref[:, C_RK + c * 128:C_RK + (c + 1) * 128]
            kk = (x * rc_ref[:, sl] + _rot(x, 32) * rs_ref[:, sl]) * RET_SCALE
            rqk_ref[:, 256 + c * 128:256 + (c + 1) * 128] = jnp.where(valid, kk, 0.0).astype(bf16)
        rv_ref[...] = p_ref[:, C_RV:C_RV + 512].astype(bf16)
        x = p_ref[:, C_KR:C_KR + 128]
        kr_ref[...] = (x * mc_ref[...] + _rot(x, 16) * ms_ref[...]).astype(bf16)

    def row(w):
        return pl.BlockSpec((tm, w), lambda i: (i, 0))

    def vec(w):
        return pl.BlockSpec((1, w), lambda i: (0, 0))

    widths = (1536, 384, 256, 512, 512, 128)
    return pl.pallas_call(
        body, name="prep_fwd", grid=(L // tm,),
        in_specs=[row(N_INP), vec(384), vec(256), row(256), row(256), row(128), row(128)],
        out_specs=tuple(row(w) for w in widths),
        out_shape=tuple(jax.ShapeDtypeStruct((L, w), bf16) for w in widths),
        compiler_params=_cp(("parallel",)))(proj, gq.reshape(1, 384), gkv.reshape(1, 256), rc, rs, mc, ms)


def _rms_bwd(x, g, dy):
    r = lax.rsqrt(jnp.mean(jnp.square(x), -1, keepdims=True) + LN_EPS)
    u = dy * g
    dx = r * u - x * (r * r * r) * jnp.mean(x * u, -1, keepdims=True)
    return dx, jnp.sum(dy * x * r, 0, keepdims=True)


def _prep_bwd(proj, gq, gkv, tabs, dcqn, dckvn, drq_r, drk_r, dkr_r):
    L = proj.shape[0]
    tm = BLK
    rc, rs, mc, ms = tabs

    def body(p_ref, gq_ref, gkv_ref, rc_ref, rs_ref, mc_ref, ms_ref, dcqn_ref, dckvn_ref, drq_ref, drk_ref,
             dkr_ref, ocq_ref, ockv_ref, orq_ref, ork_ref, okr_ref, dgq_ref, dgkv_ref):
        i = pl.program_id(0)

        @pl.when(i == 0)
        def _():
            dgq_ref[...] = jnp.zeros_like(dgq_ref)
            dgkv_ref[...] = jnp.zeros_like(dgkv_ref)

        dx, dg = _rms_bwd(p_ref[:, C_CQ:C_CQ + 384], gq_ref[...], dcqn_ref[...])
        ocq_ref[...] = dx
        dgq_ref[0:1, :] += dg
        dx, dg = _rms_bwd(p_ref[:, C_CKV:C_CKV + 256], gkv_ref[...], dckvn_ref[...])
        ockv_ref[...] = dx
        dgkv_ref[0:1, :] += dg
        valid = (i * tm + lax.broadcasted_iota(jnp.int32, (tm, 128), 0)) >= N_PAD
        for c in range(2):
            sl = slice(c * 128, (c + 1) * 128)
            dy = drq_ref[:, sl]
            orq_ref[:, sl] = dy * rc_ref[:, sl] - _rot(dy * rs_ref[:, sl], 32)
            dy = jnp.where(valid, drk_ref[:, sl], 0.0) * RET_SCALE
            ork_ref[:, sl] = dy * rc_ref[:, sl] - _rot(dy * rs_ref[:, sl], 32)
        dy = dkr_ref[...]
        okr_ref[...] = dy * mc_ref[...] - _rot(dy * ms_ref[...], 16)

    def row(w):
        return pl.BlockSpec((tm, w), lambda i: (i, 0))

    def vec(w):
        return pl.BlockSpec((1, w), lambda i: (0, 0))

    def acc(w):
        return pl.BlockSpec((8, w), lambda i: (0, 0))

    widths = (384, 256, 256, 256, 128)
    outs = pl.pallas_call(
        body, name="prep_bwd", grid=(L // tm,),
        in_specs=[row(N_INP), vec(384), vec(256), row(256), row(256), row(128), row(128),
                  row(384), row(256), row(256), row(256), row(128)],
        out_specs=tuple(row(w) for w in widths) + (acc(384), acc(256)),
        out_shape=tuple(jax.ShapeDtypeStruct((L, w), f32) for w in widths)
        + (jax.ShapeDtypeStruct((8, 384), f32), jax.ShapeDtypeStruct((8, 256), f32)),
        compiler_params=_cp(("arbitrary",)))(
            proj, gq.reshape(1, 384), gkv.reshape(1, 256), rc, rs, mc, ms, dcqn, dckvn, drq_r, drk_r, dkr_r)
    return outs[:5] + (outs[5][0], outs[6][0])


def _qrows(L):
    return _pick(L, (384, 256, 128))


def _tri_ones(strict):
    r = lax.broadcasted_iota(jnp.int32, (BLK, 2 * BLK), 0)
    c = lax.broadcasted_iota(jnp.int32, (BLK, 2 * BLK), 1)
    tri = (r > c) if strict else (r >= c)
    return jnp.where(tri | (c >= BLK), 1.0, 0.0).astype(_MXU)


def _dot2(x, u):
    hi = x.astype(_MXU)
    lo = (x - hi.astype(f32)).astype(_MXU)
    return jnp.dot(hi, u, preferred_element_type=f32) + jnp.dot(lo, u, preferred_element_type=f32)


def _sb_tile(qx, k, mask, c_ref, x, u_gt):
    z = _dot(qx, k, NT)
    t = jnp.log1p(jnp.exp(-jnp.abs(z)))
    lb = jnp.minimum(z, 0.0) - t
    lk = jnp.where(mask, -jnp.maximum(z, 0.0) - t, 0.0)
    el = _dot2(lk, u_gt)
    c = c_ref[x]
    w = jnp.where(mask, jnp.exp(lb + el[:, 0:BLK] + c), 0.0)
    c_ref[x] = c + el[:, BLK:2 * BLK]
    return w, lb


def _sb_mask(i, j, qb):
    row = i * qb + lax.broadcasted_iota(jnp.int32, (qb, BLK), 0)
    col = j * BLK + lax.broadcasted_iota(jnp.int32, (qb, BLK), 1)
    return (col < row) & (col >= N_PAD)


def _sb_fwd(sb):
    L = sb.shape[0]
    qb = _qrows(L)
    nq, r = L // qb, qb // BLK

    def body(q_ref, k_ref, v_ref, o_ref, acc_ref, c_ref):
        i = pl.program_id(1)
        m_a = lax.broadcasted_iota(jnp.int32, (1, BLK), 1) < 64
        q = q_ref[...]
        zq = jnp.zeros_like(q)
        qs = (jnp.where(m_a, q, zq), jnp.where(m_a, zq, q))
        u_gt = _tri_ones(True)
        acc_ref[...] = jnp.zeros_like(acc_ref)
        c_ref[...] = jnp.zeros_like(c_ref)

        def step(jj, carry):
            j = r * (i + 1) - 1 - jj
            off = pl.multiple_of(j * BLK, BLK)
            k = k_ref[pl.ds(off, BLK), :]
            v = v_ref[pl.ds(off, BLK), :]
            mask = _sb_mask(i, j, qb)
            for x in range(2):
                w, _ = _sb_tile(qs[x], k, mask, c_ref, x, u_gt)
                acc_ref[x] += _dot(w, v)
            return carry

        lax.fori_loop(0, r * (i + 1), step, 0)
        o_ref[...] = jnp.where(m_a, acc_ref[0], acc_ref[1])

    return pl.pallas_call(
        body, name="sb_fwd", grid=(4, nq),
        in_specs=[pl.BlockSpec((qb, 128), lambda p, i: (i, p)),
                  pl.BlockSpec((L, 128), lambda p, i: (0, 4 + p)),
                  pl.BlockSpec((L, 128), lambda p, i: (0, 8 + p))],
        out_specs=pl.BlockSpec((qb, 128), lambda p, i: (i, p)),
        out_shape=jax.ShapeDtypeStruct((L, 512), f32),
        scratch_shapes=[pltpu.VMEM((2, qb, BLK), f32), pltpu.VMEM((2, qb, BLK), f32)],
        compiler_params=_cp(("parallel", "arbitrary")))(sb, sb, sb)


def _sb_bwd(dmixed, sb, out_a):
    L = sb.shape[0]
    qb = _qrows(L)
    nq, r = L // qb, qb // BLK

    def body(do_ref, o_ref, q_ref, k_ref, v_ref, dq_ref, dk_ref, dv_ref, dqa_ref, c_ref, cg_ref, ds_ref):
        i = pl.program_id(1)

        @pl.when(i == 0)
        def _():
            dk_ref[...] = jnp.zeros_like(dk_ref)
            dv_ref[...] = jnp.zeros_like(dv_ref)

        m_a = lax.broadcasted_iota(jnp.int32, (1, BLK), 1) < 64
        q = q_ref[...]
        zq = jnp.zeros_like(q)
        qs = (jnp.where(m_a, q, zq), jnp.where(m_a, zq, q))
        do = do_ref[...]
        zd = jnp.zeros_like(do)
        dos = (jnp.where(m_a, do, zd).astype(_MXU), jnp.where(m_a, zd, do).astype(_MXU))
        prod = do.astype(_MXU).astype(f32) * o_ref[...]
        ds_ref[0] = jnp.broadcast_to(jnp.sum(jnp.where(m_a, prod, 0.0), 1, keepdims=True), (qb, BLK))
        ds_ref[1] = jnp.broadcast_to(jnp.sum(jnp.where(m_a, 0.0, prod), 1, keepdims=True), (qb, BLK))
        u_gt = _tri_ones(True)
        u_ge = _tri_ones(False)
        dqa_ref[...] = jnp.zeros_like(dqa_ref)
        c_ref[...] = jnp.zeros_like(c_ref)
        cg_ref[...] = jnp.zeros_like(cg_ref)

        def step(jj, carry):
            j = r * (i + 1) - 1 - jj
            off = pl.multiple_of(j * BLK, BLK)
            k = k_ref[pl.ds(off, BLK), :]
            v = v_ref[pl.ds(off, BLK), :]
            mask = _sb_mask(i, j, qb)
            dk_t = jnp.zeros((BLK, BLK), f32)
            dv_t = jnp.zeros((BLK, BLK), f32)
            for x in range(2):
                w, lb = _sb_tile(qs[x], k, mask, c_ref, x, u_gt)
                wb = w.astype(_MXU)
                gr = wb.astype(f32) * _dot(dos[x], v, NT)
                eg = _dot3(gr, u_ge)
                cg = cg_ref[x]
                suffix = eg[:, 0:BLK] + cg
                cg_ref[x] = cg + eg[:, BLK:2 * BLK]
                dz = jnp.where(mask, gr - jnp.exp(lb) * (gr + ds_ref[x] - suffix), 0.0).astype(_MXU)
                dqa_ref[x] += _dot(dz, k)
                dk_t += _dot(dz, qs[x], TN)
                dv_t += _dot(wb, dos[x], TN)
            dk_ref[pl.ds(off, BLK), :] += dk_t
            dv_ref[pl.ds(off, BLK), :] += dv_t
            return carry

        lax.fori_loop(0, r * (i + 1), step, 0)
        dq_ref[...] = jnp.where(m_a, dqa_ref[0], dqa_ref[1]) * SB_SCALE

    blk = pl.BlockSpec((qb, 128), lambda p, i: (i, p))
    scr = pltpu.VMEM((2, qb, BLK), f32)
    return pl.pallas_call(
        body, name="sb_bwd", grid=(4, nq),
        in_specs=[blk, blk, blk,
                  pl.BlockSpec((L, 128), lambda p, i: (0, 4 + p)),
                  pl.BlockSpec((L, 128), lambda p, i: (0, 8 + p))],
        out_specs=(blk, pl.BlockSpec((L, 128), lambda p, i: (0, p)), pl.BlockSpec((L, 128), lambda p, i: (0, p))),
        out_shape=tuple(jax.ShapeDtypeStruct((L, 512), f32) for _ in range(3)),
        scratch_shapes=[scr, scr, scr, scr],
        compiler_params=_cp(("parallel", "arbitrary")))(dmixed, out_a, sb, sb, sb)


def _mla_mask(i, j, qb):
    row = i * qb + lax.broadcasted_iota(jnp.int32, (qb, BLK), 0)
    col = j * BLK + lax.broadcasted_iota(jnp.int32, (qb, BLK), 1)
    return (col <= row) & ((col >= N_PAD) | (col == row))


def _pair_mask2():
    return (lax.broadcasted_iota(jnp.int32, (1, 256), 1) % 128) < 64


def _mla_q2(qn_ref, qr_ref, mc_ref, ms_ref):
    qr = qr_ref[...]
    qr = qr * mc_ref[...] + _rot(qr, 16) * ms_ref[...]
    q2 = jnp.concatenate([qn_ref[...], qr], axis=1)
    m2 = _pair_mask2()
    z2 = jnp.zeros_like(q2)
    return (jnp.where(m2, q2, z2).astype(_MXU), jnp.where(m2, z2, q2).astype(_MXU))


def _mla_fwd(q, kv, kr2, mc, ms):
    L = q.shape[0]
    qb = _qrows(L)
    nq, r = L // qb, qb // BLK

    def body(qn_ref, qr_ref, mc_ref, ms_ref, kn_ref, v_ref, kr_ref, o_ref, lse_ref, acc_ref, m_ref):
        i = pl.program_id(1)
        m_a = lax.broadcasted_iota(jnp.int32, (1, BLK), 1) < 64
        qs = _mla_q2(qn_ref, qr_ref, mc_ref, ms_ref)
        acc_ref[...] = jnp.zeros_like(acc_ref)
        m_ref[...] = jnp.full(m_ref.shape, -1e30, f32)
        ones = jnp.ones((BLK, BLK), _MXU)

        def step(jj, carry):
            j = r * (i + 1) - 1 - jj
            off = pl.multiple_of(j * BLK, BLK)
            k2 = jnp.concatenate([kn_ref[pl.ds(off, BLK), :], kr_ref[pl.ds(off, BLK), :]], axis=1)
            v1 = jnp.concatenate([v_ref[pl.ds(off, BLK), :], ones], axis=1)
            mask = _mla_mask(i, j, qb)
            for x in range(2):
                s = jnp.where(mask, _dot(qs[x], k2, NT) * MLA_SCALE, -1e30)
                m_old = m_ref[x]
                m_new = jnp.maximum(m_old, jnp.max(s, 1, keepdims=True))
                a = jnp.exp(m_old - m_new)
                p = jnp.where(mask, jnp.exp(s - m_new), 0.0)
                acc_ref[x] = jnp.concatenate([a, a], axis=1) * acc_ref[x] + _dot(p, v1)
                m_ref[x] = m_new
            return carry

        lax.fori_loop(0, r * (i + 1), step, 0)
        o_ref[...] = jnp.where(m_a, acc_ref[0, :, 0:BLK] / acc_ref[0, :, BLK:2 * BLK],
                               acc_ref[1, :, 0:BLK] / acc_ref[1, :, BLK:2 * BLK])
        for x in range(2):
            lse_ref[0, x] = m_ref[x] + jnp.log(acc_ref[x, :, BLK:2 * BLK])

    blk = lambda cb: pl.BlockSpec((qb, 128), lambda p, i: (i, cb + p))
    full = lambda cb: pl.BlockSpec((L, 128), lambda p, i: (0, cb + p))
    tab = pl.BlockSpec((qb, 128), lambda p, i: (i, 0))
    return pl.pallas_call(
        body, name="mla_fwd", grid=(4, nq),
        in_specs=[blk(0), blk(4), tab, tab, full(0), full(4), pl.BlockSpec((L, 128), lambda p, i: (0, 0))],
        out_specs=(blk(0), pl.BlockSpec((1, 2, qb, 128), lambda p, i: (p, 0, i, 0))),
        out_shape=(jax.ShapeDtypeStruct((L, 512), f32), jax.ShapeDtypeStruct((4, 2, L, 128), f32)),
        scratch_shapes=[pltpu.VMEM((2, qb, 2 * BLK), f32), pltpu.VMEM((2, qb, BLK), f32)],
        compiler_params=_cp(("parallel", "arbitrary")))(q, q, mc, ms, kv, kv, kr2)


def _mla_bwd(dmixed, q, kv, kr2, out_b, lse, mc, ms):
    L = q.shape[0]
    qb = _qrows(L)
    nq, r = L // qb, qb // BLK

    def body(do_ref, o_ref, lse_ref, qn_ref, qr_ref, mc_ref, ms_ref, kn_ref, v_ref, kr_ref,
             dqn_ref, dqr_ref, dkn_ref, dv_ref, dkr_ref, dqa_ref, ds_ref):
        i = pl.program_id(1)

        @pl.when(i == 0)
        def _():
            dkn_ref[...] = jnp.zeros_like(dkn_ref)
            dv_ref[...] = jnp.zeros_like(dv_ref)
            dkr_ref[...] = jnp.zeros_like(dkr_ref)

        m_a = lax.broadcasted_iota(jnp.int32, (1, BLK), 1) < 64
        qs = _mla_q2(qn_ref, qr_ref, mc_ref, ms_ref)
        do = do_ref[...]
        zd = jnp.zeros_like(do)
        dos = (jnp.where(m_a, do, zd).astype(_MXU), jnp.where(m_a, zd, do).astype(_MXU))
        prod = do * o_ref[...]
        ds_ref[0] = jnp.broadcast_to(jnp.sum(jnp.where(m_a, prod, 0.0), 1, keepdims=True), (qb, BLK))
        ds_ref[1] = jnp.broadcast_to(jnp.sum(jnp.where(m_a, 0.0, prod), 1, keepdims=True), (qb, BLK))
        dqa_ref[...] = jnp.zeros_like(dqa_ref)

        def step(jj, carry):
            j = r * (i + 1) - 1 - jj
            off = pl.multiple_of(j * BLK, BLK)
            k2 = jnp.concatenate([kn_ref[pl.ds(off, BLK), :], kr_ref[pl.ds(off, BLK), :]], axis=1)
            v = v_ref[pl.ds(off, BLK), :]
            mask = _mla_mask(i, j, qb)
            dk_t = jnp.zeros((BLK, 256), f32)
            dv_t = jnp.zeros((BLK, BLK), f32)
            for x in range(2):
                s = _dot(qs[x], k2, NT) * MLA_SCALE
                p = jnp.where(mask, jnp.exp(jnp.where(mask, s, 0.0) - lse_ref[0, x]), 0.0)
                ds = (p * (_dot(dos[x], v, NT) - ds_ref[x]) * MLA_SCALE).astype(_MXU)
                dqa_ref[x] += _dot(ds, k2)
                dk_t += _dot(ds, qs[x], TN)
                dv_t += _dot(p, dos[x], TN)
            dkn_ref[pl.ds(off, BLK), :] += dk_t[:, 0:128]
            dkr_ref[0, pl.ds(off, BLK), :] += dk_t[:, 128:256]
            dv_ref[pl.ds(off, BLK), :] += dv_t
            return carry

        lax.fori_loop(0, r * (i + 1), step, 0)
        dq2 = jnp.where(_pair_mask2(), dqa_ref[0], dqa_ref[1])
        dqn_ref[...] = dq2[:, 0:128]
        dy = dq2[:, 128:256]
        dqr_ref[...] = dy * mc_ref[...] - _rot(dy * ms_ref[...], 16)

    blk = lambda cb: pl.BlockSpec((qb, 128), lambda p, i: (i, cb + p))
    full = lambda cb: pl.BlockSpec((L, 128), lambda p, i: (0, cb + p))
    tab = pl.BlockSpec((qb, 128), lambda p, i: (i, 0))
    o512 = jax.ShapeDtypeStruct((L, 512), f32)
    return pl.pallas_call(
        body, name="mla_bwd", grid=(4, nq),
        in_specs=[blk(4), blk(0), pl.BlockSpec((1, 2, qb, 128), lambda p, i: (p, 0, i, 0)), blk(0), blk(4), tab, tab,
                  full(0), full(4), pl.BlockSpec((L, 128), lambda p, i: (0, 0))],
        out_specs=(blk(0), blk(0), full(0), full(0), pl.BlockSpec((1, L, 128), lambda p, i: (p, 0, 0))),
        out_shape=(o512, o512, o512, o512, jax.ShapeDtypeStruct((4, L, 128), f32)),
        scratch_shapes=[pltpu.VMEM((2, qb, 2 * BLK), f32), pltpu.VMEM((2, qb, BLK), f32)],
        compiler_params=_cp(("parallel", "arbitrary")))(dmixed, out_b, lse, q, q, mc, ms, kv, kv, kr2)


def _ret_tables():
    log_g = jnp.log(jnp.array(RET_GAMMA, f32))
    idx = jnp.arange(BLK, dtype=f32)
    diff = idx[:, None] - idx[None, :]
    d_in = jnp.where(diff[None] >= 0, jnp.exp(jnp.maximum(diff, 0.0)[None] * log_g[:, None, None]), 0.0)
    q_dec = jnp.exp((idx[None, :] + 1.0) * log_g[:, None])
    k_dec = jnp.exp((BLK - 1.0 - idx[None, :]) * log_g[:, None])
    c_dec = jnp.exp(BLK * log_g)
    bc = lambda a: jnp.broadcast_to(a[:, :, None], (4, BLK, BLK))
    return d_in, bc(q_dec), bc(k_dec), jnp.broadcast_to(c_dec[:, None, None], (4, 8, BLK))


def _head_mask(x):
    lane = lax.broadcasted_iota(jnp.int32, (1, BLK), 1)
    return (lane < 64) if x == 0 else (lane >= 64)


def _ret_fwd(rqk, rv, proj, rtabs):
    L = rqk.shape[0]
    n = L // BLK
    d_in, q_dec, k_dec, c_dec = rtabs

    def body(q_ref, k_ref, v_ref, g_ref, din_ref, qd_ref, kd_ref, cd_ref, y_ref, o_ref, st_ref, s_scr):
        @pl.when(pl.program_id(1) == 0)
        def _():
            s_scr[...] = jnp.zeros_like(s_scr)

        q = q_ref[...]
        k = k_ref[...]
        zq = jnp.zeros_like(q)
        for x in range(2):
            hm = _head_mask(x)
            sl = slice(x * 128, (x + 1) * 128)
            qm = jnp.where(hm, q, zq)
            km = jnp.where(hm, k, zq)
            v = v_ref[:, sl]
            s_in = s_scr[x]
            st_ref[0, 0, x] = s_in
            inner = _dot(qm, km, NT) * din_ref[x]
            y = _dot(inner, v) + _dot(qm, s_in) * qd_ref[x]
            s_scr[x] = s_in * cd_ref[x, 0:1, :] + _dot(km.astype(f32) * kd_ref[x], v, TN)
            y_ref[:, sl] = y
            mu = jnp.mean(y, -1, keepdims=True)
            yc = y - mu
            yn = yc * lax.rsqrt(jnp.mean(jnp.square(yc), -1, keepdims=True) + LN_EPS)
            g = g_ref[:, sl]
            o_ref[:, sl] = g * jax.nn.sigmoid(g) * yn

    tab = pl.BlockSpec((2, BLK, BLK), lambda p, i: (p, 0, 0))
    return pl.pallas_call(
        body, name="ret_fwd", grid=(2, n),
        in_specs=[pl.BlockSpec((BLK, 128), lambda p, i: (i, p)), pl.BlockSpec((BLK, 128), lambda p, i: (i, 2 + p)),
                  pl.BlockSpec((BLK, 256), lambda p, i: (i, p)),
                  pl.BlockSpec((BLK, 256), lambda p, i: (i, C_RG // 256 + p)),
                  tab, tab, tab, pl.BlockSpec((2, 8, BLK), lambda p, i: (p, 0, 0))],
        out_specs=(pl.BlockSpec((BLK, 256), lambda p, i: (i, p)), pl.BlockSpec((BLK, 256), lambda p, i: (i, p)),
                   pl.BlockSpec((1, 1, 2, BLK, BLK), lambda p, i: (p, i, 0, 0, 0))),
        out_shape=(jax.ShapeDtypeStruct((L, 512), f32), jax.ShapeDtypeStruct((L, 512), f32),
                   jax.ShapeDtypeStruct((2, n, 2, BLK, BLK), f32)),
        scratch_shapes=[pltpu.VMEM((2, BLK, BLK), f32)],
        compiler_params=_cp(("parallel", "arbitrary")))(rqk, rqk, rv, proj, d_in, q_dec, k_dec, c_dec)


def _ret_bwd(dmixed, rqk, rv, proj, y, states, rtabs):
    L = rqk.shape[0]
    n = L // BLK
    d_in, q_dec, k_dec, c_dec = rtabs

    def body(do_ref, q_ref, k_ref, v_ref, g_ref, y_ref, st_ref, din_ref, qd_ref, kd_ref, cd_ref,
             dq_ref, dk_ref, dv_ref, dg_ref, ds_scr):
        @pl.when(pl.program_id(1) == 0)
        def _():
            ds_scr[...] = jnp.zeros_like(ds_scr)

        q = q_ref[...]
        k = k_ref[...]
        zq = jnp.zeros_like(q)
        dq_acc = jnp.zeros((BLK, BLK), f32)
        dk_acc = jnp.zeros((BLK, BLK), f32)
        for x in range(2):
            hm = _head_mask(x)
            sl = slice(x * 128, (x + 1) * 128)
            qm = jnp.where(hm, q, zq)
            km = jnp.where(hm, k, zq)
            v = v_ref[:, sl]
            yv = y_ref[:, sl]
            g = g_ref[:, sl]
            do = do_ref[:, sl]
            mu = jnp.mean(yv, -1, keepdims=True)
            yc = yv - mu
            rstd = lax.rsqrt(jnp.mean(jnp.square(yc), -1, keepdims=True) + LN_EPS)
            yn = yc * rstd
            sg = jax.nn.sigmoid(g)
            dg_ref[:, sl] = do * yn * sg * (1.0 + g * (1.0 - sg))
            dyn = do * g * sg
            dy = rstd * (dyn - jnp.mean(dyn, -1, keepdims=True) - yn * jnp.mean(dyn * yn, -1, keepdims=True))
            s_in = st_ref[0, 0, x]
            ds_out = ds_scr[x]
            kd = km.astype(f32) * kd_ref[x]
            a = _dot(qm, km, NT) * din_ref[x]
            da = _dot(dy, v, NT) * din_ref[x]
            dyq = dy * qd_ref[x]
            dq_acc += _dot(da, km) + _dot(dyq, s_in, NT)
            dk_acc += _dot(da, qm, TN) + _dot(v, ds_out, NT) * kd_ref[x]
            dv_ref[:, sl] = _dot(a, dy, TN) + _dot(kd, ds_out)
            ds_scr[x] = ds_out * cd_ref[x, 0:1, :] + _dot(qm, dyq, TN)
        dq_ref[...] = dq_acc
        dk_ref[...] = dk_acc

    rev = lambda w, cb: pl.BlockSpec((BLK, w), lambda p, i: (n - 1 - i, cb + p))
    tab = pl.BlockSpec((2, BLK, BLK), lambda p, i: (p, 0, 0))
    return pl.pallas_call(
        body, name="ret_bwd", grid=(2, n),
        in_specs=[rev(256, 4), rev(128, 0), rev(128, 2), rev(256, 0), rev(256, C_RG // 256), rev(256, 0),
                  pl.BlockSpec((1, 1, 2, BLK, BLK), lambda p, i: (p, n - 1 - i, 0, 0, 0)),
                  tab, tab, tab, pl.BlockSpec((2, 8, BLK), lambda p, i: (p, 0, 0))],
        out_specs=(rev(128, 0), rev(128, 0), rev(256, 0), rev(256, 0)),
        out_shape=(jax.ShapeDtypeStruct((L, 256), f32), jax.ShapeDtypeStruct((L, 256), f32),
                   jax.ShapeDtypeStruct((L, 512), f32), jax.ShapeDtypeStruct((L, 512), f32)),
        scratch_shapes=[pltpu.VMEM((2, BLK, BLK), f32)],
        compiler_params=_cp(("parallel", "arbitrary")))(dmixed, rqk, rqk, rv, proj, y, states, d_in, q_dec, k_dec, c_dec)


def _loss_head(h, target):
    L = h.shape[0]
    n = L // BLK

    def body(h_ref, t_ref, dy_ref, l_ref):
        i = pl.program_id(0)

        @pl.when(i == 0)
        def _():
            dy_ref[...] = jnp.zeros_like(dy_ref)
            l_ref[...] = jnp.zeros_like(l_ref)

        @pl.when(i > 0)
        def _():
            err = h_ref[...] - t_ref[...]
            dy_ref[...] = err * (1.0 / D)
            sq = jnp.sum(jnp.sum(jnp.square(err), 1, keepdims=True), 0, keepdims=True)
            l_ref[...] += (0.5 / D) * sq

    return pl.pallas_call(
        body, name="loss_head", grid=(n,),
        in_specs=[pl.BlockSpec((BLK, D), lambda i: (i, 0)),
                  pl.BlockSpec((BLK, D), lambda i: (jnp.maximum(i - 1, 0), 0))],
        out_specs=(pl.BlockSpec((BLK, D), lambda i: (i, 0)), pl.BlockSpec((8, 128), lambda i: (0, 0))),
        out_shape=(jax.ShapeDtypeStruct((L, D), f32), jax.ShapeDtypeStruct((8, 128), f32)),
        compiler_params=_cp(("arbitrary",)))(h, target)


def _adam_math(w, g, m, v):
    m = ADAM_B1 * m + (1.0 - ADAM_B1) * g
    v = ADAM_B2 * v + (1.0 - ADAM_B2) * jnp.square(g)
    m_hat = m / (1.0 - ADAM_B1 ** ADAM_STEP)
    v_hat = v / (1.0 - ADAM_B2 ** ADAM_STEP)
    delta = -ADAM_LR * (m_hat / (jnp.sqrt(v_hat) + ADAM_EPS) + ADAM_WD * w)
    return delta, m, v


def _adamw(parts, w, m, v, name):
    R = w.shape[0]
    tr = _pick(R, (96, 64, 48, 32, 16, 8))
    row = pl.BlockSpec((tr, D), lambda i: (i, 0))

    def body(p_ref, w_ref, m_ref, v_ref, g_ref, d_ref, nm_ref, nv_ref):
        g = p_ref[0].astype(f32)
        for k in range(1, N_DEV):
            g = g + p_ref[k].astype(f32)
        d, nm, nv = _adam_math(w_ref[...], g, m_ref[...], v_ref[...])
        g_ref[...] = g
        d_ref[...] = d
        nm_ref[...] = nm
        nv_ref[...] = nv

    o = jax.ShapeDtypeStruct((R, D), f32)
    return pl.pallas_call(
        body, name=name, grid=(R // tr,),
        in_specs=[pl.BlockSpec((N_DEV, tr, D), lambda i: (0, i, 0)), row, row, row],
        out_specs=(row, row, row, row), out_shape=(o, o, o, o),
        compiler_params=_cp(("parallel",)))(parts, w, m, v)


def _exchange(x, all_to_all, name):
    shape = x.shape[1:] if all_to_all else x.shape
    n_peer = N_DEV - 1

    def body(x_ref, o_ref, send_sems, recv_sems, local_sem):
        mx, my, mc = lax.axis_index("x"), lax.axis_index("y"), lax.axis_index("c")
        me = 4 * mx + 2 * my + mc

        def peer(k):
            px = (1 - mx) if k & 4 else mx
            py = (1 - my) if k & 2 else my
            pc = (1 - mc) if k & 1 else mc
            return (px, py, pc), 4 * px + 2 * py + pc

        def copy(k):
            dev, idx = peer(k)
            src = x_ref.at[idx] if all_to_all else x_ref
            return pltpu.make_async_remote_copy(
                src_ref=src, dst_ref=o_ref.at[me], send_sem=send_sems.at[k - 1], recv_sem=recv_sems.at[k - 1],
                device_id=dev, device_id_type=pl.DeviceIdType.MESH)

        mine = pltpu.make_async_copy(x_ref.at[me] if all_to_all else x_ref, o_ref.at[me], local_sem)
        mine.start()
        copies = [copy(k) for k in range(1, N_DEV)]
        for cp in copies:
            cp.start()
        for k in range(1, N_DEV):
            _, idx = peer(k)
            pltpu.make_async_remote_copy(
                src_ref=o_ref.at[idx], dst_ref=o_ref.at[idx], send_sem=send_sems.at[k - 1],
                recv_sem=recv_sems.at[k - 1], device_id=peer(k)[0], device_id_type=pl.DeviceIdType.MESH).wait_recv()
        for cp in copies:
            cp.wait_send()
        mine.wait()

    hbm = pl.BlockSpec(memory_space=pltpu.HBM)
    return pl.pallas_call(
        body, name=name, in_specs=[hbm], out_specs=hbm,
        out_shape=jax.ShapeDtypeStruct((N_DEV,) + tuple(shape), x.dtype),
        scratch_shapes=[pltpu.SemaphoreType.DMA((n_peer,)), pltpu.SemaphoreType.DMA((n_peer,)),
                        pltpu.SemaphoreType.DMA])(x)


_BIG = ("w_in", "w_uq", "w_ukv", "w_out", "w_ff1", "w_ff2")


def _pack_shards(ws):
    return jnp.concatenate([w.reshape(-1, D) for w in ws], axis=0)


def _big_rows(depth):
    sizes = [depth * 1024 * 468, depth * 384 * 96, depth * 256 * 128, depth * 192 * 1024, depth * 1024 * 512,
             depth * 512 * 1024]
    return [s // D for s in sizes]


def _unpack_full(gathered, depth):
    rows = _big_rows(depth)
    offs = np.cumsum([0] + rows)
    part = lambda k: gathered[:, offs[k]:offs[k + 1]]
    w_in = part(0).reshape(8, depth, 1024, 468).transpose(1, 2, 0, 3).reshape(depth, 1024, 3744)
    z32 = jnp.zeros((depth, 1024, 32), w_in.dtype)
    kr = w_in[:, :, 2176:2208]
    w_in = jnp.concatenate([w_in[:, :, 0:1536], w_in[:, :, 1920:2176], w_in[:, :, 2208:3744], w_in[:, :, 1536:1920],
                            kr, z32, kr, z32], axis=2)
    w_uq = part(1).reshape(8, depth, 384, 96).transpose(1, 2, 0, 3).reshape(depth, 384, 8, 96)
    rope = jnp.concatenate([w_uq[..., 64:96], jnp.zeros((depth, 384, 8, 32), w_uq.dtype)], axis=-1)
    w_uq = jnp.concatenate([w_uq[..., 0:64].reshape(depth, 384, 512), rope.reshape(depth, 384, 512)], axis=2)
    w_ukv = part(2).reshape(8, depth, 256, 128).transpose(1, 2, 0, 3).reshape(depth, 256, 8, 128)
    w_ukv = jnp.concatenate([w_ukv[..., 0:64].reshape(depth, 256, 512), w_ukv[..., 64:128].reshape(depth, 256, 512)],
                            axis=2)
    w_out = part(3).reshape(8, depth, 192, 1024).transpose(1, 0, 2, 3).reshape(depth, 1536, 1024)
    w_ff1 = part(4).reshape(8, depth, 1024, 512).transpose(1, 2, 0, 3).reshape(depth, 1024, 4096)
    w_ff2 = part(5).reshape(8, depth, 512, 1024).transpose(1, 0, 2, 3).reshape(depth, 4096, 1024)
    return w_in, w_uq, w_ukv, w_out, w_ff1, w_ff2


def _pack_grads(g_in, g_uq, g_ukv, g_out, g_ff1, g_ff2):
    depth = g_in.shape[0]
    kr = g_in[:, :, C_KR:C_KR + 32] + g_in[:, :, C_KR + 64:C_KR + 96]
    g_in = jnp.concatenate([g_in[:, :, 0:1536], g_in[:, :, C_CQ:C_CQ + 384], g_in[:, :, C_CKV:C_CKV + 256], kr,
                            g_in[:, :, C_RQ:C_CQ]], axis=2)
    g_in = g_in.reshape(depth, 1024, 8, 468).transpose(2, 0, 1, 3)
    g_uq = jnp.concatenate([g_uq[:, :, 0:512].reshape(depth, 384, 8, 64),
                            g_uq[:, :, 512:1024].reshape(depth, 384, 8, 64)[..., 0:32]], axis=-1)
    g_uq = g_uq.reshape(depth, 384, 8, 96).transpose(2, 0, 1, 3)
    g_ukv = jnp.concatenate([g_ukv[:, :, 0:512].reshape(depth, 256, 8, 64),
                             g_ukv[:, :, 512:1024].reshape(depth, 256, 8, 64)], axis=-1)
    g_ukv = g_ukv.transpose(2, 0, 1, 3)
    g_out = g_out.reshape(depth, 8, 192, 1024).transpose(1, 0, 2, 3)
    g_ff1 = g_ff1.reshape(depth, 1024, 8, 512).transpose(2, 0, 1, 3)
    g_ff2 = g_ff2.reshape(depth, 8, 512, 1024).transpose(1, 0, 2, 3)
    return jnp.concatenate([g.reshape(8, -1, D) for g in (g_in, g_uq, g_ukv, g_out, g_ff1, g_ff2)], axis=1)


def _rope_tables(L):
    pos = (jnp.arange(L) - N_PAD).astype(f32)

    def cs(half):
        inv = ROPE_THETA ** (-jnp.arange(half, dtype=f32) / half)
        ang = pos[:, None] * inv[None, :]
        return jnp.cos(ang), jnp.sin(ang)

    c, s = cs(32)
    rc, rs = jnp.tile(c, (1, 8)), jnp.tile(s, (1, 8))
    c, s = cs(16)
    z = jnp.zeros((L, 32), f32)
    mc, ms = jnp.concatenate([c, c, z, c, c, z], 1), jnp.concatenate([s, s, z, s, s, z], 1)
    return rc, rs, mc, ms


def _layer_fwd(h, wl, gq, gkv, g1, b1, g2, b2, tabs, rtabs):
    w_in, w_uq, w_ukv, w_out, w_ff1, w_ff2 = wl
    proj = _mm(h, w_in, "nn", "mm_in")
    sb, cqn, ckvn, rqk, rv, kr2 = _prep_fwd(proj, gq, gkv, tabs)
    q = _mm(cqn, w_uq, "nn", "mm_uq")
    kv = _mm(ckvn, w_ukv, "nn", "mm_ukv", out_dtype=bf16)
    out_a = _sb_fwd(sb)
    out_b, lse = _mla_fwd(q, kv, kr2, tabs[2], tabs[3])
    y, out_c, states = _ret_fwd(rqk, rv, proj, rtabs)
    mixed = jnp.concatenate([out_a, out_b, out_c], axis=1)
    mix = _mm(mixed, w_out, "nn", "mm_out")
    h1 = _ln_fwd(h, mix, g1, b1, DN_ALPHA, "ln_fwd")
    u, a = _mm(h1, w_ff1, "nn", "mm_ff1", epi="relu2")
    ff = _mm(a, w_ff2, "nn", "mm_ff2")
    h2 = _ln_fwd(h1, ff, g2, b2, DN_ALPHA, "ln_fwd")
    saved = (h, proj, sb, cqn, ckvn, rqk, rv, kr2, q, kv, out_a, out_b, lse, y, states, mixed, mix, h1, u, a, ff)
    return h2, saved


def _layer_bwd(dh2, saved, wl, gq, gkv, g1, g2, tabs, rtabs):
    w_in, w_uq, w_ukv, w_out, w_ff1, w_ff2 = wl
    h, proj, sb, cqn, ckvn, rqk, rv, kr2, q, kv, out_a, out_b, lse, y, states, mixed, mix, h1, u, a, ff = saved
    dz2, dg2, db2 = _ln_bwd(dh2, h1, ff, g2, DN_ALPHA, "ln_bwd")
    du = _mm(dz2, w_ff2, "nt", "mm_dff2", epi="mul_relu", extra=u)
    gw_ff2 = _mm(a, dz2, "tn", "mm_gff2")
    dh1 = _mm(du, w_ff1, "nt", "mm_dff1", epi="add", extra=dz2, alpha=DN_ALPHA)
    gw_ff1 = _mm(h1, du, "tn", "mm_gff1")
    dz1, dg1, db1 = _ln_bwd(dh1, h, mix, g1, DN_ALPHA, "ln_bwd")
    dmixed = _mm(dz1, w_out, "nt", "mm_dout")
    gw_out = _mm(mixed, dz1, "tn", "mm_gout")
    dsq, dsk, dsv = _sb_bwd(dmixed, sb, out_a)
    dqn, dqr, dkn, dv, dkr_p = _mla_bwd(dmixed, q, kv, kr2, out_b, lse, tabs[2], tabs[3])
    dq = jnp.concatenate([dqn, dqr], axis=1)
    dkv = jnp.concatenate([dkn, dv], axis=1)
    dcqn = _mm(dq, w_uq, "nt", "mm_duq")
    gw_uq = _mm(cqn, dq, "tn", "mm_guq")
    dckvn = _mm(dkv, w_ukv, "nt", "mm_dukv")
    gw_ukv = _mm(ckvn, dkv, "tn", "mm_gukv")
    drq_r, drk_r, drv, drg = _ret_bwd(dmixed, rqk, rv, proj, y, states, rtabs)
    dkr_r = dkr_p[0] + dkr_p[1] + dkr_p[2] + dkr_p[3]
    dcq, dckv, drq, drk, dkr2, dgq, dgkv = _prep_bwd(proj, gq, gkv, tabs, dcqn, dckvn, drq_r, drk_r, dkr_r)
    dproj = jnp.concatenate([dsq, dsk, dsv, dckv, drq, drk, drv, drg, dcq, dkr2], axis=1)
    dh = _mm(dproj, w_in, "nt", "mm_din", epi="add", extra=dz1, alpha=DN_ALPHA)
    gw_in = _mm(h, dproj, "tn", "mm_gin")
    return dh, (gw_in, gw_uq, gw_ukv, gw_out, gw_ff1, gw_ff2), (dgq, dgkv, dg1, db1, dg2, db2)


def kernel(x, meta_tokens, ln_emb_g, ln_emb_b, w_in, mla_q_norm, mla_kv_norm, w_uq, w_ukv, w_out, ln1_g, ln1_b, w_ff1, w_ff2, ln2_g, ln2_b, loss_target, m_meta_tokens, m_ln_emb_g, m_ln_emb_b, m_w_in, m_mla_q_norm, m_mla_kv_norm, m_w_uq, m_w_ukv, m_w_out, m_ln1_g, m_ln1_b, m_w_ff1, m_w_ff2, m_ln2_g, m_ln2_b, v_meta_tokens, v_ln_emb_g, v_ln_emb_b, v_w_in, v_mla_q_norm, v_mla_kv_norm, v_w_uq, v_w_ukv, v_w_out, v_ln1_g, v_ln1_b, v_w_ff1, v_w_ff2, v_ln2_g, v_ln2_b):
    depth = w_in.shape[0]
    S = x.shape[1]
    L = S + BLK
    me = 4 * lax.axis_index("x") + 2 * lax.axis_index("y") + lax.axis_index("c")
    big = (w_in, w_uq, w_ukv, w_out, w_ff1, w_ff2)
    big_m = (m_w_in, m_w_uq, m_w_ukv, m_w_out, m_w_ff1, m_w_ff2)
    big_v = (v_w_in, v_w_uq, v_w_ukv, v_w_out, v_w_ff1, v_w_ff2)
    small = (ln_emb_g, ln_emb_b, mla_q_norm, mla_kv_norm, ln1_g, ln1_b, ln2_g, ln2_b)
    small_m = (m_ln_emb_g, m_ln_emb_b, m_mla_q_norm, m_mla_kv_norm, m_ln1_g, m_ln1_b, m_ln2_g, m_ln2_b)
    small_v = (v_ln_emb_g, v_ln_emb_b, v_mla_q_norm, v_mla_kv_norm, v_ln1_g, v_ln1_b, v_ln2_g, v_ln2_b)

    w_shard = _pack_shards(big)
    gathered = _exchange(w_shard.astype(bf16), False, "gather_weights")
    full = _unpack_full(gathered, depth)
    meta_all = _exchange(meta_tokens, False, "gather_meta")
    meta_full = meta_all.transpose(1, 0, 2).reshape(N_META, D)

    tabs = _rope_tables(L)
    rtabs = _ret_tables()

    hcat = jnp.concatenate([jnp.zeros((N_PAD, D), f32), meta_full, x[0]], axis=0)
    h = _ln_fwd(hcat, None, ln_emb_g, ln_emb_b, 1.0, "ln_emb_fwd")
    saved = []
    for l in range(depth):
        wl = tuple(w[l] for w in full)
        h, sv = _layer_fwd(h, wl, mla_q_norm[l], mla_kv_norm[l], ln1_g[l], ln1_b[l], ln2_g[l], ln2_b[l], tabs, rtabs)
        saved.append(sv)

    dh, loss_part = _loss_head(h, loss_target[0])
    gbig, gsmall = [None] * depth, [None] * depth
    for l in reversed(range(depth)):
        wl = tuple(w[l] for w in full)
        dh, gbig[l], gsmall[l] = _layer_bwd(dh, saved[l], wl, mla_q_norm[l], mla_kv_norm[l], ln1_g[l], ln2_g[l],
                                            tabs, rtabs)
    dz0, dg_emb, db_emb = _ln_bwd(dh, hcat, None, ln_emb_g, 1.0, "ln_emb_bwd")
    grad_x = dz0[BLK:][None]
    dmeta = dz0[N_PAD:BLK]

    gfull = [jnp.stack([gbig[l][k] for l in range(depth)]) for k in range(6)]
    parts = _exchange(_pack_grads(*gfull).astype(bf16), True, "scatter_grads")
    g_sh, d_sh, m_sh, v_sh = _adamw(parts, w_shard, _pack_shards(big_m), _pack_shards(big_v), "adamw_big")

    st = lambda k: jnp.stack([gsmall[l][k] for l in range(depth)])
    g_small = (dg_emb, db_emb, st(0), st(1), st(2), st(3), st(4), st(5))
    n_small = sum(int(np.prod(a.shape)) for a in small)
    flat = jnp.concatenate([a.reshape(-1) for a in g_small] + [dmeta.reshape(-1), loss_part[0, 0:1]])
    rows = -(-(flat.shape[0]) // (8 * D)) * 8
    pad = rows * D - flat.shape[0]
    flat = jnp.concatenate([flat, jnp.zeros((pad,), f32)]).reshape(rows, D)
    parts_s = _exchange(flat, False, "gather_small")

    def pack_small(arrs, meta_shard):
        col = jnp.zeros((N_META, D), f32)
        col = lax.dynamic_update_slice(col, meta_shard, (0, me * 128))
        fl = jnp.concatenate([a.reshape(-1) for a in arrs] + [col.reshape(-1), jnp.zeros((1 + pad,), f32)])
        return fl.reshape(rows, D)

    g_s, d_s, m_s, v_s = _adamw(parts_s, pack_small(small, meta_tokens), pack_small(small_m, m_meta_tokens),
                                pack_small(small_v, v_meta_tokens), "adamw_small")
    loss = g_s.reshape(-1)[n_small + N_META * D]

    def unpack_big(flat_rows):
        outs, off = [], 0
        for w, r in zip(big, _big_rows(depth)):
            outs.append(flat_rows[off:off + r].reshape(w.shape))
            off += r
        return outs

    def unpack_small(flat_rows):
        fl = flat_rows.reshape(-1)
        outs, off = [], 0
        for a in small:
            n = int(np.prod(a.shape))
            outs.append(fl[off:off + n].reshape(a.shape))
            off += n
        meta = lax.dynamic_slice(fl[off:off + N_META * D].reshape(N_META, D), (0, me * 128), (N_META, 128))
        return meta, outs

    def assemble(big_rows_arr, small_rows_arr):
        b = unpack_big(big_rows_arr)
        meta, s = unpack_small(small_rows_arr)
        return [meta, s[0], s[1], b[0], s[2], s[3], b[1], b[2], b[3], s[4], s[5], b[4], b[5], s[6], s[7]]

    return (loss, grad_x, *assemble(g_sh, g_s), *assemble(d_sh, d_s), *assemble(m_sh, m_s), *assemble(v_sh, v_s))
```

```python
import functools
import math

import numpy as np
import jax
import jax.numpy as jnp
from jax import lax
from jax.experimental import pallas as pl
from jax.experimental.pallas import tpu as pltpu

f32 = jnp.float32
bf16 = jnp.bfloat16
_MXU = jnp.bfloat16

BLK = 128
N_META = 16
N_PAD = 112
D = 1024
N_DEV = 8
LN_EPS = 1e-5
DEPTH = 4
DN_ALPHA = (2 * DEPTH) ** 0.25
ROPE_THETA = 10000.0
MLA_SCALE = (64 + 32) ** -0.5
SB_SCALE = 0.125
RET_SCALE = 0.125
RET_GAMMA = tuple(1.0 - 2.0 ** (-5 - h) for h in range(4))

ADAM_LR, ADAM_B1, ADAM_B2, ADAM_EPS, ADAM_WD, ADAM_STEP = 0.001, 0.9, 0.999, 1e-08, 0.01, 10

C_SBQ, C_SBK, C_SBV, C_CKV, C_RQ, C_RK, C_RV, C_RG, C_CQ, C_KR, N_INP = (
    0, 512, 1024, 1536, 1792, 2048, 2304, 2816, 3328, 3712, 3840)

VMEM_LIMIT = 56 * 1024 * 1024


def _cp(sem):
    return pltpu.CompilerParams(dimension_semantics=sem, vmem_limit_bytes=VMEM_LIMIT)


def _pick(n, cands):
    for c in cands:
        if n % c == 0:
            return c
    return n


def _dot(a, b, dims=(((1,), (0,)), ((), ()))):
    return lax.dot_general(a.astype(_MXU), b.astype(_MXU), dims, preferred_element_type=f32)


NT = (((1,), (1,)), ((), ()))
TN = (((0,), (0,)), ((), ()))


def _dot3(x, u):
    hi = x.astype(_MXU)
    r1 = x - hi.astype(f32)
    mid = r1.astype(_MXU)
    lo = (r1 - mid.astype(f32)).astype(_MXU)
    return (jnp.dot(hi, u, preferred_element_type=f32) + jnp.dot(mid, u, preferred_element_type=f32)
            + jnp.dot(lo, u, preferred_element_type=f32))


def _rot(x, half):
    lane = lax.broadcasted_iota(jnp.int32, x.shape, 1)
    first = (lane % 64) < half
    return jnp.where(first, -pltpu.roll(x, 128 - half, 1), pltpu.roll(x, half, 1))


def _mm(a, b, mode, name, epi=None, extra=None, alpha=1.0, out_dtype=f32):
    if mode == "nn":
        (M, K), N = a.shape, b.shape[1]
    elif mode == "nt":
        (M, K), N = a.shape, b.shape[0]
    else:
        (K, M), N = a.shape, b.shape[1]
    tm = _pick(M, (1408, 1024, 768, 512, 384, 256, 128))
    tn = _pick(N, ((1920,) if mode == "tn" else ()) + (1024, 768, 512, 384, 256, 128))
    tk = _pick(K, (1024, 768, 512, 384, 256, 128))
    nk = K // tk
    if mode == "nn":
        a_spec = pl.BlockSpec((tm, tk), lambda i, j, k: (i, k))
        b_spec = pl.BlockSpec((tk, tn), lambda i, j, k: (k, j))
        dims = (((1,), (0,)), ((), ()))
    elif mode == "nt":
        a_spec = pl.BlockSpec((tm, tk), lambda i, j, k: (i, k))
        b_spec = pl.BlockSpec((tn, tk), lambda i, j, k: (j, k))
        dims = NT
    else:
        a_spec = pl.BlockSpec((tk, tm), lambda i, j, k: (k, i))
        b_spec = pl.BlockSpec((tk, tn), lambda i, j, k: (k, j))
        dims = TN
    o_spec = pl.BlockSpec((tm, tn), lambda i, j, k: (i, j))
    in_specs, args = [a_spec, b_spec], [a, b]
    if extra is not None:
        in_specs.append(o_spec)
        args.append(extra)
    if epi == "relu2":
        out_shape = (jax.ShapeDtypeStruct((M, N), f32), jax.ShapeDtypeStruct((M, N), bf16))
        out_specs = (o_spec, o_spec)
    else:
        out_shape = jax.ShapeDtypeStruct((M, N), out_dtype)
        out_specs = o_spec

    def body(*refs):
        a_ref, b_ref = refs[0], refs[1]
        acc = refs[-1]
        k = pl.program_id(2)

        @pl.when(k == 0)
        def _():
            acc[...] = jnp.zeros_like(acc)

        acc[...] += _dot(a_ref[...], b_ref[...], dims)

        @pl.when(k == nk - 1)
        def _():
            r = acc[...]
            if epi == "relu2":
                refs[2][...] = r
                refs[3][...] = jnp.square(jnp.maximum(r, 0.0)).astype(bf16)
            elif epi == "mul_relu":
                refs[3][...] = r * (2.0 * jnp.maximum(refs[2][...], 0.0))
            elif epi == "add":
                refs[3][...] = r + alpha * refs[2][...]
            else:
                refs[2][...] = r.astype(out_dtype)

    return pl.pallas_call(
        body, name=name, grid=(M // tm, N // tn, nk), in_specs=in_specs, out_specs=out_specs,
        out_shape=out_shape, scratch_shapes=[pltpu.VMEM((tm, tn), f32)],
        compiler_params=_cp(("parallel", "parallel", "arbitrary")))(*args)


def _ln_fwd(h, m, g, b, alpha, name):
    L = h.shape[0]
    tm = _pick(L, (384, 256, 128))
    row = pl.BlockSpec((tm, D), lambda i: (i, 0))
    vec = pl.BlockSpec((1, D), lambda i: (0, 0))

    def body(*refs):
        if m is None:
            h_ref, g_ref, b_ref, o_ref = refs
            z = h_ref[...]
        else:
            h_ref, m_ref, g_ref, b_ref, o_ref = refs
            z = alpha * h_ref[...] + m_ref[...]
        mu = jnp.mean(z, -1, keepdims=True)
        var = jnp.mean(jnp.square(z - mu), -1, keepdims=True)
        o_ref[...] = (z - mu) * lax.rsqrt(var + LN_EPS) * g_ref[...] + b_ref[...]

    args = [h] + ([] if m is None else [m]) + [g.reshape(1, D), b.reshape(1, D)]
    specs = [row] + ([] if m is None else [row]) + [vec, vec]
    return pl.pallas_call(body, name=name, grid=(L // tm,), in_specs=specs, out_specs=row,
                          out_shape=jax.ShapeDtypeStruct((L, D), f32), compiler_params=_cp(("parallel",)))(*args)


def _ln_bwd(dy, h, m, g, alpha, name):
    L = h.shape[0]
    tm = _pick(L, (384, 256, 128))
    row = pl.BlockSpec((tm, D), lambda i: (i, 0))
    vec = pl.BlockSpec((1, D), lambda i: (0, 0))
    acc = pl.BlockSpec((8, D), lambda i: (0, 0))

    def body(*refs):
        if m is None:
            dy_ref, h_ref, g_ref, dz_ref, dg_ref, db_ref = refs
            z = h_ref[...]
        else:
            dy_ref, h_ref, m_ref, g_ref, dz_ref, dg_ref, db_ref = refs
            z = alpha * h_ref[...] + m_ref[...]

        @pl.when(pl.program_id(0) == 0)
        def _():
            dg_ref[...] = jnp.zeros_like(dg_ref)
            db_ref[...] = jnp.zeros_like(db_ref)

        dyv = dy_ref[...]
        mu = jnp.mean(z, -1, keepdims=True)
        zc = z - mu
        rstd = lax.rsqrt(jnp.mean(jnp.square(zc), -1, keepdims=True) + LN_EPS)
        xh = zc * rstd
        dxh = dyv * g_ref[...]
        dz_ref[...] = rstd * (dxh - jnp.mean(dxh, -1, keepdims=True) - xh * jnp.mean(dxh * xh, -1, keepdims=True))
        dg_ref[0:1, :] += jnp.sum(dyv * xh, 0, keepdims=True)
        db_ref[0:1, :] += jnp.sum(dyv, 0, keepdims=True)

    args = [dy, h] + ([] if m is None else [m]) + [g.reshape(1, D)]
    specs = [row, row] + ([] if m is None else [row]) + [vec]
    dz, dg, db = pl.pallas_call(
        body, name=name, grid=(L // tm,), in_specs=specs, out_specs=(row, acc, acc),
        out_shape=(jax.ShapeDtypeStruct((L, D), f32), jax.ShapeDtypeStruct((8, D), f32),
                   jax.ShapeDtypeStruct((8, D), f32)),
        compiler_params=_cp(("arbitrary",)))(*args)
    return dz, dg[0], db[0]


def _rms(x, g):
    r = lax.rsqrt(jnp.mean(jnp.square(x), -1, keepdims=True) + LN_EPS)
    return x * r * g


def _prep_fwd(proj, gq, gkv, tabs):
    L = proj.shape[0]
    tm = BLK
    rc, rs, mc, ms = tabs

    def body(p_ref, gq_ref, gkv_ref, rc_ref, rs_ref, mc_ref, ms_ref, sb_ref, cq_ref, ckv_ref, rqk_ref, rv_ref, kr_ref):
        i = pl.program_id(0)
        sb_ref[:, 0:512] = (p_ref[:, C_SBQ:C_SBQ + 512] * SB_SCALE).astype(bf16)
        sb_ref[:, 512:1536] = p_ref[:, C_SBK:C_SBK + 1024].astype(bf16)
        cq_ref[...] = _rms(p_ref[:, C_CQ:C_CQ + 384], gq_ref[...]).astype(bf16)
        ckv_ref[...] = _rms(p_ref[:, C_CKV:C_CKV + 256], gkv_ref[...]).astype(bf16)
        valid = (i * tm + lax.broadcasted_iota(jnp.int32, (tm, 128), 0)) >= N_PAD
        for c in range(2):
            sl = slice(c * 128, (c + 1) * 128)
            x = p_ref[:, C_RQ + c * 128:C_RQ + (c + 1) * 128]
            rqk_ref[:, sl] = (x * rc_ref[:, sl] + _rot(x, 32) * rs_ref[:, sl]).astype(bf16)
            x = p_ref[:, C_RK + c * 128:C_RK + (c + 1) * 128]
            kk = (x * rc_ref[:, sl] + _rot(x, 32) * rs_ref[:, sl]) * RET_SCALE
            rqk_ref[:, 256 + c * 128:256 + (c + 1) * 128] = jnp.where(valid, kk, 0.0).astype(bf16)
        rv_ref[...] = p_ref[:, C_RV:C_RV + 512].astype(bf16)
        x = p_ref[:, C_KR:C_KR + 128]
        kr_ref[...] = (x * mc_ref[...] + _rot(x, 16) * ms_ref[...]).astype(bf16)

    def row(w):
        return pl.BlockSpec((tm, w), lambda i: (i, 0))

    def vec(w):
        return pl.BlockSpec((1, w), lambda i: (0, 0))

    widths = (1536, 384, 256, 512, 512, 128)
    return pl.pallas_call(
        body, name="prep_fwd", grid=(L // tm,),
        in_specs=[row(N_INP), vec(384), vec(256), row(256), row(256), row(128), row(128)],
        out_specs=tuple(row(w) for w in widths),
        out_shape=tuple(jax.ShapeDtypeStruct((L, w), bf16) for w in widths),
        compiler_params=_cp(("parallel",)))(proj, gq.reshape(1, 384), gkv.reshape(1, 256), rc, rs, mc, ms)


def _rms_bwd(x, g, dy):
    r = lax.rsqrt(jnp.mean(jnp.square(x), -1, keepdims=True) + LN_EPS)
    u = dy * g
    dx = r * u - x * (r * r * r) * jnp.mean(x * u, -1, keepdims=True)
    return dx, jnp.sum(dy * x * r, 0, keepdims=True)


def _prep_bwd(proj, gq, gkv, tabs, dcqn, dckvn, drq_r, drk_r, dkr_r):
    L = proj.shape[0]
    tm = BLK
    rc, rs, mc, ms = tabs

    def body(p_ref, gq_ref, gkv_ref, rc_ref, rs_ref, mc_ref, ms_ref, dcqn_ref, dckvn_ref, drq_ref, drk_ref,
             dkr_ref, ocq_ref, ockv_ref, orq_ref, ork_ref, okr_ref, dgq_ref, dgkv_ref):
        i = pl.program_id(0)

        @pl.when(i == 0)
        def _():
            dgq_ref[...] = jnp.zeros_like(dgq_ref)
            dgkv_ref[...] = jnp.zeros_like(dgkv_ref)

        dx, dg = _rms_bwd(p_ref[:, C_CQ:C_CQ + 384], gq_ref[...], dcqn_ref[...])
        ocq_ref[...] = dx
        dgq_ref[0:1, :] += dg
        dx, dg = _rms_bwd(p_ref[:, C_CKV:C_CKV + 256], gkv_ref[...], dckvn_ref[...])
        ockv_ref[...] = dx
        dgkv_ref[0:1, :] += dg
        valid = (i * tm + lax.broadcasted_iota(jnp.int32, (tm, 128), 0)) >= N_PAD
        for c in range(2):
            sl = slice(c * 128, (c + 1) * 128)
            dy = drq_ref[:, sl]
            orq_ref[:, sl] = dy * rc_ref[:, sl] - _rot(dy * rs_ref[:, sl], 32)
            dy = jnp.where(valid, drk_ref[:, sl], 0.0) * RET_SCALE
            ork_ref[:, sl] = dy * rc_ref[:, sl] - _rot(dy * rs_ref[:, sl], 32)
        dy = dkr_ref[...]
        okr_ref[...] = dy * mc_ref[...] - _rot(dy * ms_ref[...], 16)

    def row(w):
        return pl.BlockSpec((tm, w), lambda i: (i, 0))

    def vec(w):
        return pl.BlockSpec((1, w), lambda i: (0, 0))

    def acc(w):
        return pl.BlockSpec((8, w), lambda i: (0, 0))

    widths = (384, 256, 256, 256, 128)
    outs = pl.pallas_call(
        body, name="prep_bwd", grid=(L // tm,),
        in_specs=[row(N_INP), vec(384), vec(256), row(256), row(256), row(128), row(128),
                  row(384), row(256), row(256), row(256), row(128)],
        out_specs=tuple(row(w) for w in widths) + (acc(384), acc(256)),
        out_shape=tuple(jax.ShapeDtypeStruct((L, w), f32) for w in widths)
        + (jax.ShapeDtypeStruct((8, 384), f32), jax.ShapeDtypeStruct((8, 256), f32)),
        compiler_params=_cp(("arbitrary",)))(
            proj, gq.reshape(1, 384), gkv.reshape(1, 256), rc, rs, mc, ms, dcqn, dckvn, drq_r, drk_r, dkr_r)
    return outs[:5] + (outs[5][0], outs[6][0])


def _qrows(L):
    return _pick(L, (384, 256, 128))


def _tri_ones(strict):
    r = lax.broadcasted_iota(jnp.int32, (BLK, 2 * BLK), 0)
    c = lax.broadcasted_iota(jnp.int32, (BLK, 2 * BLK), 1)
    tri = (r > c) if strict else (r >= c)
    return jnp.where(tri | (c >= BLK), 1.0, 0.0).astype(_MXU)


def _dot2(x, u):
    hi = x.astype(_MXU)
    lo = (x - hi.astype(f32)).astype(_MXU)
    return jnp.dot(hi, u, preferred_element_type=f32) + jnp.dot(lo, u, preferred_element_type=f32)


def _sb_tile(qx, k, mask, c_ref, x, u_gt):
    z = _dot(qx, k, NT)
    lb = jnp.minimum(z, 0.0) - jnp.log1p(jnp.exp(-jnp.abs(z)))
    lk = lb - z
    if mask is not None:
        lk = jnp.where(mask, lk, 0.0)
    el = _dot2(lk, u_gt)
    c = c_ref[x]
    w = jnp.exp(lb + el[:, 0:BLK] + c)
    if mask is not None:
        w = jnp.where(mask, w, 0.0)
    c_ref[x] = c + el[:, BLK:2 * BLK]
    return w, lb


def _sb_sweep(i, r, tile):
    n_t = r * (i + 1)
    lax.fori_loop(0, r, lambda jj, c: tile(n_t - 1 - jj, True) or c, 0)
    lax.fori_loop(0, jnp.maximum(r * i - 1, 0), lambda jj, c: tile(r * i - 1 - jj, False) or c, 0)

    @pl.when(i > 0)
    def _():
        tile(0, True)


def _sb_mask(i, j, qb):
    row = i * qb + lax.broadcasted_iota(jnp.int32, (qb, BLK), 0)
    col = j * BLK + lax.broadcasted_iota(jnp.int32, (qb, BLK), 1)
    return (col < row) & (col >= N_PAD)


def _sb_fwd(sb):
    L = sb.shape[0]
    qb = _qrows(L)
    nq, r = L // qb, qb // BLK

    def body(q_ref, k_ref, v_ref, o_ref, acc_ref, c_ref):
        i = pl.program_id(1)
        m_a = lax.broadcasted_iota(jnp.int32, (1, BLK), 1) < 64
        q = q_ref[...]
        zq = jnp.zeros_like(q)
        qs = (jnp.where(m_a, q, zq), jnp.where(m_a, zq, q))
        u_gt = _tri_ones(True)
        acc_ref[...] = jnp.zeros_like(acc_ref)
        c_ref[...] = jnp.zeros_like(c_ref)

        def tile(j, masked):
            off = j * BLK if isinstance(j, int) else pl.multiple_of(j * BLK, BLK)
            k = k_ref[pl.ds(off, BLK), :]
            v = v_ref[pl.ds(off, BLK), :]
            mask = _sb_mask(i, j, qb) if masked else None
            for x in range(2):
                w, _ = _sb_tile(qs[x], k, mask, c_ref, x, u_gt)
                acc_ref[x] += _dot(w, v)

        _sb_sweep(i, r, tile)
        o_ref[...] = jnp.where(m_a, acc_ref[0], acc_ref[1])

    return pl.pallas_call(
        body, name="sb_fwd", grid=(4, nq),
        in_specs=[pl.BlockSpec((qb, 128), lambda p, i: (i, p)),
                  pl.BlockSpec((L, 128), lambda p, i: (0, 4 + p)),
                  pl.BlockSpec((L, 128), lambda p, i: (0, 8 + p))],
        out_specs=pl.BlockSpec((qb, 128), lambda p, i: (i, p)),
        out_shape=jax.ShapeDtypeStruct((L, 512), f32),
        scratch_shapes=[pltpu.VMEM((2, qb, BLK), f32), pltpu.VMEM((2, qb, BLK), f32)],
        compiler_params=_cp(("parallel", "arbitrary")))(sb, sb, sb)


def _sb_bwd(dmixed, sb, out_a):
    L = sb.shape[0]
    qb = _qrows(L)
    nq, r = L // qb, qb // BLK

    def body(do_ref, o_ref, q_ref, k_ref, v_ref, dq_ref, dk_ref, dv_ref, dqa_ref, c_ref, cg_ref, ds_ref):
        i = pl.program_id(1)

        @pl.when(i == 0)
        def _():
            dk_ref[...] = jnp.zeros_like(dk_ref)
            dv_ref[...] = jnp.zeros_like(dv_ref)

        m_a = lax.broadcasted_iota(jnp.int32, (1, BLK), 1) < 64
        q = q_ref[...]
        zq = jnp.zeros_like(q)
        qs = (jnp.where(m_a, q, zq), jnp.where(m_a, zq, q))
        do = do_ref[...]
        zd = jnp.zeros_like(do)
        dos = (jnp.where(m_a, do, zd).astype(_MXU), jnp.where(m_a, zd, do).astype(_MXU))
        prod = do.astype(_MXU).astype(f32) * o_ref[...]
        ds_ref[0] = jnp.broadcast_to(jnp.sum(jnp.where(m_a, prod, 0.0), 1, keepdims=True), (qb, BLK))
        ds_ref[1] = jnp.broadcast_to(jnp.sum(jnp.where(m_a, 0.0, prod), 1, keepdims=True), (qb, BLK))
        u_gt = _tri_ones(True)
        u_ge = _tri_ones(False)
        dqa_ref[...] = jnp.zeros_like(dqa_ref)
        c_ref[...] = jnp.zeros_like(c_ref)
        cg_ref[...] = jnp.zeros_like(cg_ref)

        def tile(j, masked):
            off = j * BLK if isinstance(j, int) else pl.multiple_of(j * BLK, BLK)
            k = k_ref[pl.ds(off, BLK), :]
            v = v_ref[pl.ds(off, BLK), :]
            mask = _sb_mask(i, j, qb) if masked else None
            dk_t = jnp.zeros((BLK, BLK), f32)
            dv_t = jnp.zeros((BLK, BLK), f32)
            for x in range(2):
                w, lb = _sb_tile(qs[x], k, mask, c_ref, x, u_gt)
                wb = w.astype(_MXU)
                gr = wb.astype(f32) * _dot(dos[x], v, NT)
                eg = _dot3(gr, u_ge)
                cg = cg_ref[x]
                suffix = eg[:, 0:BLK] + cg
                cg_ref[x] = cg + eg[:, BLK:2 * BLK]
                dz = gr - jnp.exp(lb) * (gr + ds_ref[x] - suffix)
                if masked:
                    dz = jnp.where(mask, dz, 0.0)
                dz = dz.astype(_MXU)
                dqa_ref[x] += _dot(dz, k)
                dk_t += _dot(dz, qs[x], TN)
                dv_t += _dot(wb, dos[x], TN)
            dk_ref[pl.ds(off, BLK), :] += dk_t
            dv_ref[pl.ds(off, BLK), :] += dv_t

        _sb_sweep(i, r, tile)
        dq_ref[...] = jnp.where(m_a, dqa_ref[0], dqa_ref[1]) * SB_SCALE

    blk = pl.BlockSpec((qb, 128), lambda p, i: (i, p))
    scr = pltpu.VMEM((2, qb, BLK), f32)
    return pl.pallas_call(
        body, name="sb_bwd", grid=(4, nq),
        in_specs=[blk, blk, blk,
                  pl.BlockSpec((L, 128), lambda p, i: (0, 4 + p)),
                  pl.BlockSpec((L, 128), lambda p, i: (0, 8 + p))],
        out_specs=(blk, pl.BlockSpec((L, 128), lambda p, i: (0, p)), pl.BlockSpec((L, 128), lambda p, i: (0, p))),
        out_shape=tuple(jax.ShapeDtypeStruct((L, 512), f32) for _ in range(3)),
        scratch_shapes=[scr, scr, scr, scr],
        compiler_params=_cp(("parallel", "arbitrary")))(dmixed, out_a, sb, sb, sb)


def _mla_mask(i, j, qb):
    row = i * qb + lax.broadcasted_iota(jnp.int32, (qb, BLK), 0)
    col = j * BLK + lax.broadcasted_iota(jnp.int32, (qb, BLK), 1)
    return (col <= row) & ((col >= N_PAD) | (col == row))


def _pair_mask2():
    return (lax.broadcasted_iota(jnp.int32, (1, 256), 1) % 128) < 64


def _mla_q2(qn_ref, qr_ref, mc_ref, ms_ref):
    qr = qr_ref[...]
    qr = qr * mc_ref[...] + _rot(qr, 16) * ms_ref[...]
    q2 = jnp.concatenate([qn_ref[...], qr], axis=1)
    m2 = _pair_mask2()
    z2 = jnp.zeros_like(q2)
    return (jnp.where(m2, q2, z2).astype(_MXU), jnp.where(m2, z2, q2).astype(_MXU))


def _mla_fwd(q, kv, kr2, mc, ms):
    L = q.shape[0]
    qb = _qrows(L)
    nq, r = L // qb, qb // BLK

    def body(qn_ref, qr_ref, mc_ref, ms_ref, kn_ref, v_ref, kr_ref, o_ref, lse_ref, acc_ref, m_ref):
        i = pl.program_id(1)
        m_a = lax.broadcasted_iota(jnp.int32, (1, BLK), 1) < 64
        qs = _mla_q2(qn_ref, qr_ref, mc_ref, ms_ref)
        acc_ref[...] = jnp.zeros_like(acc_ref)
        m_ref[...] = jnp.full(m_ref.shape, -1e30, f32)
        ones = jnp.ones((BLK, BLK), _MXU)

        def tile(j, masked):
            off = j * BLK if isinstance(j, int) else pl.multiple_of(j * BLK, BLK)
            k2 = jnp.concatenate([kn_ref[pl.ds(off, BLK), :], kr_ref[pl.ds(off, BLK), :]], axis=1)
            v1 = jnp.concatenate([v_ref[pl.ds(off, BLK), :], ones], axis=1)
            mask = _mla_mask(i, j, qb) if masked else None
            for x in range(2):
                s = _dot(qs[x], k2, NT) * MLA_SCALE
                if masked:
                    s = jnp.where(mask, s, -1e30)
                m_old = m_ref[x]
                m_new = jnp.maximum(m_old, jnp.max(s, 1, keepdims=True))
                a = jnp.exp(m_old - m_new)
                p = jnp.exp(s - m_new)
                if masked:
                    p = jnp.where(mask, p, 0.0)
                acc_ref[x] = jnp.concatenate([a, a], axis=1) * acc_ref[x] + _dot(p, v1)
                m_ref[x] = m_new

        _sb_sweep(i, r, tile)
        o_ref[...] = jnp.where(m_a, acc_ref[0, :, 0:BLK] / acc_ref[0, :, BLK:2 * BLK],
                               acc_ref[1, :, 0:BLK] / acc_ref[1, :, BLK:2 * BLK])
        for x in range(2):
            lse_ref[0, x] = m_ref[x] + jnp.log(acc_ref[x, :, BLK:2 * BLK])

    blk = lambda cb: pl.BlockSpec((qb, 128), lambda p, i: (i, cb + p))
    full = lambda cb: pl.BlockSpec((L, 128), lambda p, i: (0, cb + p))
    tab = pl.BlockSpec((qb, 128), lambda p, i: (i, 0))
    return pl.pallas_call(
        body, name="mla_fwd", grid=(4, nq),
        in_specs=[blk(0), blk(4), tab, tab, full(0), full(4), pl.BlockSpec((L, 128), lambda p, i: (0, 0))],
        out_specs=(blk(0), pl.BlockSpec((1, 2, qb, 128), lambda p, i: (p, 0, i, 0))),
        out_shape=(jax.ShapeDtypeStruct((L, 512), f32), jax.ShapeDtypeStruct((4, 2, L, 128), f32)),
        scratch_shapes=[pltpu.VMEM((2, qb, 2 * BLK), f32), pltpu.VMEM((2, qb, BLK), f32)],
        compiler_params=_cp(("parallel", "arbitrary")))(q, q, mc, ms, kv, kv, kr2)


def _mla_bwd(dmixed, q, kv, kr2, out_b, lse, mc, ms):
    L = q.shape[0]
    qb = _qrows(L)
    nq, r = L // qb, qb // BLK

    def body(do_ref, o_ref, lse_ref, qn_ref, qr_ref, mc_ref, ms_ref, kn_ref, v_ref, kr_ref,
             dqn_ref, dqr_ref, dkn_ref, dv_ref, dkr_ref, dqa_ref, ds_ref):
        i = pl.program_id(1)

        @pl.when(i == 0)
        def _():
            dkn_ref[...] = jnp.zeros_like(dkn_ref)
            dv_ref[...] = jnp.zeros_like(dv_ref)
            dkr_ref[...] = jnp.zeros_like(dkr_ref)

        m_a = lax.broadcasted_iota(jnp.int32, (1, BLK), 1) < 64
        qs = _mla_q2(qn_ref, qr_ref, mc_ref, ms_ref)
        do = do_ref[...]
        zd = jnp.zeros_like(do)
        dos = (jnp.where(m_a, do, zd).astype(_MXU), jnp.where(m_a, zd, do).astype(_MXU))
        prod = do * o_ref[...]
        ds_ref[0] = jnp.broadcast_to(jnp.sum(jnp.where(m_a, prod, 0.0), 1, keepdims=True), (qb, BLK))
        ds_ref[1] = jnp.broadcast_to(jnp.sum(jnp.where(m_a, 0.0, prod), 1, keepdims=True), (qb, BLK))
        dqa_ref[...] = jnp.zeros_like(dqa_ref)

        def tile(j, masked):
            off = j * BLK if isinstance(j, int) else pl.multiple_of(j * BLK, BLK)
            k2 = jnp.concatenate([kn_ref[pl.ds(off, BLK), :], kr_ref[pl.ds(off, BLK), :]], axis=1)
            v = v_ref[pl.ds(off, BLK), :]
            mask = _mla_mask(i, j, qb) if masked else None
            dk_t = jnp.zeros((BLK, 256), f32)
            dv_t = jnp.zeros((BLK, BLK), f32)
            for x in range(2):
                s = _dot(qs[x], k2, NT) * MLA_SCALE
                if masked:
                    p = jnp.where(mask, jnp.exp(jnp.where(mask, s, 0.0) - lse_ref[0, x]), 0.0)
                else:
                    p = jnp.exp(s - lse_ref[0, x])
                pb = p.astype(_MXU)
                ds = (p * (_dot(dos[x], v, NT) - ds_ref[x]) * MLA_SCALE).astype(_MXU)
                dqa_ref[x] += _dot(ds, k2)
                dk_t += _dot(ds, qs[x], TN)
                dv_t += _dot(pb, dos[x], TN)
            dkn_ref[pl.ds(off, BLK), :] += dk_t[:, 0:128]
            dkr_ref[0, pl.ds(off, BLK), :] += dk_t[:, 128:256]
            dv_ref[pl.ds(off, BLK), :] += dv_t

        _sb_sweep(i, r, tile)
        dq2 = jnp.where(_pair_mask2(), dqa_ref[0], dqa_ref[1])
        dqn_ref[...] = dq2[:, 0:128]
        dy = dq2[:, 128:256]
        dqr_ref[...] = dy * mc_ref[...] - _rot(dy * ms_ref[...], 16)

    blk = lambda cb: pl.BlockSpec((qb, 128), lambda p, i: (i, cb + p))
    full = lambda cb: pl.BlockSpec((L, 128), lambda p, i: (0, cb + p))
    tab = pl.BlockSpec((qb, 128), lambda p, i: (i, 0))
    o512 = jax.ShapeDtypeStruct((L, 512), f32)
    return pl.pallas_call(
        body, name="mla_bwd", grid=(4, nq),
        in_specs=[blk(4), blk(0), pl.BlockSpec((1, 2, qb, 128), lambda p, i: (p, 0, i, 0)), blk(0), blk(4), tab, tab,
                  full(0), full(4), pl.BlockSpec((L, 128), lambda p, i: (0, 0))],
        out_specs=(blk(0), blk(0), full(0), full(0), pl.BlockSpec((1, L, 128), lambda p, i: (p, 0, 0))),
        out_shape=(o512, o512, o512, o512, jax.ShapeDtypeStruct((4, L, 128), f32)),
        scratch_shapes=[pltpu.VMEM((2, qb, 2 * BLK), f32), pltpu.VMEM((2, qb, BLK), f32)],
        compiler_params=_cp(("parallel", "arbitrary")))(dmixed, out_b, lse, q, q, mc, ms, kv, kv, kr2)


def _ret_tables():
    log_g = jnp.log(jnp.array(RET_GAMMA, f32))
    idx = jnp.arange(BLK, dtype=f32)
    diff = idx[:, None] - idx[None, :]
    d_in = jnp.where(diff[None] >= 0, jnp.exp(jnp.maximum(diff, 0.0)[None] * log_g[:, None, None]), 0.0)
    q_dec = jnp.exp((idx[None, :] + 1.0) * log_g[:, None])
    k_dec = jnp.exp((BLK - 1.0 - idx[None, :]) * log_g[:, None])
    c_dec = jnp.exp(BLK * log_g)
    bc = lambda a: jnp.broadcast_to(a[:, :, None], (4, BLK, BLK))
    return d_in, bc(q_dec), bc(k_dec), jnp.broadcast_to(c_dec[:, None, None], (4, 8, BLK))


def _head_mask(x):
    lane = lax.broadcasted_iota(jnp.int32, (1, BLK), 1)
    return (lane < 64) if x == 0 else (lane >= 64)


def _ret_fwd(rqk, rv, proj, rtabs):
    L = rqk.shape[0]
    n = L // BLK
    d_in, q_dec, k_dec, c_dec = rtabs

    def body(q_ref, k_ref, v_ref, g_ref, din_ref, qd_ref, kd_ref, cd_ref, y_ref, o_ref, st_ref, s_scr):
        @pl.when(pl.program_id(1) == 0)
        def _():
            s_scr[...] = jnp.zeros_like(s_scr)

        q = q_ref[...]
        k = k_ref[...]
        zq = jnp.zeros_like(q)
        for x in range(2):
            hm = _head_mask(x)
            sl = slice(x * 128, (x + 1) * 128)
            qm = jnp.where(hm, q, zq)
            km = jnp.where(hm, k, zq)
            v = v_ref[:, sl]
            s_in = s_scr[x]
            st_ref[0, 0, x] = s_in
            inner = _dot(qm, km, NT) * din_ref[x]
            y = _dot(inner, v) + _dot(qm, s_in) * qd_ref[x]
            s_scr[x] = s_in * cd_ref[x, 0:1, :] + _dot(km.astype(f32) * kd_ref[x], v, TN)
            y_ref[:, sl] = y
            mu = jnp.mean(y, -1, keepdims=True)
            yc = y - mu
            yn = yc * lax.rsqrt(jnp.mean(jnp.square(yc), -1, keepdims=True) + LN_EPS)
            g = g_ref[:, sl]
            o_ref[:, sl] = g * jax.nn.sigmoid(g) * yn

    tab = pl.BlockSpec((2, BLK, BLK), lambda p, i: (p, 0, 0))
    return pl.pallas_call(
        body, name="ret_fwd", grid=(2, n),
        in_specs=[pl.BlockSpec((BLK, 128), lambda p, i: (i, p)), pl.BlockSpec((BLK, 128), lambda p, i: (i, 2 + p)),
                  pl.BlockSpec((BLK, 256), lambda p, i: (i, p)),
                  pl.BlockSpec((BLK, 256), lambda p, i: (i, C_RG // 256 + p)),
                  tab, tab, tab, pl.BlockSpec((2, 8, BLK), lambda p, i: (p, 0, 0))],
        out_specs=(pl.BlockSpec((BLK, 256), lambda p, i: (i, p)), pl.BlockSpec((BLK, 256), lambda p, i: (i, p)),
                   pl.BlockSpec((1, 1, 2, BLK, BLK), lambda p, i: (p, i, 0, 0, 0))),
        out_shape=(jax.ShapeDtypeStruct((L, 512), f32), jax.ShapeDtypeStruct((L, 512), f32),
                   jax.ShapeDtypeStruct((2, n, 2, BLK, BLK), f32)),
        scratch_shapes=[pltpu.VMEM((2, BLK, BLK), f32)],
        compiler_params=_cp(("parallel", "arbitrary")))(rqk, rqk, rv, proj, d_in, q_dec, k_dec, c_dec)


def _ret_bwd(dmixed, rqk, rv, proj, y, states, rtabs):
    L = rqk.shape[0]
    n = L // BLK
    d_in, q_dec, k_dec, c_dec = rtabs

    def body(do_ref, q_ref, k_ref, v_ref, g_ref, y_ref, st_ref, din_ref, qd_ref, kd_ref, cd_ref,
             dq_ref, dk_ref, dv_ref, dg_ref, ds_scr):
        @pl.when(pl.program_id(1) == 0)
        def _():
            ds_scr[...] = jnp.zeros_like(ds_scr)

        q = q_ref[...]
        k = k_ref[...]
        zq = jnp.zeros_like(q)
        dq_acc = jnp.zeros((BLK, BLK), f32)
        dk_acc = jnp.zeros((BLK, BLK), f32)
        for x in range(2):
            hm = _head_mask(x)
            sl = slice(x * 128, (x + 1) * 128)
            qm = jnp.where(hm, q, zq)
            km = jnp.where(hm, k, zq)
            v = v_ref[:, sl]
            yv = y_ref[:, sl]
            g = g_ref[:, sl]
            do = do_ref[:, sl]
            mu = jnp.mean(yv, -1, keepdims=True)
            yc = yv - mu
            rstd = lax.rsqrt(jnp.mean(jnp.square(yc), -1, keepdims=True) + LN_EPS)
            yn = yc * rstd
            sg = jax.nn.sigmoid(g)
            dg_ref[:, sl] = do * yn * sg * (1.0 + g * (1.0 - sg))
            dyn = do * g * sg
            dy = rstd * (dyn - jnp.mean(dyn, -1, keepdims=True) - yn * jnp.mean(dyn * yn, -1, keepdims=True))
            s_in = st_ref[0, 0, x]
            ds_out = ds_scr[x]
            kd = km.astype(f32) * kd_ref[x]
            a = _dot(qm, km, NT) * din_ref[x]
            da = _dot(dy, v, NT) * din_ref[x]
            dyq = dy * qd_ref[x]
            dq_acc += _dot(da, km) + _dot(dyq, s_in, NT)
            dk_acc += _dot(da, qm, TN) + _dot(v, ds_out, NT) * kd_ref[x]
            dv_ref[:, sl] = _dot(a, dy, TN) + _dot(kd, ds_out)
            ds_scr[x] = ds_out * cd_ref[x, 0:1, :] + _dot(qm, dyq, TN)
        dq_ref[...] = dq_acc
        dk_ref[...] = dk_acc

    rev = lambda w, cb: pl.BlockSpec((BLK, w), lambda p, i: (n - 1 - i, cb + p))
    tab = pl.BlockSpec((2, BLK, BLK), lambda p, i: (p, 0, 0))
    return pl.pallas_call(
        body, name="ret_bwd", grid=(2, n),
        in_specs=[rev(256, 4), rev(128, 0), rev(128, 2), rev(256, 0), rev(256, C_RG // 256), rev(256, 0),
                  pl.BlockSpec((1, 1, 2, BLK, BLK), lambda p, i: (p, n - 1 - i, 0, 0, 0)),
                  tab, tab, tab, pl.BlockSpec((2, 8, BLK), lambda p, i: (p, 0, 0))],
        out_specs=(rev(128, 0), rev(128, 0), rev(256, 0), rev(256, 0)),
        out_shape=(jax.ShapeDtypeStruct((L, 256), f32), jax.ShapeDtypeStruct((L, 256), f32),
                   jax.ShapeDtypeStruct((L, 512), f32), jax.ShapeDtypeStruct((L, 512), f32)),
        scratch_shapes=[pltpu.VMEM((2, BLK, BLK), f32)],
        compiler_params=_cp(("parallel", "arbitrary")))(dmixed, rqk, rqk, rv, proj, y, states, d_in, q_dec, k_dec, c_dec)


def _loss_head(h, target):
    L = h.shape[0]
    n = L // BLK

    def body(h_ref, t_ref, dy_ref, l_ref):
        i = pl.program_id(0)

        @pl.when(i == 0)
        def _():
            dy_ref[...] = jnp.zeros_like(dy_ref)
            l_ref[...] = jnp.zeros_like(l_ref)

        @pl.when(i > 0)
        def _():
            err = h_ref[...] - t_ref[...]
            dy_ref[...] = err * (1.0 / D)
            sq = jnp.sum(jnp.sum(jnp.square(err), 1, keepdims=True), 0, keepdims=True)
            l_ref[...] += (0.5 / D) * sq

    return pl.pallas_call(
        body, name="loss_head", grid=(n,),
        in_specs=[pl.BlockSpec((BLK, D), lambda i: (i, 0)),
                  pl.BlockSpec((BLK, D), lambda i: (jnp.maximum(i - 1, 0), 0))],
        out_specs=(pl.BlockSpec((BLK, D), lambda i: (i, 0)), pl.BlockSpec((8, 128), lambda i: (0, 0))),
        out_shape=(jax.ShapeDtypeStruct((L, D), f32), jax.ShapeDtypeStruct((8, 128), f32)),
        compiler_params=_cp(("arbitrary",)))(h, target)


def _adam_math(w, g, m, v):
    m = ADAM_B1 * m + (1.0 - ADAM_B1) * g
    v = ADAM_B2 * v + (1.0 - ADAM_B2) * jnp.square(g)
    m_hat = m / (1.0 - ADAM_B1 ** ADAM_STEP)
    v_hat = v / (1.0 - ADAM_B2 ** ADAM_STEP)
    delta = -ADAM_LR * (m_hat / (jnp.sqrt(v_hat) + ADAM_EPS) + ADAM_WD * w)
    return delta, m, v


def _adamw(parts, w, m, v, name):
    R = w.shape[0]
    tr = _pick(R, (96, 64, 48, 32, 16, 8))
    row = pl.BlockSpec((tr, D), lambda i: (i, 0))

    def body(p_ref, w_ref, m_ref, v_ref, g_ref, d_ref, nm_ref, nv_ref):
        g = p_ref[0].astype(f32)
        for k in range(1, N_DEV):
            g = g + p_ref[k].astype(f32)
        d, nm, nv = _adam_math(w_ref[...], g, m_ref[...], v_ref[...])
        g_ref[...] = g
        d_ref[...] = d
        nm_ref[...] = nm
        nv_ref[...] = nv

    o = jax.ShapeDtypeStruct((R, D), f32)
    return pl.pallas_call(
        body, name=name, grid=(R // tr,),
        in_specs=[pl.BlockSpec((N_DEV, tr, D), lambda i: (0, i, 0)), row, row, row],
        out_specs=(row, row, row, row), out_shape=(o, o, o, o),
        compiler_params=_cp(("parallel",)))(parts, w, m, v)


def _exchange(x, all_to_all, name):
    shape = x.shape[1:] if all_to_all else x.shape
    n_peer = N_DEV - 1

    def body(x_ref, o_ref, send_sems, recv_sems, local_sem):
        mx, my, mc = lax.axis_index("x"), lax.axis_index("y"), lax.axis_index("c")
        me = 4 * mx + 2 * my + mc

        def peer(k):
            px = (1 - mx) if k & 4 else mx
            py = (1 - my) if k & 2 else my
            pc = (1 - mc) if k & 1 else mc
            return (px, py, pc), 4 * px + 2 * py + pc

        def copy(k):
            dev, idx = peer(k)
            src = x_ref.at[idx] if all_to_all else x_ref
            return pltpu.make_async_remote_copy(
                src_ref=src, dst_ref=o_ref.at[me], send_sem=send_sems.at[k - 1], recv_sem=recv_sems.at[k - 1],
                device_id=dev, device_id_type=pl.DeviceIdType.MESH)

        mine = pltpu.make_async_copy(x_ref.at[me] if all_to_all else x_ref, o_ref.at[me], local_sem)
        mine.start()
        copies = [copy(k) for k in range(1, N_DEV)]
        for cp in copies:
            cp.start()
        for k in range(1, N_DEV):
            _, idx = peer(k)
            pltpu.make_async_remote_copy(
                src_ref=o_ref.at[idx], dst_ref=o_ref.at[idx], send_sem=send_sems.at[k - 1],
                recv_sem=recv_sems.at[k - 1], device_id=peer(k)[0], device_id_type=pl.DeviceIdType.MESH).wait_recv()
        for cp in copies:
            cp.wait_send()
        mine.wait()

    hbm = pl.BlockSpec(memory_space=pltpu.HBM)
    return pl.pallas_call(
        body, name=name, in_specs=[hbm], out_specs=hbm,
        out_shape=jax.ShapeDtypeStruct((N_DEV,) + tuple(shape), x.dtype),
        scratch_shapes=[pltpu.SemaphoreType.DMA((n_peer,)), pltpu.SemaphoreType.DMA((n_peer,)),
                        pltpu.SemaphoreType.DMA])(x)


_BIG = ("w_in", "w_uq", "w_ukv", "w_out", "w_ff1", "w_ff2")


def _pack_shards(ws):
    return jnp.concatenate([w.reshape(-1, D) for w in ws], axis=0)


def _big_rows(depth):
    sizes = [depth * 1024 * 468, depth * 384 * 96, depth * 256 * 128, depth * 192 * 1024, depth * 1024 * 512,
             depth * 512 * 1024]
    return [s // D for s in sizes]


def _unpack_full(gathered, depth):
    rows = _big_rows(depth)
    offs = np.cumsum([0] + rows)
    part = lambda k: gathered[:, offs[k]:offs[k + 1]]
    w_in = part(0).reshape(8, depth, 1024, 468).transpose(1, 2, 0, 3).reshape(depth, 1024, 3744)
    z32 = jnp.zeros((depth, 1024, 32), w_in.dtype)
    kr = w_in[:, :, 2176:2208]
    w_in = jnp.concatenate([w_in[:, :, 0:1536], w_in[:, :, 1920:2176], w_in[:, :, 2208:3744], w_in[:, :, 1536:1920],
                            kr, z32, kr, z32], axis=2)
    w_uq = part(1).reshape(8, depth, 384, 96).transpose(1, 2, 0, 3).reshape(depth, 384, 8, 96)
    rope = jnp.concatenate([w_uq[..., 64:96], jnp.zeros((depth, 384, 8, 32), w_uq.dtype)], axis=-1)
    w_uq = jnp.concatenate([w_uq[..., 0:64].reshape(depth, 384, 512), rope.reshape(depth, 384, 512)], axis=2)
    w_ukv = part(2).reshape(8, depth, 256, 128).transpose(1, 2, 0, 3).reshape(depth, 256, 8, 128)
    w_ukv = jnp.concatenate([w_ukv[..., 0:64].reshape(depth, 256, 512), w_ukv[..., 64:128].reshape(depth, 256, 512)],
                            axis=2)
    w_out = part(3).reshape(8, depth, 192, 1024).transpose(1, 0, 2, 3).reshape(depth, 1536, 1024)
    w_ff1 = part(4).reshape(8, depth, 1024, 512).transpose(1, 2, 0, 3).reshape(depth, 1024, 4096)
    w_ff2 = part(5).reshape(8, depth, 512, 1024).transpose(1, 0, 2, 3).reshape(depth, 4096, 1024)
    return w_in, w_uq, w_ukv, w_out, w_ff1, w_ff2


def _pack_grads(g_in, g_uq, g_ukv, g_out, g_ff1, g_ff2):
    depth = g_in.shape[0]
    kr = g_in[:, :, C_KR:C_KR + 32] + g_in[:, :, C_KR + 64:C_KR + 96]
    g_in = jnp.concatenate([g_in[:, :, 0:1536], g_in[:, :, C_CQ:C_CQ + 384], g_in[:, :, C_CKV:C_CKV + 256], kr,
                            g_in[:, :, C_RQ:C_CQ]], axis=2)
    g_in = g_in.reshape(depth, 1024, 8, 468).transpose(2, 0, 1, 3)
    g_uq = jnp.concatenate([g_uq[:, :, 0:512].reshape(depth, 384, 8, 64),
                            g_uq[:, :, 512:1024].reshape(depth, 384, 8, 64)[..., 0:32]], axis=-1)
    g_uq = g_uq.reshape(depth, 384, 8, 96).transpose(2, 0, 1, 3)
    g_ukv = jnp.concatenate([g_ukv[:, :, 0:512].reshape(depth, 256, 8, 64),
                             g_ukv[:, :, 512:1024].reshape(depth, 256, 8, 64)], axis=-1)
    g_ukv = g_ukv.transpose(2, 0, 1, 3)
    g_out = g_out.reshape(depth, 8, 192, 1024).transpose(1, 0, 2, 3)
    g_ff1 = g_ff1.reshape(depth, 1024, 8, 512).transpose(2, 0, 1, 3)
    g_ff2 = g_ff2.reshape(depth, 8, 512, 1024).transpose(1, 0, 2, 3)
    return jnp.concatenate([g.reshape(8, -1, D) for g in (g_in, g_uq, g_ukv, g_out, g_ff1, g_ff2)], axis=1)


def _rope_tables(L):
    pos = (jnp.arange(L) - N_PAD).astype(f32)

    def cs(half):
        inv = ROPE_THETA ** (-jnp.arange(half, dtype=f32) / half)
        ang = pos[:, None] * inv[None, :]
        return jnp.cos(ang), jnp.sin(ang)

    c, s = cs(32)
    rc, rs = jnp.tile(c, (1, 8)), jnp.tile(s, (1, 8))
    c, s = cs(16)
    z = jnp.zeros((L, 32), f32)
    mc, ms = jnp.concatenate([c, c, z, c, c, z], 1), jnp.concatenate([s, s, z, s, s, z], 1)
    return rc, rs, mc, ms


def _layer_fwd(h, wl, gq, gkv, g1, b1, g2, b2, tabs, rtabs):
    w_in, w_uq, w_ukv, w_out, w_ff1, w_ff2 = wl
    proj = _mm(h, w_in, "nn", "mm_in")
    sb, cqn, ckvn, rqk, rv, kr2 = _prep_fwd(proj, gq, gkv, tabs)
    q = _mm(cqn, w_uq, "nn", "mm_uq")
    kv = _mm(ckvn, w_ukv, "nn", "mm_ukv", out_dtype=bf16)
    out_a = _sb_fwd(sb)
    out_b, lse = _mla_fwd(q, kv, kr2, tabs[2], tabs[3])
    y, out_c, states = _ret_fwd(rqk, rv, proj, rtabs)
    mixed = jnp.concatenate([out_a, out_b, out_c], axis=1)
    mix = _mm(mixed, w_out, "nn", "mm_out")
    h1 = _ln_fwd(h, mix, g1, b1, DN_ALPHA, "ln_fwd")
    u, a = _mm(h1, w_ff1, "nn", "mm_ff1", epi="relu2")
    ff = _mm(a, w_ff2, "nn", "mm_ff2")
    h2 = _ln_fwd(h1, ff, g2, b2, DN_ALPHA, "ln_fwd")
    saved = (h, proj, sb, cqn, ckvn, rqk, rv, kr2, q, kv, out_a, out_b, lse, y, states, mixed, mix, h1, u, a, ff)
    return h2, saved


def _layer_bwd(dh2, saved, wl, gq, gkv, g1, g2, tabs, rtabs):
    w_in, w_uq, w_ukv, w_out, w_ff1, w_ff2 = wl
    h, proj, sb, cqn, ckvn, rqk, rv, kr2, q, kv, out_a, out_b, lse, y, states, mixed, mix, h1, u, a, ff = saved
    dz2, dg2, db2 = _ln_bwd(dh2, h1, ff, g2, DN_ALPHA, "ln_bwd")
    du = _mm(dz2, w_ff2, "nt", "mm_dff2", epi="mul_relu", extra=u)
    gw_ff2 = _mm(a, dz2, "tn", "mm_gff2")
    dh1 = _mm(du, w_ff1, "nt", "mm_dff1", epi="add", extra=dz2, alpha=DN_ALPHA)
    gw_ff1 = _mm(h1, du, "tn", "mm_gff1")
    dz1, dg1, db1 = _ln_bwd(dh1, h, mix, g1, DN_ALPHA, "ln_bwd")
    dmixed = _mm(dz1, w_out, "nt", "mm_dout")
    gw_out = _mm(mixed, dz1, "tn", "mm_gout")
    dsq, dsk, dsv = _sb_bwd(dmixed, sb, out_a)
    dqn, dqr, dkn, dv, dkr_p = _mla_bwd(dmixed, q, kv, kr2, out_b, lse, tabs[2], tabs[3])
    dq = jnp.concatenate([dqn, dqr], axis=1)
    dkv = jnp.concatenate([dkn, dv], axis=1)
    dcqn = _mm(dq, w_uq, "nt", "mm_duq")
    gw_uq = _mm(cqn, dq, "tn", "mm_guq")
    dckvn = _mm(dkv, w_ukv, "nt", "mm_dukv")
    gw_ukv = _mm(ckvn, dkv, "tn", "mm_gukv")
    drq_r, drk_r, drv, drg = _ret_bwd(dmixed, rqk, rv, proj, y, states, rtabs)
    dkr_r = dkr_p[0] + dkr_p[1] + dkr_p[2] + dkr_p[3]
    dcq, dckv, drq, drk, dkr2, dgq, dgkv = _prep_bwd(proj, gq, gkv, tabs, dcqn, dckvn, drq_r, drk_r, dkr_r)
    dproj = jnp.concatenate([dsq, dsk, dsv, dckv, drq, drk, drv, drg, dcq, dkr2], axis=1)
    dh = _mm(dproj, w_in, "nt", "mm_din", epi="add", extra=dz1, alpha=DN_ALPHA)
    gw_in = _mm(h, dproj, "tn", "mm_gin")
    return dh, (gw_in, gw_uq, gw_ukv, gw_out, gw_ff1, gw_ff2), (dgq, dgkv, dg1, db1, dg2, db2)


def kernel(x, meta_tokens, ln_emb_g, ln_emb_b, w_in, mla_q_norm, mla_kv_norm, w_uq, w_ukv, w_out, ln1_g, ln1_b, w_ff1, w_ff2, ln2_g, ln2_b, loss_target, m_meta_tokens, m_ln_emb_g, m_ln_emb_b, m_w_in, m_mla_q_norm, m_mla_kv_norm, m_w_uq, m_w_ukv, m_w_out, m_ln1_g, m_ln1_b, m_w_ff1, m_w_ff2, m_ln2_g, m_ln2_b, v_meta_tokens, v_ln_emb_g, v_ln_emb_b, v_w_in, v_mla_q_norm, v_mla_kv_norm, v_w_uq, v_w_ukv, v_w_out, v_ln1_g, v_ln1_b, v_w_ff1, v_w_ff2, v_ln2_g, v_ln2_b):
    depth = w_in.shape[0]
    S = x.shape[1]
    L = S + BLK
    me = 4 * lax.axis_index("x") + 2 * lax.axis_index("y") + lax.axis_index("c")
    big = (w_in, w_uq, w_ukv, w_out, w_ff1, w_ff2)
    big_m = (m_w_in, m_w_uq, m_w_ukv, m_w_out, m_w_ff1, m_w_ff2)
    big_v = (v_w_in, v_w_uq, v_w_ukv, v_w_out, v_w_ff1, v_w_ff2)
    small = (ln_emb_g, ln_emb_b, mla_q_norm, mla_kv_norm, ln1_g, ln1_b, ln2_g, ln2_b)
    small_m = (m_ln_emb_g, m_ln_emb_b, m_mla_q_norm, m_mla_kv_norm, m_ln1_g, m_ln1_b, m_ln2_g, m_ln2_b)
    small_v = (v_ln_emb_g, v_ln_emb_b, v_mla_q_norm, v_mla_kv_norm, v_ln1_g, v_ln1_b, v_ln2_g, v_ln2_b)

    w_shard = _pack_shards(big)
    gathered = _exchange(w_shard.astype(bf16), False, "gather_weights")
    full = _unpack_full(gathered, depth)
    meta_all = _exchange(meta_tokens, False, "gather_meta")
    meta_full = meta_all.transpose(1, 0, 2).reshape(N_META, D)

    tabs = _rope_tables(L)
    rtabs = _ret_tables()

    hcat = jnp.concatenate([jnp.zeros((N_PAD, D), f32), meta_full, x[0]], axis=0)
    h = _ln_fwd(hcat, None, ln_emb_g, ln_emb_b, 1.0, "ln_emb_fwd")
    saved = []
    for l in range(depth):
        wl = tuple(w[l] for w in full)
        h, sv = _layer_fwd(h, wl, mla_q_norm[l], mla_kv_norm[l], ln1_g[l], ln1_b[l], ln2_g[l], ln2_b[l], tabs, rtabs)
        saved.append(sv)

    dh, loss_part = _loss_head(h, loss_target[0])
    gbig, gsmall = [None] * depth, [None] * depth
    for l in reversed(range(depth)):
        wl = tuple(w[l] for w in full)
        dh, gbig[l], gsmall[l] = _layer_bwd(dh, saved[l], wl, mla_q_norm[l], mla_kv_norm[l], ln1_g[l], ln2_g[l],
                                            tabs, rtabs)
    dz0, dg_emb, db_emb = _ln_bwd(dh, hcat, None, ln_emb_g, 1.0, "ln_emb_bwd")
    grad_x = dz0[BLK:][None]
    dmeta = dz0[N_PAD:BLK]

    gfull = [jnp.stack([gbig[l][k] for l in range(depth)]) for k in range(6)]
    parts = _exchange(_pack_grads(*gfull).astype(bf16), True, "scatter_grads")
    g_sh, d_sh, m_sh, v_sh = _adamw(parts, w_shard, _pack_shards(big_m), _pack_shards(big_v), "adamw_big")

    st = lambda k: jnp.stack([gsmall[l][k] for l in range(depth)])
    g_small = (dg_emb, db_emb, st(0), st(1), st(2), st(3), st(4), st(5))
    n_small = sum(int(np.prod(a.shape)) for a in small)
    flat = jnp.concatenate([a.reshape(-1) for a in g_small] + [dmeta.reshape(-1), loss_part[0, 0:1]])
    rows = -(-(flat.shape[0]) // (8 * D)) * 8
    pad = rows * D - flat.shape[0]
    flat = jnp.concatenate([flat, jnp.zeros((pad,), f32)]).reshape(rows, D)
    parts_s = _exchange(flat, False, "gather_small")

    def pack_small(arrs, meta_shard):
        col = jnp.zeros((N_META, D), f32)
        col = lax.dynamic_update_slice(col, meta_shard, (0, me * 128))
        fl = jnp.concatenate([a.reshape(-1) for a in arrs] + [col.reshape(-1), jnp.zeros((1 + pad,), f32)])
        return fl.reshape(rows, D)

    g_s, d_s, m_s, v_s = _adamw(parts_s, pack_small(small, meta_tokens), pack_small(small_m, m_meta_tokens),
                                pack_small(small_v, v_meta_tokens), "adamw_small")
    loss = g_s.reshape(-1)[n_small + N_META * D]

    def unpack_big(flat_rows):
        outs, off = [], 0
        for w, r in zip(big, _big_rows(depth)):
            outs.append(flat_rows[off:off + r].reshape(w.shape))
            off += r
        return outs

    def unpack_small(flat_rows):
        fl = flat_rows.reshape(-1)
        outs, off = [], 0
        for a in small:
            n = int(np.prod(a.shape))
            outs.append(fl[off:off + n].reshape(a.shape))
            off += n
        meta = lax.dynamic_slice(fl[off:off + N_META * D].reshape(N_META, D), (0, me * 128), (N_META, 128))
        return meta, outs

    def assemble(big_rows_arr, small_rows_arr):
        b = unpack_big(big_rows_arr)
        meta, s = unpack_small(small_rows_arr)
        return [meta, s[0], s[1], b[0], s[2], s[3], b[1], b[2], b[3], s[4], s[5], b[4], b[5], s[6], s[7]]

    return (loss, grad_x, *assemble(g_sh, g_s), *assemble(d_sh, d_s), *assemble(m_sh, m_s), *assemble(v_sh, v_s))
```

```python
import functools
import math

import numpy as np
import jax
import jax.numpy as jnp
from jax import lax
from jax.experimental import pallas as pl
from jax.experimental.pallas import tpu as pltpu

f32 = jnp.float32
bf16 = jnp.bfloat16
_MXU = jnp.bfloat16

BLK = 128
N_META = 16
N_PAD = 112
D = 1024
N_DEV = 8
LN_EPS = 1e-5
DEPTH = 4
DN_ALPHA = (2 * DEPTH) ** 0.25
ROPE_THETA = 10000.0
MLA_SCALE = (64 + 32) ** -0.5
SB_SCALE = 0.125
RET_SCALE = 0.125
RET_GAMMA = tuple(1.0 - 2.0 ** (-5 - h) for h in range(4))

ADAM_LR, ADAM_B1, ADAM_B2, ADAM_EPS, ADAM_WD, ADAM_STEP = 0.001, 0.9, 0.999, 1e-08, 0.01, 10

C_SBQ, C_SBK, C_SBV, C_CKV, C_RQ, C_RK, C_RV, C_RG, C_CQ, C_KR, N_INP = (
    0, 512, 1024, 1536, 1792, 2048, 2304, 2816, 3328, 3712, 3840)

VMEM_LIMIT = 56 * 1024 * 1024


def _cp(sem):
    return pltpu.CompilerParams(dimension_semantics=sem, vmem_limit_bytes=VMEM_LIMIT)


def _pick(n, cands):
    for c in cands:
        if n % c == 0:
            return c
    return n


def _dot(a, b, dims=(((1,), (0,)), ((), ()))):
    return lax.dot_general(a.astype(_MXU), b.astype(_MXU), dims, preferred_element_type=f32)


NT = (((1,), (1,)), ((), ()))
TN = (((0,), (0,)), ((), ()))


def _dot3(x, u):
    hi = x.astype(_MXU)
    r1 = x - hi.astype(f32)
    mid = r1.astype(_MXU)
    lo = (r1 - mid.astype(f32)).astype(_MXU)
    return (jnp.dot(hi, u, preferred_element_type=f32) + jnp.dot(mid, u, preferred_element_type=f32)
            + jnp.dot(lo, u, preferred_element_type=f32))


def _rot(x, half):
    lane = lax.broadcasted_iota(jnp.int32, x.shape, 1)
    first = (lane % 64) < half
    return jnp.where(first, -pltpu.roll(x, 128 - half, 1), pltpu.roll(x, half, 1))


def _mm(a, b, mode, name, epi=None, extra=None, alpha=1.0, out_dtype=f32):
    if mode == "nn":
        (M, K), N = a.shape, b.shape[1]
    elif mode == "nt":
        (M, K), N = a.shape, b.shape[0]
    else:
        (K, M), N = a.shape, b.shape[1]
    tm = _pick(M, (1408, 1024, 768, 512, 384, 256, 128))
    tn = _pick(N, ((1920,) if mode == "tn" else ()) + (1024, 768, 512, 384, 256, 128))
    tk = _pick(K, (1024, 768, 512, 384, 256, 128))
    nk = K // tk
    if mode == "nn":
        a_spec = pl.BlockSpec((tm, tk), lambda i, j, k: (i, k))
        b_spec = pl.BlockSpec((tk, tn), lambda i, j, k: (k, j))
        dims = (((1,), (0,)), ((), ()))
    elif mode == "nt":
        a_spec = pl.BlockSpec((tm, tk), lambda i, j, k: (i, k))
        b_spec = pl.BlockSpec((tn, tk), lambda i, j, k: (j, k))
        dims = NT
    else:
        a_spec = pl.BlockSpec((tk, tm), lambda i, j, k: (k, i))
        b_spec = pl.BlockSpec((tk, tn), lambda i, j, k: (k, j))
        dims = TN
    o_spec = pl.BlockSpec((tm, tn), lambda i, j, k: (i, j))
    in_specs, args = [a_spec, b_spec], [a, b]
    if extra is not None:
        in_specs.append(o_spec)
        args.append(extra)
    if epi == "relu2":
        out_shape = (jax.ShapeDtypeStruct((M, N), f32), jax.ShapeDtypeStruct((M, N), bf16))
        out_specs = (o_spec, o_spec)
    else:
        out_shape = jax.ShapeDtypeStruct((M, N), out_dtype)
        out_specs = o_spec

    def body(*refs):
        a_ref, b_ref = refs[0], refs[1]
        acc = refs[-1]
        k = pl.program_id(2)

        @pl.when(k == 0)
        def _():
            acc[...] = jnp.zeros_like(acc)

        acc[...] += _dot(a_ref[...], b_ref[...], dims)

        @pl.when(k == nk - 1)
        def _():
            r = acc[...]
            if epi == "relu2":
                refs[2][...] = r
                refs[3][...] = jnp.square(jnp.maximum(r, 0.0)).astype(bf16)
            elif epi == "mul_relu":
                refs[3][...] = r * (2.0 * jnp.maximum(refs[2][...], 0.0))
            elif epi == "add":
                refs[3][...] = r + alpha * refs[2][...]
            else:
                refs[2][...] = r.astype(out_dtype)

    return pl.pallas_call(
        body, name=name, grid=(M // tm, N // tn, nk), in_specs=in_specs, out_specs=out_specs,
        out_shape=out_shape, scratch_shapes=[pltpu.VMEM((tm, tn), f32)],
        compiler_params=_cp(("parallel", "parallel", "arbitrary")))(*args)


def _ln_fwd(h, m, g, b, alpha, name):
    L = h.shape[0]
    tm = _pick(L, (384, 256, 128))
    row = pl.BlockSpec((tm, D), lambda i: (i, 0))
    vec = pl.BlockSpec((1, D), lambda i: (0, 0))

    def body(*refs):
        if m is None:
            h_ref, g_ref, b_ref, o_ref = refs
            z = h_ref[...]
        else:
            h_ref, m_ref, g_ref, b_ref, o_ref = refs
            z = alpha * h_ref[...] + m_ref[...]
        mu = jnp.mean(z, -1, keepdims=True)
        var = jnp.mean(jnp.square(z - mu), -1, keepdims=True)
        o_ref[...] = (z - mu) * lax.rsqrt(var + LN_EPS) * g_ref[...] + b_ref[...]

    args = [h] + ([] if m is None else [m]) + [g.reshape(1, D), b.reshape(1, D)]
    specs = [row] + ([] if m is None else [row]) + [vec, vec]
    return pl.pallas_call(body, name=name, grid=(L // tm,), in_specs=specs, out_specs=row,
                          out_shape=jax.ShapeDtypeStruct((L, D), f32), compiler_params=_cp(("parallel",)))(*args)


def _ln_bwd(dy, h, m, g, alpha, name):
    L = h.shape[0]
    tm = _pick(L, (384, 256, 128))
    row = pl.BlockSpec((tm, D), lambda i: (i, 0))
    vec = pl.BlockSpec((1, D), lambda i: (0, 0))
    acc = pl.BlockSpec((8, D), lambda i: (0, 0))

    def body(*refs):
        if m is None:
            dy_ref, h_ref, g_ref, dz_ref, dg_ref, db_ref = refs
            z = h_ref[...]
        else:
            dy_ref, h_ref, m_ref, g_ref, dz_ref, dg_ref, db_ref = refs
            z = alpha * h_ref[...] + m_ref[...]

        @pl.when(pl.program_id(0) == 0)
        def _():
            dg_ref[...] = jnp.zeros_like(dg_ref)
            db_ref[...] = jnp.zeros_like(db_ref)

        dyv = dy_ref[...]
        mu = jnp.mean(z, -1, keepdims=True)
        zc = z - mu
        rstd = lax.rsqrt(jnp.mean(jnp.square(zc), -1, keepdims=True) + LN_EPS)
        xh = zc * rstd
        dxh = dyv * g_ref[...]
        dz_ref[...] = rstd * (dxh - jnp.mean(dxh, -1, keepdims=True) - xh * jnp.mean(dxh * xh, -1, keepdims=True))
        dg_ref[0:1, :] += jnp.sum(dyv * xh, 0, keepdims=True)
        db_ref[0:1, :] += jnp.sum(dyv, 0, keepdims=True)

    args = [dy, h] + ([] if m is None else [m]) + [g.reshape(1, D)]
    specs = [row, row] + ([] if m is None else [row]) + [vec]
    dz, dg, db = pl.pallas_call(
        body, name=name, grid=(L // tm,), in_specs=specs, out_specs=(row, acc, acc),
        out_shape=(jax.ShapeDtypeStruct((L, D), f32), jax.ShapeDtypeStruct((8, D), f32),
                   jax.ShapeDtypeStruct((8, D), f32)),
        compiler_params=_cp(("arbitrary",)))(*args)
    return dz, dg[0], db[0]


def _rms(x, g):
    r = lax.rsqrt(jnp.mean(jnp.square(x), -1, keepdims=True) + LN_EPS)
    return x * r * g


def _prep_fwd(proj, gq, gkv, tabs):
    L = proj.shape[0]
    tm = BLK
    rc, rs, mc, ms = tabs

    def body(p_ref, gq_ref, gkv_ref, rc_ref, rs_ref, mc_ref, ms_ref, sb_ref, cq_ref, ckv_ref, rqk_ref, rv_ref, kr_ref):
        i = pl.program_id(0)
        sb_ref[:, 0:512] = (p_ref[:, C_SBQ:C_SBQ + 512] * SB_SCALE).astype(bf16)
        sb_ref[:, 512:1536] = p_ref[:, C_SBK:C_SBK + 1024].astype(bf16)
        cq_ref[...] = _rms(p_ref[:, C_CQ:C_CQ + 384], gq_ref[...]).astype(bf16)
        ckv_ref[...] = _rms(p_ref[:, C_CKV:C_CKV + 256], gkv_ref[...]).astype(bf16)
        valid = (i * tm + lax.broadcasted_iota(jnp.int32, (tm, 128), 0)) >= N_PAD
        for c in range(2):
            sl = slice(c * 128, (c + 1) * 128)
            x = p_ref[:, C_RQ + c * 128:C_RQ + (c + 1) * 128]
            rqk_ref[:, sl] = (x * rc_ref[:, sl] + _rot(x, 32) * rs_ref[:, sl]).astype(bf16)
            x = p_ref[:, C_RK + c * 128:C_RK + (c + 1) * 128]
            kk = (x * rc_ref[:, sl] + _rot(x, 32) * rs_ref[:, sl]) * RET_SCALE
            rqk_ref[:, 256 + c * 128:256 + (c + 1) * 128] = jnp.where(valid, kk, 0.0).astype(bf16)
        rv_ref[...] = p_ref[:, C_RV:C_RV + 512].astype(bf16)
        x = p_ref[:, C_KR:C_KR + 128]
        kr_ref[...] = (x * mc_ref[...] + _rot(x, 16) * ms_ref[...]).astype(bf16)

    def row(w):
        return pl.BlockSpec((tm, w), lambda i: (i, 0))

    def vec(w):
        return pl.BlockSpec((1, w), lambda i: (0, 0))

    widths = (1536, 384, 256, 512, 512, 128)
    return pl.pallas_call(
        body, name="prep_fwd", grid=(L // tm,),
        in_specs=[row(N_INP), vec(384), vec(256), row(256), row(256), row(128), row(128)],
        out_specs=tuple(row(w) for w in widths),
        out_shape=tuple(jax.ShapeDtypeStruct((L, w), bf16) for w in widths),
        compiler_params=_cp(("parallel",)))(proj, gq.reshape(1, 384), gkv.reshape(1, 256), rc, rs, mc, ms)


def _rms_bwd(x, g, dy):
    r = lax.rsqrt(jnp.mean(jnp.square(x), -1, keepdims=True) + LN_EPS)
    u = dy * g
    dx = r * u - x * (r * r * r) * jnp.mean(x * u, -1, keepdims=True)
    return dx, jnp.sum(dy * x * r, 0, keepdims=True)


def _prep_bwd(proj, gq, gkv, tabs, dcqn, dckvn, drq_r, drk_r, dkr_r):
    L = proj.shape[0]
    tm = BLK
    rc, rs, mc, ms = tabs

    def body(p_ref, gq_ref, gkv_ref, rc_ref, rs_ref, mc_ref, ms_ref, dcqn_ref, dckvn_ref, drq_ref, drk_ref,
             dkr_ref, ocq_ref, ockv_ref, orq_ref, ork_ref, okr_ref, dgq_ref, dgkv_ref):
        i = pl.program_id(0)

        @pl.when(i == 0)
        def _():
            dgq_ref[...] = jnp.zeros_like(dgq_ref)
            dgkv_ref[...] = jnp.zeros_like(dgkv_ref)

        dx, dg = _rms_bwd(p_ref[:, C_CQ:C_CQ + 384], gq_ref[...], dcqn_ref[...])
        ocq_ref[...] = dx
        dgq_ref[0:1, :] += dg
        dx, dg = _rms_bwd(p_ref[:, C_CKV:C_CKV + 256], gkv_ref[...], dckvn_ref[...])
        ockv_ref[...] = dx
        dgkv_ref[0:1, :] += dg
        valid = (i * tm + lax.broadcasted_iota(jnp.int32, (tm, 128), 0)) >= N_PAD
        for c in range(2):
            sl = slice(c * 128, (c + 1) * 128)
            dy = drq_ref[:, sl]
            orq_ref[:, sl] = dy * rc_ref[:, sl] - _rot(dy * rs_ref[:, sl], 32)
            dy = jnp.where(valid, drk_ref[:, sl], 0.0) * RET_SCALE
            ork_ref[:, sl] = dy * rc_ref[:, sl] - _rot(dy * rs_ref[:, sl], 32)
        dy = dkr_ref[...]
        okr_ref[...] = dy * mc_ref[...] - _rot(dy * ms_ref[...], 16)

    def row(w):
        return pl.BlockSpec((tm, w), lambda i: (i, 0))

    def vec(w):
        return pl.BlockSpec((1, w), lambda i: (0, 0))

    def acc(w):
        return pl.BlockSpec((8, w), lambda i: (0, 0))

    widths = (384, 256, 256, 256, 128)
    outs = pl.pallas_call(
        body, name="prep_bwd", grid=(L // tm,),
        in_specs=[row(N_INP), vec(384), vec(256), row(256), row(256), row(128), row(128),
                  row(384), row(256), row(256), row(256), row(128)],
        out_specs=tuple(row(w) for w in widths) + (acc(384), acc(256)),
        out_shape=tuple(jax.ShapeDtypeStruct((L, w), f32) for w in widths)
        + (jax.ShapeDtypeStruct((8, 384), f32), jax.ShapeDtypeStruct((8, 256), f32)),
        compiler_params=_cp(("arbitrary",)))(
            proj, gq.reshape(1, 384), gkv.reshape(1, 256), rc, rs, mc, ms, dcqn, dckvn, drq_r, drk_r, dkr_r)
    return outs[:5] + (outs[5][0], outs[6][0])


def _qrows(L):
    return _pick(L, (384, 256, 128))


def _tri_ones(strict):
    r = lax.broadcasted_iota(jnp.int32, (BLK, 2 * BLK), 0)
    c = lax.broadcasted_iota(jnp.int32, (BLK, 2 * BLK), 1)
    tri = (r > c) if strict else (r >= c)
    return jnp.where(tri | (c >= BLK), 1.0, 0.0).astype(_MXU)


def _dot2(x, u):
    hi = x.astype(_MXU)
    lo = (x - hi.astype(f32)).astype(_MXU)
    return jnp.dot(hi, u, preferred_element_type=f32) + jnp.dot(lo, u, preferred_element_type=f32)


def _sb_tile(qx, k, mask, c_ref, x, u_gt):
    z = _dot(qx, k, NT)
    lb = jnp.minimum(z, 0.0) - jnp.log1p(jnp.exp(-jnp.abs(z)))
    lk = lb - z
    if mask is not None:
        lk = jnp.where(mask, lk, 0.0)
    el = _dot2(lk, u_gt)
    c = c_ref[x]
    w = jnp.exp(lb + el[:, 0:BLK] + c)
    if mask is not None:
        w = jnp.where(mask, w, 0.0)
    c_ref[x] = c + el[:, BLK:2 * BLK]
    return w, lb


def _sb_sweep(i, r, tile):
    n_t = r * (i + 1)
    lax.fori_loop(0, r, lambda jj, c: tile(n_t - 1 - jj, True) or c, 0)
    n_bulk = jnp.maximum(r * i - 1, 0)

    def two(jj, c):
        tile(r * i - 1 - 2 * jj, False)
        tile(r * i - 2 - 2 * jj, False)
        return c

    lax.fori_loop(0, n_bulk // 2, two, 0)

    @pl.when(n_bulk % 2 == 1)
    def _():
        tile(1, False)

    @pl.when(i > 0)
    def _():
        tile(0, True)


def _sb_mask(i, j, qb):
    row = i * qb + lax.broadcasted_iota(jnp.int32, (qb, BLK), 0)
    col = j * BLK + lax.broadcasted_iota(jnp.int32, (qb, BLK), 1)
    return (col < row) & (col >= N_PAD)


def _first_last(n0, n1):
    p, i = pl.program_id(0), pl.program_id(1)
    return (p == 0) & (i == 0), (p == n0 - 1) & (i == n1 - 1)


def _sb_fwd(sb, bg=None):
    L = sb.shape[0]
    qb = _qrows(L)
    nq, r = L // qb, qb // BLK

    def body(*refs):
        if bg is None:
            q_ref, k_ref, v_ref, o_ref, acc_ref, c_ref = refs
        else:
            q_ref, k_ref, v_ref, x_ref, o_ref, g_ref, acc_ref, c_ref = refs[:8]
            start, wait = _xchg_ops(x_ref, g_ref, *refs[8:], False)
            first, last = _first_last(4, nq)
            pl.when(first)(start)
        i = pl.program_id(1)
        m_a = lax.broadcasted_iota(jnp.int32, (1, BLK), 1) < 64
        q = q_ref[...]
        zq = jnp.zeros_like(q)
        qs = (jnp.where(m_a, q, zq), jnp.where(m_a, zq, q))
        u_gt = _tri_ones(True)
        acc_ref[...] = jnp.zeros_like(acc_ref)
        c_ref[...] = jnp.zeros_like(c_ref)

        def tile(j, masked):
            off = j * BLK if isinstance(j, int) else pl.multiple_of(j * BLK, BLK)
            k = k_ref[pl.ds(off, BLK), :]
            v = v_ref[pl.ds(off, BLK), :]
            mask = _sb_mask(i, j, qb) if masked else None
            for x in range(2):
                w, _ = _sb_tile(qs[x], k, mask, c_ref, x, u_gt)
                acc_ref[x] += _dot(w, v)

        _sb_sweep(i, r, tile)
        o_ref[...] = jnp.where(m_a, acc_ref[0], acc_ref[1])
        if bg is not None:
            pl.when(last)(wait)

    in_specs = [pl.BlockSpec((qb, 128), lambda p, i: (i, p)),
                pl.BlockSpec((L, 128), lambda p, i: (0, 4 + p)),
                pl.BlockSpec((L, 128), lambda p, i: (0, 8 + p))]
    o_spec = pl.BlockSpec((qb, 128), lambda p, i: (i, p))
    o_shape = jax.ShapeDtypeStruct((L, 512), f32)
    scratch = [pltpu.VMEM((2, qb, BLK), f32), pltpu.VMEM((2, qb, BLK), f32)]
    if bg is None:
        return pl.pallas_call(body, name="sb_fwd", grid=(4, nq), in_specs=in_specs, out_specs=o_spec,
                              out_shape=o_shape, scratch_shapes=scratch,
                              compiler_params=_cp(("parallel", "arbitrary")))(sb, sb, sb)
    return pl.pallas_call(body, name="sb_fwd_gather", grid=(4, nq), in_specs=in_specs + [_HBM],
                          out_specs=(o_spec, _HBM), out_shape=(o_shape, _xchg_shape(bg, False)),
                          scratch_shapes=scratch + list(_XCHG_SEMS),
                          compiler_params=_cp(("arbitrary", "arbitrary")))(sb, sb, sb, bg)


def _sb_bwd(dmixed, sb, out_a, bg=None):
    L = sb.shape[0]
    qb = _qrows(L)
    nq, r = L // qb, qb // BLK

    def body(*refs):
        if bg is None:
            do_ref, o_ref, q_ref, k_ref, v_ref, dq_ref, dk_ref, dv_ref, dqa_ref, c_ref, cg_ref, ds_ref = refs
        else:
            do_ref, o_ref, q_ref, k_ref, v_ref, x_ref, dq_ref, dk_ref, dv_ref, g_ref = refs[:10]
            dqa_ref, c_ref, cg_ref, ds_ref = refs[10:14]
            start, wait = _xchg_ops(x_ref, g_ref, *refs[14:], True)
            first, last = _first_last(4, nq)
            pl.when(first)(start)
        i = pl.program_id(1)

        @pl.when(i == 0)
        def _():
            dk_ref[...] = jnp.zeros_like(dk_ref)
            dv_ref[...] = jnp.zeros_like(dv_ref)

        m_a = lax.broadcasted_iota(jnp.int32, (1, BLK), 1) < 64
        q = q_ref[...]
        zq = jnp.zeros_like(q)
        qs = (jnp.where(m_a, q, zq), jnp.where(m_a, zq, q))
        do = do_ref[...]
        zd = jnp.zeros_like(do)
        dos = (jnp.where(m_a, do, zd).astype(_MXU), jnp.where(m_a, zd, do).astype(_MXU))
        prod = do.astype(_MXU).astype(f32) * o_ref[...]
        ds_ref[0] = jnp.broadcast_to(jnp.sum(jnp.where(m_a, prod, 0.0), 1, keepdims=True), (qb, BLK))
        ds_ref[1] = jnp.broadcast_to(jnp.sum(jnp.where(m_a, 0.0, prod), 1, keepdims=True), (qb, BLK))
        u_gt = _tri_ones(True)
        u_ge = _tri_ones(False)
        dqa_ref[...] = jnp.zeros_like(dqa_ref)
        c_ref[...] = jnp.zeros_like(c_ref)
        cg_ref[...] = jnp.zeros_like(cg_ref)

        def tile(j, masked):
            off = j * BLK if isinstance(j, int) else pl.multiple_of(j * BLK, BLK)
            k = k_ref[pl.ds(off, BLK), :]
            v = v_ref[pl.ds(off, BLK), :]
            mask = _sb_mask(i, j, qb) if masked else None
            dk_t = jnp.zeros((BLK, BLK), f32)
            dv_t = jnp.zeros((BLK, BLK), f32)
            for x in range(2):
                w, lb = _sb_tile(qs[x], k, mask, c_ref, x, u_gt)
                wb = w.astype(_MXU)
                gr = wb.astype(f32) * _dot(dos[x], v, NT)
                eg = _dot3(gr, u_ge)
                cg = cg_ref[x]
                suffix = eg[:, 0:BLK] + cg
                cg_ref[x] = cg + eg[:, BLK:2 * BLK]
                dz = gr - jnp.exp(lb) * (gr + ds_ref[x] - suffix)
                if masked:
                    dz = jnp.where(mask, dz, 0.0)
                dz = dz.astype(_MXU)
                dqa_ref[x] += _dot(dz, k)
                dk_t += _dot(dz, qs[x], TN)
                dv_t += _dot(wb, dos[x], TN)
            dk_ref[pl.ds(off, BLK), :] += dk_t
            dv_ref[pl.ds(off, BLK), :] += dv_t

        _sb_sweep(i, r, tile)
        dq_ref[...] = jnp.where(m_a, dqa_ref[0], dqa_ref[1]) * SB_SCALE
        if bg is not None:
            pl.when(last)(wait)

    blk = pl.BlockSpec((qb, 128), lambda p, i: (i, p))
    scr = pltpu.VMEM((2, qb, BLK), f32)
    in_specs = [blk, blk, blk, pl.BlockSpec((L, 128), lambda p, i: (0, 4 + p)),
                pl.BlockSpec((L, 128), lambda p, i: (0, 8 + p))]
    out_specs = (blk, pl.BlockSpec((L, 128), lambda p, i: (0, p)), pl.BlockSpec((L, 128), lambda p, i: (0, p)))
    out_shape = tuple(jax.ShapeDtypeStruct((L, 512), f32) for _ in range(3))
    if bg is None:
        return pl.pallas_call(body, name="sb_bwd", grid=(4, nq), in_specs=in_specs, out_specs=out_specs,
                              out_shape=out_shape, scratch_shapes=[scr, scr, scr, scr],
                              compiler_params=_cp(("parallel", "arbitrary")))(dmixed, out_a, sb, sb, sb)
    return pl.pallas_call(body, name="sb_bwd_scatter", grid=(4, nq), in_specs=in_specs + [_HBM],
                          out_specs=out_specs + (_HBM,), out_shape=out_shape + (_xchg_shape(bg, True),),
                          scratch_shapes=[scr, scr, scr, scr] + list(_XCHG_SEMS),
                          compiler_params=_cp(("arbitrary", "arbitrary")))(dmixed, out_a, sb, sb, sb, bg)


def _mla_mask(i, j, qb):
    row = i * qb + lax.broadcasted_iota(jnp.int32, (qb, BLK), 0)
    col = j * BLK + lax.broadcasted_iota(jnp.int32, (qb, BLK), 1)
    return (col <= row) & ((col >= N_PAD) | (col == row))


def _pair_mask2():
    return (lax.broadcasted_iota(jnp.int32, (1, 256), 1) % 128) < 64


def _mla_q2(qn_ref, qr_ref, mc_ref, ms_ref):
    qr = qr_ref[...]
    qr = qr * mc_ref[...] + _rot(qr, 16) * ms_ref[...]
    q2 = jnp.concatenate([qn_ref[...], qr], axis=1)
    m2 = _pair_mask2()
    z2 = jnp.zeros_like(q2)
    return (jnp.where(m2, q2, z2).astype(_MXU), jnp.where(m2, z2, q2).astype(_MXU))


def _mla_fwd(q, kv, kr2, mc, ms):
    L = q.shape[0]
    qb = _qrows(L)
    nq, r = L // qb, qb // BLK

    def body(qn_ref, qr_ref, mc_ref, ms_ref, kn_ref, v_ref, kr_ref, o_ref, lse_ref, acc_ref, m_ref):
        i = pl.program_id(1)
        m_a = lax.broadcasted_iota(jnp.int32, (1, BLK), 1) < 64
        qs = _mla_q2(qn_ref, qr_ref, mc_ref, ms_ref)
        acc_ref[...] = jnp.zeros_like(acc_ref)
        m_ref[...] = jnp.full(m_ref.shape, -1e30, f32)
        ones = jnp.ones((BLK, BLK), _MXU)

        def tile(j, masked):
            off = j * BLK if isinstance(j, int) else pl.multiple_of(j * BLK, BLK)
            k2 = jnp.concatenate([kn_ref[pl.ds(off, BLK), :], kr_ref[pl.ds(off, BLK), :]], axis=1)
            v1 = jnp.concatenate([v_ref[pl.ds(off, BLK), :], ones], axis=1)
            mask = _mla_mask(i, j, qb) if masked else None
            for x in range(2):
                s = _dot(qs[x], k2, NT) * MLA_SCALE
                if masked:
                    s = jnp.where(mask, s, -1e30)
                m_old = m_ref[x]
                m_new = jnp.maximum(m_old, jnp.max(s, 1, keepdims=True))
                a = jnp.exp(m_old - m_new)
                p = jnp.exp(s - m_new)
                if masked:
                    p = jnp.where(mask, p, 0.0)
                acc_ref[x] = jnp.concatenate([a, a], axis=1) * acc_ref[x] + _dot(p, v1)
                m_ref[x] = m_new

        _sb_sweep(i, r, tile)
        o_ref[...] = jnp.where(m_a, acc_ref[0, :, 0:BLK] / acc_ref[0, :, BLK:2 * BLK],
                               acc_ref[1, :, 0:BLK] / acc_ref[1, :, BLK:2 * BLK])
        for x in range(2):
            lse_ref[0, x] = m_ref[x] + jnp.log(acc_ref[x, :, BLK:2 * BLK])

    blk = lambda cb: pl.BlockSpec((qb, 128), lambda p, i: (i, cb + p))
    full = lambda cb: pl.BlockSpec((L, 128), lambda p, i: (0, cb + p))
    tab = pl.BlockSpec((qb, 128), lambda p, i: (i, 0))
    return pl.pallas_call(
        body, name="mla_fwd", grid=(4, nq),
        in_specs=[blk(0), blk(4), tab, tab, full(0), full(4), pl.BlockSpec((L, 128), lambda p, i: (0, 0))],
        out_specs=(blk(0), pl.BlockSpec((1, 2, qb, 128), lambda p, i: (p, 0, i, 0))),
        out_shape=(jax.ShapeDtypeStruct((L, 512), f32), jax.ShapeDtypeStruct((4, 2, L, 128), f32)),
        scratch_shapes=[pltpu.VMEM((2, qb, 2 * BLK), f32), pltpu.VMEM((2, qb, BLK), f32)],
        compiler_params=_cp(("parallel", "arbitrary")))(q, q, mc, ms, kv, kv, kr2)


def _mla_bwd(dmixed, q, kv, kr2, out_b, lse, mc, ms):
    L = q.shape[0]
    qb = _qrows(L)
    nq, r = L // qb, qb // BLK

    def body(do_ref, o_ref, lse_ref, qn_ref, qr_ref, mc_ref, ms_ref, kn_ref, v_ref, kr_ref,
             dqn_ref, dqr_ref, dkn_ref, dv_ref, dkr_ref, dqa_ref, ds_ref):
        i = pl.program_id(1)

        @pl.when(i == 0)
        def _():
            dkn_ref[...] = jnp.zeros_like(dkn_ref)
            dv_ref[...] = jnp.zeros_like(dv_ref)
            dkr_ref[...] = jnp.zeros_like(dkr_ref)

        m_a = lax.broadcasted_iota(jnp.int32, (1, BLK), 1) < 64
        qs = _mla_q2(qn_ref, qr_ref, mc_ref, ms_ref)
        do = do_ref[...]
        zd = jnp.zeros_like(do)
        dos = (jnp.where(m_a, do, zd).astype(_MXU), jnp.where(m_a, zd, do).astype(_MXU))
        prod = do * o_ref[...]
        ds_ref[0] = jnp.broadcast_to(jnp.sum(jnp.where(m_a, prod, 0.0), 1, keepdims=True), (qb, BLK))
        ds_ref[1] = jnp.broadcast_to(jnp.sum(jnp.where(m_a, 0.0, prod), 1, keepdims=True), (qb, BLK))
        dqa_ref[...] = jnp.zeros_like(dqa_ref)

        def tile(j, masked):
            off = j * BLK if isinstance(j, int) else pl.multiple_of(j * BLK, BLK)
            k2 = jnp.concatenate([kn_ref[pl.ds(off, BLK), :], kr_ref[pl.ds(off, BLK), :]], axis=1)
            v = v_ref[pl.ds(off, BLK), :]
            mask = _mla_mask(i, j, qb) if masked else None
            dk_t = jnp.zeros((BLK, 256), f32)
            dv_t = jnp.zeros((BLK, BLK), f32)
            for x in range(2):
                s = _dot(qs[x], k2, NT) * MLA_SCALE
                if masked:
                    p = jnp.where(mask, jnp.exp(jnp.where(mask, s, 0.0) - lse_ref[0, x]), 0.0)
                else:
                    p = jnp.exp(s - lse_ref[0, x])
                pb = p.astype(_MXU)
                ds = (p * (_dot(dos[x], v, NT) - ds_ref[x]) * MLA_SCALE).astype(_MXU)
                dqa_ref[x] += _dot(ds, k2)
                dk_t += _dot(ds, qs[x], TN)
                dv_t += _dot(pb, dos[x], TN)
            dkn_ref[pl.ds(off, BLK), :] += dk_t[:, 0:128]
            dkr_ref[0, pl.ds(off, BLK), :] += dk_t[:, 128:256]
            dv_ref[pl.ds(off, BLK), :] += dv_t

        _sb_sweep(i, r, tile)
        dq2 = jnp.where(_pair_mask2(), dqa_ref[0], dqa_ref[1])
        dqn_ref[...] = dq2[:, 0:128]
        dy = dq2[:, 128:256]
        dqr_ref[...] = dy * mc_ref[...] - _rot(dy * ms_ref[...], 16)

    blk = lambda cb: pl.BlockSpec((qb, 128), lambda p, i: (i, cb + p))
    full = lambda cb: pl.BlockSpec((L, 128), lambda p, i: (0, cb + p))
    tab = pl.BlockSpec((qb, 128), lambda p, i: (i, 0))
    o512 = jax.ShapeDtypeStruct((L, 512), f32)
    return pl.pallas_call(
        body, name="mla_bwd", grid=(4, nq),
        in_specs=[blk(4), blk(0), pl.BlockSpec((1, 2, qb, 128), lambda p, i: (p, 0, i, 0)), blk(0), blk(4), tab, tab,
                  full(0), full(4), pl.BlockSpec((L, 128), lambda p, i: (0, 0))],
        out_specs=(blk(0), blk(0), full(0), full(0), pl.BlockSpec((1, L, 128), lambda p, i: (p, 0, 0))),
        out_shape=(o512, o512, o512, o512, jax.ShapeDtypeStruct((4, L, 128), f32)),
        scratch_shapes=[pltpu.VMEM((2, qb, 2 * BLK), f32), pltpu.VMEM((2, qb, BLK), f32)],
        compiler_params=_cp(("parallel", "arbitrary")))(dmixed, out_b, lse, q, q, mc, ms, kv, kv, kr2)


def _ret_tables():
    log_g = jnp.log(jnp.array(RET_GAMMA, f32))
    idx = jnp.arange(BLK, dtype=f32)
    diff = idx[:, None] - idx[None, :]
    d_in = jnp.where(diff[None] >= 0, jnp.exp(jnp.maximum(diff, 0.0)[None] * log_g[:, None, None]), 0.0)
    q_dec = jnp.exp((idx[None, :] + 1.0) * log_g[:, None])
    k_dec = jnp.exp((BLK - 1.0 - idx[None, :]) * log_g[:, None])
    c_dec = jnp.exp(BLK * log_g)
    bc = lambda a: jnp.broadcast_to(a[:, :, None], (4, BLK, BLK))
    return d_in, bc(q_dec), bc(k_dec), jnp.broadcast_to(c_dec[:, None, None], (4, 8, BLK))


def _head_mask(x):
    lane = lax.broadcasted_iota(jnp.int32, (1, BLK), 1)
    return (lane < 64) if x == 0 else (lane >= 64)


def _ret_fwd(rqk, rv, proj, rtabs):
    L = rqk.shape[0]
    n = L // BLK
    d_in, q_dec, k_dec, c_dec = rtabs

    def body(q_ref, k_ref, v_ref, g_ref, din_ref, qd_ref, kd_ref, cd_ref, y_ref, o_ref, st_ref, s_scr):
        @pl.when(pl.program_id(1) == 0)
        def _():
            s_scr[...] = jnp.zeros_like(s_scr)

        q = q_ref[...]
        k = k_ref[...]
        zq = jnp.zeros_like(q)
        for x in range(2):
            hm = _head_mask(x)
            sl = slice(x * 128, (x + 1) * 128)
            qm = jnp.where(hm, q, zq)
            km = jnp.where(hm, k, zq)
            v = v_ref[:, sl]
            s_in = s_scr[x]
            st_ref[0, 0, x] = s_in
            inner = _dot(qm, km, NT) * din_ref[x]
            y = _dot(inner, v) + _dot(qm, s_in) * qd_ref[x]
            s_scr[x] = s_in * cd_ref[x, 0:1, :] + _dot(km.astype(f32) * kd_ref[x], v, TN)
            y_ref[:, sl] = y
            mu = jnp.mean(y, -1, keepdims=True)
            yc = y - mu
            yn = yc * lax.rsqrt(jnp.mean(jnp.square(yc), -1, keepdims=True) + LN_EPS)
            g = g_ref[:, sl]
            o_ref[:, sl] = g * jax.nn.sigmoid(g) * yn

    tab = pl.BlockSpec((2, BLK, BLK), lambda p, i: (p, 0, 0))
    return pl.pallas_call(
        body, name="ret_fwd", grid=(2, n),
        in_specs=[pl.BlockSpec((BLK, 128), lambda p, i: (i, p)), pl.BlockSpec((BLK, 128), lambda p, i: (i, 2 + p)),
                  pl.BlockSpec((BLK, 256), lambda p, i: (i, p)),
                  pl.BlockSpec((BLK, 256), lambda p, i: (i, C_RG // 256 + p)),
                  tab, tab, tab, pl.BlockSpec((2, 8, BLK), lambda p, i: (p, 0, 0))],
        out_specs=(pl.BlockSpec((BLK, 256), lambda p, i: (i, p)), pl.BlockSpec((BLK, 256), lambda p, i: (i, p)),
                   pl.BlockSpec((1, 1, 2, BLK, BLK), lambda p, i: (p, i, 0, 0, 0))),
        out_shape=(jax.ShapeDtypeStruct((L, 512), f32), jax.ShapeDtypeStruct((L, 512), f32),
                   jax.ShapeDtypeStruct((2, n, 2, BLK, BLK), f32)),
        scratch_shapes=[pltpu.VMEM((2, BLK, BLK), f32)],
        compiler_params=_cp(("parallel", "arbitrary")))(rqk, rqk, rv, proj, d_in, q_dec, k_dec, c_dec)


def _ret_bwd(dmixed, rqk, rv, proj, y, states, rtabs):
    L = rqk.shape[0]
    n = L // BLK
    d_in, q_dec, k_dec, c_dec = rtabs

    def body(do_ref, q_ref, k_ref, v_ref, g_ref, y_ref, st_ref, din_ref, qd_ref, kd_ref, cd_ref,
             dq_ref, dk_ref, dv_ref, dg_ref, ds_scr):
        @pl.when(pl.program_id(1) == 0)
        def _():
            ds_scr[...] = jnp.zeros_like(ds_scr)

        q = q_ref[...]
        k = k_ref[...]
        zq = jnp.zeros_like(q)
        dq_acc = jnp.zeros((BLK, BLK), f32)
        dk_acc = jnp.zeros((BLK, BLK), f32)
        for x in range(2):
            hm = _head_mask(x)
            sl = slice(x * 128, (x + 1) * 128)
            qm = jnp.where(hm, q, zq)
            km = jnp.where(hm, k, zq)
            v = v_ref[:, sl]
            yv = y_ref[:, sl]
            g = g_ref[:, sl]
            do = do_ref[:, sl]
            mu = jnp.mean(yv, -1, keepdims=True)
            yc = yv - mu
            rstd = lax.rsqrt(jnp.mean(jnp.square(yc), -1, keepdims=True) + LN_EPS)
            yn = yc * rstd
            sg = jax.nn.sigmoid(g)
            dg_ref[:, sl] = do * yn * sg * (1.0 + g * (1.0 - sg))
            dyn = do * g * sg
            dy = rstd * (dyn - jnp.mean(dyn, -1, keepdims=True) - yn * jnp.mean(dyn * yn, -1, keepdims=True))
            s_in = st_ref[0, 0, x]
            ds_out = ds_scr[x]
            kd = km.astype(f32) * kd_ref[x]
            a = _dot(qm, km, NT) * din_ref[x]
            da = _dot(dy, v, NT) * din_ref[x]
            dyq = dy * qd_ref[x]
            dq_acc += _dot(da, km) + _dot(dyq, s_in, NT)
            dk_acc += _dot(da, qm, TN) + _dot(v, ds_out, NT) * kd_ref[x]
            dv_ref[:, sl] = _dot(a, dy, TN) + _dot(kd, ds_out)
            ds_scr[x] = ds_out * cd_ref[x, 0:1, :] + _dot(qm, dyq, TN)
        dq_ref[...] = dq_acc
        dk_ref[...] = dk_acc

    rev = lambda w, cb: pl.BlockSpec((BLK, w), lambda p, i: (n - 1 - i, cb + p))
    tab = pl.BlockSpec((2, BLK, BLK), lambda p, i: (p, 0, 0))
    return pl.pallas_call(
        body, name="ret_bwd", grid=(2, n),
        in_specs=[rev(256, 4), rev(128, 0), rev(128, 2), rev(256, 0), rev(256, C_RG // 256), rev(256, 0),
                  pl.BlockSpec((1, 1, 2, BLK, BLK), lambda p, i: (p, n - 1 - i, 0, 0, 0)),
                  tab, tab, tab, pl.BlockSpec((2, 8, BLK), lambda p, i: (p, 0, 0))],
        out_specs=(rev(128, 0), rev(128, 0), rev(256, 0), rev(256, 0)),
        out_shape=(jax.ShapeDtypeStruct((L, 256), f32), jax.ShapeDtypeStruct((L, 256), f32),
                   jax.ShapeDtypeStruct((L, 512), f32), jax.ShapeDtypeStruct((L, 512), f32)),
        scratch_shapes=[pltpu.VMEM((2, BLK, BLK), f32)],
        compiler_params=_cp(("parallel", "arbitrary")))(dmixed, rqk, rqk, rv, proj, y, states, d_in, q_dec, k_dec, c_dec)


def _loss_head(h, target):
    L = h.shape[0]
    n = L // BLK

    def body(h_ref, t_ref, dy_ref, l_ref):
        i = pl.program_id(0)

        @pl.when(i == 0)
        def _():
            dy_ref[...] = jnp.zeros_like(dy_ref)
            l_ref[...] = jnp.zeros_like(l_ref)

        @pl.when(i > 0)
        def _():
            err = h_ref[...] - t_ref[...]
            dy_ref[...] = err * (1.0 / D)
            sq = jnp.sum(jnp.sum(jnp.square(err), 1, keepdims=True), 0, keepdims=True)
            l_ref[...] += (0.5 / D) * sq

    return pl.pallas_call(
        body, name="loss_head", grid=(n,),
        in_specs=[pl.BlockSpec((BLK, D), lambda i: (i, 0)),
                  pl.BlockSpec((BLK, D), lambda i: (jnp.maximum(i - 1, 0), 0))],
        out_specs=(pl.BlockSpec((BLK, D), lambda i: (i, 0)), pl.BlockSpec((8, 128), lambda i: (0, 0))),
        out_shape=(jax.ShapeDtypeStruct((L, D), f32), jax.ShapeDtypeStruct((8, 128), f32)),
        compiler_params=_cp(("arbitrary",)))(h, target)


def _adam_math(w, g, m, v):
    m = ADAM_B1 * m + (1.0 - ADAM_B1) * g
    v = ADAM_B2 * v + (1.0 - ADAM_B2) * jnp.square(g)
    m_hat = m / (1.0 - ADAM_B1 ** ADAM_STEP)
    v_hat = v / (1.0 - ADAM_B2 ** ADAM_STEP)
    delta = -ADAM_LR * (m_hat / (jnp.sqrt(v_hat) + ADAM_EPS) + ADAM_WD * w)
    return delta, m, v


def _adamw(parts, w, m, v, name):
    R, C = w.shape
    tr = _pick(R, (240, 192, 144, 96, 64, 48, 32, 16, 8))
    row = pl.BlockSpec((tr, C), lambda i: (i, 0))

    def body(p_ref, w_ref, m_ref, v_ref, g_ref, d_ref, nm_ref, nv_ref):
        g = p_ref[0].astype(f32)
        for k in range(1, N_DEV):
            g = g + p_ref[k].astype(f32)
        d, nm, nv = _adam_math(w_ref[...], g, m_ref[...], v_ref[...])
        g_ref[...] = g
        d_ref[...] = d
        nm_ref[...] = nm
        nv_ref[...] = nv

    o = jax.ShapeDtypeStruct((R, C), f32)
    return pl.pallas_call(
        body, name=name, grid=(R // tr,),
        in_specs=[pl.BlockSpec((N_DEV, tr, C), lambda i: (0, i, 0)), row, row, row],
        out_specs=(row, row, row, row), out_shape=(o, o, o, o),
        compiler_params=_cp(("parallel",)))(parts, w, m, v)


_XCHG_SEMS = [pltpu.SemaphoreType.DMA((N_DEV - 1,)), pltpu.SemaphoreType.DMA((N_DEV - 1,)), pltpu.SemaphoreType.DMA]
_HBM = pl.BlockSpec(memory_space=pltpu.HBM)


def _xchg_shape(x, all_to_all):
    return jax.ShapeDtypeStruct((N_DEV,) + tuple(x.shape[1:] if all_to_all else x.shape), x.dtype)


def _xchg_ops(x_ref, o_ref, send_sems, recv_sems, local_sem, all_to_all):
    mx, my, mc = lax.axis_index("x"), lax.axis_index("y"), lax.axis_index("c")
    me = 4 * mx + 2 * my + mc

    def peer(k):
        px = (1 - mx) if k & 4 else mx
        py = (1 - my) if k & 2 else my
        pc = (1 - mc) if k & 1 else mc
        return (px, py, pc), 4 * px + 2 * py + pc

    def copy(k):
        dev, idx = peer(k)
        src = x_ref.at[idx] if all_to_all else x_ref
        return pltpu.make_async_remote_copy(
            src_ref=src, dst_ref=o_ref.at[me], send_sem=send_sems.at[k - 1], recv_sem=recv_sems.at[k - 1],
            device_id=dev, device_id_type=pl.DeviceIdType.MESH)

    def start():
        pltpu.make_async_copy(x_ref.at[me] if all_to_all else x_ref, o_ref.at[me], local_sem).start()
        for k in range(1, N_DEV):
            copy(k).start()

    def wait():
        for k in range(1, N_DEV):
            dev, idx = peer(k)
            pltpu.make_async_remote_copy(
                src_ref=o_ref.at[idx], dst_ref=o_ref.at[idx], send_sem=send_sems.at[k - 1],
                recv_sem=recv_sems.at[k - 1], device_id=dev, device_id_type=pl.DeviceIdType.MESH).wait_recv()
        for k in range(1, N_DEV):
            copy(k).wait_send()
        pltpu.make_async_copy(x_ref.at[me] if all_to_all else x_ref, o_ref.at[me], local_sem).wait()

    return start, wait


def _exchange(x, all_to_all, name):
    def body(x_ref, o_ref, send_sems, recv_sems, local_sem):
        start, wait = _xchg_ops(x_ref, o_ref, send_sems, recv_sems, local_sem, all_to_all)
        start()
        wait()

    return pl.pallas_call(body, name=name, in_specs=[_HBM], out_specs=_HBM, out_shape=_xchg_shape(x, all_to_all),
                          scratch_shapes=list(_XCHG_SEMS))(x)


WC = 512
LAYER_ROWS = tuple(n // WC for n in (1024 * 468, 384 * 96, 256 * 128, 192 * 1024, 1024 * 512, 512 * 1024))


def _pack_layer(ws, l):
    return jnp.concatenate([w[l].reshape(-1, WC) for w in ws], axis=0)


def _unpack_layer(gathered):
    offs = np.cumsum((0,) + LAYER_ROWS)
    part = lambda k: gathered[:, offs[k]:offs[k + 1]]
    w_in = part(0).reshape(8, 1024, 468).transpose(1, 0, 2).reshape(1024, 3744)
    z32 = jnp.zeros((1024, 32), w_in.dtype)
    kr = w_in[:, 2176:2208]
    w_in = jnp.concatenate([w_in[:, 0:1536], w_in[:, 1920:2176], w_in[:, 2208:3744], w_in[:, 1536:1920],
                            kr, z32, kr, z32], axis=1)
    w_uq = part(1).reshape(8, 384, 96).transpose(1, 0, 2)
    rope = jnp.concatenate([w_uq[..., 64:96], jnp.zeros((384, 8, 32), w_uq.dtype)], axis=-1)
    w_uq = jnp.concatenate([w_uq[..., 0:64].reshape(384, 512), rope.reshape(384, 512)], axis=1)
    w_ukv = part(2).reshape(8, 256, 128).transpose(1, 0, 2)
    w_ukv = jnp.concatenate([w_ukv[..., 0:64].reshape(256, 512), w_ukv[..., 64:128].reshape(256, 512)], axis=1)
    w_out = part(3).reshape(1536, 1024)
    w_ff1 = part(4).reshape(8, 1024, 512).transpose(1, 0, 2).reshape(1024, 4096)
    w_ff2 = part(5).reshape(4096, 1024)
    return w_in, w_uq, w_ukv, w_out, w_ff1, w_ff2


def _pack_layer_grads(g_in, g_uq, g_ukv, g_out, g_ff1, g_ff2):
    kr = g_in[:, C_KR:C_KR + 32] + g_in[:, C_KR + 64:C_KR + 96]
    g_in = jnp.concatenate([g_in[:, 0:1536], g_in[:, C_CQ:C_CQ + 384], g_in[:, C_CKV:C_CKV + 256], kr,
                            g_in[:, C_RQ:C_CQ]], axis=1)
    g_in = g_in.reshape(1024, 8, 468).transpose(1, 0, 2)
    g_uq = jnp.concatenate([g_uq[:, 0:512].reshape(384, 8, 64), g_uq[:, 512:1024].reshape(384, 8, 64)[..., 0:32]],
                           axis=-1).transpose(1, 0, 2)
    g_ukv = jnp.concatenate([g_ukv[:, 0:512].reshape(256, 8, 64), g_ukv[:, 512:1024].reshape(256, 8, 64)],
                            axis=-1).transpose(1, 0, 2)
    g_ff1 = g_ff1.reshape(1024, 8, 512).transpose(1, 0, 2)
    return jnp.concatenate([g.reshape(8, -1, WC) for g in (g_in, g_uq, g_ukv, g_out, g_ff1, g_ff2)], axis=1)


def _rope_tables(L):
    pos = (jnp.arange(L) - N_PAD).astype(f32)

    def cs(half):
        inv = ROPE_THETA ** (-jnp.arange(half, dtype=f32) / half)
        ang = pos[:, None] * inv[None, :]
        return jnp.cos(ang), jnp.sin(ang)

    c, s = cs(32)
    rc, rs = jnp.tile(c, (1, 8)), jnp.tile(s, (1, 8))
    c, s = cs(16)
    z = jnp.zeros((L, 32), f32)
    mc, ms = jnp.concatenate([c, c, z, c, c, z], 1), jnp.concatenate([s, s, z, s, s, z], 1)
    return rc, rs, mc, ms


def _layer_fwd(h, wl, gq, gkv, g1, b1, g2, b2, tabs, rtabs, next_shard):
    w_in, w_uq, w_ukv, w_out, w_ff1, w_ff2 = wl
    proj = _mm(h, w_in, "nn", "mm_in")
    sb, cqn, ckvn, rqk, rv, kr2 = _prep_fwd(proj, gq, gkv, tabs)
    q = _mm(cqn, w_uq, "nn", "mm_uq")
    kv = _mm(ckvn, w_ukv, "nn", "mm_ukv", out_dtype=bf16)
    if next_shard is None:
        out_a, gathered = _sb_fwd(sb), None
    else:
        out_a, gathered = _sb_fwd(sb, next_shard)
    out_b, lse = _mla_fwd(q, kv, kr2, tabs[2], tabs[3])
    y, out_c, states = _ret_fwd(rqk, rv, proj, rtabs)
    mixed = jnp.concatenate([out_a, out_b, out_c], axis=1)
    mix = _mm(mixed, w_out, "nn", "mm_out")
    h1 = _ln_fwd(h, mix, g1, b1, DN_ALPHA, "ln_fwd")
    u, a = _mm(h1, w_ff1, "nn", "mm_ff1", epi="relu2")
    ff = _mm(a, w_ff2, "nn", "mm_ff2")
    h2 = _ln_fwd(h1, ff, g2, b2, DN_ALPHA, "ln_fwd")
    saved = (h, proj, sb, cqn, ckvn, rqk, rv, kr2, q, kv, out_a, out_b, lse, y, states, mixed, mix, h1, u, a, ff)
    return h2, saved, gathered


def _layer_bwd(dh2, saved, wl, gq, gkv, g1, g2, tabs, rtabs, grads_above):
    w_in, w_uq, w_ukv, w_out, w_ff1, w_ff2 = wl
    h, proj, sb, cqn, ckvn, rqk, rv, kr2, q, kv, out_a, out_b, lse, y, states, mixed, mix, h1, u, a, ff = saved
    dz2, dg2, db2 = _ln_bwd(dh2, h1, ff, g2, DN_ALPHA, "ln_bwd")
    du = _mm(dz2, w_ff2, "nt", "mm_dff2", epi="mul_relu", extra=u)
    gw_ff2 = _mm(a, dz2, "tn", "mm_gff2")
    dh1 = _mm(du, w_ff1, "nt", "mm_dff1", epi="add", extra=dz2, alpha=DN_ALPHA)
    gw_ff1 = _mm(h1, du, "tn", "mm_gff1")
    dz1, dg1, db1 = _ln_bwd(dh1, h, mix, g1, DN_ALPHA, "ln_bwd")
    dmixed = _mm(dz1, w_out, "nt", "mm_dout")
    gw_out = _mm(mixed, dz1, "tn", "mm_gout")
    if grads_above is None:
        (dsq, dsk, dsv), parts_above = _sb_bwd(dmixed, sb, out_a), None
    else:
        dsq, dsk, dsv, parts_above = _sb_bwd(dmixed, sb, out_a, grads_above)
    dqn, dqr, dkn, dv, dkr_p = _mla_bwd(dmixed, q, kv, kr2, out_b, lse, tabs[2], tabs[3])
    dq = jnp.concatenate([dqn, dqr], axis=1)
    dkv = jnp.concatenate([dkn, dv], axis=1)
    dcqn = _mm(dq, w_uq, "nt", "mm_duq")
    gw_uq = _mm(cqn, dq, "tn", "mm_guq")
    dckvn = _mm(dkv, w_ukv, "nt", "mm_dukv")
    gw_ukv = _mm(ckvn, dkv, "tn", "mm_gukv")
    drq_r, drk_r, drv, drg = _ret_bwd(dmixed, rqk, rv, proj, y, states, rtabs)
    dkr_r = dkr_p[0] + dkr_p[1] + dkr_p[2] + dkr_p[3]
    dcq, dckv, drq, drk, dkr2, dgq, dgkv = _prep_bwd(proj, gq, gkv, tabs, dcqn, dckvn, drq_r, drk_r, dkr_r)
    dproj = jnp.concatenate([dsq, dsk, dsv, dckv, drq, drk, drv, drg, dcq, dkr2], axis=1)
    dh = _mm(dproj, w_in, "nt", "mm_din", epi="add", extra=dz1, alpha=DN_ALPHA)
    gw_in = _mm(h, dproj, "tn", "mm_gin")
    return dh, (gw_in, gw_uq, gw_ukv, gw_out, gw_ff1, gw_ff2), (dgq, dgkv, dg1, db1, dg2, db2), parts_above


def kernel(x, meta_tokens, ln_emb_g, ln_emb_b, w_in, mla_q_norm, mla_kv_norm, w_uq, w_ukv, w_out, ln1_g, ln1_b, w_ff1, w_ff2, ln2_g, ln2_b, loss_target, m_meta_tokens, m_ln_emb_g, m_ln_emb_b, m_w_in, m_mla_q_norm, m_mla_kv_norm, m_w_uq, m_w_ukv, m_w_out, m_ln1_g, m_ln1_b, m_w_ff1, m_w_ff2, m_ln2_g, m_ln2_b, v_meta_tokens, v_ln_emb_g, v_ln_emb_b, v_w_in, v_mla_q_norm, v_mla_kv_norm, v_w_uq, v_w_ukv, v_w_out, v_ln1_g, v_ln1_b, v_w_ff1, v_w_ff2, v_ln2_g, v_ln2_b):
    depth = w_in.shape[0]
    S = x.shape[1]
    L = S + BLK
    me = 4 * lax.axis_index("x") + 2 * lax.axis_index("y") + lax.axis_index("c")
    big = (w_in, w_uq, w_ukv, w_out, w_ff1, w_ff2)
    big_m = (m_w_in, m_w_uq, m_w_ukv, m_w_out, m_w_ff1, m_w_ff2)
    big_v = (v_w_in, v_w_uq, v_w_ukv, v_w_out, v_w_ff1, v_w_ff2)
    small = (ln_emb_g, ln_emb_b, mla_q_norm, mla_kv_norm, ln1_g, ln1_b, ln2_g, ln2_b)
    small_m = (m_ln_emb_g, m_ln_emb_b, m_mla_q_norm, m_mla_kv_norm, m_ln1_g, m_ln1_b, m_ln2_g, m_ln2_b)
    small_v = (v_ln_emb_g, v_ln_emb_b, v_mla_q_norm, v_mla_kv_norm, v_ln1_g, v_ln1_b, v_ln2_g, v_ln2_b)

    shards = [_pack_layer(big, l) for l in range(depth)]
    gathered = _exchange(shards[0].astype(bf16), False, "gather_w0")
    meta_all = _exchange(meta_tokens, False, "gather_meta")
    meta_full = meta_all.transpose(1, 0, 2).reshape(N_META, D)

    tabs = _rope_tables(L)
    rtabs = _ret_tables()

    hcat = jnp.concatenate([jnp.zeros((N_PAD, D), f32), meta_full, x[0]], axis=0)
    h = _ln_fwd(hcat, None, ln_emb_g, ln_emb_b, 1.0, "ln_emb_fwd")
    saved, full = [], []
    for l in range(depth):
        full.append(_unpack_layer(gathered))
        nxt = shards[l + 1].astype(bf16) if l + 1 < depth else None
        h, sv, gathered = _layer_fwd(h, full[l], mla_q_norm[l], mla_kv_norm[l], ln1_g[l], ln1_b[l], ln2_g[l],
                                     ln2_b[l], tabs, rtabs, nxt)
        saved.append(sv)

    dh, loss_part = _loss_head(h, loss_target[0])
    gsmall, parts, pending = [None] * depth, [None] * depth, None
    for l in reversed(range(depth)):
        dh, gbig, gsmall[l], got = _layer_bwd(dh, saved[l], full[l], mla_q_norm[l], mla_kv_norm[l], ln1_g[l],
                                              ln2_g[l], tabs, rtabs, pending)
        if pending is not None:
            parts[l + 1] = got
        pending = _pack_layer_grads(*gbig).astype(bf16)
    parts[0] = _exchange(pending, True, "scatter_g0")
    dz0, dg_emb, db_emb = _ln_bwd(dh, hcat, None, ln_emb_g, 1.0, "ln_emb_bwd")
    grad_x = dz0[BLK:][None]
    dmeta = dz0[N_PAD:BLK]

    adam = [_adamw(parts[l], shards[l], _pack_layer(big_m, l), _pack_layer(big_v, l), "adamw_big")
            for l in range(depth)]

    st = lambda k: jnp.stack([gsmall[l][k] for l in range(depth)])
    g_small = (dg_emb, db_emb, st(0), st(1), st(2), st(3), st(4), st(5))
    n_small = sum(int(np.prod(a.shape)) for a in small)
    flat = jnp.concatenate([a.reshape(-1) for a in g_small] + [dmeta.reshape(-1), loss_part[0, 0:1]])
    rows = -(-(flat.shape[0]) // (8 * D)) * 8
    pad = rows * D - flat.shape[0]
    flat = jnp.concatenate([flat, jnp.zeros((pad,), f32)]).reshape(rows, D)
    parts_s = _exchange(flat, False, "gather_small")

    def pack_small(arrs, meta_shard):
        col = jnp.zeros((N_META, D), f32)
        col = lax.dynamic_update_slice(col, meta_shard, (0, me * 128))
        fl = jnp.concatenate([a.reshape(-1) for a in arrs] + [col.reshape(-1), jnp.zeros((1 + pad,), f32)])
        return fl.reshape(rows, D)

    g_s, d_s, m_s, v_s = _adamw(parts_s, pack_small(small, meta_tokens), pack_small(small_m, m_meta_tokens),
                                pack_small(small_v, v_meta_tokens), "adamw_small")
    loss = g_s.reshape(-1)[n_small + N_META * D]

    def unpack_big(which):
        outs, off = [], 0
        for w, r in zip(big, LAYER_ROWS):
            outs.append(jnp.stack([adam[l][which][off:off + r].reshape(w.shape[1:]) for l in range(depth)]))
            off += r
        return outs

    def unpack_small(flat_rows):
        fl = flat_rows.reshape(-1)
        outs, off = [], 0
        for a in small:
            n = int(np.prod(a.shape))
            outs.append(fl[off:off + n].reshape(a.shape))
            off += n
        meta = lax.dynamic_slice(fl[off:off + N_META * D].reshape(N_META, D), (0, me * 128), (N_META, 128))
        return meta, outs

    def assemble(which, small_rows_arr):
        b = unpack_big(which)
        meta, s = unpack_small(small_rows_arr)
        return [meta, s[0], s[1], b[0], s[2], s[3], b[1], b[2], b[3], s[4], s[5], b[4], b[5], s[6], s[7]]

    return (loss, grad_x, *assemble(0, g_s), *assemble(1, d_s), *assemble(2, m_s), *assemble(3, v_s))
```

```python
import functools
import math

import numpy as np
import jax
import jax.numpy as jnp
from jax import lax
from jax.experimental import pallas as pl
from jax.experimental.pallas import tpu as pltpu

f32 = jnp.float32
bf16 = jnp.bfloat16
_MXU = jnp.bfloat16

BLK = 128
N_META = 16
N_PAD = 112
D = 1024
N_DEV = 8
LN_EPS = 1e-5
DEPTH = 4
DN_ALPHA = (2 * DEPTH) ** 0.25
ROPE_THETA = 10000.0
MLA_SCALE = (64 + 32) ** -0.5
SB_SCALE = 0.125
RET_SCALE = 0.125
RET_GAMMA = tuple(1.0 - 2.0 ** (-5 - h) for h in range(4))

ADAM_LR, ADAM_B1, ADAM_B2, ADAM_EPS, ADAM_WD, ADAM_STEP = 0.001, 0.9, 0.999, 1e-08, 0.01, 10

C_SBQ, C_SBK, C_SBV, C_CKV, C_RQ, C_RK, C_RV, C_RG, C_CQ, C_KR, N_INP = (
    0, 512, 1024, 1536, 1792, 2048, 2304, 2816, 3328, 3712, 3840)

VMEM_LIMIT = 56 * 1024 * 1024


def _cp(sem):
    return pltpu.CompilerParams(dimension_semantics=sem, vmem_limit_bytes=VMEM_LIMIT)


def _pick(n, cands):
    for c in cands:
        if n % c == 0:
            return c
    return n


def _dot(a, b, dims=(((1,), (0,)), ((), ()))):
    return lax.dot_general(a.astype(_MXU), b.astype(_MXU), dims, preferred_element_type=f32)


NT = (((1,), (1,)), ((), ()))
TN = (((0,), (0,)), ((), ()))


def _dot3(x, u):
    hi = x.astype(_MXU)
    r1 = x - hi.astype(f32)
    mid = r1.astype(_MXU)
    lo = (r1 - mid.astype(f32)).astype(_MXU)
    return (jnp.dot(hi, u, preferred_element_type=f32) + jnp.dot(mid, u, preferred_element_type=f32)
            + jnp.dot(lo, u, preferred_element_type=f32))


def _rot(x, half):
    lane = lax.broadcasted_iota(jnp.int32, x.shape, 1)
    first = (lane % 64) < half
    return jnp.where(first, -pltpu.roll(x, 128 - half, 1), pltpu.roll(x, half, 1))


def _mm(a, b, mode, name, epi=None, extra=None, alpha=1.0, out_dtype=f32):
    if mode == "nn":
        (M, K), N = a.shape, b.shape[1]
    elif mode == "nt":
        (M, K), N = a.shape, b.shape[0]
    else:
        (K, M), N = a.shape, b.shape[1]
    tm = _pick(M, (1408, 1024, 768, 512, 384, 256, 128))
    tn = _pick(N, ((1920,) if mode == "tn" else ()) + (1024, 768, 512, 384, 256, 128))
    tk = _pick(K, (1024, 768, 512, 384, 256, 128))
    nk = K // tk
    if mode == "nn":
        a_spec = pl.BlockSpec((tm, tk), lambda i, j, k: (i, k))
        b_spec = pl.BlockSpec((tk, tn), lambda i, j, k: (k, j))
        dims = (((1,), (0,)), ((), ()))
    elif mode == "nt":
        a_spec = pl.BlockSpec((tm, tk), lambda i, j, k: (i, k))
        b_spec = pl.BlockSpec((tn, tk), lambda i, j, k: (j, k))
        dims = NT
    else:
        a_spec = pl.BlockSpec((tk, tm), lambda i, j, k: (k, i))
        b_spec = pl.BlockSpec((tk, tn), lambda i, j, k: (k, j))
        dims = TN
    o_spec = pl.BlockSpec((tm, tn), lambda i, j, k: (i, j))
    in_specs, args = [a_spec, b_spec], [a, b]
    if extra is not None:
        in_specs.append(o_spec)
        args.append(extra)
    if epi == "relu2":
        out_shape = (jax.ShapeDtypeStruct((M, N), f32), jax.ShapeDtypeStruct((M, N), bf16))
        out_specs = (o_spec, o_spec)
    else:
        out_shape = jax.ShapeDtypeStruct((M, N), out_dtype)
        out_specs = o_spec

    def body(*refs):
        a_ref, b_ref = refs[0], refs[1]
        acc = refs[-1]
        k = pl.program_id(2)

        @pl.when(k == 0)
        def _():
            acc[...] = jnp.zeros_like(acc)

        acc[...] += _dot(a_ref[...], b_ref[...], dims)

        @pl.when(k == nk - 1)
        def _():
            r = acc[...]
            if epi == "relu2":
                refs[2][...] = r
                refs[3][...] = jnp.square(jnp.maximum(r, 0.0)).astype(bf16)
            elif epi == "mul_relu":
                refs[3][...] = r * (2.0 * jnp.maximum(refs[2][...], 0.0))
            elif epi == "add":
                refs[3][...] = r + alpha * refs[2][...]
            else:
                refs[2][...] = r.astype(out_dtype)

    return pl.pallas_call(
        body, name=name, grid=(M // tm, N // tn, nk), in_specs=in_specs, out_specs=out_specs,
        out_shape=out_shape, scratch_shapes=[pltpu.VMEM((tm, tn), f32)],
        compiler_params=_cp(("parallel", "parallel", "arbitrary")))(*args)


def _ln_fwd(h, m, g, b, alpha, name):
    L = h.shape[0]
    tm = _pick(L, (384, 256, 128))
    row = pl.BlockSpec((tm, D), lambda i: (i, 0))
    vec = pl.BlockSpec((1, D), lambda i: (0, 0))

    def body(*refs):
        if m is None:
            h_ref, g_ref, b_ref, o_ref = refs
            z = h_ref[...]
        else:
            h_ref, m_ref, g_ref, b_ref, o_ref = refs
            z = alpha * h_ref[...] + m_ref[...]
        mu = jnp.mean(z, -1, keepdims=True)
        var = jnp.mean(jnp.square(z - mu), -1, keepdims=True)
        o_ref[...] = (z - mu) * lax.rsqrt(var + LN_EPS) * g_ref[...] + b_ref[...]

    args = [h] + ([] if m is None else [m]) + [g.reshape(1, D), b.reshape(1, D)]
    specs = [row] + ([] if m is None else [row]) + [vec, vec]
    return pl.pallas_call(body, name=name, grid=(L // tm,), in_specs=specs, out_specs=row,
                          out_shape=jax.ShapeDtypeStruct((L, D), f32), compiler_params=_cp(("parallel",)))(*args)


def _ln_bwd(dy, h, m, g, alpha, name):
    L = h.shape[0]
    tm = _pick(L, (384, 256, 128))
    row = pl.BlockSpec((tm, D), lambda i: (i, 0))
    vec = pl.BlockSpec((1, D), lambda i: (0, 0))
    acc = pl.BlockSpec((8, D), lambda i: (0, 0))

    def body(*refs):
        if m is None:
            dy_ref, h_ref, g_ref, dz_ref, dg_ref, db_ref = refs
            z = h_ref[...]
        else:
            dy_ref, h_ref, m_ref, g_ref, dz_ref, dg_ref, db_ref = refs
            z = alpha * h_ref[...] + m_ref[...]

        @pl.when(pl.program_id(0) == 0)
        def _():
            dg_ref[...] = jnp.zeros_like(dg_ref)
            db_ref[...] = jnp.zeros_like(db_ref)

        dyv = dy_ref[...]
        mu = jnp.mean(z, -1, keepdims=True)
        zc = z - mu
        rstd = lax.rsqrt(jnp.mean(jnp.square(zc), -1, keepdims=True) + LN_EPS)
        xh = zc * rstd
        dxh = dyv * g_ref[...]
        dz_ref[...] = rstd * (dxh - jnp.mean(dxh, -1, keepdims=True) - xh * jnp.mean(dxh * xh, -1, keepdims=True))
        dg_ref[0:1, :] += jnp.sum(dyv * xh, 0, keepdims=True)
        db_ref[0:1, :] += jnp.sum(dyv, 0, keepdims=True)

    args = [dy, h] + ([] if m is None else [m]) + [g.reshape(1, D)]
    specs = [row, row] + ([] if m is None else [row]) + [vec]
    dz, dg, db = pl.pallas_call(
        body, name=name, grid=(L // tm,), in_specs=specs, out_specs=(row, acc, acc),
        out_shape=(jax.ShapeDtypeStruct((L, D), f32), jax.ShapeDtypeStruct((8, D), f32),
                   jax.ShapeDtypeStruct((8, D), f32)),
        compiler_params=_cp(("arbitrary",)))(*args)
    return dz, dg[0], db[0]


def _rms(x, g):
    r = lax.rsqrt(jnp.mean(jnp.square(x), -1, keepdims=True) + LN_EPS)
    return x * r * g


def _prep_fwd(proj, gq, gkv, tabs):
    L = proj.shape[0]
    tm = BLK
    rc, rs, mc, ms = tabs

    def body(p_ref, gq_ref, gkv_ref, rc_ref, rs_ref, mc_ref, ms_ref, sb_ref, cq_ref, ckv_ref, rqk_ref, rv_ref, kr_ref):
        i = pl.program_id(0)
        sb_ref[:, 0:512] = (p_ref[:, C_SBQ:C_SBQ + 512] * SB_SCALE).astype(bf16)
        sb_ref[:, 512:1536] = p_ref[:, C_SBK:C_SBK + 1024].astype(bf16)
        cq_ref[...] = _rms(p_ref[:, C_CQ:C_CQ + 384], gq_ref[...]).astype(bf16)
        ckv_ref[...] = _rms(p_ref[:, C_CKV:C_CKV + 256], gkv_ref[...]).astype(bf16)
        valid = (i * tm + lax.broadcasted_iota(jnp.int32, (tm, 128), 0)) >= N_PAD
        for c in range(2):
            sl = slice(c * 128, (c + 1) * 128)
            x = p_ref[:, C_RQ + c * 128:C_RQ + (c + 1) * 128]
            rqk_ref[:, sl] = (x * rc_ref[:, sl] + _rot(x, 32) * rs_ref[:, sl]).astype(bf16)
            x = p_ref[:, C_RK + c * 128:C_RK + (c + 1) * 128]
            kk = (x * rc_ref[:, sl] + _rot(x, 32) * rs_ref[:, sl]) * RET_SCALE
            rqk_ref[:, 256 + c * 128:256 + (c + 1) * 128] = jnp.where(valid, kk, 0.0).astype(bf16)
        rv_ref[...] = p_ref[:, C_RV:C_RV + 512].astype(bf16)
        x = p_ref[:, C_KR:C_KR + 128]
        kr_ref[...] = (x * mc_ref[...] + _rot(x, 16) * ms_ref[...]).astype(bf16)

    def row(w):
        return pl.BlockSpec((tm, w), lambda i: (i, 0))

    def vec(w):
        return pl.BlockSpec((1, w), lambda i: (0, 0))

    widths = (1536, 384, 256, 512, 512, 128)
    return pl.pallas_call(
        body, name="prep_fwd", grid=(L // tm,),
        in_specs=[row(N_INP), vec(384), vec(256), row(256), row(256), row(128), row(128)],
        out_specs=tuple(row(w) for w in widths),
        out_shape=tuple(jax.ShapeDtypeStruct((L, w), bf16) for w in widths),
        compiler_params=_cp(("parallel",)))(proj, gq.reshape(1, 384), gkv.reshape(1, 256), rc, rs, mc, ms)


def _rms_bwd(x, g, dy):
    r = lax.rsqrt(jnp.mean(jnp.square(x), -1, keepdims=True) + LN_EPS)
    u = dy * g
    dx = r * u - x * (r * r * r) * jnp.mean(x * u, -1, keepdims=True)
    return dx, jnp.sum(dy * x * r, 0, keepdims=True)


def _prep_bwd(proj, gq, gkv, tabs, dcqn, dckvn, drq_r, drk_r, dkr_r):
    L = proj.shape[0]
    tm = BLK
    rc, rs, mc, ms = tabs

    def body(p_ref, gq_ref, gkv_ref, rc_ref, rs_ref, mc_ref, ms_ref, dcqn_ref, dckvn_ref, drq_ref, drk_ref,
             dkr_ref, ocq_ref, ockv_ref, orq_ref, ork_ref, okr_ref, dgq_ref, dgkv_ref):
        i = pl.program_id(0)

        @pl.when(i == 0)
        def _():
            dgq_ref[...] = jnp.zeros_like(dgq_ref)
            dgkv_ref[...] = jnp.zeros_like(dgkv_ref)

        dx, dg = _rms_bwd(p_ref[:, C_CQ:C_CQ + 384], gq_ref[...], dcqn_ref[...])
        ocq_ref[...] = dx
        dgq_ref[0:1, :] += dg
        dx, dg = _rms_bwd(p_ref[:, C_CKV:C_CKV + 256], gkv_ref[...], dckvn_ref[...])
        ockv_ref[...] = dx
        dgkv_ref[0:1, :] += dg
        valid = (i * tm + lax.broadcasted_iota(jnp.int32, (tm, 128), 0)) >= N_PAD
        for c in range(2):
            sl = slice(c * 128, (c + 1) * 128)
            dy = drq_ref[:, sl]
            orq_ref[:, sl] = dy * rc_ref[:, sl] - _rot(dy * rs_ref[:, sl], 32)
            dy = jnp.where(valid, drk_ref[:, sl], 0.0) * RET_SCALE
            ork_ref[:, sl] = dy * rc_ref[:, sl] - _rot(dy * rs_ref[:, sl], 32)
        dy = dkr_ref[...]
        okr_ref[...] = dy * mc_ref[...] - _rot(dy * ms_ref[...], 16)

    def row(w):
        return pl.BlockSpec((tm, w), lambda i: (i, 0))

    def vec(w):
        return pl.BlockSpec((1, w), lambda i: (0, 0))

    def acc(w):
        return pl.BlockSpec((8, w), lambda i: (0, 0))

    widths = (384, 256, 256, 256, 128)
    outs = pl.pallas_call(
        body, name="prep_bwd", grid=(L // tm,),
        in_specs=[row(N_INP), vec(384), vec(256), row(256), row(256), row(128), row(128),
                  row(384), row(256), row(256), row(256), row(128)],
        out_specs=tuple(row(w) for w in widths) + (acc(384), acc(256)),
        out_shape=tuple(jax.ShapeDtypeStruct((L, w), f32) for w in widths)
        + (jax.ShapeDtypeStruct((8, 384), f32), jax.ShapeDtypeStruct((8, 256), f32)),
        compiler_params=_cp(("arbitrary",)))(
            proj, gq.reshape(1, 384), gkv.reshape(1, 256), rc, rs, mc, ms, dcqn, dckvn, drq_r, drk_r, dkr_r)
    return outs[:5] + (outs[5][0], outs[6][0])


def _qrows(L):
    return _pick(L, (384, 256, 128))


def _tri_ones(strict):
    r = lax.broadcasted_iota(jnp.int32, (BLK, 2 * BLK), 0)
    c = lax.broadcasted_iota(jnp.int32, (BLK, 2 * BLK), 1)
    tri = (r > c) if strict else (r >= c)
    return jnp.where(tri | (c >= BLK), 1.0, 0.0).astype(_MXU)


def _dot2(x, u):
    hi = x.astype(_MXU)
    lo = (x - hi.astype(f32)).astype(_MXU)
    return jnp.dot(hi, u, preferred_element_type=f32) + jnp.dot(lo, u, preferred_element_type=f32)


def _staggered(chains):
    live = list(chains)
    step = 0
    while live:
        for ci, g in enumerate(chains):
            if g in live and step >= ci and next(g, True):
                live.remove(g)
        step += 1


def _sb_stages(qx, k, mask, c_ref, x, u_gt, out):
    z = _dot(qx, k, NT)
    yield
    lb = jnp.minimum(z, 0.0) - jnp.log1p(jnp.exp(-jnp.abs(z)))
    lk = lb - z
    if mask is not None:
        lk = jnp.where(mask, lk, 0.0)
    hi = lk.astype(_MXU)
    lo = (lk - hi.astype(f32)).astype(_MXU)
    yield
    el = (jnp.dot(hi, u_gt, preferred_element_type=f32)
          + jnp.dot(lo, u_gt, preferred_element_type=f32))
    yield
    c = c_ref[x]
    w = jnp.exp(lb + el[:, 0:BLK] + c)
    if mask is not None:
        w = jnp.where(mask, w, 0.0)
    c_ref[x] = c + el[:, BLK:2 * BLK]
    out["w"], out["lb"] = w.astype(_MXU), lb


def _sb_sweep(i, r, tiles):
    n_t = r * (i + 1)
    lax.fori_loop(0, r, lambda jj, c: tiles([n_t - 1 - jj], True) or c, 0)
    n_bulk = jnp.maximum(r * i - 1, 0)
    lax.fori_loop(0, n_bulk // 2, lambda jj, c: tiles([r * i - 1 - 2 * jj, r * i - 2 - 2 * jj], False) or c, 0)

    @pl.when(n_bulk % 2 == 1)
    def _():
        tiles([1], False)

    @pl.when(i > 0)
    def _():
        tiles([0], True)


def _tile_off(j):
    return j * BLK if isinstance(j, int) else pl.multiple_of(j * BLK, BLK)


def _sb_mask(i, j, qb):
    row = i * qb + lax.broadcasted_iota(jnp.int32, (qb, BLK), 0)
    col = j * BLK + lax.broadcasted_iota(jnp.int32, (qb, BLK), 1)
    return (col < row) & (col >= N_PAD)


def _first_last(n0, n1):
    p, i = pl.program_id(0), pl.program_id(1)
    return (p == 0) & (i == 0), (p == n0 - 1) & (i == n1 - 1)


def _sb_fwd(sb, bg=None):
    L = sb.shape[0]
    qb = _qrows(L)
    nq, r = L // qb, qb // BLK

    def body(*refs):
        if bg is None:
            q_ref, k_ref, v_ref, o_ref, acc_ref, c_ref = refs
        else:
            q_ref, k_ref, v_ref, x_ref, o_ref, g_ref, acc_ref, c_ref = refs[:8]
            start, wait = _xchg_ops(x_ref, g_ref, *refs[8:], False)
            first, last = _first_last(4, nq)
            pl.when(first)(start)
        i = pl.program_id(1)
        m_a = lax.broadcasted_iota(jnp.int32, (1, BLK), 1) < 64
        q = q_ref[...]
        zq = jnp.zeros_like(q)
        qs = (jnp.where(m_a, q, zq), jnp.where(m_a, zq, q))
        u_gt = _tri_ones(True)
        acc_ref[...] = jnp.zeros_like(acc_ref)
        c_ref[...] = jnp.zeros_like(c_ref)

        def chain(x, j, masked):
            off = _tile_off(j)
            k = k_ref[pl.ds(off, BLK), :]
            mask = _sb_mask(i, j, qb) if masked else None
            o = {}
            yield from _sb_stages(qs[x], k, mask, c_ref, x, u_gt, o)
            yield
            acc_ref[x] += _dot(o["w"], v_ref[pl.ds(off, BLK), :])

        def tiles(js, masked):
            _staggered([chain(x, j, masked) for j in js for x in range(2)])

        _sb_sweep(i, r, tiles)
        o_ref[...] = jnp.where(m_a, acc_ref[0], acc_ref[1])
        if bg is not None:
            pl.when(last)(wait)

    in_specs = [pl.BlockSpec((qb, 128), lambda p, i: (i, p)),
                pl.BlockSpec((L, 128), lambda p, i: (0, 4 + p)),
                pl.BlockSpec((L, 128), lambda p, i: (0, 8 + p))]
    o_spec = pl.BlockSpec((qb, 128), lambda p, i: (i, p))
    o_shape = jax.ShapeDtypeStruct((L, 512), f32)
    scratch = [pltpu.VMEM((2, qb, BLK), f32), pltpu.VMEM((2, qb, BLK), f32)]
    if bg is None:
        return pl.pallas_call(body, name="sb_fwd", grid=(4, nq), in_specs=in_specs, out_specs=o_spec,
                              out_shape=o_shape, scratch_shapes=scratch,
                              compiler_params=_cp(("parallel", "arbitrary")))(sb, sb, sb)
    return pl.pallas_call(body, name="sb_fwd_gather", grid=(4, nq), in_specs=in_specs + [_HBM],
                          out_specs=(o_spec, _HBM), out_shape=(o_shape, _xchg_shape(bg, False)),
                          scratch_shapes=scratch + list(_XCHG_SEMS),
                          compiler_params=_cp(("arbitrary", "arbitrary")))(sb, sb, sb, bg)


def _sb_bwd(dmixed, sb, out_a, bg=None):
    L = sb.shape[0]
    qb = _qrows(L)
    nq, r = L // qb, qb // BLK

    def body(*refs):
        if bg is None:
            do_ref, o_ref, q_ref, k_ref, v_ref, dq_ref, dk_ref, dv_ref, dqa_ref, c_ref, cg_ref, ds_ref = refs
        else:
            do_ref, o_ref, q_ref, k_ref, v_ref, x_ref, dq_ref, dk_ref, dv_ref, g_ref = refs[:10]
            dqa_ref, c_ref, cg_ref, ds_ref = refs[10:14]
            start, wait = _xchg_ops(x_ref, g_ref, *refs[14:], True)
            first, last = _first_last(4, nq)
            pl.when(first)(start)
        i = pl.program_id(1)

        @pl.when(i == 0)
        def _():
            dk_ref[...] = jnp.zeros_like(dk_ref)
            dv_ref[...] = jnp.zeros_like(dv_ref)

        m_a = lax.broadcasted_iota(jnp.int32, (1, BLK), 1) < 64
        q = q_ref[...]
        zq = jnp.zeros_like(q)
        qs = (jnp.where(m_a, q, zq), jnp.where(m_a, zq, q))
        do = do_ref[...]
        zd = jnp.zeros_like(do)
        dos = (jnp.where(m_a, do, zd).astype(_MXU), jnp.where(m_a, zd, do).astype(_MXU))
        prod = do.astype(_MXU).astype(f32) * o_ref[...]
        ds_ref[0] = jnp.broadcast_to(jnp.sum(jnp.where(m_a, prod, 0.0), 1, keepdims=True), (qb, BLK))
        ds_ref[1] = jnp.broadcast_to(jnp.sum(jnp.where(m_a, 0.0, prod), 1, keepdims=True), (qb, BLK))
        u_gt = _tri_ones(True)
        u_ge = _tri_ones(False)
        dqa_ref[...] = jnp.zeros_like(dqa_ref)
        c_ref[...] = jnp.zeros_like(c_ref)
        cg_ref[...] = jnp.zeros_like(cg_ref)

        def chain(x, j, masked):
            off = _tile_off(j)
            k = k_ref[pl.ds(off, BLK), :]
            v = v_ref[pl.ds(off, BLK), :]
            mask = _sb_mask(i, j, qb) if masked else None
            o = {}
            dw = _dot(dos[x], v, NT)
            yield from _sb_stages(qs[x], k, mask, c_ref, x, u_gt, o)
            wb = o["w"]
            gr = wb.astype(f32) * dw
            hi = gr.astype(_MXU)
            r1 = gr - hi.astype(f32)
            mid = r1.astype(_MXU)
            lo = (r1 - mid.astype(f32)).astype(_MXU)
            yield
            eg = (jnp.dot(hi, u_ge, preferred_element_type=f32)
                  + jnp.dot(mid, u_ge, preferred_element_type=f32) + jnp.dot(lo, u_ge, preferred_element_type=f32))
            yield
            cg = cg_ref[x]
            suffix = eg[:, 0:BLK] + cg
            cg_ref[x] = cg + eg[:, BLK:2 * BLK]
            dz = gr - jnp.exp(o["lb"]) * (gr + ds_ref[x] - suffix)
            if masked:
                dz = jnp.where(mask, dz, 0.0)
            dz = dz.astype(_MXU)
            yield
            dqa_ref[x] += _dot(dz, k)
            dk_ref[pl.ds(off, BLK), :] += _dot(dz, qs[x], TN)
            dv_ref[pl.ds(off, BLK), :] += _dot(wb, dos[x], TN)

        def tiles(js, masked):
            _staggered([chain(x, j, masked) for j in js for x in range(2)])

        _sb_sweep(i, r, tiles)
        dq_ref[...] = jnp.where(m_a, dqa_ref[0], dqa_ref[1]) * SB_SCALE
        if bg is not None:
            pl.when(last)(wait)

    blk = pl.BlockSpec((qb, 128), lambda p, i: (i, p))
    scr = pltpu.VMEM((2, qb, BLK), f32)
    in_specs = [blk, blk, blk, pl.BlockSpec((L, 128), lambda p, i: (0, 4 + p)),
                pl.BlockSpec((L, 128), lambda p, i: (0, 8 + p))]
    out_specs = (blk, pl.BlockSpec((L, 128), lambda p, i: (0, p)), pl.BlockSpec((L, 128), lambda p, i: (0, p)))
    out_shape = tuple(jax.ShapeDtypeStruct((L, 512), f32) for _ in range(3))
    if bg is None:
        return pl.pallas_call(body, name="sb_bwd", grid=(4, nq), in_specs=in_specs, out_specs=out_specs,
                              out_shape=out_shape, scratch_shapes=[scr, scr, scr, scr],
                              compiler_params=_cp(("parallel", "arbitrary")))(dmixed, out_a, sb, sb, sb)
    return pl.pallas_call(body, name="sb_bwd_scatter", grid=(4, nq), in_specs=in_specs + [_HBM],
                          out_specs=out_specs + (_HBM,), out_shape=out_shape + (_xchg_shape(bg, True),),
                          scratch_shapes=[scr, scr, scr, scr] + list(_XCHG_SEMS),
                          compiler_params=_cp(("arbitrary", "arbitrary")))(dmixed, out_a, sb, sb, sb, bg)


def _mla_mask(i, j, qb):
    row = i * qb + lax.broadcasted_iota(jnp.int32, (qb, BLK), 0)
    col = j * BLK + lax.broadcasted_iota(jnp.int32, (qb, BLK), 1)
    return (col <= row) & ((col >= N_PAD) | (col == row))


def _pair_mask2():
    return (lax.broadcasted_iota(jnp.int32, (1, 256), 1) % 128) < 64


def _mla_q2(qn_ref, qr_ref, mc_ref, ms_ref):
    qr = qr_ref[...]
    qr = qr * mc_ref[...] + _rot(qr, 16) * ms_ref[...]
    q2 = jnp.concatenate([qn_ref[...], qr], axis=1)
    m2 = _pair_mask2()
    z2 = jnp.zeros_like(q2)
    return (jnp.where(m2, q2, z2).astype(_MXU), jnp.where(m2, z2, q2).astype(_MXU))


def _mla_fwd(q, kv, kr2, mc, ms):
    L = q.shape[0]
    qb = _qrows(L)
    nq, r = L // qb, qb // BLK

    def body(qn_ref, qr_ref, mc_ref, ms_ref, kn_ref, v_ref, kr_ref, o_ref, lse_ref, acc_ref, m_ref):
        i = pl.program_id(1)
        m_a = lax.broadcasted_iota(jnp.int32, (1, BLK), 1) < 64
        qs = _mla_q2(qn_ref, qr_ref, mc_ref, ms_ref)
        acc_ref[...] = jnp.zeros_like(acc_ref)
        m_ref[...] = jnp.full(m_ref.shape, -1e30, f32)
        ones = jnp.ones((BLK, BLK), _MXU)

        def chain(x, j, masked):
            off = _tile_off(j)
            k2 = jnp.concatenate([kn_ref[pl.ds(off, BLK), :], kr_ref[pl.ds(off, BLK), :]], axis=1)
            s = _dot(qs[x], k2, NT)
            yield
            s = s * MLA_SCALE
            if masked:
                mask = _mla_mask(i, j, qb)
                s = jnp.where(mask, s, -1e30)
            m_old = m_ref[x]
            m_new = jnp.maximum(m_old, jnp.max(s, 1, keepdims=True))
            a = jnp.exp(m_old - m_new)
            p = jnp.exp(s - m_new)
            if masked:
                p = jnp.where(mask, p, 0.0)
            m_ref[x] = m_new
            p = p.astype(_MXU)
            yield
            v1 = jnp.concatenate([v_ref[pl.ds(off, BLK), :], ones], axis=1)
            acc_ref[x] = jnp.concatenate([a, a], axis=1) * acc_ref[x] + _dot(p, v1)

        def tiles(js, masked):
            _staggered([chain(x, j, masked) for j in js for x in range(2)])

        _sb_sweep(i, r, tiles)
        o_ref[...] = jnp.where(m_a, acc_ref[0, :, 0:BLK] / acc_ref[0, :, BLK:2 * BLK],
                               acc_ref[1, :, 0:BLK] / acc_ref[1, :, BLK:2 * BLK])
        for x in range(2):
            lse_ref[0, x] = m_ref[x] + jnp.log(acc_ref[x, :, BLK:2 * BLK])

    blk = lambda cb: pl.BlockSpec((qb, 128), lambda p, i: (i, cb + p))
    full = lambda cb: pl.BlockSpec((L, 128), lambda p, i: (0, cb + p))
    tab = pl.BlockSpec((qb, 128), lambda p, i: (i, 0))
    return pl.pallas_call(
        body, name="mla_fwd", grid=(4, nq),
        in_specs=[blk(0), blk(4), tab, tab, full(0), full(4), pl.BlockSpec((L, 128), lambda p, i: (0, 0))],
        out_specs=(blk(0), pl.BlockSpec((1, 2, qb, 128), lambda p, i: (p, 0, i, 0))),
        out_shape=(jax.ShapeDtypeStruct((L, 512), f32), jax.ShapeDtypeStruct((4, 2, L, 128), f32)),
        scratch_shapes=[pltpu.VMEM((2, qb, 2 * BLK), f32), pltpu.VMEM((2, qb, BLK), f32)],
        compiler_params=_cp(("parallel", "arbitrary")))(q, q, mc, ms, kv, kv, kr2)


def _mla_bwd(dmixed, q, kv, kr2, out_b, lse, mc, ms):
    L = q.shape[0]
    qb = _qrows(L)
    nq, r = L // qb, qb // BLK

    def body(do_ref, o_ref, lse_ref, qn_ref, qr_ref, mc_ref, ms_ref, kn_ref, v_ref, kr_ref,
             dqn_ref, dqr_ref, dkn_ref, dv_ref, dkr_ref, dqa_ref, ds_ref):
        i = pl.program_id(1)

        @pl.when(i == 0)
        def _():
            dkn_ref[...] = jnp.zeros_like(dkn_ref)
            dv_ref[...] = jnp.zeros_like(dv_ref)
            dkr_ref[...] = jnp.zeros_like(dkr_ref)

        m_a = lax.broadcasted_iota(jnp.int32, (1, BLK), 1) < 64
        qs = _mla_q2(qn_ref, qr_ref, mc_ref, ms_ref)
        do = do_ref[...]
        zd = jnp.zeros_like(do)
        dos = (jnp.where(m_a, do, zd).astype(_MXU), jnp.where(m_a, zd, do).astype(_MXU))
        prod = do * o_ref[...]
        ds_ref[0] = jnp.broadcast_to(jnp.sum(jnp.where(m_a, prod, 0.0), 1, keepdims=True), (qb, BLK))
        ds_ref[1] = jnp.broadcast_to(jnp.sum(jnp.where(m_a, 0.0, prod), 1, keepdims=True), (qb, BLK))
        dqa_ref[...] = jnp.zeros_like(dqa_ref)

        def chain(x, j, masked):
            off = _tile_off(j)
            k2 = jnp.concatenate([kn_ref[pl.ds(off, BLK), :], kr_ref[pl.ds(off, BLK), :]], axis=1)
            s = _dot(qs[x], k2, NT)
            dp = _dot(dos[x], v_ref[pl.ds(off, BLK), :], NT)
            yield
            s = s * MLA_SCALE
            if masked:
                mask = _mla_mask(i, j, qb)
                p = jnp.where(mask, jnp.exp(jnp.where(mask, s, 0.0) - lse_ref[0, x]), 0.0)
            else:
                p = jnp.exp(s - lse_ref[0, x])
            pb = p.astype(_MXU)
            ds = (p * (dp - ds_ref[x]) * MLA_SCALE).astype(_MXU)
            yield
            dqa_ref[x] += _dot(ds, k2)
            dk_t = _dot(ds, qs[x], TN)
            dkn_ref[pl.ds(off, BLK), :] += dk_t[:, 0:128]
            dkr_ref[0, pl.ds(off, BLK), :] += dk_t[:, 128:256]
            dv_ref[pl.ds(off, BLK), :] += _dot(pb, dos[x], TN)

        def tiles(js, masked):
            _staggered([chain(x, j, masked) for j in js for x in range(2)])

        _sb_sweep(i, r, tiles)
        dq2 = jnp.where(_pair_mask2(), dqa_ref[0], dqa_ref[1])
        dqn_ref[...] = dq2[:, 0:128]
        dy = dq2[:, 128:256]
        dqr_ref[...] = dy * mc_ref[...] - _rot(dy * ms_ref[...], 16)

    blk = lambda cb: pl.BlockSpec((qb, 128), lambda p, i: (i, cb + p))
    full = lambda cb: pl.BlockSpec((L, 128), lambda p, i: (0, cb + p))
    tab = pl.BlockSpec((qb, 128), lambda p, i: (i, 0))
    o512 = jax.ShapeDtypeStruct((L, 512), f32)
    return pl.pallas_call(
        body, name="mla_bwd", grid=(4, nq),
        in_specs=[blk(4), blk(0), pl.BlockSpec((1, 2, qb, 128), lambda p, i: (p, 0, i, 0)), blk(0), blk(4), tab, tab,
                  full(0), full(4), pl.BlockSpec((L, 128), lambda p, i: (0, 0))],
        out_specs=(blk(0), blk(0), full(0), full(0), pl.BlockSpec((1, L, 128), lambda p, i: (p, 0, 0))),
        out_shape=(o512, o512, o512, o512, jax.ShapeDtypeStruct((4, L, 128), f32)),
        scratch_shapes=[pltpu.VMEM((2, qb, 2 * BLK), f32), pltpu.VMEM((2, qb, BLK), f32)],
        compiler_params=_cp(("parallel", "arbitrary")))(dmixed, out_b, lse, q, q, mc, ms, kv, kv, kr2)


def _ret_tables():
    log_g = jnp.log(jnp.array(RET_GAMMA, f32))
    idx = jnp.arange(BLK, dtype=f32)
    diff = idx[:, None] - idx[None, :]
    d_in = jnp.where(diff[None] >= 0, jnp.exp(jnp.maximum(diff, 0.0)[None] * log_g[:, None, None]), 0.0)
    q_dec = jnp.exp((idx[None, :] + 1.0) * log_g[:, None])
    k_dec = jnp.exp((BLK - 1.0 - idx[None, :]) * log_g[:, None])
    c_dec = jnp.exp(BLK * log_g)
    bc = lambda a: jnp.broadcast_to(a[:, :, None], (4, BLK, BLK))
    return d_in, bc(q_dec), bc(k_dec), jnp.broadcast_to(c_dec[:, None, None], (4, 8, BLK))


def _head_mask(x):
    lane = lax.broadcasted_iota(jnp.int32, (1, BLK), 1)
    return (lane < 64) if x == 0 else (lane >= 64)


def _ret_fwd(rqk, rv, proj, rtabs):
    L = rqk.shape[0]
    n = L // BLK
    d_in, q_dec, k_dec, c_dec = rtabs

    def body(q_ref, k_ref, v_ref, g_ref, din_ref, qd_ref, kd_ref, cd_ref, y_ref, o_ref, st_ref, s_scr):
        @pl.when(pl.program_id(1) == 0)
        def _():
            s_scr[...] = jnp.zeros_like(s_scr)

        q = q_ref[...]
        k = k_ref[...]
        zq = jnp.zeros_like(q)
        for x in range(2):
            hm = _head_mask(x)
            sl = slice(x * 128, (x + 1) * 128)
            qm = jnp.where(hm, q, zq)
            km = jnp.where(hm, k, zq)
            v = v_ref[:, sl]
            s_in = s_scr[x]
            st_ref[0, 0, x] = s_in
            inner = _dot(qm, km, NT) * din_ref[x]
            y = _dot(inner, v) + _dot(qm, s_in) * qd_ref[x]
            s_scr[x] = s_in * cd_ref[x, 0:1, :] + _dot(km.astype(f32) * kd_ref[x], v, TN)
            y_ref[:, sl] = y
            mu = jnp.mean(y, -1, keepdims=True)
            yc = y - mu
            yn = yc * lax.rsqrt(jnp.mean(jnp.square(yc), -1, keepdims=True) + LN_EPS)
            g = g_ref[:, sl]
            o_ref[:, sl] = g * jax.nn.sigmoid(g) * yn

    tab = pl.BlockSpec((2, BLK, BLK), lambda p, i: (p, 0, 0))
    return pl.pallas_call(
        body, name="ret_fwd", grid=(2, n),
        in_specs=[pl.BlockSpec((BLK, 128), lambda p, i: (i, p)), pl.BlockSpec((BLK, 128), lambda p, i: (i, 2 + p)),
                  pl.BlockSpec((BLK, 256), lambda p, i: (i, p)),
                  pl.BlockSpec((BLK, 256), lambda p, i: (i, C_RG // 256 + p)),
                  tab, tab, tab, pl.BlockSpec((2, 8, BLK), lambda p, i: (p, 0, 0))],
        out_specs=(pl.BlockSpec((BLK, 256), lambda p, i: (i, p)), pl.BlockSpec((BLK, 256), lambda p, i: (i, p)),
                   pl.BlockSpec((1, 1, 2, BLK, BLK), lambda p, i: (p, i, 0, 0, 0))),
        out_shape=(jax.ShapeDtypeStruct((L, 512), f32), jax.ShapeDtypeStruct((L, 512), f32),
                   jax.ShapeDtypeStruct((2, n, 2, BLK, BLK), f32)),
        scratch_shapes=[pltpu.VMEM((2, BLK, BLK), f32)],
        compiler_params=_cp(("parallel", "arbitrary")))(rqk, rqk, rv, proj, d_in, q_dec, k_dec, c_dec)


def _ret_bwd(dmixed, rqk, rv, proj, y, states, rtabs):
    L = rqk.shape[0]
    n = L // BLK
    d_in, q_dec, k_dec, c_dec = rtabs

    def body(do_ref, q_ref, k_ref, v_ref, g_ref, y_ref, st_ref, din_ref, qd_ref, kd_ref, cd_ref,
             dq_ref, dk_ref, dv_ref, dg_ref, ds_scr):
        @pl.when(pl.program_id(1) == 0)
        def _():
            ds_scr[...] = jnp.zeros_like(ds_scr)

        q = q_ref[...]
        k = k_ref[...]
        zq = jnp.zeros_like(q)
        dq_acc = jnp.zeros((BLK, BLK), f32)
        dk_acc = jnp.zeros((BLK, BLK), f32)
        for x in range(2):
            hm = _head_mask(x)
            sl = slice(x * 128, (x + 1) * 128)
            qm = jnp.where(hm, q, zq)
            km = jnp.where(hm, k, zq)
            v = v_ref[:, sl]
            yv = y_ref[:, sl]
            g = g_ref[:, sl]
            do = do_ref[:, sl]
            mu = jnp.mean(yv, -1, keepdims=True)
            yc = yv - mu
            rstd = lax.rsqrt(jnp.mean(jnp.square(yc), -1, keepdims=True) + LN_EPS)
            yn = yc * rstd
            sg = jax.nn.sigmoid(g)
            dg_ref[:, sl] = do * yn * sg * (1.0 + g * (1.0 - sg))
            dyn = do * g * sg
            dy = rstd * (dyn - jnp.mean(dyn, -1, keepdims=True) - yn * jnp.mean(dyn * yn, -1, keepdims=True))
            s_in = st_ref[0, 0, x]
            ds_out = ds_scr[x]
            kd = km.astype(f32) * kd_ref[x]
            a = _dot(qm, km, NT) * din_ref[x]
            da = _dot(dy, v, NT) * din_ref[x]
            dyq = dy * qd_ref[x]
            dq_acc += _dot(da, km) + _dot(dyq, s_in, NT)
            dk_acc += _dot(da, qm, TN) + _dot(v, ds_out, NT) * kd_ref[x]
            dv_ref[:, sl] = _dot(a, dy, TN) + _dot(kd, ds_out)
            ds_scr[x] = ds_out * cd_ref[x, 0:1, :] + _dot(qm, dyq, TN)
        dq_ref[...] = dq_acc
        dk_ref[...] = dk_acc

    rev = lambda w, cb: pl.BlockSpec((BLK, w), lambda p, i: (n - 1 - i, cb + p))
    tab = pl.BlockSpec((2, BLK, BLK), lambda p, i: (p, 0, 0))
    return pl.pallas_call(
        body, name="ret_bwd", grid=(2, n),
        in_specs=[rev(256, 4), rev(128, 0), rev(128, 2), rev(256, 0), rev(256, C_RG // 256), rev(256, 0),
                  pl.BlockSpec((1, 1, 2, BLK, BLK), lambda p, i: (p, n - 1 - i, 0, 0, 0)),
                  tab, tab, tab, pl.BlockSpec((2, 8, BLK), lambda p, i: (p, 0, 0))],
        out_specs=(rev(128, 0), rev(128, 0), rev(256, 0), rev(256, 0)),
        out_shape=(jax.ShapeDtypeStruct((L, 256), f32), jax.ShapeDtypeStruct((L, 256), f32),
                   jax.ShapeDtypeStruct((L, 512), f32), jax.ShapeDtypeStruct((L, 512), f32)),
        scratch_shapes=[pltpu.VMEM((2, BLK, BLK), f32)],
        compiler_params=_cp(("parallel", "arbitrary")))(dmixed, rqk, rqk, rv, proj, y, states, d_in, q_dec, k_dec, c_dec)


def _loss_head(h, target):
    L = h.shape[0]
    n = L // BLK

    def body(h_ref, t_ref, dy_ref, l_ref):
        i = pl.program_id(0)

        @pl.when(i == 0)
        def _():
            dy_ref[...] = jnp.zeros_like(dy_ref)
            l_ref[...] = jnp.zeros_like(l_ref)

        @pl.when(i > 0)
        def _():
            err = h_ref[...] - t_ref[...]
            dy_ref[...] = err * (1.0 / D)
            sq = jnp.sum(jnp.sum(jnp.square(err), 1, keepdims=True), 0, keepdims=True)
            l_ref[...] += (0.5 / D) * sq

    return pl.pallas_call(
        body, name="loss_head", grid=(n,),
        in_specs=[pl.BlockSpec((BLK, D), lambda i: (i, 0)),
                  pl.BlockSpec((BLK, D), lambda i: (jnp.maximum(i - 1, 0), 0))],
        out_specs=(pl.BlockSpec((BLK, D), lambda i: (i, 0)), pl.BlockSpec((8, 128), lambda i: (0, 0))),
        out_shape=(jax.ShapeDtypeStruct((L, D), f32), jax.ShapeDtypeStruct((8, 128), f32)),
        compiler_params=_cp(("arbitrary",)))(h, target)


def _adam_math(w, g, m, v):
    m = ADAM_B1 * m + (1.0 - ADAM_B1) * g
    v = ADAM_B2 * v + (1.0 - ADAM_B2) * jnp.square(g)
    m_hat = m / (1.0 - ADAM_B1 ** ADAM_STEP)
    v_hat = v / (1.0 - ADAM_B2 ** ADAM_STEP)
    delta = -ADAM_LR * (m_hat / (jnp.sqrt(v_hat) + ADAM_EPS) + ADAM_WD * w)
    return delta, m, v


def _adamw(parts, w, m, v, name):
    R, C = w.shape
    tr = _pick(R, (240, 192, 144, 96, 64, 48, 32, 16, 8))
    row = pl.BlockSpec((tr, C), lambda i: (i, 0))

    def body(p_ref, w_ref, m_ref, v_ref, g_ref, d_ref, nm_ref, nv_ref):
        g = p_ref[0].astype(f32)
        for k in range(1, N_DEV):
            g = g + p_ref[k].astype(f32)
        d, nm, nv = _adam_math(w_ref[...], g, m_ref[...], v_ref[...])
        g_ref[...] = g
        d_ref[...] = d
        nm_ref[...] = nm
        nv_ref[...] = nv

    o = jax.ShapeDtypeStruct((R, C), f32)
    return pl.pallas_call(
        body, name=name, grid=(R // tr,),
        in_specs=[pl.BlockSpec((N_DEV, tr, C), lambda i: (0, i, 0)), row, row, row],
        out_specs=(row, row, row, row), out_shape=(o, o, o, o),
        compiler_params=_cp(("parallel",)))(parts, w, m, v)


_XCHG_SEMS = [pltpu.SemaphoreType.DMA((N_DEV - 1,)), pltpu.SemaphoreType.DMA((N_DEV - 1,)), pltpu.SemaphoreType.DMA]
_HBM = pl.BlockSpec(memory_space=pltpu.HBM)


def _xchg_shape(x, all_to_all):
    return jax.ShapeDtypeStruct((N_DEV,) + tuple(x.shape[1:] if all_to_all else x.shape), x.dtype)


def _xchg_ops(x_ref, o_ref, send_sems, recv_sems, local_sem, all_to_all):
    mx, my, mc = lax.axis_index("x"), lax.axis_index("y"), lax.axis_index("c")
    me = 4 * mx + 2 * my + mc

    def peer(k):
        px = (1 - mx) if k & 4 else mx
        py = (1 - my) if k & 2 else my
        pc = (1 - mc) if k & 1 else mc
        return (px, py, pc), 4 * px + 2 * py + pc

    def copy(k):
        dev, idx = peer(k)
        src = x_ref.at[idx] if all_to_all else x_ref
        return pltpu.make_async_remote_copy(
            src_ref=src, dst_ref=o_ref.at[me], send_sem=send_sems.at[k - 1], recv_sem=recv_sems.at[k - 1],
            device_id=dev, device_id_type=pl.DeviceIdType.MESH)

    def start():
        pltpu.make_async_copy(x_ref.at[me] if all_to_all else x_ref, o_ref.at[me], local_sem).start()
        for k in range(1, N_DEV):
            copy(k).start()

    def wait():
        for k in range(1, N_DEV):
            dev, idx = peer(k)
            pltpu.make_async_remote_copy(
                src_ref=o_ref.at[idx], dst_ref=o_ref.at[idx], send_sem=send_sems.at[k - 1],
                recv_sem=recv_sems.at[k - 1], device_id=dev, device_id_type=pl.DeviceIdType.MESH).wait_recv()
        for k in range(1, N_DEV):
            copy(k).wait_send()
        pltpu.make_async_copy(x_ref.at[me] if all_to_all else x_ref, o_ref.at[me], local_sem).wait()

    return start, wait


def _exchange(x, all_to_all, name):
    def body(x_ref, o_ref, send_sems, recv_sems, local_sem):
        start, wait = _xchg_ops(x_ref, o_ref, send_sems, recv_sems, local_sem, all_to_all)
        start()
        wait()

    return pl.pallas_call(body, name=name, in_specs=[_HBM], out_specs=_HBM, out_shape=_xchg_shape(x, all_to_all),
                          scratch_shapes=list(_XCHG_SEMS))(x)


WC = 512
LAYER_ROWS = tuple(n // WC for n in (1024 * 468, 384 * 96, 256 * 128, 192 * 1024, 1024 * 512, 512 * 1024))


def _pack_layer(ws, l):
    return jnp.concatenate([w[l].reshape(-1, WC) for w in ws], axis=0)


def _unpack_layer(gathered):
    offs = np.cumsum((0,) + LAYER_ROWS)
    part = lambda k: gathered[:, offs[k]:offs[k + 1]]
    w_in = part(0).reshape(8, 1024, 468).transpose(1, 0, 2).reshape(1024, 3744)
    z32 = jnp.zeros((1024, 32), w_in.dtype)
    kr = w_in[:, 2176:2208]
    w_in = jnp.concatenate([w_in[:, 0:1536], w_in[:, 1920:2176], w_in[:, 2208:3744], w_in[:, 1536:1920],
                            kr, z32, kr, z32], axis=1)
    w_uq = part(1).reshape(8, 384, 96).transpose(1, 0, 2)
    rope = jnp.concatenate([w_uq[..., 64:96], jnp.zeros((384, 8, 32), w_uq.dtype)], axis=-1)
    w_uq = jnp.concatenate([w_uq[..., 0:64].reshape(384, 512), rope.reshape(384, 512)], axis=1)
    w_ukv = part(2).reshape(8, 256, 128).transpose(1, 0, 2)
    w_ukv = jnp.concatenate([w_ukv[..., 0:64].reshape(256, 512), w_ukv[..., 64:128].reshape(256, 512)], axis=1)
    w_out = part(3).reshape(1536, 1024)
    w_ff1 = part(4).reshape(8, 1024, 512).transpose(1, 0, 2).reshape(1024, 4096)
    w_ff2 = part(5).reshape(4096, 1024)
    return w_in, w_uq, w_ukv, w_out, w_ff1, w_ff2


def _pack_layer_grads(g_in, g_uq, g_ukv, g_out, g_ff1, g_ff2):
    kr = g_in[:, C_KR:C_KR + 32] + g_in[:, C_KR + 64:C_KR + 96]
    g_in = jnp.concatenate([g_in[:, 0:1536], g_in[:, C_CQ:C_CQ + 384], g_in[:, C_CKV:C_CKV + 256], kr,
                            g_in[:, C_RQ:C_CQ]], axis=1)
    g_in = g_in.reshape(1024, 8, 468).transpose(1, 0, 2)
    g_uq = jnp.concatenate([g_uq[:, 0:512].reshape(384, 8, 64), g_uq[:, 512:1024].reshape(384, 8, 64)[..., 0:32]],
                           axis=-1).transpose(1, 0, 2)
    g_ukv = jnp.concatenate([g_ukv[:, 0:512].reshape(256, 8, 64), g_ukv[:, 512:1024].reshape(256, 8, 64)],
                            axis=-1).transpose(1, 0, 2)
    g_ff1 = g_ff1.reshape(1024, 8, 512).transpose(1, 0, 2)
    return jnp.concatenate([g.reshape(8, -1, WC) for g in (g_in, g_uq, g_ukv, g_out, g_ff1, g_ff2)], axis=1)


def _rope_tables(L):
    pos = (jnp.arange(L) - N_PAD).astype(f32)

    def cs(half):
        inv = ROPE_THETA ** (-jnp.arange(half, dtype=f32) / half)
        ang = pos[:, None] * inv[None, :]
        return jnp.cos(ang), jnp.sin(ang)

    c, s = cs(32)
    rc, rs = jnp.tile(c, (1, 8)), jnp.tile(s, (1, 8))
    c, s = cs(16)
    z = jnp.zeros((L, 32), f32)
    mc, ms = jnp.concatenate([c, c, z, c, c, z], 1), jnp.concatenate([s, s, z, s, s, z], 1)
    return rc, rs, mc, ms


def _layer_fwd(h, wl, gq, gkv, g1, b1, g2, b2, tabs, rtabs, next_shard):
    w_in, w_uq, w_ukv, w_out, w_ff1, w_ff2 = wl
    proj = _mm(h, w_in, "nn", "mm_in")
    sb, cqn, ckvn, rqk, rv, kr2 = _prep_fwd(proj, gq, gkv, tabs)
    q = _mm(cqn, w_uq, "nn", "mm_uq")
    kv = _mm(ckvn, w_ukv, "nn", "mm_ukv", out_dtype=bf16)
    if next_shard is None:
        out_a, gathered = _sb_fwd(sb), None
    else:
        out_a, gathered = _sb_fwd(sb, next_shard)
    out_b, lse = _mla_fwd(q, kv, kr2, tabs[2], tabs[3])
    y, out_c, states = _ret_fwd(rqk, rv, proj, rtabs)
    mixed = jnp.concatenate([out_a, out_b, out_c], axis=1)
    mix = _mm(mixed, w_out, "nn", "mm_out")
    h1 = _ln_fwd(h, mix, g1, b1, DN_ALPHA, "ln_fwd")
    u, a = _mm(h1, w_ff1, "nn", "mm_ff1", epi="relu2")
    ff = _mm(a, w_ff2, "nn", "mm_ff2")
    h2 = _ln_fwd(h1, ff, g2, b2, DN_ALPHA, "ln_fwd")
    saved = (h, proj, sb, cqn, ckvn, rqk, rv, kr2, q, kv, out_a, out_b, lse, y, states, mixed, mix, h1, u, a, ff)
    return h2, saved, gathered


def _layer_bwd(dh2, saved, wl, gq, gkv, g1, g2, tabs, rtabs, grads_above):
    w_in, w_uq, w_ukv, w_out, w_ff1, w_ff2 = wl
    h, proj, sb, cqn, ckvn, rqk, rv, kr2, q, kv, out_a, out_b, lse, y, states, mixed, mix, h1, u, a, ff = saved
    dz2, dg2, db2 = _ln_bwd(dh2, h1, ff, g2, DN_ALPHA, "ln_bwd")
    du = _mm(dz2, w_ff2, "nt", "mm_dff2", epi="mul_relu", extra=u)
    gw_ff2 = _mm(a, dz2, "tn", "mm_gff2")
    dh1 = _mm(du, w_ff1, "nt", "mm_dff1", epi="add", extra=dz2, alpha=DN_ALPHA)
    gw_ff1 = _mm(h1, du, "tn", "mm_gff1")
    dz1, dg1, db1 = _ln_bwd(dh1, h, mix, g1, DN_ALPHA, "ln_bwd")
    dmixed = _mm(dz1, w_out, "nt", "mm_dout")
    gw_out = _mm(mixed, dz1, "tn", "mm_gout")
    if grads_above is None:
        (dsq, dsk, dsv), parts_above = _sb_bwd(dmixed, sb, out_a), None
    else:
        dsq, dsk, dsv, parts_above = _sb_bwd(dmixed, sb, out_a, grads_above)
    dqn, dqr, dkn, dv, dkr_p = _mla_bwd(dmixed, q, kv, kr2, out_b, lse, tabs[2], tabs[3])
    dq = jnp.concatenate([dqn, dqr], axis=1)
    dkv = jnp.concatenate([dkn, dv], axis=1)
    dcqn = _mm(dq, w_uq, "nt", "mm_duq")
    gw_uq = _mm(cqn, dq, "tn", "mm_guq")
    dckvn = _mm(dkv, w_ukv, "nt", "mm_dukv")
    gw_ukv = _mm(ckvn, dkv, "tn", "mm_gukv")
    drq_r, drk_r, drv, drg = _ret_bwd(dmixed, rqk, rv, proj, y, states, rtabs)
    dkr_r = dkr_p[0] + dkr_p[1] + dkr_p[2] + dkr_p[3]
    dcq, dckv, drq, drk, dkr2, dgq, dgkv = _prep_bwd(proj, gq, gkv, tabs, dcqn, dckvn, drq_r, drk_r, dkr_r)
    dproj = jnp.concatenate([dsq, dsk, dsv, dckv, drq, drk, drv, drg, dcq, dkr2], axis=1)
    dh = _mm(dproj, w_in, "nt", "mm_din", epi="add", extra=dz1, alpha=DN_ALPHA)
    gw_in = _mm(h, dproj, "tn", "mm_gin")
    return dh, (gw_in, gw_uq, gw_ukv, gw_out, gw_ff1, gw_ff2), (dgq, dgkv, dg1, db1, dg2, db2), parts_above


def kernel(x, meta_tokens, ln_emb_g, ln_emb_b, w_in, mla_q_norm, mla_kv_norm, w_uq, w_ukv, w_out, ln1_g, ln1_b, w_ff1, w_ff2, ln2_g, ln2_b, loss_target, m_meta_tokens, m_ln_emb_g, m_ln_emb_b, m_w_in, m_mla_q_norm, m_mla_kv_norm, m_w_uq, m_w_ukv, m_w_out, m_ln1_g, m_ln1_b, m_w_ff1, m_w_ff2, m_ln2_g, m_ln2_b, v_meta_tokens, v_ln_emb_g, v_ln_emb_b, v_w_in, v_mla_q_norm, v_mla_kv_norm, v_w_uq, v_w_ukv, v_w_out, v_ln1_g, v_ln1_b, v_w_ff1, v_w_ff2, v_ln2_g, v_ln2_b):
    depth = w_in.shape[0]
    S = x.shape[1]
    L = S + BLK
    me = 4 * lax.axis_index("x") + 2 * lax.axis_index("y") + lax.axis_index("c")
    big = (w_in, w_uq, w_ukv, w_out, w_ff1, w_ff2)
    big_m = (m_w_in, m_w_uq, m_w_ukv, m_w_out, m_w_ff1, m_w_ff2)
    big_v = (v_w_in, v_w_uq, v_w_ukv, v_w_out, v_w_ff1, v_w_ff2)
    small = (ln_emb_g, ln_emb_b, mla_q_norm, mla_kv_norm, ln1_g, ln1_b, ln2_g, ln2_b)
    small_m = (m_ln_emb_g, m_ln_emb_b, m_mla_q_norm, m_mla_kv_norm, m_ln1_g, m_ln1_b, m_ln2_g, m_ln2_b)
    small_v = (v_ln_emb_g, v_ln_emb_b, v_mla_q_norm, v_mla_kv_norm, v_ln1_g, v_ln1_b, v_ln2_g, v_ln2_b)

    shards = [_pack_layer(big, l) for l in range(depth)]
    gathered = _exchange(shards[0].astype(bf16), False, "gather_w0")
    meta_all = _exchange(meta_tokens, False, "gather_meta")
    meta_full = meta_all.transpose(1, 0, 2).reshape(N_META, D)

    tabs = _rope_tables(L)
    rtabs = _ret_tables()

    hcat = jnp.concatenate([jnp.zeros((N_PAD, D), f32), meta_full, x[0]], axis=0)
    h = _ln_fwd(hcat, None, ln_emb_g, ln_emb_b, 1.0, "ln_emb_fwd")
    saved, full = [], []
    for l in range(depth):
        full.append(_unpack_layer(gathered))
        nxt = shards[l + 1].astype(bf16) if l + 1 < depth else None
        h, sv, gathered = _layer_fwd(h, full[l], mla_q_norm[l], mla_kv_norm[l], ln1_g[l], ln1_b[l], ln2_g[l],
                                     ln2_b[l], tabs, rtabs, nxt)
        saved.append(sv)

    dh, loss_part = _loss_head(h, loss_target[0])
    gsmall, parts, pending = [None] * depth, [None] * depth, None
    for l in reversed(range(depth)):
        dh, gbig, gsmall[l], got = _layer_bwd(dh, saved[l], full[l], mla_q_norm[l], mla_kv_norm[l], ln1_g[l],
                                              ln2_g[l], tabs, rtabs, pending)
        if pending is not None:
            parts[l + 1] = got
        pending = _pack_layer_grads(*gbig).astype(bf16)
    parts[0] = _exchange(pending, True, "scatter_g0")
    dz0, dg_emb, db_emb = _ln_bwd(dh, hcat, None, ln_emb_g, 1.0, "ln_emb_bwd")
    grad_x = dz0[BLK:][None]
    dmeta = dz0[N_PAD:BLK]

    adam = [_adamw(parts[l], shards[l], _pack_layer(big_m, l), _pack_layer(big_v, l), "adamw_big")
            for l in range(depth)]

    st = lambda k: jnp.stack([gsmall[l][k] for l in range(depth)])
    g_small = (dg_emb, db_emb, st(0), st(1), st(2), st(3), st(4), st(5))
    n_small = sum(int(np.prod(a.shape)) for a in small)
    flat = jnp.concatenate([a.reshape(-1) for a in g_small] + [dmeta.reshape(-1), loss_part[0, 0:1]])
    rows = -(-(flat.shape[0]) // (8 * D)) * 8
    pad = rows * D - flat.shape[0]
    flat = jnp.concatenate([flat, jnp.zeros((pad,), f32)]).reshape(rows, D)
    parts_s = _exchange(flat, False, "gather_small")

    def pack_small(arrs, meta_shard):
        col = jnp.zeros((N_META, D), f32)
        col = lax.dynamic_update_slice(col, meta_shard, (0, me * 128))
        fl = jnp.concatenate([a.reshape(-1) for a in arrs] + [col.reshape(-1), jnp.zeros((1 + pad,), f32)])
        return fl.reshape(rows, D)

    g_s, d_s, m_s, v_s = _adamw(parts_s, pack_small(small, meta_tokens), pack_small(small_m, m_meta_tokens),
                                pack_small(small_v, v_meta_tokens), "adamw_small")
    loss = g_s.reshape(-1)[n_small + N_META * D]

    def unpack_big(which):
        outs, off = [], 0
        for w, r in zip(big, LAYER_ROWS):
            outs.append(jnp.stack([adam[l][which][off:off + r].reshape(w.shape[1:]) for l in range(depth)]))
            off += r
        return outs

    def unpack_small(flat_rows):
        fl = flat_rows.reshape(-1)
        outs, off = [], 0
        for a in small:
            n = int(np.prod(a.shape))
            outs.append(fl[off:off + n].reshape(a.shape))
            off += n
        meta = lax.dynamic_slice(fl[off:off + N_META * D].reshape(N_META, D), (0, me * 128), (N_META, 128))
        return meta, outs

    def assemble(which, small_rows_arr):
        b = unpack_big(which)
        meta, s = unpack_small(small_rows_arr)
        return [meta, s[0], s[1], b[0], s[2], s[3], b[1], b[2], b[3], s[4], s[5], b[4], b[5], s[6], s[7]]

    return (loss, grad_x, *assemble(0, g_s), *assemble(1, d_s), *assemble(2, m_s), *assemble(3, v_s))
```

```python
import functools
import math

import numpy as np
import jax
import jax.numpy as jnp
from jax import lax
from jax.experimental import pallas as pl
from jax.experimental.pallas import tpu as pltpu

f32 = jnp.float32
bf16 = jnp.bfloat16
_MXU = jnp.bfloat16

BLK = 128
N_META = 16
N_PAD = 112
D = 1024
N_DEV = 8
LN_EPS = 1e-5
DEPTH = 4
DN_ALPHA = (2 * DEPTH) ** 0.25
ROPE_THETA = 10000.0
MLA_SCALE = (64 + 32) ** -0.5
SB_SCALE = 0.125
RET_SCALE = 0.125
RET_GAMMA = tuple(1.0 - 2.0 ** (-5 - h) for h in range(4))

ADAM_LR, ADAM_B1, ADAM_B2, ADAM_EPS, ADAM_WD, ADAM_STEP = 0.001, 0.9, 0.999, 1e-08, 0.01, 10

C_SBQ, C_SBK, C_SBV, C_CKV, C_RQ, C_RK, C_RV, C_RG, C_CQ, C_KR, N_INP = (
    0, 512, 1024, 1536, 1792, 2048, 2304, 2816, 3328, 3712, 3840)

VMEM_LIMIT = 56 * 1024 * 1024


def _cp(sem):
    return pltpu.CompilerParams(dimension_semantics=sem, vmem_limit_bytes=VMEM_LIMIT)


def _pick(n, cands):
    for c in cands:
        if n % c == 0:
            return c
    return n


def _dot(a, b, dims=(((1,), (0,)), ((), ()))):
    return lax.dot_general(a.astype(_MXU), b.astype(_MXU), dims, preferred_element_type=f32)


NT = (((1,), (1,)), ((), ()))
TN = (((0,), (0,)), ((), ()))


def _dot3(x, u):
    hi = x.astype(_MXU)
    r1 = x - hi.astype(f32)
    mid = r1.astype(_MXU)
    lo = (r1 - mid.astype(f32)).astype(_MXU)
    return (jnp.dot(hi, u, preferred_element_type=f32) + jnp.dot(mid, u, preferred_element_type=f32)
            + jnp.dot(lo, u, preferred_element_type=f32))


def _rot(x, half):
    lane = lax.broadcasted_iota(jnp.int32, x.shape, 1)
    first = (lane % 64) < half
    return jnp.where(first, -pltpu.roll(x, 128 - half, 1), pltpu.roll(x, half, 1))


def _mm(a, b, mode, name, epi=None, extra=None, alpha=1.0, out_dtype=f32):
    if mode == "nn":
        (M, K), N = a.shape, b.shape[1]
    elif mode == "nt":
        (M, K), N = a.shape, b.shape[0]
    else:
        (K, M), N = a.shape, b.shape[1]
    tm = _pick(M, (1408, 1024, 768, 512, 384, 256, 128))
    tn = _pick(N, ((1920,) if mode == "tn" else ()) + (1024, 768, 512, 384, 256, 128))
    tk = _pick(K, (1024, 768, 512, 384, 256, 128))
    nk = K // tk
    if mode == "nn":
        a_spec = pl.BlockSpec((tm, tk), lambda i, j, k: (i, k))
        b_spec = pl.BlockSpec((tk, tn), lambda i, j, k: (k, j))
        dims = (((1,), (0,)), ((), ()))
    elif mode == "nt":
        a_spec = pl.BlockSpec((tm, tk), lambda i, j, k: (i, k))
        b_spec = pl.BlockSpec((tn, tk), lambda i, j, k: (j, k))
        dims = NT
    else:
        a_spec = pl.BlockSpec((tk, tm), lambda i, j, k: (k, i))
        b_spec = pl.BlockSpec((tk, tn), lambda i, j, k: (k, j))
        dims = TN
    o_spec = pl.BlockSpec((tm, tn), lambda i, j, k: (i, j))
    in_specs, args = [a_spec, b_spec], [a, b]
    if extra is not None:
        in_specs.append(o_spec)
        args.append(extra)
    if epi == "relu2":
        out_shape = (jax.ShapeDtypeStruct((M, N), f32), jax.ShapeDtypeStruct((M, N), bf16))
        out_specs = (o_spec, o_spec)
    else:
        out_shape = jax.ShapeDtypeStruct((M, N), out_dtype)
        out_specs = o_spec

    def body(*refs):
        a_ref, b_ref = refs[0], refs[1]
        acc = refs[-1]
        k = pl.program_id(2)

        @pl.when(k == 0)
        def _():
            acc[...] = jnp.zeros_like(acc)

        acc[...] += _dot(a_ref[...], b_ref[...], dims)

        @pl.when(k == nk - 1)
        def _():
            r = acc[...]
            if epi == "relu2":
                refs[2][...] = r
                refs[3][...] = jnp.square(jnp.maximum(r, 0.0)).astype(bf16)
            elif epi == "mul_relu":
                refs[3][...] = r * (2.0 * jnp.maximum(refs[2][...], 0.0))
            elif epi == "add":
                refs[3][...] = r + alpha * refs[2][...]
            else:
                refs[2][...] = r.astype(out_dtype)

    return pl.pallas_call(
        body, name=name, grid=(M // tm, N // tn, nk), in_specs=in_specs, out_specs=out_specs,
        out_shape=out_shape, scratch_shapes=[pltpu.VMEM((tm, tn), f32)],
        compiler_params=_cp(("parallel", "parallel", "arbitrary")))(*args)


def _ln_fwd(h, m, g, b, alpha, name):
    L = h.shape[0]
    tm = _pick(L, (384, 256, 128))
    row = pl.BlockSpec((tm, D), lambda i: (i, 0))
    vec = pl.BlockSpec((1, D), lambda i: (0, 0))

    def body(*refs):
        if m is None:
            h_ref, g_ref, b_ref, o_ref = refs
            z = h_ref[...]
        else:
            h_ref, m_ref, g_ref, b_ref, o_ref = refs
            z = alpha * h_ref[...] + m_ref[...]
        mu = jnp.mean(z, -1, keepdims=True)
        var = jnp.mean(jnp.square(z - mu), -1, keepdims=True)
        o_ref[...] = (z - mu) * lax.rsqrt(var + LN_EPS) * g_ref[...] + b_ref[...]

    args = [h] + ([] if m is None else [m]) + [g.reshape(1, D), b.reshape(1, D)]
    specs = [row] + ([] if m is None else [row]) + [vec, vec]
    return pl.pallas_call(body, name=name, grid=(L // tm,), in_specs=specs, out_specs=row,
                          out_shape=jax.ShapeDtypeStruct((L, D), f32), compiler_params=_cp(("parallel",)))(*args)


def _ln_bwd(dy, h, m, g, alpha, name):
    L = h.shape[0]
    tm = _pick(L, (384, 256, 128))
    row = pl.BlockSpec((tm, D), lambda i: (i, 0))
    vec = pl.BlockSpec((1, D), lambda i: (0, 0))
    acc = pl.BlockSpec((8, D), lambda i: (0, 0))

    def body(*refs):
        if m is None:
            dy_ref, h_ref, g_ref, dz_ref, dg_ref, db_ref = refs
            z = h_ref[...]
        else:
            dy_ref, h_ref, m_ref, g_ref, dz_ref, dg_ref, db_ref = refs
            z = alpha * h_ref[...] + m_ref[...]

        @pl.when(pl.program_id(0) == 0)
        def _():
            dg_ref[...] = jnp.zeros_like(dg_ref)
            db_ref[...] = jnp.zeros_like(db_ref)

        dyv = dy_ref[...]
        mu = jnp.mean(z, -1, keepdims=True)
        zc = z - mu
        rstd = lax.rsqrt(jnp.mean(jnp.square(zc), -1, keepdims=True) + LN_EPS)
        xh = zc * rstd
        dxh = dyv * g_ref[...]
        dz_ref[...] = rstd * (dxh - jnp.mean(dxh, -1, keepdims=True) - xh * jnp.mean(dxh * xh, -1, keepdims=True))
        dg_ref[0:1, :] += jnp.sum(dyv * xh, 0, keepdims=True)
        db_ref[0:1, :] += jnp.sum(dyv, 0, keepdims=True)

    args = [dy, h] + ([] if m is None else [m]) + [g.reshape(1, D)]
    specs = [row, row] + ([] if m is None else [row]) + [vec]
    dz, dg, db = pl.pallas_call(
        body, name=name, grid=(L // tm,), in_specs=specs, out_specs=(row, acc, acc),
        out_shape=(jax.ShapeDtypeStruct((L, D), f32), jax.ShapeDtypeStruct((8, D), f32),
                   jax.ShapeDtypeStruct((8, D), f32)),
        compiler_params=_cp(("arbitrary",)))(*args)
    return dz, dg[0], db[0]


def _rms(x, g):
    r = lax.rsqrt(jnp.mean(jnp.square(x), -1, keepdims=True) + LN_EPS)
    return x * r * g


def _prep_fwd(proj, gq, gkv, tabs):
    L = proj.shape[0]
    tm = BLK
    rc, rs, mc, ms = tabs

    def body(p_ref, gq_ref, gkv_ref, rc_ref, rs_ref, mc_ref, ms_ref, sb_ref, cq_ref, ckv_ref, rqk_ref, rv_ref, kr_ref):
        i = pl.program_id(0)
        sb_ref[:, 0:512] = (p_ref[:, C_SBQ:C_SBQ + 512] * SB_SCALE).astype(bf16)
        sb_ref[:, 512:1536] = p_ref[:, C_SBK:C_SBK + 1024].astype(bf16)
        cq_ref[...] = _rms(p_ref[:, C_CQ:C_CQ + 384], gq_ref[...]).astype(bf16)
        ckv_ref[...] = _rms(p_ref[:, C_CKV:C_CKV + 256], gkv_ref[...]).astype(bf16)
        valid = (i * tm + lax.broadcasted_iota(jnp.int32, (tm, 128), 0)) >= N_PAD
        for c in range(2):
            sl = slice(c * 128, (c + 1) * 128)
            x = p_ref[:, C_RQ + c * 128:C_RQ + (c + 1) * 128]
            rqk_ref[:, sl] = (x * rc_ref[:, sl] + _rot(x, 32) * rs_ref[:, sl]).astype(bf16)
            x = p_ref[:, C_RK + c * 128:C_RK + (c + 1) * 128]
            kk = (x * rc_ref[:, sl] + _rot(x, 32) * rs_ref[:, sl]) * RET_SCALE
            rqk_ref[:, 256 + c * 128:256 + (c + 1) * 128] = jnp.where(valid, kk, 0.0).astype(bf16)
        rv_ref[...] = p_ref[:, C_RV:C_RV + 512].astype(bf16)
        x = p_ref[:, C_KR:C_KR + 128]
        kr_ref[...] = (x * mc_ref[...] + _rot(x, 16) * ms_ref[...]).astype(bf16)

    def row(w):
        return pl.BlockSpec((tm, w), lambda i: (i, 0))

    def vec(w):
        return pl.BlockSpec((1, w), lambda i: (0, 0))

    widths = (1536, 384, 256, 512, 512, 128)
    return pl.pallas_call(
        body, name="prep_fwd", grid=(L // tm,),
        in_specs=[row(N_INP), vec(384), vec(256), row(256), row(256), row(128), row(128)],
        out_specs=tuple(row(w) for w in widths),
        out_shape=tuple(jax.ShapeDtypeStruct((L, w), bf16) for w in widths),
        compiler_params=_cp(("parallel",)))(proj, gq.reshape(1, 384), gkv.reshape(1, 256), rc, rs, mc, ms)


def _rms_bwd(x, g, dy):
    r = lax.rsqrt(jnp.mean(jnp.square(x), -1, keepdims=True) + LN_EPS)
    u = dy * g
    dx = r * u - x * (r * r * r) * jnp.mean(x * u, -1, keepdims=True)
    return dx, jnp.sum(dy * x * r, 0, keepdims=True)


def _prep_bwd(proj, gq, gkv, tabs, dcqn, dckvn, drq_r, drk_r, dkr_r):
    L = proj.shape[0]
    tm = BLK
    rc, rs, mc, ms = tabs

    def body(p_ref, gq_ref, gkv_ref, rc_ref, rs_ref, mc_ref, ms_ref, dcqn_ref, dckvn_ref, drq_ref, drk_ref,
             dkr_ref, ocq_ref, ockv_ref, orq_ref, ork_ref, okr_ref, dgq_ref, dgkv_ref):
        i = pl.program_id(0)

        @pl.when(i == 0)
        def _():
            dgq_ref[...] = jnp.zeros_like(dgq_ref)
            dgkv_ref[...] = jnp.zeros_like(dgkv_ref)

        dx, dg = _rms_bwd(p_ref[:, C_CQ:C_CQ + 384], gq_ref[...], dcqn_ref[...])
        ocq_ref[...] = dx
        dgq_ref[0:1, :] += dg
        dx, dg = _rms_bwd(p_ref[:, C_CKV:C_CKV + 256], gkv_ref[...], dckvn_ref[...])
        ockv_ref[...] = dx
        dgkv_ref[0:1, :] += dg
        valid = (i * tm + lax.broadcasted_iota(jnp.int32, (tm, 128), 0)) >= N_PAD
        for c in range(2):
            sl = slice(c * 128, (c + 1) * 128)
            dy = drq_ref[:, sl]
            orq_ref[:, sl] = dy * rc_ref[:, sl] - _rot(dy * rs_ref[:, sl], 32)
            dy = jnp.where(valid, drk_ref[:, sl], 0.0) * RET_SCALE
            ork_ref[:, sl] = dy * rc_ref[:, sl] - _rot(dy * rs_ref[:, sl], 32)
        dy = dkr_ref[...]
        okr_ref[...] = dy * mc_ref[...] - _rot(dy * ms_ref[...], 16)

    def row(w):
        return pl.BlockSpec((tm, w), lambda i: (i, 0))

    def vec(w):
        return pl.BlockSpec((1, w), lambda i: (0, 0))

    def acc(w):
        return pl.BlockSpec((8, w), lambda i: (0, 0))

    widths = (384, 256, 256, 256, 128)
    outs = pl.pallas_call(
        body, name="prep_bwd", grid=(L // tm,),
        in_specs=[row(N_INP), vec(384), vec(256), row(256), row(256), row(128), row(128),
                  row(384), row(256), row(256), row(256), row(128)],
        out_specs=tuple(row(w) for w in widths) + (acc(384), acc(256)),
        out_shape=tuple(jax.ShapeDtypeStruct((L, w), f32) for w in widths)
        + (jax.ShapeDtypeStruct((8, 384), f32), jax.ShapeDtypeStruct((8, 256), f32)),
        compiler_params=_cp(("arbitrary",)))(
            proj, gq.reshape(1, 384), gkv.reshape(1, 256), rc, rs, mc, ms, dcqn, dckvn, drq_r, drk_r, dkr_r)
    return outs[:5] + (outs[5][0], outs[6][0])


def _qrows(L):
    return _pick(L, (384, 256, 128))


def _tri_ones(strict):
    r = lax.broadcasted_iota(jnp.int32, (BLK, 2 * BLK), 0)
    c = lax.broadcasted_iota(jnp.int32, (BLK, 2 * BLK), 1)
    tri = (r > c) if strict else (r >= c)
    return jnp.where(tri | (c >= BLK), 1.0, 0.0).astype(_MXU)


def _dot2(x, u):
    hi = x.astype(_MXU)
    lo = (x - hi.astype(f32)).astype(_MXU)
    return jnp.dot(hi, u, preferred_element_type=f32) + jnp.dot(lo, u, preferred_element_type=f32)


def _staggered(chains):
    live = list(chains)
    step = 0
    while live:
        for ci, g in enumerate(chains):
            if g in live and step >= ci and next(g, True):
                live.remove(g)
        step += 1


def _pair_rhs(t, m):
    zt = jnp.zeros_like(t)
    return jnp.concatenate([jnp.where(m, t, zt), jnp.where(m, zt, t)], axis=0)


def _sb_stages(z, mask, c_ref, x, u_gt, out):
    yield
    lb = jnp.minimum(z, 0.0) - jnp.log1p(jnp.exp(-jnp.abs(z)))
    lk = lb - z
    if mask is not None:
        lk = jnp.where(mask, lk, 0.0)
    hi = lk.astype(_MXU)
    lo = (lk - hi.astype(f32)).astype(_MXU)
    yield
    el = (jnp.dot(hi, u_gt, preferred_element_type=f32)
          + jnp.dot(lo, u_gt, preferred_element_type=f32))
    yield
    c = c_ref[x]
    w = jnp.exp(lb + el[:, 0:BLK] + c)
    if mask is not None:
        w = jnp.where(mask, w, 0.0)
    c_ref[x] = c + el[:, BLK:2 * BLK]
    out["w"], out["lb"] = w.astype(_MXU), lb


def _sb_sweep(i, r, tiles, per_pass):
    n_t = r * (i + 1)
    lax.fori_loop(0, r, lambda jj, c: tiles([n_t - 1 - jj], True) or c, 0)
    n_bulk = jnp.maximum(r * i - 1, 0)
    n_full = n_bulk // per_pass
    lax.fori_loop(0, n_full,
                  lambda jj, c: tiles([r * i - 1 - per_pass * jj - t for t in range(per_pass)], False) or c, 0)
    rem = n_bulk - n_full * per_pass

    if per_pass == 4:
        @pl.when(rem >= 2)
        def _():
            tiles([rem, rem - 1], False)

    @pl.when(rem % 2 == 1)
    def _():
        tiles([1], False)

    @pl.when(i > 0)
    def _():
        tiles([0], True)


def _tile_off(j):
    return j * BLK if isinstance(j, int) else pl.multiple_of(j * BLK, BLK)


def _sb_mask(i, j, qb):
    row = i * qb + lax.broadcasted_iota(jnp.int32, (qb, BLK), 0)
    col = j * BLK + lax.broadcasted_iota(jnp.int32, (qb, BLK), 1)
    return (col < row) & (col >= N_PAD)


def _first_last(n0, n1):
    p, i = pl.program_id(0), pl.program_id(1)
    return (p == 0) & (i == 0), (p == n0 - 1) & (i == n1 - 1)


def _sb_fwd(sb, bg=None):
    L = sb.shape[0]
    qb = _qrows(L)
    nq, r = L // qb, qb // BLK

    def body(*refs):
        if bg is None:
            q_ref, k_ref, v_ref, o_ref, acc_ref, c_ref = refs
        else:
            q_ref, k_ref, v_ref, x_ref, o_ref, g_ref, acc_ref, c_ref = refs[:8]
            start, wait = _xchg_ops(x_ref, g_ref, *refs[8:], False)
            first, last = _first_last(4, nq)
            pl.when(first)(start)
        i = pl.program_id(1)
        m_a = lax.broadcasted_iota(jnp.int32, (1, BLK), 1) < 64
        q = q_ref[...]
        u_gt = _tri_ones(True)
        acc_ref[...] = jnp.zeros_like(acc_ref)
        c_ref[...] = jnp.zeros_like(c_ref)

        def chain(x, j, masked, both):
            off = _tile_off(j)
            if x == 0:
                both["z"] = _dot(q, _pair_rhs(k_ref[pl.ds(off, BLK), :], m_a), NT)
            mask = _sb_mask(i, j, qb) if masked else None
            o = {}
            yield from _sb_stages(both["z"][:, x * BLK:(x + 1) * BLK], mask, c_ref, x, u_gt, o)
            yield
            acc_ref[x] += _dot(o["w"], v_ref[pl.ds(off, BLK), :])

        def tiles(js, masked):
            shared = [{} for _ in js]
            _staggered([chain(x, j, masked, shared[t]) for t, j in enumerate(js) for x in range(2)])

        _sb_sweep(i, r, tiles, 4)
        o_ref[...] = jnp.where(m_a, acc_ref[0], acc_ref[1])
        if bg is not None:
            pl.when(last)(wait)

    in_specs = [pl.BlockSpec((qb, 128), lambda p, i: (i, p)),
                pl.BlockSpec((L, 128), lambda p, i: (0, 4 + p)),
                pl.BlockSpec((L, 128), lambda p, i: (0, 8 + p))]
    o_spec = pl.BlockSpec((qb, 128), lambda p, i: (i, p))
    o_shape = jax.ShapeDtypeStruct((L, 512), f32)
    scratch = [pltpu.VMEM((2, qb, BLK), f32), pltpu.VMEM((2, qb, BLK), f32)]
    if bg is None:
        return pl.pallas_call(body, name="sb_fwd", grid=(4, nq), in_specs=in_specs, out_specs=o_spec,
                              out_shape=o_shape, scratch_shapes=scratch,
                              compiler_params=_cp(("parallel", "arbitrary")))(sb, sb, sb)
    return pl.pallas_call(body, name="sb_fwd_gather", grid=(4, nq), in_specs=in_specs + [_HBM],
                          out_specs=(o_spec, _HBM), out_shape=(o_shape, _xchg_shape(bg, False)),
                          scratch_shapes=scratch + list(_XCHG_SEMS),
                          compiler_params=_cp(("arbitrary", "arbitrary")))(sb, sb, sb, bg)


def _sb_bwd(dmixed, sb, out_a, bg=None):
    L = sb.shape[0]
    qb = _qrows(L)
    nq, r = L // qb, qb // BLK

    def body(*refs):
        if bg is None:
            do_ref, o_ref, q_ref, k_ref, v_ref, dq_ref, dk_ref, dv_ref, dqa_ref, c_ref, cg_ref, ds_ref = refs
        else:
            do_ref, o_ref, q_ref, k_ref, v_ref, x_ref, dq_ref, dk_ref, dv_ref, g_ref = refs[:10]
            dqa_ref, c_ref, cg_ref, ds_ref = refs[10:14]
            start, wait = _xchg_ops(x_ref, g_ref, *refs[14:], True)
            first, last = _first_last(4, nq)
            pl.when(first)(start)
        i = pl.program_id(1)

        @pl.when(i == 0)
        def _():
            dk_ref[...] = jnp.zeros_like(dk_ref)
            dv_ref[...] = jnp.zeros_like(dv_ref)

        m_a = lax.broadcasted_iota(jnp.int32, (1, BLK), 1) < 64
        q = q_ref[...]
        zq = jnp.zeros_like(q)
        qs = (jnp.where(m_a, q, zq), jnp.where(m_a, zq, q))
        do = do_ref[...]
        zd = jnp.zeros_like(do)
        dos = (jnp.where(m_a, do, zd).astype(_MXU), jnp.where(m_a, zd, do).astype(_MXU))
        do_b = do.astype(_MXU)
        prod = do_b.astype(f32) * o_ref[...]
        ds_ref[0] = jnp.broadcast_to(jnp.sum(jnp.where(m_a, prod, 0.0), 1, keepdims=True), (qb, BLK))
        ds_ref[1] = jnp.broadcast_to(jnp.sum(jnp.where(m_a, 0.0, prod), 1, keepdims=True), (qb, BLK))
        u_gt = _tri_ones(True)
        u_ge = _tri_ones(False)
        dqa_ref[...] = jnp.zeros_like(dqa_ref)
        c_ref[...] = jnp.zeros_like(c_ref)
        cg_ref[...] = jnp.zeros_like(cg_ref)

        def chain(x, j, masked, both):
            off = _tile_off(j)
            k = k_ref[pl.ds(off, BLK), :]
            if x == 0:
                both["z"] = _dot(q, _pair_rhs(k, m_a), NT)
                both["dw"] = _dot(do_b, _pair_rhs(v_ref[pl.ds(off, BLK), :], m_a), NT)
            mask = _sb_mask(i, j, qb) if masked else None
            o = {}
            yield from _sb_stages(both["z"][:, x * BLK:(x + 1) * BLK], mask, c_ref, x, u_gt, o)
            wb = o["w"]
            gr = wb.astype(f32) * both["dw"][:, x * BLK:(x + 1) * BLK]
            hi = gr.astype(_MXU)
            lo = (gr - hi.astype(f32)).astype(_MXU)
            yield
            eg = (jnp.dot(hi, u_ge, preferred_element_type=f32)
                  + jnp.dot(lo, u_ge, preferred_element_type=f32))
            yield
            cg = cg_ref[x]
            suffix = eg[:, 0:BLK] + cg
            cg_ref[x] = cg + eg[:, BLK:2 * BLK]
            dz = gr - jnp.exp(o["lb"]) * (gr + ds_ref[x] - suffix)
            if masked:
                dz = jnp.where(mask, dz, 0.0)
            dz = dz.astype(_MXU)
            yield
            dqa_ref[x] += _dot(dz, k)
            dk_ref[pl.ds(off, BLK), :] += _dot(dz, qs[x], TN)
            dv_ref[pl.ds(off, BLK), :] += _dot(wb, dos[x], TN)

        def tiles(js, masked):
            shared = [{} for _ in js]
            _staggered([chain(x, j, masked, shared[t]) for t, j in enumerate(js) for x in range(2)])

        _sb_sweep(i, r, tiles, 2)
        dq_ref[...] = jnp.where(m_a, dqa_ref[0], dqa_ref[1]) * SB_SCALE
        if bg is not None:
            pl.when(last)(wait)

    blk = pl.BlockSpec((qb, 128), lambda p, i: (i, p))
    scr = pltpu.VMEM((2, qb, BLK), f32)
    in_specs = [blk, blk, blk, pl.BlockSpec((L, 128), lambda p, i: (0, 4 + p)),
                pl.BlockSpec((L, 128), lambda p, i: (0, 8 + p))]
    out_specs = (blk, pl.BlockSpec((L, 128), lambda p, i: (0, p)), pl.BlockSpec((L, 128), lambda p, i: (0, p)))
    out_shape = tuple(jax.ShapeDtypeStruct((L, 512), f32) for _ in range(3))
    if bg is None:
        return pl.pallas_call(body, name="sb_bwd", grid=(4, nq), in_specs=in_specs, out_specs=out_specs,
                              out_shape=out_shape, scratch_shapes=[scr, scr, scr, scr],
                              compiler_params=_cp(("parallel", "arbitrary")))(dmixed, out_a, sb, sb, sb)
    return pl.pallas_call(body, name="sb_bwd_scatter", grid=(4, nq), in_specs=in_specs + [_HBM],
                          out_specs=out_specs + (_HBM,), out_shape=out_shape + (_xchg_shape(bg, True),),
                          scratch_shapes=[scr, scr, scr, scr] + list(_XCHG_SEMS),
                          compiler_params=_cp(("arbitrary", "arbitrary")))(dmixed, out_a, sb, sb, sb, bg)


def _mla_mask(i, j, qb):
    row = i * qb + lax.broadcasted_iota(jnp.int32, (qb, BLK), 0)
    col = j * BLK + lax.broadcasted_iota(jnp.int32, (qb, BLK), 1)
    return (col <= row) & ((col >= N_PAD) | (col == row))


def _pair_mask2():
    return (lax.broadcasted_iota(jnp.int32, (1, 256), 1) % 128) < 64


def _mla_q2(qn_ref, qr_ref, mc_ref, ms_ref):
    qr = qr_ref[...]
    qr = qr * mc_ref[...] + _rot(qr, 16) * ms_ref[...]
    q2 = jnp.concatenate([qn_ref[...], qr], axis=1)
    m2 = _pair_mask2()
    z2 = jnp.zeros_like(q2)
    return (jnp.where(m2, q2, z2).astype(_MXU), jnp.where(m2, z2, q2).astype(_MXU)), q2.astype(_MXU)


def _mla_fwd(q, kv, kr2, mc, ms):
    L = q.shape[0]
    qb = _qrows(L)
    nq, r = L // qb, qb // BLK

    def body(qn_ref, qr_ref, mc_ref, ms_ref, kn_ref, v_ref, kr_ref, o_ref, lse_ref, acc_ref, m_ref):
        i = pl.program_id(1)
        m_a = lax.broadcasted_iota(jnp.int32, (1, BLK), 1) < 64
        _, q2 = _mla_q2(qn_ref, qr_ref, mc_ref, ms_ref)
        m2 = _pair_mask2()
        acc_ref[...] = jnp.zeros_like(acc_ref)
        m_ref[...] = jnp.full(m_ref.shape, -1e30, f32)
        ones = jnp.ones((BLK, BLK), _MXU)

        def chain(x, j, masked, both):
            off = _tile_off(j)
            if x == 0:
                k2 = jnp.concatenate([kn_ref[pl.ds(off, BLK), :], kr_ref[pl.ds(off, BLK), :]], axis=1)
                both["s"] = _dot(q2, _pair_rhs(k2, m2), NT)
            yield
            s = both["s"][:, x * BLK:(x + 1) * BLK] * MLA_SCALE
            if masked:
                mask = _mla_mask(i, j, qb)
                s = jnp.where(mask, s, -1e30)
            m_old = m_ref[x]
            m_new = jnp.maximum(m_old, jnp.max(s, 1, keepdims=True))
            a = jnp.exp(m_old - m_new)
            p = jnp.exp(s - m_new)
            if masked:
                p = jnp.where(mask, p, 0.0)
            m_ref[x] = m_new
            p = p.astype(_MXU)
            yield
            v1 = jnp.concatenate([v_ref[pl.ds(off, BLK), :], ones], axis=1)
            acc_ref[x] = jnp.concatenate([a, a], axis=1) * acc_ref[x] + _dot(p, v1)

        def tiles(js, masked):
            shared = [{} for _ in js]
            _staggered([chain(x, j, masked, shared[t]) for t, j in enumerate(js) for x in range(2)])

        _sb_sweep(i, r, tiles, 4)
        o_ref[...] = jnp.where(m_a, acc_ref[0, :, 0:BLK] / acc_ref[0, :, BLK:2 * BLK],
                               acc_ref[1, :, 0:BLK] / acc_ref[1, :, BLK:2 * BLK])
        for x in range(2):
            lse_ref[0, x] = m_ref[x] + jnp.log(acc_ref[x, :, BLK:2 * BLK])

    blk = lambda cb: pl.BlockSpec((qb, 128), lambda p, i: (i, cb + p))
    full = lambda cb: pl.BlockSpec((L, 128), lambda p, i: (0, cb + p))
    tab = pl.BlockSpec((qb, 128), lambda p, i: (i, 0))
    return pl.pallas_call(
        body, name="mla_fwd", grid=(4, nq),
        in_specs=[blk(0), blk(4), tab, tab, full(0), full(4), pl.BlockSpec((L, 128), lambda p, i: (0, 0))],
        out_specs=(blk(0), pl.BlockSpec((1, 2, qb, 128), lambda p, i: (p, 0, i, 0))),
        out_shape=(jax.ShapeDtypeStruct((L, 512), f32), jax.ShapeDtypeStruct((4, 2, L, 128), f32)),
        scratch_shapes=[pltpu.VMEM((2, qb, 2 * BLK), f32), pltpu.VMEM((2, qb, BLK), f32)],
        compiler_params=_cp(("parallel", "arbitrary")))(q, q, mc, ms, kv, kv, kr2)


def _mla_bwd(dmixed, q, kv, kr2, out_b, lse, mc, ms):
    L = q.shape[0]
    qb = _qrows(L)
    nq, r = L // qb, qb // BLK

    def body(do_ref, o_ref, lse_ref, qn_ref, qr_ref, mc_ref, ms_ref, kn_ref, v_ref, kr_ref,
             dqn_ref, dqr_ref, dkn_ref, dv_ref, dkr_ref, dqa_ref, ds_ref):
        i = pl.program_id(1)

        @pl.when(i == 0)
        def _():
            dkn_ref[...] = jnp.zeros_like(dkn_ref)
            dv_ref[...] = jnp.zeros_like(dv_ref)
            dkr_ref[...] = jnp.zeros_like(dkr_ref)

        m_a = lax.broadcasted_iota(jnp.int32, (1, BLK), 1) < 64
        qs, q2 = _mla_q2(qn_ref, qr_ref, mc_ref, ms_ref)
        m2 = _pair_mask2()
        do = do_ref[...]
        zd = jnp.zeros_like(do)
        dos = (jnp.where(m_a, do, zd).astype(_MXU), jnp.where(m_a, zd, do).astype(_MXU))
        do_b = do.astype(_MXU)
        prod = do * o_ref[...]
        ds_ref[0] = jnp.broadcast_to(jnp.sum(jnp.where(m_a, prod, 0.0), 1, keepdims=True), (qb, BLK))
        ds_ref[1] = jnp.broadcast_to(jnp.sum(jnp.where(m_a, 0.0, prod), 1, keepdims=True), (qb, BLK))
        dqa_ref[...] = jnp.zeros_like(dqa_ref)

        def chain(x, j, masked, both):
            off = _tile_off(j)
            k2 = jnp.concatenate([kn_ref[pl.ds(off, BLK), :], kr_ref[pl.ds(off, BLK), :]], axis=1)
            if x == 0:
                both["s"] = _dot(q2, _pair_rhs(k2, m2), NT)
                both["dp"] = _dot(do_b, _pair_rhs(v_ref[pl.ds(off, BLK), :], m_a), NT)
            yield
            s = both["s"][:, x * BLK:(x + 1) * BLK] * MLA_SCALE
            dp = both["dp"][:, x * BLK:(x + 1) * BLK]
            if masked:
                mask = _mla_mask(i, j, qb)
                p = jnp.where(mask, jnp.exp(jnp.where(mask, s, 0.0) - lse_ref[0, x]), 0.0)
            else:
                p = jnp.exp(s - lse_ref[0, x])
            pb = p.astype(_MXU)
            ds = (p * (dp - ds_ref[x]) * MLA_SCALE).astype(_MXU)
            yield
            dqa_ref[x] += _dot(ds, k2)
            dk_t = _dot(ds, qs[x], TN)
            dkn_ref[pl.ds(off, BLK), :] += dk_t[:, 0:128]
            dkr_ref[0, pl.ds(off, BLK), :] += dk_t[:, 128:256]
            dv_ref[pl.ds(off, BLK), :] += _dot(pb, dos[x], TN)

        def tiles(js, masked):
            shared = [{} for _ in js]
            _staggered([chain(x, j, masked, shared[t]) for t, j in enumerate(js) for x in range(2)])

        _sb_sweep(i, r, tiles, 2)
        dq2 = jnp.where(_pair_mask2(), dqa_ref[0], dqa_ref[1])
        dqn_ref[...] = dq2[:, 0:128]
        dy = dq2[:, 128:256]
        dqr_ref[...] = dy * mc_ref[...] - _rot(dy * ms_ref[...], 16)

    blk = lambda cb: pl.BlockSpec((qb, 128), lambda p, i: (i, cb + p))
    full = lambda cb: pl.BlockSpec((L, 128), lambda p, i: (0, cb + p))
    tab = pl.BlockSpec((qb, 128), lambda p, i: (i, 0))
    o512 = jax.ShapeDtypeStruct((L, 512), f32)
    return pl.pallas_call(
        body, name="mla_bwd", grid=(4, nq),
        in_specs=[blk(4), blk(0), pl.BlockSpec((1, 2, qb, 128), lambda p, i: (p, 0, i, 0)), blk(0), blk(4), tab, tab,
                  full(0), full(4), pl.BlockSpec((L, 128), lambda p, i: (0, 0))],
        out_specs=(blk(0), blk(0), full(0), full(0), pl.BlockSpec((1, L, 128), lambda p, i: (p, 0, 0))),
        out_shape=(o512, o512, o512, o512, jax.ShapeDtypeStruct((4, L, 128), f32)),
        scratch_shapes=[pltpu.VMEM((2, qb, 2 * BLK), f32), pltpu.VMEM((2, qb, BLK), f32)],
        compiler_params=_cp(("parallel", "arbitrary")))(dmixed, out_b, lse, q, q, mc, ms, kv, kv, kr2)


def _ret_tables():
    log_g = jnp.log(jnp.array(RET_GAMMA, f32))
    idx = jnp.arange(BLK, dtype=f32)
    diff = idx[:, None] - idx[None, :]
    d_in = jnp.where(diff[None] >= 0, jnp.exp(jnp.maximum(diff, 0.0)[None] * log_g[:, None, None]), 0.0)
    q_dec = jnp.exp((idx[None, :] + 1.0) * log_g[:, None])
    k_dec = jnp.exp((BLK - 1.0 - idx[None, :]) * log_g[:, None])
    c_dec = jnp.exp(BLK * log_g)
    bc = lambda a: jnp.broadcast_to(a[:, :, None], (4, BLK, BLK))
    return d_in, bc(q_dec), bc(k_dec), jnp.broadcast_to(c_dec[:, None, None], (4, 8, BLK))


def _head_mask(x):
    lane = lax.broadcasted_iota(jnp.int32, (1, BLK), 1)
    return (lane < 64) if x == 0 else (lane >= 64)


def _ret_fwd(rqk, rv, proj, rtabs):
    L = rqk.shape[0]
    n = L // BLK
    d_in, q_dec, k_dec, c_dec = rtabs

    def body(q_ref, k_ref, v_ref, g_ref, din_ref, qd_ref, kd_ref, cd_ref, y_ref, o_ref, st_ref, s_scr):
        @pl.when(pl.program_id(1) == 0)
        def _():
            s_scr[...] = jnp.zeros_like(s_scr)

        q = q_ref[...]
        k = k_ref[...]
        zq = jnp.zeros_like(q)
        for x in range(2):
            hm = _head_mask(x)
            sl = slice(x * 128, (x + 1) * 128)
            qm = jnp.where(hm, q, zq)
            km = jnp.where(hm, k, zq)
            v = v_ref[:, sl]
            s_in = s_scr[x]
            st_ref[0, 0, x] = s_in
            inner = _dot(qm, km, NT) * din_ref[x]
            y = _dot(inner, v) + _dot(qm, s_in) * qd_ref[x]
            s_scr[x] = s_in * cd_ref[x, 0:1, :] + _dot(km.astype(f32) * kd_ref[x], v, TN)
            y_ref[:, sl] = y
            mu = jnp.mean(y, -1, keepdims=True)
            yc = y - mu
            yn = yc * lax.rsqrt(jnp.mean(jnp.square(yc), -1, keepdims=True) + LN_EPS)
            g = g_ref[:, sl]
            o_ref[:, sl] = g * jax.nn.sigmoid(g) * yn

    tab = pl.BlockSpec((2, BLK, BLK), lambda p, i: (p, 0, 0))
    return pl.pallas_call(
        body, name="ret_fwd", grid=(2, n),
        in_specs=[pl.BlockSpec((BLK, 128), lambda p, i: (i, p)), pl.BlockSpec((BLK, 128), lambda p, i: (i, 2 + p)),
                  pl.BlockSpec((BLK, 256), lambda p, i: (i, p)),
                  pl.BlockSpec((BLK, 256), lambda p, i: (i, C_RG // 256 + p)),
                  tab, tab, tab, pl.BlockSpec((2, 8, BLK), lambda p, i: (p, 0, 0))],
        out_specs=(pl.BlockSpec((BLK, 256), lambda p, i: (i, p)), pl.BlockSpec((BLK, 256), lambda p, i: (i, p)),
                   pl.BlockSpec((1, 1, 2, BLK, BLK), lambda p, i: (p, i, 0, 0, 0))),
        out_shape=(jax.ShapeDtypeStruct((L, 512), f32), jax.ShapeDtypeStruct((L, 512), f32),
                   jax.ShapeDtypeStruct((2, n, 2, BLK, BLK), f32)),
        scratch_shapes=[pltpu.VMEM((2, BLK, BLK), f32)],
        compiler_params=_cp(("parallel", "arbitrary")))(rqk, rqk, rv, proj, d_in, q_dec, k_dec, c_dec)


def _ret_bwd(dmixed, rqk, rv, proj, y, states, rtabs):
    L = rqk.shape[0]
    n = L // BLK
    d_in, q_dec, k_dec, c_dec = rtabs

    def body(do_ref, q_ref, k_ref, v_ref, g_ref, y_ref, st_ref, din_ref, qd_ref, kd_ref, cd_ref,
             dq_ref, dk_ref, dv_ref, dg_ref, ds_scr):
        @pl.when(pl.program_id(1) == 0)
        def _():
            ds_scr[...] = jnp.zeros_like(ds_scr)

        q = q_ref[...]
        k = k_ref[...]
        zq = jnp.zeros_like(q)
        dq_acc = jnp.zeros((BLK, BLK), f32)
        dk_acc = jnp.zeros((BLK, BLK), f32)
        for x in range(2):
            hm = _head_mask(x)
            sl = slice(x * 128, (x + 1) * 128)
            qm = jnp.where(hm, q, zq)
            km = jnp.where(hm, k, zq)
            v = v_ref[:, sl]
            yv = y_ref[:, sl]
            g = g_ref[:, sl]
            do = do_ref[:, sl]
            mu = jnp.mean(yv, -1, keepdims=True)
            yc = yv - mu
            rstd = lax.rsqrt(jnp.mean(jnp.square(yc), -1, keepdims=True) + LN_EPS)
            yn = yc * rstd
            sg = jax.nn.sigmoid(g)
            dg_ref[:, sl] = do * yn * sg * (1.0 + g * (1.0 - sg))
            dyn = do * g * sg
            dy = rstd * (dyn - jnp.mean(dyn, -1, keepdims=True) - yn * jnp.mean(dyn * yn, -1, keepdims=True))
            s_in = st_ref[0, 0, x]
            ds_out = ds_scr[x]
            kd = km.astype(f32) * kd_ref[x]
            a = _dot(qm, km, NT) * din_ref[x]
            da = _dot(dy, v, NT) * din_ref[x]
            dyq = dy * qd_ref[x]
            dq_acc += _dot(da, km) + _dot(dyq, s_in, NT)
            dk_acc += _dot(da, qm, TN) + _dot(v, ds_out, NT) * kd_ref[x]
            dv_ref[:, sl] = _dot(a, dy, TN) + _dot(kd, ds_out)
            ds_scr[x] = ds_out * cd_ref[x, 0:1, :] + _dot(qm, dyq, TN)
        dq_ref[...] = dq_acc
        dk_ref[...] = dk_acc

    rev = lambda w, cb: pl.BlockSpec((BLK, w), lambda p, i: (n - 1 - i, cb + p))
    tab = pl.BlockSpec((2, BLK, BLK), lambda p, i: (p, 0, 0))
    return pl.pallas_call(
        body, name="ret_bwd", grid=(2, n),
        in_specs=[rev(256, 4), rev(128, 0), rev(128, 2), rev(256, 0), rev(256, C_RG // 256), rev(256, 0),
                  pl.BlockSpec((1, 1, 2, BLK, BLK), lambda p, i: (p, n - 1 - i, 0, 0, 0)),
                  tab, tab, tab, pl.BlockSpec((2, 8, BLK), lambda p, i: (p, 0, 0))],
        out_specs=(rev(128, 0), rev(128, 0), rev(256, 0), rev(256, 0)),
        out_shape=(jax.ShapeDtypeStruct((L, 256), f32), jax.ShapeDtypeStruct((L, 256), f32),
                   jax.ShapeDtypeStruct((L, 512), f32), jax.ShapeDtypeStruct((L, 512), f32)),
        scratch_shapes=[pltpu.VMEM((2, BLK, BLK), f32)],
        compiler_params=_cp(("parallel", "arbitrary")))(dmixed, rqk, rqk, rv, proj, y, states, d_in, q_dec, k_dec, c_dec)


def _loss_head(h, target):
    L = h.shape[0]
    n = L // BLK

    def body(h_ref, t_ref, dy_ref, l_ref):
        i = pl.program_id(0)

        @pl.when(i == 0)
        def _():
            dy_ref[...] = jnp.zeros_like(dy_ref)
            l_ref[...] = jnp.zeros_like(l_ref)

        @pl.when(i > 0)
        def _():
            err = h_ref[...] - t_ref[...]
            dy_ref[...] = err * (1.0 / D)
            sq = jnp.sum(jnp.sum(jnp.square(err), 1, keepdims=True), 0, keepdims=True)
            l_ref[...] += (0.5 / D) * sq

    return pl.pallas_call(
        body, name="loss_head", grid=(n,),
        in_specs=[pl.BlockSpec((BLK, D), lambda i: (i, 0)),
                  pl.BlockSpec((BLK, D), lambda i: (jnp.maximum(i - 1, 0), 0))],
        out_specs=(pl.BlockSpec((BLK, D), lambda i: (i, 0)), pl.BlockSpec((8, 128), lambda i: (0, 0))),
        out_shape=(jax.ShapeDtypeStruct((L, D), f32), jax.ShapeDtypeStruct((8, 128), f32)),
        compiler_params=_cp(("arbitrary",)))(h, target)


def _adam_math(w, g, m, v):
    m = ADAM_B1 * m + (1.0 - ADAM_B1) * g
    v = ADAM_B2 * v + (1.0 - ADAM_B2) * jnp.square(g)
    m_hat = m / (1.0 - ADAM_B1 ** ADAM_STEP)
    v_hat = v / (1.0 - ADAM_B2 ** ADAM_STEP)
    delta = -ADAM_LR * (m_hat / (jnp.sqrt(v_hat) + ADAM_EPS) + ADAM_WD * w)
    return delta, m, v


def _adamw(parts, w, m, v, name):
    R, C = w.shape
    tr = _pick(R, (240, 192, 144, 96, 64, 48, 32, 16, 8))
    row = pl.BlockSpec((tr, C), lambda i: (i, 0))

    def body(p_ref, w_ref, m_ref, v_ref, g_ref, d_ref, nm_ref, nv_ref):
        g = p_ref[0].astype(f32)
        for k in range(1, N_DEV):
            g = g + p_ref[k].astype(f32)
        d, nm, nv = _adam_math(w_ref[...], g, m_ref[...], v_ref[...])
        g_ref[...] = g
        d_ref[...] = d
        nm_ref[...] = nm
        nv_ref[...] = nv

    o = jax.ShapeDtypeStruct((R, C), f32)
    return pl.pallas_call(
        body, name=name, grid=(R // tr,),
        in_specs=[pl.BlockSpec((N_DEV, tr, C), lambda i: (0, i, 0)), row, row, row],
        out_specs=(row, row, row, row), out_shape=(o, o, o, o),
        compiler_params=_cp(("parallel",)))(parts, w, m, v)


_XCHG_SEMS = [pltpu.SemaphoreType.DMA((N_DEV - 1,)), pltpu.SemaphoreType.DMA((N_DEV - 1,)), pltpu.SemaphoreType.DMA]
_HBM = pl.BlockSpec(memory_space=pltpu.HBM)


def _xchg_shape(x, all_to_all):
    return jax.ShapeDtypeStruct((N_DEV,) + tuple(x.shape[1:] if all_to_all else x.shape), x.dtype)


def _xchg_ops(x_ref, o_ref, send_sems, recv_sems, local_sem, all_to_all):
    mx, my, mc = lax.axis_index("x"), lax.axis_index("y"), lax.axis_index("c")
    me = 4 * mx + 2 * my + mc

    def peer(k):
        px = (1 - mx) if k & 4 else mx
        py = (1 - my) if k & 2 else my
        pc = (1 - mc) if k & 1 else mc
        return (px, py, pc), 4 * px + 2 * py + pc

    def copy(k):
        dev, idx = peer(k)
        src = x_ref.at[idx] if all_to_all else x_ref
        return pltpu.make_async_remote_copy(
            src_ref=src, dst_ref=o_ref.at[me], send_sem=send_sems.at[k - 1], recv_sem=recv_sems.at[k - 1],
            device_id=dev, device_id_type=pl.DeviceIdType.MESH)

    def start():
        pltpu.make_async_copy(x_ref.at[me] if all_to_all else x_ref, o_ref.at[me], local_sem).start()
        for k in range(1, N_DEV):
            copy(k).start()

    def wait():
        for k in range(1, N_DEV):
            dev, idx = peer(k)
            pltpu.make_async_remote_copy(
                src_ref=o_ref.at[idx], dst_ref=o_ref.at[idx], send_sem=send_sems.at[k - 1],
                recv_sem=recv_sems.at[k - 1], device_id=dev, device_id_type=pl.DeviceIdType.MESH).wait_recv()
        for k in range(1, N_DEV):
            copy(k).wait_send()
        pltpu.make_async_copy(x_ref.at[me] if all_to_all else x_ref, o_ref.at[me], local_sem).wait()

    return start, wait


def _exchange(x, all_to_all, name):
    def body(x_ref, o_ref, send_sems, recv_sems, local_sem):
        start, wait = _xchg_ops(x_ref, o_ref, send_sems, recv_sems, local_sem, all_to_all)
        start()
        wait()

    return pl.pallas_call(body, name=name, in_specs=[_HBM], out_specs=_HBM, out_shape=_xchg_shape(x, all_to_all),
                          scratch_shapes=list(_XCHG_SEMS))(x)


WC = 512
LAYER_ROWS = tuple(n // WC for n in (1024 * 468, 384 * 96, 256 * 128, 192 * 1024, 1024 * 512, 512 * 1024))


def _pack_layer(ws, l):
    return jnp.concatenate([w[l].reshape(-1, WC) for w in ws], axis=0)


def _unpack_layer(gathered):
    offs = np.cumsum((0,) + LAYER_ROWS)
    part = lambda k: gathered[:, offs[k]:offs[k + 1]]
    w_in = part(0).reshape(8, 1024, 468).transpose(1, 0, 2).reshape(1024, 3744)
    z32 = jnp.zeros((1024, 32), w_in.dtype)
    kr = w_in[:, 2176:2208]
    w_in = jnp.concatenate([w_in[:, 0:1536], w_in[:, 1920:2176], w_in[:, 2208:3744], w_in[:, 1536:1920],
                            kr, z32, kr, z32], axis=1)
    w_uq = part(1).reshape(8, 384, 96).transpose(1, 0, 2)
    rope = jnp.concatenate([w_uq[..., 64:96], jnp.zeros((384, 8, 32), w_uq.dtype)], axis=-1)
    w_uq = jnp.concatenate([w_uq[..., 0:64].reshape(384, 512), rope.reshape(384, 512)], axis=1)
    w_ukv = part(2).reshape(8, 256, 128).transpose(1, 0, 2)
    w_ukv = jnp.concatenate([w_ukv[..., 0:64].reshape(256, 512), w_ukv[..., 64:128].reshape(256, 512)], axis=1)
    w_out = part(3).reshape(1536, 1024)
    w_ff1 = part(4).reshape(8, 1024, 512).transpose(1, 0, 2).reshape(1024, 4096)
    w_ff2 = part(5).reshape(4096, 1024)
    return w_in, w_uq, w_ukv, w_out, w_ff1, w_ff2


def _pack_layer_grads(g_in, g_uq, g_ukv, g_out, g_ff1, g_ff2):
    kr = g_in[:, C_KR:C_KR + 32] + g_in[:, C_KR + 64:C_KR + 96]
    g_in = jnp.concatenate([g_in[:, 0:1536], g_in[:, C_CQ:C_CQ + 384], g_in[:, C_CKV:C_CKV + 256], kr,
                            g_in[:, C_RQ:C_CQ]], axis=1)
    g_in = g_in.reshape(1024, 8, 468).transpose(1, 0, 2)
    g_uq = jnp.concatenate([g_uq[:, 0:512].reshape(384, 8, 64), g_uq[:, 512:1024].reshape(384, 8, 64)[..., 0:32]],
                           axis=-1).transpose(1, 0, 2)
    g_ukv = jnp.concatenate([g_ukv[:, 0:512].reshape(256, 8, 64), g_ukv[:, 512:1024].reshape(256, 8, 64)],
                            axis=-1).transpose(1, 0, 2)
    g_ff1 = g_ff1.reshape(1024, 8, 512).transpose(1, 0, 2)
    return jnp.concatenate([g.reshape(8, -1, WC) for g in (g_in, g_uq, g_ukv, g_out, g_ff1, g_ff2)], axis=1)


def _rope_tables(L):
    pos = (jnp.arange(L) - N_PAD).astype(f32)

    def cs(half):
        inv = ROPE_THETA ** (-jnp.arange(half, dtype=f32) / half)
        ang = pos[:, None] * inv[None, :]
        return jnp.cos(ang), jnp.sin(ang)

    c, s = cs(32)
    rc, rs = jnp.tile(c, (1, 8)), jnp.tile(s, (1, 8))
    c, s = cs(16)
    z = jnp.zeros((L, 32), f32)
    mc, ms = jnp.concatenate([c, c, z, c, c, z], 1), jnp.concatenate([s, s, z, s, s, z], 1)
    return rc, rs, mc, ms


def _layer_fwd(h, wl, gq, gkv, g1, b1, g2, b2, tabs, rtabs, next_shard):
    w_in, w_uq, w_ukv, w_out, w_ff1, w_ff2 = wl
    proj = _mm(h, w_in, "nn", "mm_in")
    sb, cqn, ckvn, rqk, rv, kr2 = _prep_fwd(proj, gq, gkv, tabs)
    q = _mm(cqn, w_uq, "nn", "mm_uq")
    kv = _mm(ckvn, w_ukv, "nn", "mm_ukv", out_dtype=bf16)
    if next_shard is None:
        out_a, gathered = _sb_fwd(sb), None
    else:
        out_a, gathered = _sb_fwd(sb, next_shard)
    out_b, lse = _mla_fwd(q, kv, kr2, tabs[2], tabs[3])
    y, out_c, states = _ret_fwd(rqk, rv, proj, rtabs)
    mixed = jnp.concatenate([out_a, out_b, out_c], axis=1)
    mix = _mm(mixed, w_out, "nn", "mm_out")
    h1 = _ln_fwd(h, mix, g1, b1, DN_ALPHA, "ln_fwd")
    u, a = _mm(h1, w_ff1, "nn", "mm_ff1", epi="relu2")
    ff = _mm(a, w_ff2, "nn", "mm_ff2")
    h2 = _ln_fwd(h1, ff, g2, b2, DN_ALPHA, "ln_fwd")
    saved = (h, proj, sb, cqn, ckvn, rqk, rv, kr2, q, kv, out_a, out_b, lse, y, states, mixed, mix, h1, u, a, ff)
    return h2, saved, gathered


def _layer_bwd(dh2, saved, wl, gq, gkv, g1, g2, tabs, rtabs, grads_above):
    w_in, w_uq, w_ukv, w_out, w_ff1, w_ff2 = wl
    h, proj, sb, cqn, ckvn, rqk, rv, kr2, q, kv, out_a, out_b, lse, y, states, mixed, mix, h1, u, a, ff = saved
    dz2, dg2, db2 = _ln_bwd(dh2, h1, ff, g2, DN_ALPHA, "ln_bwd")
    du = _mm(dz2, w_ff2, "nt", "mm_dff2", epi="mul_relu", extra=u)
    gw_ff2 = _mm(a, dz2, "tn", "mm_gff2")
    dh1 = _mm(du, w_ff1, "nt", "mm_dff1", epi="add", extra=dz2, alpha=DN_ALPHA)
    gw_ff1 = _mm(h1, du, "tn", "mm_gff1")
    dz1, dg1, db1 = _ln_bwd(dh1, h, mix, g1, DN_ALPHA, "ln_bwd")
    dmixed = _mm(dz1, w_out, "nt", "mm_dout")
    gw_out = _mm(mixed, dz1, "tn", "mm_gout")
    if grads_above is None:
        (dsq, dsk, dsv), parts_above = _sb_bwd(dmixed, sb, out_a), None
    else:
        dsq, dsk, dsv, parts_above = _sb_bwd(dmixed, sb, out_a, grads_above)
    dqn, dqr, dkn, dv, dkr_p = _mla_bwd(dmixed, q, kv, kr2, out_b, lse, tabs[2], tabs[3])
    dq = jnp.concatenate([dqn, dqr], axis=1)
    dkv = jnp.concatenate([dkn, dv], axis=1)
    dcqn = _mm(dq, w_uq, "nt", "mm_duq")
    gw_uq = _mm(cqn, dq, "tn", "mm_guq")
    dckvn = _mm(dkv, w_ukv, "nt", "mm_dukv")
    gw_ukv = _mm(ckvn, dkv, "tn", "mm_gukv")
    drq_r, drk_r, drv, drg = _ret_bwd(dmixed, rqk, rv, proj, y, states, rtabs)
    dkr_r = dkr_p[0] + dkr_p[1] + dkr_p[2] + dkr_p[3]
    dcq, dckv, drq, drk, dkr2, dgq, dgkv = _prep_bwd(proj, gq, gkv, tabs, dcqn, dckvn, drq_r, drk_r, dkr_r)
    dproj = jnp.concatenate([dsq, dsk, dsv, dckv, drq, drk, drv, drg, dcq, dkr2], axis=1)
    dh = _mm(dproj, w_in, "nt", "mm_din", epi="add", extra=dz1, alpha=DN_ALPHA)
    gw_in = _mm(h, dproj, "tn", "mm_gin")
    return dh, (gw_in, gw_uq, gw_ukv, gw_out, gw_ff1, gw_ff2), (dgq, dgkv, dg1, db1, dg2, db2), parts_above


def kernel(x, meta_tokens, ln_emb_g, ln_emb_b, w_in, mla_q_norm, mla_kv_norm, w_uq, w_ukv, w_out, ln1_g, ln1_b, w_ff1, w_ff2, ln2_g, ln2_b, loss_target, m_meta_tokens, m_ln_emb_g, m_ln_emb_b, m_w_in, m_mla_q_norm, m_mla_kv_norm, m_w_uq, m_w_ukv, m_w_out, m_ln1_g, m_ln1_b, m_w_ff1, m_w_ff2, m_ln2_g, m_ln2_b, v_meta_tokens, v_ln_emb_g, v_ln_emb_b, v_w_in, v_mla_q_norm, v_mla_kv_norm, v_w_uq, v_w_ukv, v_w_out, v_ln1_g, v_ln1_b, v_w_ff1, v_w_ff2, v_ln2_g, v_ln2_b):
    depth = w_in.shape[0]
    S = x.shape[1]
    L = S + BLK
    me = 4 * lax.axis_index("x") + 2 * lax.axis_index("y") + lax.axis_index("c")
    big = (w_in, w_uq, w_ukv, w_out, w_ff1, w_ff2)
    big_m = (m_w_in, m_w_uq, m_w_ukv, m_w_out, m_w_ff1, m_w_ff2)
    big_v = (v_w_in, v_w_uq, v_w_ukv, v_w_out, v_w_ff1, v_w_ff2)
    small = (ln_emb_g, ln_emb_b, mla_q_norm, mla_kv_norm, ln1_g, ln1_b, ln2_g, ln2_b)
    small_m = (m_ln_emb_g, m_ln_emb_b, m_mla_q_norm, m_mla_kv_norm, m_ln1_g, m_ln1_b, m_ln2_g, m_ln2_b)
    small_v = (v_ln_emb_g, v_ln_emb_b, v_mla_q_norm, v_mla_kv_norm, v_ln1_g, v_ln1_b, v_ln2_g, v_ln2_b)

    shards = [_pack_layer(big, l) for l in range(depth)]
    gathered = _exchange(shards[0].astype(bf16), False, "gather_w0")
    meta_all = _exchange(meta_tokens, False, "gather_meta")
    meta_full = meta_all.transpose(1, 0, 2).reshape(N_META, D)

    tabs = _rope_tables(L)
    rtabs = _ret_tables()

    hcat = jnp.concatenate([jnp.zeros((N_PAD, D), f32), meta_full, x[0]], axis=0)
    h = _ln_fwd(hcat, None, ln_emb_g, ln_emb_b, 1.0, "ln_emb_fwd")
    saved, full = [], []
    for l in range(depth):
        full.append(_unpack_layer(gathered))
        nxt = shards[l + 1].astype(bf16) if l + 1 < depth else None
        h, sv, gathered = _layer_fwd(h, full[l], mla_q_norm[l], mla_kv_norm[l], ln1_g[l], ln1_b[l], ln2_g[l],
                                     ln2_b[l], tabs, rtabs, nxt)
        saved.append(sv)

    dh, loss_part = _loss_head(h, loss_target[0])
    gsmall, parts, pending = [None] * depth, [None] * depth, None
    for l in reversed(range(depth)):
        dh, gbig, gsmall[l], got = _layer_bwd(dh, saved[l], full[l], mla_q_norm[l], mla_kv_norm[l], ln1_g[l],
                                              ln2_g[l], tabs, rtabs, pending)
        if pending is not None:
            parts[l + 1] = got
        pending = _pack_layer_grads(*gbig).astype(bf16)
    parts[0] = _exchange(pending, True, "scatter_g0")
    dz0, dg_emb, db_emb = _ln_bwd(dh, hcat, None, ln_emb_g, 1.0, "ln_emb_bwd")
    grad_x = dz0[BLK:][None]
    dmeta = dz0[N_PAD:BLK]

    adam = [_adamw(parts[l], shards[l], _pack_layer(big_m, l), _pack_layer(big_v, l), "adamw_big")
            for l in range(depth)]

    st = lambda k: jnp.stack([gsmall[l][k] for l in range(depth)])
    g_small = (dg_emb, db_emb, st(0), st(1), st(2), st(3), st(4), st(5))
    n_small = sum(int(np.prod(a.shape)) for a in small)
    flat = jnp.concatenate([a.reshape(-1) for a in g_small] + [dmeta.reshape(-1), loss_part[0, 0:1]])
    rows = -(-(flat.shape[0]) // (8 * D)) * 8
    pad = rows * D - flat.shape[0]
    flat = jnp.concatenate([flat, jnp.zeros((pad,), f32)]).reshape(rows, D)
    parts_s = _exchange(flat, False, "gather_small")

    def pack_small(arrs, meta_shard):
        col = jnp.zeros((N_META, D), f32)
        col = lax.dynamic_update_slice(col, meta_shard, (0, me * 128))
        fl = jnp.concatenate([a.reshape(-1) for a in arrs] + [col.reshape(-1), jnp.zeros((1 + pad,), f32)])
        return fl.reshape(rows, D)

    g_s, d_s, m_s, v_s = _adamw(parts_s, pack_small(small, meta_tokens), pack_small(small_m, m_meta_tokens),
                                pack_small(small_v, v_meta_tokens), "adamw_small")
    loss = g_s.reshape(-1)[n_small + N_META * D]

    def unpack_big(which):
        outs, off = [], 0
        for w, r in zip(big, LAYER_ROWS):
            outs.append(jnp.stack([adam[l][which][off:off + r].reshape(w.shape[1:]) for l in range(depth)]))
            off += r
        return outs

    def unpack_small(flat_rows):
        fl = flat_rows.reshape(-1)
        outs, off = [], 0
        for a in small:
            n = int(np.prod(a.shape))
            outs.append(fl[off:off + n].reshape(a.shape))
            off += n
        meta = lax.dynamic_slice(fl[off:off + N_META * D].reshape(N_META, D), (0, me * 128), (N_META, 128))
        return meta, outs

    def assemble(which, small_rows_arr):
        b = unpack_big(which)
        meta, s = unpack_small(small_rows_arr)
        return [meta, s[0], s[1], b[0], s[2], s[3], b[1], b[2], b[3], s[4], s[5], b[4], b[5], s[6], s[7]]

    return (loss, grad_x, *assemble(0, g_s), *assemble(1, d_s), *assemble(2, m_s), *assemble(3, v_s))
```

```python
import functools
import math

import numpy as np
import jax
import jax.numpy as jnp
from jax import lax
from jax.experimental import pallas as pl
from jax.experimental.pallas import tpu as pltpu

f32 = jnp.float32
bf16 = jnp.bfloat16
_MXU = jnp.bfloat16

BLK = 128
N_META = 16
N_PAD = 112
D = 1024
N_DEV = 8
LN_EPS = 1e-5
DEPTH = 4
DN_ALPHA = (2 * DEPTH) ** 0.25
ROPE_THETA = 10000.0
MLA_SCALE = (64 + 32) ** -0.5
SB_SCALE = 0.125
RET_SCALE = 0.125
RET_GAMMA = tuple(1.0 - 2.0 ** (-5 - h) for h in range(4))

ADAM_LR, ADAM_B1, ADAM_B2, ADAM_EPS, ADAM_WD, ADAM_STEP = 0.001, 0.9, 0.999, 1e-08, 0.01, 10

C_SBQ, C_SBK, C_SBV, C_CKV, C_RQ, C_RK, C_RV, C_RG, C_CQ, C_KR, N_INP = (
    0, 512, 1024, 1536, 1792, 2048, 2304, 2816, 3328, 3712, 3840)

VMEM_LIMIT = 56 * 1024 * 1024


def _cp(sem):
    return pltpu.CompilerParams(dimension_semantics=sem, vmem_limit_bytes=VMEM_LIMIT)


def _pick(n, cands):
    for c in cands:
        if n % c == 0:
            return c
    return n


def _dot(a, b, dims=(((1,), (0,)), ((), ()))):
    return lax.dot_general(a.astype(_MXU), b.astype(_MXU), dims, preferred_element_type=f32)


NT = (((1,), (1,)), ((), ()))
TN = (((0,), (0,)), ((), ()))


def _dot3(x, u):
    hi = x.astype(_MXU)
    r1 = x - hi.astype(f32)
    mid = r1.astype(_MXU)
    lo = (r1 - mid.astype(f32)).astype(_MXU)
    return (jnp.dot(hi, u, preferred_element_type=f32) + jnp.dot(mid, u, preferred_element_type=f32)
            + jnp.dot(lo, u, preferred_element_type=f32))


def _rot(x, half):
    lane = lax.broadcasted_iota(jnp.int32, x.shape, 1)
    first = (lane % 64) < half
    return jnp.where(first, -pltpu.roll(x, 128 - half, 1), pltpu.roll(x, half, 1))


def _mm(a, b, mode, name, epi=None, extra=None, alpha=1.0, out_dtype=f32):
    if mode == "nn":
        (M, K), N = a.shape, b.shape[1]
    elif mode == "nt":
        (M, K), N = a.shape, b.shape[0]
    else:
        (K, M), N = a.shape, b.shape[1]
    tm = _pick(M, (1408, 1024, 768, 512, 384, 256, 128))
    tn = _pick(N, ((1920,) if mode == "tn" else ()) + (1024, 768, 512, 384, 256, 128))
    tk = _pick(K, (1024, 768, 512, 384, 256, 128))
    nk = K // tk
    if mode == "nn":
        a_spec = pl.BlockSpec((tm, tk), lambda i, j, k: (i, k))
        b_spec = pl.BlockSpec((tk, tn), lambda i, j, k: (k, j))
        dims = (((1,), (0,)), ((), ()))
    elif mode == "nt":
        a_spec = pl.BlockSpec((tm, tk), lambda i, j, k: (i, k))
        b_spec = pl.BlockSpec((tn, tk), lambda i, j, k: (j, k))
        dims = NT
    else:
        a_spec = pl.BlockSpec((tk, tm), lambda i, j, k: (k, i))
        b_spec = pl.BlockSpec((tk, tn), lambda i, j, k: (k, j))
        dims = TN
    o_spec = pl.BlockSpec((tm, tn), lambda i, j, k: (i, j))
    in_specs, args = [a_spec, b_spec], [a, b]
    if extra is not None:
        in_specs.append(o_spec)
        args.append(extra)
    if epi == "relu2":
        out_shape = (jax.ShapeDtypeStruct((M, N), f32), jax.ShapeDtypeStruct((M, N), bf16))
        out_specs = (o_spec, o_spec)
    else:
        out_shape = jax.ShapeDtypeStruct((M, N), out_dtype)
        out_specs = o_spec

    def body(*refs):
        a_ref, b_ref = refs[0], refs[1]
        acc = refs[-1]
        k = pl.program_id(2)

        @pl.when(k == 0)
        def _():
            acc[...] = jnp.zeros_like(acc)

        acc[...] += _dot(a_ref[...], b_ref[...], dims)

        @pl.when(k == nk - 1)
        def _():
            r = acc[...]
            if epi == "relu2":
                refs[2][...] = r
                refs[3][...] = jnp.square(jnp.maximum(r, 0.0)).astype(bf16)
            elif epi == "mul_relu":
                refs[3][...] = (r * (2.0 * jnp.maximum(refs[2][...], 0.0))).astype(out_dtype)
            elif epi == "add":
                refs[3][...] = r + alpha * refs[2][...]
            else:
                refs[2][...] = r.astype(out_dtype)

    return pl.pallas_call(
        body, name=name, grid=(M // tm, N // tn, nk), in_specs=in_specs, out_specs=out_specs,
        out_shape=out_shape, scratch_shapes=[pltpu.VMEM((tm, tn), f32)],
        compiler_params=_cp(("parallel", "parallel", "arbitrary")))(*args)


def _ln_fwd(h, m, g, b, alpha, name):
    L = h.shape[0]
    tm = _pick(L, (384, 256, 128))
    row = pl.BlockSpec((tm, D), lambda i: (i, 0))
    vec = pl.BlockSpec((1, D), lambda i: (0, 0))

    def body(*refs):
        if m is None:
            h_ref, g_ref, b_ref, o_ref = refs
            z = h_ref[...]
        else:
            h_ref, m_ref, g_ref, b_ref, o_ref = refs
            z = alpha * h_ref[...] + m_ref[...]
        mu = jnp.mean(z, -1, keepdims=True)
        var = jnp.mean(jnp.square(z - mu), -1, keepdims=True)
        o_ref[...] = (z - mu) * lax.rsqrt(var + LN_EPS) * g_ref[...] + b_ref[...]

    args = [h] + ([] if m is None else [m]) + [g.reshape(1, D), b.reshape(1, D)]
    specs = [row] + ([] if m is None else [row]) + [vec, vec]
    return pl.pallas_call(body, name=name, grid=(L // tm,), in_specs=specs, out_specs=row,
                          out_shape=jax.ShapeDtypeStruct((L, D), f32), compiler_params=_cp(("parallel",)))(*args)


def _ln_bwd(dy, h, m, g, alpha, name):
    L = h.shape[0]
    tm = _pick(L, (384, 256, 128))
    row = pl.BlockSpec((tm, D), lambda i: (i, 0))
    vec = pl.BlockSpec((1, D), lambda i: (0, 0))
    acc = pl.BlockSpec((8, D), lambda i: (0, 0))

    def body(*refs):
        if m is None:
            dy_ref, h_ref, g_ref, dz_ref, dg_ref, db_ref = refs
            z = h_ref[...]
        else:
            dy_ref, h_ref, m_ref, g_ref, dz_ref, dg_ref, db_ref = refs
            z = alpha * h_ref[...] + m_ref[...]

        @pl.when(pl.program_id(0) == 0)
        def _():
            dg_ref[...] = jnp.zeros_like(dg_ref)
            db_ref[...] = jnp.zeros_like(db_ref)

        dyv = dy_ref[...]
        mu = jnp.mean(z, -1, keepdims=True)
        zc = z - mu
        rstd = lax.rsqrt(jnp.mean(jnp.square(zc), -1, keepdims=True) + LN_EPS)
        xh = zc * rstd
        dxh = dyv * g_ref[...]
        dz_ref[...] = rstd * (dxh - jnp.mean(dxh, -1, keepdims=True) - xh * jnp.mean(dxh * xh, -1, keepdims=True))
        dg_ref[0:1, :] += jnp.sum(dyv * xh, 0, keepdims=True)
        db_ref[0:1, :] += jnp.sum(dyv, 0, keepdims=True)

    args = [dy, h] + ([] if m is None else [m]) + [g.reshape(1, D)]
    specs = [row, row] + ([] if m is None else [row]) + [vec]
    dz, dg, db = pl.pallas_call(
        body, name=name, grid=(L // tm,), in_specs=specs, out_specs=(row, acc, acc),
        out_shape=(jax.ShapeDtypeStruct((L, D), f32), jax.ShapeDtypeStruct((8, D), f32),
                   jax.ShapeDtypeStruct((8, D), f32)),
        compiler_params=_cp(("arbitrary",)))(*args)
    return dz, dg[0], db[0]


def _rms(x, g):
    r = lax.rsqrt(jnp.mean(jnp.square(x), -1, keepdims=True) + LN_EPS)
    return x * r * g


def _prep_fwd(proj, gq, gkv, tabs):
    L = proj.shape[0]
    tm = BLK
    rc, rs, mc, ms = tabs

    def body(p_ref, gq_ref, gkv_ref, rc_ref, rs_ref, mc_ref, ms_ref, sb_ref, cq_ref, ckv_ref, rqk_ref, rv_ref, kr_ref):
        i = pl.program_id(0)
        sb_ref[:, 0:512] = (p_ref[:, C_SBQ:C_SBQ + 512] * SB_SCALE).astype(bf16)
        sb_ref[:, 512:1536] = p_ref[:, C_SBK:C_SBK + 1024].astype(bf16)
        cq_ref[...] = _rms(p_ref[:, C_CQ:C_CQ + 384], gq_ref[...]).astype(bf16)
        ckv_ref[...] = _rms(p_ref[:, C_CKV:C_CKV + 256], gkv_ref[...]).astype(bf16)
        valid = (i * tm + lax.broadcasted_iota(jnp.int32, (tm, 128), 0)) >= N_PAD
        for c in range(2):
            sl = slice(c * 128, (c + 1) * 128)
            x = p_ref[:, C_RQ + c * 128:C_RQ + (c + 1) * 128]
            rqk_ref[:, sl] = (x * rc_ref[:, sl] + _rot(x, 32) * rs_ref[:, sl]).astype(bf16)
            x = p_ref[:, C_RK + c * 128:C_RK + (c + 1) * 128]
            kk = (x * rc_ref[:, sl] + _rot(x, 32) * rs_ref[:, sl]) * RET_SCALE
            rqk_ref[:, 256 + c * 128:256 + (c + 1) * 128] = jnp.where(valid, kk, 0.0).astype(bf16)
        rv_ref[...] = p_ref[:, C_RV:C_RV + 512].astype(bf16)
        x = p_ref[:, C_KR:C_KR + 128]
        kr_ref[...] = (x * mc_ref[...] + _rot(x, 16) * ms_ref[...]).astype(bf16)

    def row(w):
        return pl.BlockSpec((tm, w), lambda i: (i, 0))

    def vec(w):
        return pl.BlockSpec((1, w), lambda i: (0, 0))

    widths = (1536, 384, 256, 512, 512, 128)
    return pl.pallas_call(
        body, name="prep_fwd", grid=(L // tm,),
        in_specs=[row(N_INP), vec(384), vec(256), row(256), row(256), row(128), row(128)],
        out_specs=tuple(row(w) for w in widths),
        out_shape=tuple(jax.ShapeDtypeStruct((L, w), bf16) for w in widths),
        compiler_params=_cp(("parallel",)))(proj, gq.reshape(1, 384), gkv.reshape(1, 256), rc, rs, mc, ms)


def _rms_bwd(x, g, dy):
    r = lax.rsqrt(jnp.mean(jnp.square(x), -1, keepdims=True) + LN_EPS)
    u = dy * g
    dx = r * u - x * (r * r * r) * jnp.mean(x * u, -1, keepdims=True)
    return dx, jnp.sum(dy * x * r, 0, keepdims=True)


def _prep_bwd(proj, gq, gkv, tabs, dcqn, dckvn, drq_r, drk_r, dkr_r):
    L = proj.shape[0]
    tm = BLK
    rc, rs, mc, ms = tabs

    def body(p_ref, gq_ref, gkv_ref, rc_ref, rs_ref, mc_ref, ms_ref, dcqn_ref, dckvn_ref, drq_ref, drk_ref,
             dkr_ref, ocq_ref, ockv_ref, orq_ref, ork_ref, okr_ref, dgq_ref, dgkv_ref):
        i = pl.program_id(0)

        @pl.when(i == 0)
        def _():
            dgq_ref[...] = jnp.zeros_like(dgq_ref)
            dgkv_ref[...] = jnp.zeros_like(dgkv_ref)

        dx, dg = _rms_bwd(p_ref[:, C_CQ:C_CQ + 384], gq_ref[...], dcqn_ref[...])
        ocq_ref[...] = dx
        dgq_ref[0:1, :] += dg
        dx, dg = _rms_bwd(p_ref[:, C_CKV:C_CKV + 256], gkv_ref[...], dckvn_ref[...])
        ockv_ref[...] = dx
        dgkv_ref[0:1, :] += dg
        valid = (i * tm + lax.broadcasted_iota(jnp.int32, (tm, 128), 0)) >= N_PAD
        for c in range(2):
            sl = slice(c * 128, (c + 1) * 128)
            dy = drq_ref[:, sl]
            orq_ref[:, sl] = dy * rc_ref[:, sl] - _rot(dy * rs_ref[:, sl], 32)
            dy = jnp.where(valid, drk_ref[:, sl], 0.0) * RET_SCALE
            ork_ref[:, sl] = dy * rc_ref[:, sl] - _rot(dy * rs_ref[:, sl], 32)
        dy = dkr_ref[...]
        okr_ref[...] = dy * mc_ref[...] - _rot(dy * ms_ref[...], 16)

    def row(w):
        return pl.BlockSpec((tm, w), lambda i: (i, 0))

    def vec(w):
        return pl.BlockSpec((1, w), lambda i: (0, 0))

    def acc(w):
        return pl.BlockSpec((8, w), lambda i: (0, 0))

    widths = (384, 256, 256, 256, 128)
    outs = pl.pallas_call(
        body, name="prep_bwd", grid=(L // tm,),
        in_specs=[row(N_INP), vec(384), vec(256), row(256), row(256), row(128), row(128),
                  row(384), row(256), row(256), row(256), row(128)],
        out_specs=tuple(row(w) for w in widths) + (acc(384), acc(256)),
        out_shape=tuple(jax.ShapeDtypeStruct((L, w), f32) for w in widths)
        + (jax.ShapeDtypeStruct((8, 384), f32), jax.ShapeDtypeStruct((8, 256), f32)),
        compiler_params=_cp(("arbitrary",)))(
            proj, gq.reshape(1, 384), gkv.reshape(1, 256), rc, rs, mc, ms, dcqn, dckvn, drq_r, drk_r, dkr_r)
    return outs[:5] + (outs[5][0], outs[6][0])


def _qrows(L):
    return _pick(L, (384, 256, 128))


def _tri_ones(strict):
    r = lax.broadcasted_iota(jnp.int32, (BLK, 2 * BLK), 0)
    c = lax.broadcasted_iota(jnp.int32, (BLK, 2 * BLK), 1)
    tri = (r > c) if strict else (r >= c)
    return jnp.where(tri | (c >= BLK), 1.0, 0.0).astype(_MXU)


def _dot2(x, u):
    hi = x.astype(_MXU)
    lo = (x - hi.astype(f32)).astype(_MXU)
    return jnp.dot(hi, u, preferred_element_type=f32) + jnp.dot(lo, u, preferred_element_type=f32)


def _staggered(chains):
    live = list(chains)
    step = 0
    while live:
        for ci, g in enumerate(chains):
            if g in live and step >= ci and next(g, True):
                live.remove(g)
        step += 1


def _pair_rhs(t, m):
    zt = jnp.zeros_like(t)
    return jnp.concatenate([jnp.where(m, t, zt), jnp.where(m, zt, t)], axis=0)


def _sb_stages(z, mask, c_ref, x, u_gt, out):
    yield
    lb = jnp.minimum(z, 0.0) - jnp.log1p(jnp.exp(-jnp.abs(z)))
    lk = lb - z
    if mask is not None:
        lk = jnp.where(mask, lk, 0.0)
    hi = lk.astype(_MXU)
    lo = (lk - hi.astype(f32)).astype(_MXU)
    yield
    el = (jnp.dot(hi, u_gt, preferred_element_type=f32)
          + jnp.dot(lo, u_gt, preferred_element_type=f32))
    yield
    c = c_ref[x]
    w = jnp.exp(lb + el[:, 0:BLK] + c)
    if mask is not None:
        w = jnp.where(mask, w, 0.0)
    c_ref[x] = c + el[:, BLK:2 * BLK]
    out["w"], out["lb"] = w.astype(_MXU), lb


def _sb_sweep(i, r, tiles, per_pass):
    n_t = r * (i + 1)
    lax.fori_loop(0, r, lambda jj, c: tiles([n_t - 1 - jj], True) or c, 0)
    n_bulk = jnp.maximum(r * i - 1, 0)
    n_full = n_bulk // per_pass
    lax.fori_loop(0, n_full,
                  lambda jj, c: tiles([r * i - 1 - per_pass * jj - t for t in range(per_pass)], False) or c, 0)
    rem = n_bulk - n_full * per_pass

    if per_pass == 4:
        @pl.when(rem >= 2)
        def _():
            tiles([rem, rem - 1], False)

    @pl.when(rem % 2 == 1)
    def _():
        tiles([1], False)

    @pl.when(i > 0)
    def _():
        tiles([0], True)


def _tile_off(j):
    return j * BLK if isinstance(j, int) else pl.multiple_of(j * BLK, BLK)


def _sb_mask(i, j, qb):
    row = i * qb + lax.broadcasted_iota(jnp.int32, (qb, BLK), 0)
    col = j * BLK + lax.broadcasted_iota(jnp.int32, (qb, BLK), 1)
    return (col < row) & (col >= N_PAD)


def _first_last(n0, n1):
    p, i = pl.program_id(0), pl.program_id(1)
    return (p == 0) & (i == 0), (p == n0 - 1) & (i == n1 - 1)


def _sb_fwd(sb, bg=None):
    L = sb.shape[0]
    qb = _qrows(L)
    nq, r = L // qb, qb // BLK

    def body(*refs):
        if bg is None:
            q_ref, k_ref, v_ref, o_ref, acc_ref, c_ref = refs
        else:
            q_ref, k_ref, v_ref, x_ref, o_ref, g_ref, acc_ref, c_ref = refs[:8]
            start, wait = _xchg_ops(x_ref, g_ref, *refs[8:], False)
            first, last = _first_last(4, nq)
            pl.when(first)(start)
        i = pl.program_id(1)
        m_a = lax.broadcasted_iota(jnp.int32, (1, BLK), 1) < 64
        q = q_ref[...]
        u_gt = _tri_ones(True)
        acc_ref[...] = jnp.zeros_like(acc_ref)
        c_ref[...] = jnp.zeros_like(c_ref)

        def chain(x, j, masked, both):
            off = _tile_off(j)
            if x == 0:
                both["z"] = _dot(q, _pair_rhs(k_ref[pl.ds(off, BLK), :], m_a), NT)
            mask = _sb_mask(i, j, qb) if masked else None
            o = {}
            yield from _sb_stages(both["z"][:, x * BLK:(x + 1) * BLK], mask, c_ref, x, u_gt, o)
            yield
            acc_ref[x] += _dot(o["w"], v_ref[pl.ds(off, BLK), :])

        def tiles(js, masked):
            shared = [{} for _ in js]
            _staggered([chain(x, j, masked, shared[t]) for t, j in enumerate(js) for x in range(2)])

        _sb_sweep(i, r, tiles, 4)
        o_ref[...] = jnp.where(m_a, acc_ref[0], acc_ref[1])
        if bg is not None:
            pl.when(last)(wait)

    in_specs = [pl.BlockSpec((qb, 128), lambda p, i: (i, p)),
                pl.BlockSpec((L, 128), lambda p, i: (0, 4 + p)),
                pl.BlockSpec((L, 128), lambda p, i: (0, 8 + p))]
    o_spec = pl.BlockSpec((qb, 128), lambda p, i: (i, p))
    o_shape = jax.ShapeDtypeStruct((L, 512), f32)
    scratch = [pltpu.VMEM((2, qb, BLK), f32), pltpu.VMEM((2, qb, BLK), f32)]
    if bg is None:
        return pl.pallas_call(body, name="sb_fwd", grid=(4, nq), in_specs=in_specs, out_specs=o_spec,
                              out_shape=o_shape, scratch_shapes=scratch,
                              compiler_params=_cp(("parallel", "arbitrary")))(sb, sb, sb)
    return pl.pallas_call(body, name="sb_fwd_gather", grid=(4, nq), in_specs=in_specs + [_HBM],
                          out_specs=(o_spec, _HBM), out_shape=(o_shape, _xchg_shape(bg, False)),
                          scratch_shapes=scratch + list(_XCHG_SEMS),
                          compiler_params=_cp(("arbitrary", "arbitrary")))(sb, sb, sb, bg)


def _sb_bwd(dmixed, sb, out_a, bg=None):
    L = sb.shape[0]
    qb = _qrows(L)
    nq, r = L // qb, qb // BLK

    def body(*refs):
        if bg is None:
            do_ref, o_ref, q_ref, k_ref, v_ref, dq_ref, dk_ref, dv_ref, dqa_ref, c_ref, cg_ref, ds_ref = refs
        else:
            do_ref, o_ref, q_ref, k_ref, v_ref, x_ref, dq_ref, dk_ref, dv_ref, g_ref = refs[:10]
            dqa_ref, c_ref, cg_ref, ds_ref = refs[10:14]
            start, wait = _xchg_ops(x_ref, g_ref, *refs[14:], True)
            first, last = _first_last(4, nq)
            pl.when(first)(start)
        i = pl.program_id(1)

        @pl.when(i == 0)
        def _():
            dk_ref[...] = jnp.zeros_like(dk_ref)
            dv_ref[...] = jnp.zeros_like(dv_ref)

        m_a = lax.broadcasted_iota(jnp.int32, (1, BLK), 1) < 64
        q = q_ref[...]
        zq = jnp.zeros_like(q)
        qs = (jnp.where(m_a, q, zq), jnp.where(m_a, zq, q))
        do = do_ref[...]
        zd = jnp.zeros_like(do)
        dos = (jnp.where(m_a, do, zd).astype(_MXU), jnp.where(m_a, zd, do).astype(_MXU))
        do_b = do.astype(_MXU)
        prod = do_b.astype(f32) * o_ref[...]
        ds_ref[0] = jnp.broadcast_to(jnp.sum(jnp.where(m_a, prod, 0.0), 1, keepdims=True), (qb, BLK))
        ds_ref[1] = jnp.broadcast_to(jnp.sum(jnp.where(m_a, 0.0, prod), 1, keepdims=True), (qb, BLK))
        u_gt = _tri_ones(True)
        u_ge = _tri_ones(False)
        dqa_ref[...] = jnp.zeros_like(dqa_ref)
        c_ref[...] = jnp.zeros_like(c_ref)
        cg_ref[...] = jnp.zeros_like(cg_ref)

        def chain(x, j, masked, both):
            off = _tile_off(j)
            k = k_ref[pl.ds(off, BLK), :]
            if x == 0:
                both["kk"] = _pair_rhs(k, m_a)
                both["z"] = _dot(q, both["kk"], NT)
                both["dw"] = _dot(do_b, _pair_rhs(v_ref[pl.ds(off, BLK), :], m_a), NT)
            mask = _sb_mask(i, j, qb) if masked else None
            o = {}
            yield from _sb_stages(both["z"][:, x * BLK:(x + 1) * BLK], mask, c_ref, x, u_gt, o)
            wb = o["w"]
            gr = wb.astype(f32) * both["dw"][:, x * BLK:(x + 1) * BLK]
            hi = gr.astype(_MXU)
            lo = (gr - hi.astype(f32)).astype(_MXU)
            yield
            eg = (jnp.dot(hi, u_ge, preferred_element_type=f32)
                  + jnp.dot(lo, u_ge, preferred_element_type=f32))
            yield
            cg = cg_ref[x]
            suffix = eg[:, 0:BLK] + cg
            cg_ref[x] = cg + eg[:, BLK:2 * BLK]
            dz = gr - jnp.exp(o["lb"]) * (gr + ds_ref[x] - suffix)
            if masked:
                dz = jnp.where(mask, dz, 0.0)
            dz = dz.astype(_MXU)
            yield
            if x == 0:
                both["dz"] = dz
            else:
                dqa_ref[0] += _dot(jnp.concatenate([both["dz"], dz], axis=1), both["kk"])
            dk_ref[pl.ds(off, BLK), :] += _dot(dz, qs[x], TN)
            dv_ref[pl.ds(off, BLK), :] += _dot(wb, dos[x], TN)

        def tiles(js, masked):
            shared = [{} for _ in js]
            _staggered([chain(x, j, masked, shared[t]) for t, j in enumerate(js) for x in range(2)])

        _sb_sweep(i, r, tiles, 2)
        dq_ref[...] = dqa_ref[0] * SB_SCALE
        if bg is not None:
            pl.when(last)(wait)

    blk = pl.BlockSpec((qb, 128), lambda p, i: (i, p))
    scr = pltpu.VMEM((2, qb, BLK), f32)
    in_specs = [blk, blk, blk, pl.BlockSpec((L, 128), lambda p, i: (0, 4 + p)),
                pl.BlockSpec((L, 128), lambda p, i: (0, 8 + p))]
    out_specs = (blk, pl.BlockSpec((L, 128), lambda p, i: (0, p)), pl.BlockSpec((L, 128), lambda p, i: (0, p)))
    out_shape = tuple(jax.ShapeDtypeStruct((L, 512), f32) for _ in range(3))
    if bg is None:
        return pl.pallas_call(body, name="sb_bwd", grid=(4, nq), in_specs=in_specs, out_specs=out_specs,
                              out_shape=out_shape, scratch_shapes=[scr, scr, scr, scr],
                              compiler_params=_cp(("parallel", "arbitrary")))(dmixed, out_a, sb, sb, sb)
    return pl.pallas_call(body, name="sb_bwd_scatter", grid=(4, nq), in_specs=in_specs + [_HBM],
                          out_specs=out_specs + (_HBM,), out_shape=out_shape + (_xchg_shape(bg, True),),
                          scratch_shapes=[scr, scr, scr, scr] + list(_XCHG_SEMS),
                          compiler_params=_cp(("arbitrary", "arbitrary")))(dmixed, out_a, sb, sb, sb, bg)


def _mla_mask(i, j, qb):
    row = i * qb + lax.broadcasted_iota(jnp.int32, (qb, BLK), 0)
    col = j * BLK + lax.broadcasted_iota(jnp.int32, (qb, BLK), 1)
    return (col <= row) & ((col >= N_PAD) | (col == row))


def _pair_mask2():
    return (lax.broadcasted_iota(jnp.int32, (1, 256), 1) % 128) < 64


def _mla_q2(qn_ref, qr_ref, mc_ref, ms_ref):
    qr = qr_ref[...]
    qr = qr * mc_ref[...] + _rot(qr, 16) * ms_ref[...]
    q2 = jnp.concatenate([qn_ref[...], qr], axis=1)
    m2 = _pair_mask2()
    z2 = jnp.zeros_like(q2)
    return (jnp.where(m2, q2, z2).astype(_MXU), jnp.where(m2, z2, q2).astype(_MXU)), q2.astype(_MXU)


def _mla_fwd(q, kv, kr2, mc, ms):
    L = q.shape[0]
    qb = _qrows(L)
    nq, r = L // qb, qb // BLK

    def body(qn_ref, qr_ref, mc_ref, ms_ref, kn_ref, v_ref, kr_ref, o_ref, lse_ref, acc_ref, m_ref):
        i = pl.program_id(1)
        m_a = lax.broadcasted_iota(jnp.int32, (1, BLK), 1) < 64
        _, q2 = _mla_q2(qn_ref, qr_ref, mc_ref, ms_ref)
        m2 = _pair_mask2()
        acc_ref[...] = jnp.zeros_like(acc_ref)
        m_ref[...] = jnp.full(m_ref.shape, -1e30, f32)
        ones = jnp.ones((BLK, BLK), _MXU)

        def chain(x, j, masked, both):
            off = _tile_off(j)
            if x == 0:
                k2 = jnp.concatenate([kn_ref[pl.ds(off, BLK), :], kr_ref[pl.ds(off, BLK), :]], axis=1)
                both["s"] = _dot(q2, _pair_rhs(k2, m2), NT)
            yield
            s = both["s"][:, x * BLK:(x + 1) * BLK] * MLA_SCALE
            if masked:
                mask = _mla_mask(i, j, qb)
                s = jnp.where(mask, s, -1e30)
            m_old = m_ref[x]
            m_new = jnp.maximum(m_old, jnp.max(s, 1, keepdims=True))
            a = jnp.exp(m_old - m_new)
            p = jnp.exp(s - m_new)
            if masked:
                p = jnp.where(mask, p, 0.0)
            m_ref[x] = m_new
            p = p.astype(_MXU)
            yield
            v1 = jnp.concatenate([v_ref[pl.ds(off, BLK), :], ones], axis=1)
            acc_ref[x] = jnp.concatenate([a, a], axis=1) * acc_ref[x] + _dot(p, v1)

        def tiles(js, masked):
            shared = [{} for _ in js]
            _staggered([chain(x, j, masked, shared[t]) for t, j in enumerate(js) for x in range(2)])

        _sb_sweep(i, r, tiles, 4)
        o_ref[...] = jnp.where(m_a, acc_ref[0, :, 0:BLK] / acc_ref[0, :, BLK:2 * BLK],
                               acc_ref[1, :, 0:BLK] / acc_ref[1, :, BLK:2 * BLK])
        for x in range(2):
            lse_ref[0, x] = m_ref[x] + jnp.log(acc_ref[x, :, BLK:2 * BLK])

    blk = lambda cb: pl.BlockSpec((qb, 128), lambda p, i: (i, cb + p))
    full = lambda cb: pl.BlockSpec((L, 128), lambda p, i: (0, cb + p))
    tab = pl.BlockSpec((qb, 128), lambda p, i: (i, 0))
    return pl.pallas_call(
        body, name="mla_fwd", grid=(4, nq),
        in_specs=[blk(0), blk(4), tab, tab, full(0), full(4), pl.BlockSpec((L, 128), lambda p, i: (0, 0))],
        out_specs=(blk(0), pl.BlockSpec((1, 2, qb, 128), lambda p, i: (p, 0, i, 0))),
        out_shape=(jax.ShapeDtypeStruct((L, 512), f32), jax.ShapeDtypeStruct((4, 2, L, 128), f32)),
        scratch_shapes=[pltpu.VMEM((2, qb, 2 * BLK), f32), pltpu.VMEM((2, qb, BLK), f32)],
        compiler_params=_cp(("parallel", "arbitrary")))(q, q, mc, ms, kv, kv, kr2)


def _mla_bwd(dmixed, q, kv, kr2, out_b, lse, mc, ms):
    L = q.shape[0]
    qb = _qrows(L)
    nq, r = L // qb, qb // BLK

    def body(do_ref, o_ref, lse_ref, qn_ref, qr_ref, mc_ref, ms_ref, kn_ref, v_ref, kr_ref,
             dqn_ref, dqr_ref, dkn_ref, dv_ref, dkr_ref, dqa_ref, ds_ref):
        i = pl.program_id(1)

        @pl.when(i == 0)
        def _():
            dkn_ref[...] = jnp.zeros_like(dkn_ref)
            dv_ref[...] = jnp.zeros_like(dv_ref)
            dkr_ref[...] = jnp.zeros_like(dkr_ref)

        m_a = lax.broadcasted_iota(jnp.int32, (1, BLK), 1) < 64
        qs, q2 = _mla_q2(qn_ref, qr_ref, mc_ref, ms_ref)
        m2 = _pair_mask2()
        do = do_ref[...]
        zd = jnp.zeros_like(do)
        dos = (jnp.where(m_a, do, zd).astype(_MXU), jnp.where(m_a, zd, do).astype(_MXU))
        do_b = do.astype(_MXU)
        prod = do * o_ref[...]
        ds_ref[0] = jnp.broadcast_to(jnp.sum(jnp.where(m_a, prod, 0.0), 1, keepdims=True), (qb, BLK))
        ds_ref[1] = jnp.broadcast_to(jnp.sum(jnp.where(m_a, 0.0, prod), 1, keepdims=True), (qb, BLK))
        dqa_ref[...] = jnp.zeros_like(dqa_ref)

        def chain(x, j, masked, both):
            off = _tile_off(j)
            if x == 0:
                k2 = jnp.concatenate([kn_ref[pl.ds(off, BLK), :], kr_ref[pl.ds(off, BLK), :]], axis=1)
                both["kk"] = _pair_rhs(k2, m2)
                both["s"] = _dot(q2, both["kk"], NT)
                both["dp"] = _dot(do_b, _pair_rhs(v_ref[pl.ds(off, BLK), :], m_a), NT)
            yield
            s = both["s"][:, x * BLK:(x + 1) * BLK] * MLA_SCALE
            dp = both["dp"][:, x * BLK:(x + 1) * BLK]
            if masked:
                mask = _mla_mask(i, j, qb)
                p = jnp.where(mask, jnp.exp(jnp.where(mask, s, 0.0) - lse_ref[0, x]), 0.0)
            else:
                p = jnp.exp(s - lse_ref[0, x])
            pb = p.astype(_MXU)
            ds = (p * (dp - ds_ref[x]) * MLA_SCALE).astype(_MXU)
            yield
            if x == 0:
                both["ds"] = ds
            else:
                dqa_ref[0] += _dot(jnp.concatenate([both["ds"], ds], axis=1), both["kk"])
            dk_t = _dot(ds, qs[x], TN)
            dkn_ref[pl.ds(off, BLK), :] += dk_t[:, 0:128]
            dkr_ref[0, pl.ds(off, BLK), :] += dk_t[:, 128:256]
            dv_ref[pl.ds(off, BLK), :] += _dot(pb, dos[x], TN)

        def tiles(js, masked):
            shared = [{} for _ in js]
            _staggered([chain(x, j, masked, shared[t]) for t, j in enumerate(js) for x in range(2)])

        _sb_sweep(i, r, tiles, 2)
        dq2 = dqa_ref[0]
        dqn_ref[...] = dq2[:, 0:128]
        dy = dq2[:, 128:256]
        dqr_ref[...] = dy * mc_ref[...] - _rot(dy * ms_ref[...], 16)

    blk = lambda cb: pl.BlockSpec((qb, 128), lambda p, i: (i, cb + p))
    full = lambda cb: pl.BlockSpec((L, 128), lambda p, i: (0, cb + p))
    tab = pl.BlockSpec((qb, 128), lambda p, i: (i, 0))
    o512 = jax.ShapeDtypeStruct((L, 512), f32)
    return pl.pallas_call(
        body, name="mla_bwd", grid=(4, nq),
        in_specs=[blk(4), blk(0), pl.BlockSpec((1, 2, qb, 128), lambda p, i: (p, 0, i, 0)), blk(0), blk(4), tab, tab,
                  full(0), full(4), pl.BlockSpec((L, 128), lambda p, i: (0, 0))],
        out_specs=(blk(0), blk(0), full(0), full(0), pl.BlockSpec((1, L, 128), lambda p, i: (p, 0, 0))),
        out_shape=(o512, o512, o512, o512, jax.ShapeDtypeStruct((4, L, 128), f32)),
        scratch_shapes=[pltpu.VMEM((2, qb, 2 * BLK), f32), pltpu.VMEM((2, qb, BLK), f32)],
        compiler_params=_cp(("parallel", "arbitrary")))(dmixed, out_b, lse, q, q, mc, ms, kv, kv, kr2)


def _ret_tables():
    log_g = jnp.log(jnp.array(RET_GAMMA, f32))
    idx = jnp.arange(BLK, dtype=f32)
    diff = idx[:, None] - idx[None, :]
    d_in = jnp.where(diff[None] >= 0, jnp.exp(jnp.maximum(diff, 0.0)[None] * log_g[:, None, None]), 0.0)
    q_dec = jnp.exp((idx[None, :] + 1.0) * log_g[:, None])
    k_dec = jnp.exp((BLK - 1.0 - idx[None, :]) * log_g[:, None])
    c_dec = jnp.exp(BLK * log_g)
    bc = lambda a: jnp.broadcast_to(a[:, :, None], (4, BLK, BLK))
    return d_in, bc(q_dec), bc(k_dec), jnp.broadcast_to(c_dec[:, None, None], (4, 8, BLK))


def _head_mask(x):
    lane = lax.broadcasted_iota(jnp.int32, (1, BLK), 1)
    return (lane < 64) if x == 0 else (lane >= 64)


def _ret_fwd(rqk, rv, proj, rtabs):
    L = rqk.shape[0]
    n = L // BLK
    d_in, q_dec, k_dec, c_dec = rtabs

    def body(q_ref, k_ref, v_ref, g_ref, din_ref, qd_ref, kd_ref, cd_ref, y_ref, o_ref, st_ref, s_scr):
        @pl.when(pl.program_id(1) == 0)
        def _():
            s_scr[...] = jnp.zeros_like(s_scr)

        q = q_ref[...]
        k = k_ref[...]
        zq = jnp.zeros_like(q)
        for x in range(2):
            hm = _head_mask(x)
            sl = slice(x * 128, (x + 1) * 128)
            qm = jnp.where(hm, q, zq)
            km = jnp.where(hm, k, zq)
            v = v_ref[:, sl]
            s_in = s_scr[x]
            st_ref[0, 0, x] = s_in
            inner = _dot(qm, km, NT) * din_ref[x]
            y = _dot(inner, v) + _dot(qm, s_in) * qd_ref[x]
            s_scr[x] = s_in * cd_ref[x, 0:1, :] + _dot(km.astype(f32) * kd_ref[x], v, TN)
            y_ref[:, sl] = y
            mu = jnp.mean(y, -1, keepdims=True)
            yc = y - mu
            yn = yc * lax.rsqrt(jnp.mean(jnp.square(yc), -1, keepdims=True) + LN_EPS)
            g = g_ref[:, sl]
            o_ref[:, sl] = g * jax.nn.sigmoid(g) * yn

    tab = pl.BlockSpec((2, BLK, BLK), lambda p, i: (p, 0, 0))
    return pl.pallas_call(
        body, name="ret_fwd", grid=(2, n),
        in_specs=[pl.BlockSpec((BLK, 128), lambda p, i: (i, p)), pl.BlockSpec((BLK, 128), lambda p, i: (i, 2 + p)),
                  pl.BlockSpec((BLK, 256), lambda p, i: (i, p)),
                  pl.BlockSpec((BLK, 256), lambda p, i: (i, C_RG // 256 + p)),
                  tab, tab, tab, pl.BlockSpec((2, 8, BLK), lambda p, i: (p, 0, 0))],
        out_specs=(pl.BlockSpec((BLK, 256), lambda p, i: (i, p)), pl.BlockSpec((BLK, 256), lambda p, i: (i, p)),
                   pl.BlockSpec((1, 1, 2, BLK, BLK), lambda p, i: (p, i, 0, 0, 0))),
        out_shape=(jax.ShapeDtypeStruct((L, 512), f32), jax.ShapeDtypeStruct((L, 512), f32),
                   jax.ShapeDtypeStruct((2, n, 2, BLK, BLK), f32)),
        scratch_shapes=[pltpu.VMEM((2, BLK, BLK), f32)],
        compiler_params=_cp(("parallel", "arbitrary")))(rqk, rqk, rv, proj, d_in, q_dec, k_dec, c_dec)


def _ret_bwd(dmixed, rqk, rv, proj, y, states, rtabs):
    L = rqk.shape[0]
    n = L // BLK
    d_in, q_dec, k_dec, c_dec = rtabs

    def body(do_ref, q_ref, k_ref, v_ref, g_ref, y_ref, st_ref, din_ref, qd_ref, kd_ref, cd_ref,
             dq_ref, dk_ref, dv_ref, dg_ref, ds_scr):
        @pl.when(pl.program_id(1) == 0)
        def _():
            ds_scr[...] = jnp.zeros_like(ds_scr)

        q = q_ref[...]
        k = k_ref[...]
        zq = jnp.zeros_like(q)
        dq_acc = jnp.zeros((BLK, BLK), f32)
        dk_acc = jnp.zeros((BLK, BLK), f32)
        for x in range(2):
            hm = _head_mask(x)
            sl = slice(x * 128, (x + 1) * 128)
            qm = jnp.where(hm, q, zq)
            km = jnp.where(hm, k, zq)
            v = v_ref[:, sl]
            yv = y_ref[:, sl]
            g = g_ref[:, sl]
            do = do_ref[:, sl]
            mu = jnp.mean(yv, -1, keepdims=True)
            yc = yv - mu
            rstd = lax.rsqrt(jnp.mean(jnp.square(yc), -1, keepdims=True) + LN_EPS)
            yn = yc * rstd
            sg = jax.nn.sigmoid(g)
            dg_ref[:, sl] = do * yn * sg * (1.0 + g * (1.0 - sg))
            dyn = do * g * sg
            dy = rstd * (dyn - jnp.mean(dyn, -1, keepdims=True) - yn * jnp.mean(dyn * yn, -1, keepdims=True))
            s_in = st_ref[0, 0, x]
            ds_out = ds_scr[x]
            kd = km.astype(f32) * kd_ref[x]
            a = _dot(qm, km, NT) * din_ref[x]
            da = _dot(dy, v, NT) * din_ref[x]
            dyq = dy * qd_ref[x]
            dq_acc += _dot(da, km) + _dot(dyq, s_in, NT)
            dk_acc += _dot(da, qm, TN) + _dot(v, ds_out, NT) * kd_ref[x]
            dv_ref[:, sl] = _dot(a, dy, TN) + _dot(kd, ds_out)
            ds_scr[x] = ds_out * cd_ref[x, 0:1, :] + _dot(qm, dyq, TN)
        dq_ref[...] = dq_acc
        dk_ref[...] = dk_acc

    rev = lambda w, cb: pl.BlockSpec((BLK, w), lambda p, i: (n - 1 - i, cb + p))
    tab = pl.BlockSpec((2, BLK, BLK), lambda p, i: (p, 0, 0))
    return pl.pallas_call(
        body, name="ret_bwd", grid=(2, n),
        in_specs=[rev(256, 4), rev(128, 0), rev(128, 2), rev(256, 0), rev(256, C_RG // 256), rev(256, 0),
                  pl.BlockSpec((1, 1, 2, BLK, BLK), lambda p, i: (p, n - 1 - i, 0, 0, 0)),
                  tab, tab, tab, pl.BlockSpec((2, 8, BLK), lambda p, i: (p, 0, 0))],
        out_specs=(rev(128, 0), rev(128, 0), rev(256, 0), rev(256, 0)),
        out_shape=(jax.ShapeDtypeStruct((L, 256), f32), jax.ShapeDtypeStruct((L, 256), f32),
                   jax.ShapeDtypeStruct((L, 512), f32), jax.ShapeDtypeStruct((L, 512), f32)),
        scratch_shapes=[pltpu.VMEM((2, BLK, BLK), f32)],
        compiler_params=_cp(("parallel", "arbitrary")))(dmixed, rqk, rqk, rv, proj, y, states, d_in, q_dec, k_dec, c_dec)


def _loss_head(h, target):
    L = h.shape[0]
    n = L // BLK

    def body(h_ref, t_ref, dy_ref, l_ref):
        i = pl.program_id(0)

        @pl.when(i == 0)
        def _():
            dy_ref[...] = jnp.zeros_like(dy_ref)
            l_ref[...] = jnp.zeros_like(l_ref)

        @pl.when(i > 0)
        def _():
            err = h_ref[...] - t_ref[...]
            dy_ref[...] = err * (1.0 / D)
            sq = jnp.sum(jnp.sum(jnp.square(err), 1, keepdims=True), 0, keepdims=True)
            l_ref[...] += (0.5 / D) * sq

    return pl.pallas_call(
        body, name="loss_head", grid=(n,),
        in_specs=[pl.BlockSpec((BLK, D), lambda i: (i, 0)),
                  pl.BlockSpec((BLK, D), lambda i: (jnp.maximum(i - 1, 0), 0))],
        out_specs=(pl.BlockSpec((BLK, D), lambda i: (i, 0)), pl.BlockSpec((8, 128), lambda i: (0, 0))),
        out_shape=(jax.ShapeDtypeStruct((L, D), f32), jax.ShapeDtypeStruct((8, 128), f32)),
        compiler_params=_cp(("arbitrary",)))(h, target)


def _adam_math(w, g, m, v):
    m = ADAM_B1 * m + (1.0 - ADAM_B1) * g
    v = ADAM_B2 * v + (1.0 - ADAM_B2) * jnp.square(g)
    m_hat = m / (1.0 - ADAM_B1 ** ADAM_STEP)
    v_hat = v / (1.0 - ADAM_B2 ** ADAM_STEP)
    delta = -ADAM_LR * (m_hat / (jnp.sqrt(v_hat) + ADAM_EPS) + ADAM_WD * w)
    return delta, m, v


def _adamw(parts, w, m, v, name):
    R, C = w.shape
    tr = _pick(R, (240, 192, 144, 96, 64, 48, 32, 16, 8))
    row = pl.BlockSpec((tr, C), lambda i: (i, 0))

    def body(p_ref, w_ref, m_ref, v_ref, g_ref, d_ref, nm_ref, nv_ref):
        g = p_ref[0].astype(f32)
        for k in range(1, N_DEV):
            g = g + p_ref[k].astype(f32)
        d, nm, nv = _adam_math(w_ref[...], g, m_ref[...], v_ref[...])
        g_ref[...] = g
        d_ref[...] = d
        nm_ref[...] = nm
        nv_ref[...] = nv

    o = jax.ShapeDtypeStruct((R, C), f32)
    return pl.pallas_call(
        body, name=name, grid=(R // tr,),
        in_specs=[pl.BlockSpec((N_DEV, tr, C), lambda i: (0, i, 0)), row, row, row],
        out_specs=(row, row, row, row), out_shape=(o, o, o, o),
        compiler_params=_cp(("parallel",)))(parts, w, m, v)


_XCHG_SEMS = [pltpu.SemaphoreType.DMA((N_DEV - 1,)), pltpu.SemaphoreType.DMA((N_DEV - 1,)), pltpu.SemaphoreType.DMA]
_HBM = pl.BlockSpec(memory_space=pltpu.HBM)


def _xchg_shape(x, all_to_all):
    return jax.ShapeDtypeStruct((N_DEV,) + tuple(x.shape[1:] if all_to_all else x.shape), x.dtype)


def _xchg_ops(x_ref, o_ref, send_sems, recv_sems, local_sem, all_to_all):
    mx, my, mc = lax.axis_index("x"), lax.axis_index("y"), lax.axis_index("c")
    me = 4 * mx + 2 * my + mc

    def peer(k):
        px = (1 - mx) if k & 4 else mx
        py = (1 - my) if k & 2 else my
        pc = (1 - mc) if k & 1 else mc
        return (px, py, pc), 4 * px + 2 * py + pc

    def copy(k):
        dev, idx = peer(k)
        src = x_ref.at[idx] if all_to_all else x_ref
        return pltpu.make_async_remote_copy(
            src_ref=src, dst_ref=o_ref.at[me], send_sem=send_sems.at[k - 1], recv_sem=recv_sems.at[k - 1],
            device_id=dev, device_id_type=pl.DeviceIdType.MESH)

    def start():
        pltpu.make_async_copy(x_ref.at[me] if all_to_all else x_ref, o_ref.at[me], local_sem).start()
        for k in range(1, N_DEV):
            copy(k).start()

    def wait():
        for k in range(1, N_DEV):
            dev, idx = peer(k)
            pltpu.make_async_remote_copy(
                src_ref=o_ref.at[idx], dst_ref=o_ref.at[idx], send_sem=send_sems.at[k - 1],
                recv_sem=recv_sems.at[k - 1], device_id=dev, device_id_type=pl.DeviceIdType.MESH).wait_recv()
        for k in range(1, N_DEV):
            copy(k).wait_send()
        pltpu.make_async_copy(x_ref.at[me] if all_to_all else x_ref, o_ref.at[me], local_sem).wait()

    return start, wait


def _exchange(x, all_to_all, name):
    def body(x_ref, o_ref, send_sems, recv_sems, local_sem):
        start, wait = _xchg_ops(x_ref, o_ref, send_sems, recv_sems, local_sem, all_to_all)
        start()
        wait()

    return pl.pallas_call(body, name=name, in_specs=[_HBM], out_specs=_HBM, out_shape=_xchg_shape(x, all_to_all),
                          scratch_shapes=list(_XCHG_SEMS))(x)


WC = 512
LAYER_ROWS = tuple(n // WC for n in (1024 * 468, 384 * 96, 256 * 128, 192 * 1024, 1024 * 512, 512 * 1024))


def _pack_layer(ws, l):
    return jnp.concatenate([w[l].reshape(-1, WC) for w in ws], axis=0)


def _unpack_layer(gathered):
    offs = np.cumsum((0,) + LAYER_ROWS)
    part = lambda k: gathered[:, offs[k]:offs[k + 1]]
    w_in = part(0).reshape(8, 1024, 468).transpose(1, 0, 2).reshape(1024, 3744)
    z32 = jnp.zeros((1024, 32), w_in.dtype)
    kr = w_in[:, 2176:2208]
    w_in = jnp.concatenate([w_in[:, 0:1536], w_in[:, 1920:2176], w_in[:, 2208:3744], w_in[:, 1536:1920],
                            kr, z32, kr, z32], axis=1)
    w_uq = part(1).reshape(8, 384, 96).transpose(1, 0, 2)
    rope = jnp.concatenate([w_uq[..., 64:96], jnp.zeros((384, 8, 32), w_uq.dtype)], axis=-1)
    w_uq = jnp.concatenate([w_uq[..., 0:64].reshape(384, 512), rope.reshape(384, 512)], axis=1)
    w_ukv = part(2).reshape(8, 256, 128).transpose(1, 0, 2)
    w_ukv = jnp.concatenate([w_ukv[..., 0:64].reshape(256, 512), w_ukv[..., 64:128].reshape(256, 512)], axis=1)
    w_out = part(3).reshape(1536, 1024)
    w_ff1 = part(4).reshape(8, 1024, 512).transpose(1, 0, 2).reshape(1024, 4096)
    w_ff2 = part(5).reshape(4096, 1024)
    return w_in, w_uq, w_ukv, w_out, w_ff1, w_ff2


def _pack_layer_grads(g_in, g_uq, g_ukv, g_out, g_ff1, g_ff2):
    kr = g_in[:, C_KR:C_KR + 32] + g_in[:, C_KR + 64:C_KR + 96]
    g_in = jnp.concatenate([g_in[:, 0:1536], g_in[:, C_CQ:C_CQ + 384], g_in[:, C_CKV:C_CKV + 256], kr,
                            g_in[:, C_RQ:C_CQ]], axis=1)
    g_in = g_in.reshape(1024, 8, 468).transpose(1, 0, 2)
    g_uq = jnp.concatenate([g_uq[:, 0:512].reshape(384, 8, 64), g_uq[:, 512:1024].reshape(384, 8, 64)[..., 0:32]],
                           axis=-1).transpose(1, 0, 2)
    g_ukv = jnp.concatenate([g_ukv[:, 0:512].reshape(256, 8, 64), g_ukv[:, 512:1024].reshape(256, 8, 64)],
                            axis=-1).transpose(1, 0, 2)
    g_ff1 = g_ff1.reshape(1024, 8, 512).transpose(1, 0, 2)
    return jnp.concatenate([g.reshape(8, -1, WC) for g in (g_in, g_uq, g_ukv, g_out, g_ff1, g_ff2)], axis=1)


def _rope_tables(L):
    pos = (jnp.arange(L) - N_PAD).astype(f32)

    def cs(half):
        inv = ROPE_THETA ** (-jnp.arange(half, dtype=f32) / half)
        ang = pos[:, None] * inv[None, :]
        return jnp.cos(ang), jnp.sin(ang)

    c, s = cs(32)
    rc, rs = jnp.tile(c, (1, 8)), jnp.tile(s, (1, 8))
    c, s = cs(16)
    z = jnp.zeros((L, 32), f32)
    mc, ms = jnp.concatenate([c, c, z, c, c, z], 1), jnp.concatenate([s, s, z, s, s, z], 1)
    return rc, rs, mc, ms


def _layer_fwd(h, wl, gq, gkv, g1, b1, g2, b2, tabs, rtabs, next_shard):
    w_in, w_uq, w_ukv, w_out, w_ff1, w_ff2 = wl
    proj = _mm(h, w_in, "nn", "mm_in")
    sb, cqn, ckvn, rqk, rv, kr2 = _prep_fwd(proj, gq, gkv, tabs)
    q = _mm(cqn, w_uq, "nn", "mm_uq")
    kv = _mm(ckvn, w_ukv, "nn", "mm_ukv", out_dtype=bf16)
    if next_shard is None:
        out_a, gathered = _sb_fwd(sb), None
    else:
        out_a, gathered = _sb_fwd(sb, next_shard)
    out_b, lse = _mla_fwd(q, kv, kr2, tabs[2], tabs[3])
    y, out_c, states = _ret_fwd(rqk, rv, proj, rtabs)
    mixed = jnp.concatenate([out_a, out_b, out_c], axis=1).astype(bf16)
    mix = _mm(mixed, w_out, "nn", "mm_out")
    h1 = _ln_fwd(h, mix, g1, b1, DN_ALPHA, "ln_fwd")
    u, a = _mm(h1, w_ff1, "nn", "mm_ff1", epi="relu2")
    ff = _mm(a, w_ff2, "nn", "mm_ff2")
    h2 = _ln_fwd(h1, ff, g2, b2, DN_ALPHA, "ln_fwd")
    saved = (h, proj, sb, cqn, ckvn, rqk, rv, kr2, q, kv, out_a, out_b, lse, y, states, mixed, mix, h1, u, a, ff)
    return h2, saved, gathered


def _layer_bwd(dh2, saved, wl, gq, gkv, g1, g2, tabs, rtabs, grads_above):
    w_in, w_uq, w_ukv, w_out, w_ff1, w_ff2 = wl
    h, proj, sb, cqn, ckvn, rqk, rv, kr2, q, kv, out_a, out_b, lse, y, states, mixed, mix, h1, u, a, ff = saved
    dz2, dg2, db2 = _ln_bwd(dh2, h1, ff, g2, DN_ALPHA, "ln_bwd")
    du = _mm(dz2, w_ff2, "nt", "mm_dff2", epi="mul_relu", extra=u, out_dtype=bf16)
    gw_ff2 = _mm(a, dz2, "tn", "mm_gff2")
    dh1 = _mm(du, w_ff1, "nt", "mm_dff1", epi="add", extra=dz2, alpha=DN_ALPHA)
    gw_ff1 = _mm(h1, du, "tn", "mm_gff1")
    dz1, dg1, db1 = _ln_bwd(dh1, h, mix, g1, DN_ALPHA, "ln_bwd")
    dmixed = _mm(dz1, w_out, "nt", "mm_dout")
    gw_out = _mm(mixed, dz1, "tn", "mm_gout")
    if grads_above is None:
        (dsq, dsk, dsv), parts_above = _sb_bwd(dmixed, sb, out_a), None
    else:
        dsq, dsk, dsv, parts_above = _sb_bwd(dmixed, sb, out_a, grads_above)
    dqn, dqr, dkn, dv, dkr_p = _mla_bwd(dmixed, q, kv, kr2, out_b, lse, tabs[2], tabs[3])
    dq = jnp.concatenate([dqn, dqr], axis=1).astype(bf16)
    dkv = jnp.concatenate([dkn, dv], axis=1).astype(bf16)
    dcqn = _mm(dq, w_uq, "nt", "mm_duq")
    gw_uq = _mm(cqn, dq, "tn", "mm_guq")
    dckvn = _mm(dkv, w_ukv, "nt", "mm_dukv")
    gw_ukv = _mm(ckvn, dkv, "tn", "mm_gukv")
    drq_r, drk_r, drv, drg = _ret_bwd(dmixed, rqk, rv, proj, y, states, rtabs)
    dkr_r = dkr_p[0] + dkr_p[1] + dkr_p[2] + dkr_p[3]
    dcq, dckv, drq, drk, dkr2, dgq, dgkv = _prep_bwd(proj, gq, gkv, tabs, dcqn, dckvn, drq_r, drk_r, dkr_r)
    dproj = jnp.concatenate([dsq, dsk, dsv, dckv, drq, drk, drv, drg, dcq, dkr2], axis=1).astype(bf16)
    dh = _mm(dproj, w_in, "nt", "mm_din", epi="add", extra=dz1, alpha=DN_ALPHA)
    gw_in = _mm(h, dproj, "tn", "mm_gin")
    return dh, (gw_in, gw_uq, gw_ukv, gw_out, gw_ff1, gw_ff2), (dgq, dgkv, dg1, db1, dg2, db2), parts_above


def kernel(x, meta_tokens, ln_emb_g, ln_emb_b, w_in, mla_q_norm, mla_kv_norm, w_uq, w_ukv, w_out, ln1_g, ln1_b, w_ff1, w_ff2, ln2_g, ln2_b, loss_target, m_meta_tokens, m_ln_emb_g, m_ln_emb_b, m_w_in, m_mla_q_norm, m_mla_kv_norm, m_w_uq, m_w_ukv, m_w_out, m_ln1_g, m_ln1_b, m_w_ff1, m_w_ff2, m_ln2_g, m_ln2_b, v_meta_tokens, v_ln_emb_g, v_ln_emb_b, v_w_in, v_mla_q_norm, v_mla_kv_norm, v_w_uq, v_w_ukv, v_w_out, v_ln1_g, v_ln1_b, v_w_ff1, v_w_ff2, v_ln2_g, v_ln2_b):
    depth = w_in.shape[0]
    S = x.shape[1]
    L = S + BLK
    me = 4 * lax.axis_index("x") + 2 * lax.axis_index("y") + lax.axis_index("c")
    big = (w_in, w_uq, w_ukv, w_out, w_ff1, w_ff2)
    big_m = (m_w_in, m_w_uq, m_w_ukv, m_w_out, m_w_ff1, m_w_ff2)
    big_v = (v_w_in, v_w_uq, v_w_ukv, v_w_out, v_w_ff1, v_w_ff2)
    small = (ln_emb_g, ln_emb_b, mla_q_norm, mla_kv_norm, ln1_g, ln1_b, ln2_g, ln2_b)
    small_m = (m_ln_emb_g, m_ln_emb_b, m_mla_q_norm, m_mla_kv_norm, m_ln1_g, m_ln1_b, m_ln2_g, m_ln2_b)
    small_v = (v_ln_emb_g, v_ln_emb_b, v_mla_q_norm, v_mla_kv_norm, v_ln1_g, v_ln1_b, v_ln2_g, v_ln2_b)

    shards = [_pack_layer(big, l) for l in range(depth)]
    gathered = _exchange(shards[0].astype(bf16), False, "gather_w0")
    meta_all = _exchange(meta_tokens, False, "gather_meta")
    meta_full = meta_all.transpose(1, 0, 2).reshape(N_META, D)

    tabs = _rope_tables(L)
    rtabs = _ret_tables()

    hcat = jnp.concatenate([jnp.zeros((N_PAD, D), f32), meta_full, x[0]], axis=0)
    h = _ln_fwd(hcat, None, ln_emb_g, ln_emb_b, 1.0, "ln_emb_fwd")
    saved, full = [], []
    for l in range(depth):
        full.append(_unpack_layer(gathered))
        nxt = shards[l + 1].astype(bf16) if l + 1 < depth else None
        h, sv, gathered = _layer_fwd(h, full[l], mla_q_norm[l], mla_kv_norm[l], ln1_g[l], ln1_b[l], ln2_g[l],
                                     ln2_b[l], tabs, rtabs, nxt)
        saved.append(sv)

    dh, loss_part = _loss_head(h, loss_target[0])
    gsmall, parts, pending = [None] * depth, [None] * depth, None
    for l in reversed(range(depth)):
        dh, gbig, gsmall[l], got = _layer_bwd(dh, saved[l], full[l], mla_q_norm[l], mla_kv_norm[l], ln1_g[l],
                                              ln2_g[l], tabs, rtabs, pending)
        if pending is not None:
            parts[l + 1] = got
        pending = _pack_layer_grads(*gbig).astype(bf16)
    parts[0] = _exchange(pending, True, "scatter_g0")
    dz0, dg_emb, db_emb = _ln_bwd(dh, hcat, None, ln_emb_g, 1.0, "ln_emb_bwd")
    grad_x = dz0[BLK:][None]
    dmeta = dz0[N_PAD:BLK]

    adam = [_adamw(parts[l], shards[l], _pack_layer(big_m, l), _pack_layer(big_v, l), "adamw_big")
            for l in range(depth)]

    st = lambda k: jnp.stack([gsmall[l][k] for l in range(depth)])
    g_small = (dg_emb, db_emb, st(0), st(1), st(2), st(3), st(4), st(5))
    n_small = sum(int(np.prod(a.shape)) for a in small)
    flat = jnp.concatenate([a.reshape(-1) for a in g_small] + [dmeta.reshape(-1), loss_part[0, 0:1]])
    rows = -(-(flat.shape[0]) // (8 * D)) * 8
    pad = rows * D - flat.shape[0]
    flat = jnp.concatenate([flat, jnp.zeros((pad,), f32)]).reshape(rows, D)
    parts_s = _exchange(flat, False, "gather_small")

    def pack_small(arrs, meta_shard):
        col = jnp.zeros((N_META, D), f32)
        col = lax.dynamic_update_slice(col, meta_shard, (0, me * 128))
        fl = jnp.concatenate([a.reshape(-1) for a in arrs] + [col.reshape(-1), jnp.zeros((1 + pad,), f32)])
        return fl.reshape(rows, D)

    g_s, d_s, m_s, v_s = _adamw(parts_s, pack_small(small, meta_tokens), pack_small(small_m, m_meta_tokens),
                                pack_small(small_v, v_meta_tokens), "adamw_small")
    loss = g_s.reshape(-1)[n_small + N_META * D]

    def unpack_big(which):
        outs, off = [], 0
        for w, r in zip(big, LAYER_ROWS):
            outs.append(jnp.stack([adam[l][which][off:off + r].reshape(w.shape[1:]) for l in range(depth)]))
            off += r
        return outs

    def unpack_small(flat_rows):
        fl = flat_rows.reshape(-1)
        outs, off = [], 0
        for a in small:
            n = int(np.prod(a.shape))
            outs.append(fl[off:off + n].reshape(a.shape))
            off += n
        meta = lax.dynamic_slice(fl[off:off + N_META * D].reshape(N_META, D), (0, me * 128), (N_META, 128))
        return meta, outs

    def assemble(which, small_rows_arr):
        b = unpack_big(which)
        meta, s = unpack_small(small_rows_arr)
        return [meta, s[0], s[1], b[0], s[2], s[3], b[1], b[2], b[3], s[4], s[5], b[4], b[5], s[6], s[7]]

    return (loss, grad_x, *assemble(0, g_s), *assemble(1, d_s), *assemble(2, m_s), *assemble(3, v_s))
```

```python
import functools
import math

import numpy as np
import jax
import jax.numpy as jnp
from jax import lax
from jax.experimental import pallas as pl
from jax.experimental.pallas import tpu as pltpu

f32 = jnp.float32
bf16 = jnp.bfloat16
_MXU = jnp.bfloat16

BLK = 128
N_META = 16
N_PAD = 112
D = 1024
N_DEV = 8
LN_EPS = 1e-5
DEPTH = 4
DN_ALPHA = (2 * DEPTH) ** 0.25
ROPE_THETA = 10000.0
MLA_SCALE = (64 + 32) ** -0.5
SB_SCALE = 0.125
RET_SCALE = 0.125
RET_GAMMA = tuple(1.0 - 2.0 ** (-5 - h) for h in range(4))

ADAM_LR, ADAM_B1, ADAM_B2, ADAM_EPS, ADAM_WD, ADAM_STEP = 0.001, 0.9, 0.999, 1e-08, 0.01, 10

C_SBQ, C_SBK, C_SBV, C_CKV, C_RQ, C_RK, C_RV, C_RG, C_CQ, C_KR, N_INP = (
    0, 512, 1024, 1536, 1792, 2048, 2304, 2816, 3328, 3712, 3840)

VMEM_LIMIT = 56 * 1024 * 1024


def _cp(sem):
    return pltpu.CompilerParams(dimension_semantics=sem, vmem_limit_bytes=VMEM_LIMIT)


def _pick(n, cands):
    for c in cands:
        if n % c == 0:
            return c
    return n


def _dot(a, b, dims=(((1,), (0,)), ((), ()))):
    return lax.dot_general(a.astype(_MXU), b.astype(_MXU), dims, preferred_element_type=f32)


NT = (((1,), (1,)), ((), ()))
TN = (((0,), (0,)), ((), ()))


def _dot3(x, u):
    hi = x.astype(_MXU)
    r1 = x - hi.astype(f32)
    mid = r1.astype(_MXU)
    lo = (r1 - mid.astype(f32)).astype(_MXU)
    return (jnp.dot(hi, u, preferred_element_type=f32) + jnp.dot(mid, u, preferred_element_type=f32)
            + jnp.dot(lo, u, preferred_element_type=f32))


def _rot(x, half):
    lane = lax.broadcasted_iota(jnp.int32, x.shape, 1)
    first = (lane % 64) < half
    return jnp.where(first, -pltpu.roll(x, 128 - half, 1), pltpu.roll(x, half, 1))


def _mm(a, b, mode, name, epi=None, extra=None, alpha=1.0, out_dtype=f32):
    if mode == "nn":
        (M, K), N = a.shape, b.shape[1]
    elif mode == "nt":
        (M, K), N = a.shape, b.shape[0]
    else:
        (K, M), N = a.shape, b.shape[1]
    tm = _pick(M, (1408, 1024, 768, 512, 384, 256, 128))
    tn = _pick(N, ((1920,) if mode == "tn" else ()) + (1024, 768, 512, 384, 256, 128))
    tk = _pick(K, (1024, 768, 512, 384, 256, 128))
    nk = K // tk
    if mode == "nn":
        a_spec = pl.BlockSpec((tm, tk), lambda i, j, k: (i, k))
        b_spec = pl.BlockSpec((tk, tn), lambda i, j, k: (k, j))
        dims = (((1,), (0,)), ((), ()))
    elif mode == "nt":
        a_spec = pl.BlockSpec((tm, tk), lambda i, j, k: (i, k))
        b_spec = pl.BlockSpec((tn, tk), lambda i, j, k: (j, k))
        dims = NT
    else:
        a_spec = pl.BlockSpec((tk, tm), lambda i, j, k: (k, i))
        b_spec = pl.BlockSpec((tk, tn), lambda i, j, k: (k, j))
        dims = TN
    o_spec = pl.BlockSpec((tm, tn), lambda i, j, k: (i, j))
    in_specs, args = [a_spec, b_spec], [a, b]
    if extra is not None:
        in_specs.append(o_spec)
        args.append(extra)
    if epi == "relu2":
        out_shape = (jax.ShapeDtypeStruct((M, N), f32), jax.ShapeDtypeStruct((M, N), bf16))
        out_specs = (o_spec, o_spec)
    else:
        out_shape = jax.ShapeDtypeStruct((M, N), out_dtype)
        out_specs = o_spec

    def body(*refs):
        a_ref, b_ref = refs[0], refs[1]
        acc = refs[-1]
        k = pl.program_id(2)

        @pl.when(k == 0)
        def _():
            acc[...] = jnp.zeros_like(acc)

        acc[...] += _dot(a_ref[...], b_ref[...], dims)

        @pl.when(k == nk - 1)
        def _():
            r = acc[...]
            if epi == "relu2":
                refs[2][...] = r
                refs[3][...] = jnp.square(jnp.maximum(r, 0.0)).astype(bf16)
            elif epi == "mul_relu":
                refs[3][...] = (r * (2.0 * jnp.maximum(refs[2][...], 0.0))).astype(out_dtype)
            elif epi == "add":
                refs[3][...] = r + alpha * refs[2][...]
            else:
                refs[2][...] = r.astype(out_dtype)

    return pl.pallas_call(
        body, name=name, grid=(M // tm, N // tn, nk), in_specs=in_specs, out_specs=out_specs,
        out_shape=out_shape, scratch_shapes=[pltpu.VMEM((tm, tn), f32)],
        compiler_params=_cp(("parallel", "parallel", "arbitrary")))(*args)


def _ln_fwd(h, m, g, b, alpha, name):
    L = h.shape[0]
    tm = _pick(L, (384, 256, 128))
    row = pl.BlockSpec((tm, D), lambda i: (i, 0))
    vec = pl.BlockSpec((1, D), lambda i: (0, 0))

    def body(*refs):
        if m is None:
            h_ref, g_ref, b_ref, o_ref = refs
            z = h_ref[...]
        else:
            h_ref, m_ref, g_ref, b_ref, o_ref = refs
            z = alpha * h_ref[...] + m_ref[...]
        mu = jnp.mean(z, -1, keepdims=True)
        var = jnp.mean(jnp.square(z - mu), -1, keepdims=True)
        o_ref[...] = (z - mu) * lax.rsqrt(var + LN_EPS) * g_ref[...] + b_ref[...]

    args = [h] + ([] if m is None else [m]) + [g.reshape(1, D), b.reshape(1, D)]
    specs = [row] + ([] if m is None else [row]) + [vec, vec]
    return pl.pallas_call(body, name=name, grid=(L // tm,), in_specs=specs, out_specs=row,
                          out_shape=jax.ShapeDtypeStruct((L, D), f32), compiler_params=_cp(("parallel",)))(*args)


def _ln_bwd(dy, h, m, g, alpha, name):
    L = h.shape[0]
    tm = _pick(L, (384, 256, 128))
    row = pl.BlockSpec((tm, D), lambda i: (i, 0))
    vec = pl.BlockSpec((1, D), lambda i: (0, 0))
    acc = pl.BlockSpec((8, D), lambda i: (0, 0))

    def body(*refs):
        if m is None:
            dy_ref, h_ref, g_ref, dz_ref, dg_ref, db_ref = refs
            z = h_ref[...]
        else:
            dy_ref, h_ref, m_ref, g_ref, dz_ref, dg_ref, db_ref = refs
            z = alpha * h_ref[...] + m_ref[...]

        @pl.when(pl.program_id(0) == 0)
        def _():
            dg_ref[...] = jnp.zeros_like(dg_ref)
            db_ref[...] = jnp.zeros_like(db_ref)

        dyv = dy_ref[...]
        mu = jnp.mean(z, -1, keepdims=True)
        zc = z - mu
        rstd = lax.rsqrt(jnp.mean(jnp.square(zc), -1, keepdims=True) + LN_EPS)
        xh = zc * rstd
        dxh = dyv * g_ref[...]
        dz_ref[...] = rstd * (dxh - jnp.mean(dxh, -1, keepdims=True) - xh * jnp.mean(dxh * xh, -1, keepdims=True))
        dg_ref[0:1, :] += jnp.sum(dyv * xh, 0, keepdims=True)
        db_ref[0:1, :] += jnp.sum(dyv, 0, keepdims=True)

    args = [dy, h] + ([] if m is None else [m]) + [g.reshape(1, D)]
    specs = [row, row] + ([] if m is None else [row]) + [vec]
    dz, dg, db = pl.pallas_call(
        body, name=name, grid=(L // tm,), in_specs=specs, out_specs=(row, acc, acc),
        out_shape=(jax.ShapeDtypeStruct((L, D), f32), jax.ShapeDtypeStruct((8, D), f32),
                   jax.ShapeDtypeStruct((8, D), f32)),
        compiler_params=_cp(("arbitrary",)))(*args)
    return dz, dg[0], db[0]


def _rms(x, g):
    r = lax.rsqrt(jnp.mean(jnp.square(x), -1, keepdims=True) + LN_EPS)
    return x * r * g


def _prep_fwd(proj, gq, gkv, tabs):
    L = proj.shape[0]
    tm = BLK
    rc, rs, mc, ms = tabs

    def body(p_ref, gq_ref, gkv_ref, rc_ref, rs_ref, mc_ref, ms_ref, sb_ref, cq_ref, ckv_ref, rqk_ref, rv_ref, kr_ref):
        i = pl.program_id(0)
        sb_ref[:, 0:512] = (p_ref[:, C_SBQ:C_SBQ + 512] * SB_SCALE).astype(bf16)
        sb_ref[:, 512:1536] = p_ref[:, C_SBK:C_SBK + 1024].astype(bf16)
        cq_ref[...] = _rms(p_ref[:, C_CQ:C_CQ + 384], gq_ref[...]).astype(bf16)
        ckv_ref[...] = _rms(p_ref[:, C_CKV:C_CKV + 256], gkv_ref[...]).astype(bf16)
        valid = (i * tm + lax.broadcasted_iota(jnp.int32, (tm, 128), 0)) >= N_PAD
        for c in range(2):
            sl = slice(c * 128, (c + 1) * 128)
            x = p_ref[:, C_RQ + c * 128:C_RQ + (c + 1) * 128]
            rqk_ref[:, sl] = (x * rc_ref[:, sl] + _rot(x, 32) * rs_ref[:, sl]).astype(bf16)
            x = p_ref[:, C_RK + c * 128:C_RK + (c + 1) * 128]
            kk = (x * rc_ref[:, sl] + _rot(x, 32) * rs_ref[:, sl]) * RET_SCALE
            rqk_ref[:, 256 + c * 128:256 + (c + 1) * 128] = jnp.where(valid, kk, 0.0).astype(bf16)
        rv_ref[...] = p_ref[:, C_RV:C_RV + 512].astype(bf16)
        x = p_ref[:, C_KR:C_KR + 128]
        kr_ref[...] = (x * mc_ref[...] + _rot(x, 16) * ms_ref[...]).astype(bf16)

    def row(w):
        return pl.BlockSpec((tm, w), lambda i: (i, 0))

    def vec(w):
        return pl.BlockSpec((1, w), lambda i: (0, 0))

    widths = (1536, 384, 256, 512, 512, 128)
    return pl.pallas_call(
        body, name="prep_fwd", grid=(L // tm,),
        in_specs=[row(N_INP), vec(384), vec(256), row(256), row(256), row(128), row(128)],
        out_specs=tuple(row(w) for w in widths),
        out_shape=tuple(jax.ShapeDtypeStruct((L, w), bf16) for w in widths),
        compiler_params=_cp(("parallel",)))(proj, gq.reshape(1, 384), gkv.reshape(1, 256), rc, rs, mc, ms)


def _rms_bwd(x, g, dy):
    r = lax.rsqrt(jnp.mean(jnp.square(x), -1, keepdims=True) + LN_EPS)
    u = dy * g
    dx = r * u - x * (r * r * r) * jnp.mean(x * u, -1, keepdims=True)
    return dx, jnp.sum(dy * x * r, 0, keepdims=True)


def _prep_bwd(proj, gq, gkv, tabs, dcqn, dckvn, drq_r, drk_r, dkr_r):
    L = proj.shape[0]
    tm = BLK
    rc, rs, mc, ms = tabs

    def body(p_ref, gq_ref, gkv_ref, rc_ref, rs_ref, mc_ref, ms_ref, dcqn_ref, dckvn_ref, drq_ref, drk_ref,
             dkr_ref, ocq_ref, ockv_ref, orq_ref, ork_ref, okr_ref, dgq_ref, dgkv_ref):
        i = pl.program_id(0)

        @pl.when(i == 0)
        def _():
            dgq_ref[...] = jnp.zeros_like(dgq_ref)
            dgkv_ref[...] = jnp.zeros_like(dgkv_ref)

        dx, dg = _rms_bwd(p_ref[:, C_CQ:C_CQ + 384], gq_ref[...], dcqn_ref[...])
        ocq_ref[...] = dx
        dgq_ref[0:1, :] += dg
        dx, dg = _rms_bwd(p_ref[:, C_CKV:C_CKV + 256], gkv_ref[...], dckvn_ref[...])
        ockv_ref[...] = dx
        dgkv_ref[0:1, :] += dg
        valid = (i * tm + lax.broadcasted_iota(jnp.int32, (tm, 128), 0)) >= N_PAD
        for c in range(2):
            sl = slice(c * 128, (c + 1) * 128)
            dy = drq_ref[:, sl]
            orq_ref[:, sl] = dy * rc_ref[:, sl] - _rot(dy * rs_ref[:, sl], 32)
            dy = jnp.where(valid, drk_ref[:, sl], 0.0) * RET_SCALE
            ork_ref[:, sl] = dy * rc_ref[:, sl] - _rot(dy * rs_ref[:, sl], 32)
        dy = dkr_ref[...]
        okr_ref[...] = dy * mc_ref[...] - _rot(dy * ms_ref[...], 16)

    def row(w):
        return pl.BlockSpec((tm, w), lambda i: (i, 0))

    def vec(w):
        return pl.BlockSpec((1, w), lambda i: (0, 0))

    def acc(w):
        return pl.BlockSpec((8, w), lambda i: (0, 0))

    widths = (384, 256, 256, 256, 128)
    outs = pl.pallas_call(
        body, name="prep_bwd", grid=(L // tm,),
        in_specs=[row(N_INP), vec(384), vec(256), row(256), row(256), row(128), row(128),
                  row(384), row(256), row(256), row(256), row(128)],
        out_specs=tuple(row(w) for w in widths) + (acc(384), acc(256)),
        out_shape=tuple(jax.ShapeDtypeStruct((L, w), f32) for w in widths)
        + (jax.ShapeDtypeStruct((8, 384), f32), jax.ShapeDtypeStruct((8, 256), f32)),
        compiler_params=_cp(("arbitrary",)))(
            proj, gq.reshape(1, 384), gkv.reshape(1, 256), rc, rs, mc, ms, dcqn, dckvn, drq_r, drk_r, dkr_r)
    return outs[:5] + (outs[5][0], outs[6][0])


def _qrows(L):
    return _pick(L, (384, 256, 128))


def _tri_ones(strict):
    r = lax.broadcasted_iota(jnp.int32, (BLK, 2 * BLK), 0)
    c = lax.broadcasted_iota(jnp.int32, (BLK, 2 * BLK), 1)
    tri = (r > c) if strict else (r >= c)
    return jnp.where(tri | (c >= BLK), 1.0, 0.0).astype(_MXU)


def _dot2(x, u):
    hi = x.astype(_MXU)
    lo = (x - hi.astype(f32)).astype(_MXU)
    return jnp.dot(hi, u, preferred_element_type=f32) + jnp.dot(lo, u, preferred_element_type=f32)


def _staggered(chains):
    live = list(chains)
    step = 0
    while live:
        for ci, g in enumerate(chains):
            if g in live and step >= ci and next(g, True):
                live.remove(g)
        step += 1


def _pair_rhs(t, m):
    zt = jnp.zeros_like(t)
    return jnp.concatenate([jnp.where(m, t, zt), jnp.where(m, zt, t)], axis=0)


def _sb_stages(z, mask, c_ref, x, u_gt, out):
    yield
    lb = jnp.minimum(z, 0.0) - jnp.log1p(jnp.exp(-jnp.abs(z)))
    lk = lb - z
    if mask is not None:
        lk = jnp.where(mask, lk, 0.0)
    hi = lk.astype(_MXU)
    lo = (lk - hi.astype(f32)).astype(_MXU)
    yield
    el = (jnp.dot(hi, u_gt, preferred_element_type=f32)
          + jnp.dot(lo, u_gt, preferred_element_type=f32))
    yield
    c = c_ref[x]
    w = jnp.exp(lb + el[:, 0:BLK] + c)
    if mask is not None:
        w = jnp.where(mask, w, 0.0)
    c_ref[x] = c + el[:, BLK:2 * BLK]
    out["w"], out["lb"] = w.astype(_MXU), lb


def _sb_sweep(i, r, tiles, per_pass):
    n_t = r * (i + 1)
    lax.fori_loop(0, r, lambda jj, c: tiles([n_t - 1 - jj], True) or c, 0)
    n_bulk = jnp.maximum(r * i - 1, 0)
    n_full = n_bulk // per_pass
    lax.fori_loop(0, n_full,
                  lambda jj, c: tiles([r * i - 1 - per_pass * jj - t for t in range(per_pass)], False) or c, 0)
    rem = n_bulk - n_full * per_pass

    if per_pass == 4:
        @pl.when(rem >= 2)
        def _():
            tiles([rem, rem - 1], False)

    @pl.when(rem % 2 == 1)
    def _():
        tiles([1], False)

    @pl.when(i > 0)
    def _():
        tiles([0], True)


def _tile_off(j):
    return j * BLK if isinstance(j, int) else pl.multiple_of(j * BLK, BLK)


def _sb_mask(i, j, qb):
    row = i * qb + lax.broadcasted_iota(jnp.int32, (qb, BLK), 0)
    col = j * BLK + lax.broadcasted_iota(jnp.int32, (qb, BLK), 1)
    return (col < row) & (col >= N_PAD)


def _first_last(n0, n1):
    p, i = pl.program_id(0), pl.program_id(1)
    return (p == 0) & (i == 0), (p == n0 - 1) & (i == n1 - 1)


def _sb_fwd(sb, bg=None):
    L = sb.shape[0]
    qb = _qrows(L)
    nq, r = L // qb, qb // BLK

    def body(*refs):
        if bg is None:
            q_ref, k_ref, v_ref, o_ref, acc_ref, c_ref = refs
        else:
            q_ref, k_ref, v_ref, x_ref, o_ref, g_ref, acc_ref, c_ref = refs[:8]
            start, wait = _xchg_ops(x_ref, g_ref, *refs[8:], False)
            first, last = _first_last(4, nq)
            pl.when(first)(start)
        i = pl.program_id(1)
        m_a = lax.broadcasted_iota(jnp.int32, (1, BLK), 1) < 64
        q = q_ref[...]
        u_gt = _tri_ones(True)
        acc_ref[...] = jnp.zeros_like(acc_ref)
        c_ref[...] = jnp.zeros_like(c_ref)

        def chain(x, j, masked, both):
            off = _tile_off(j)
            if x == 0:
                both["z"] = _dot(q, _pair_rhs(k_ref[pl.ds(off, BLK), :], m_a), NT)
            mask = _sb_mask(i, j, qb) if masked else None
            o = {}
            yield from _sb_stages(both["z"][:, x * BLK:(x + 1) * BLK], mask, c_ref, x, u_gt, o)
            yield
            acc_ref[x] += _dot(o["w"], v_ref[pl.ds(off, BLK), :])

        def tiles(js, masked):
            shared = [{} for _ in js]
            _staggered([chain(x, j, masked, shared[t]) for t, j in enumerate(js) for x in range(2)])

        _sb_sweep(i, r, tiles, 4)
        o_ref[...] = jnp.where(m_a, acc_ref[0], acc_ref[1])
        if bg is not None:
            pl.when(last)(wait)

    in_specs = [pl.BlockSpec((qb, 128), lambda p, i: (i, p)),
                pl.BlockSpec((L, 128), lambda p, i: (0, 4 + p)),
                pl.BlockSpec((L, 128), lambda p, i: (0, 8 + p))]
    o_spec = pl.BlockSpec((qb, 128), lambda p, i: (i, p))
    o_shape = jax.ShapeDtypeStruct((L, 512), f32)
    scratch = [pltpu.VMEM((2, qb, BLK), f32), pltpu.VMEM((2, qb, BLK), f32)]
    if bg is None:
        return pl.pallas_call(body, name="sb_fwd", grid=(4, nq), in_specs=in_specs, out_specs=o_spec,
                              out_shape=o_shape, scratch_shapes=scratch,
                              compiler_params=_cp(("parallel", "arbitrary")))(sb, sb, sb)
    return pl.pallas_call(body, name="sb_fwd_gather", grid=(4, nq), in_specs=in_specs + [_HBM],
                          out_specs=(o_spec, _HBM), out_shape=(o_shape, _xchg_shape(bg, False)),
                          scratch_shapes=scratch + list(_XCHG_SEMS),
                          compiler_params=_cp(("arbitrary", "arbitrary")))(sb, sb, sb, bg)


def _sb_bwd(dmixed, sb, out_a, bg=None):
    L = sb.shape[0]
    qb = _qrows(L)
    nq, r = L // qb, qb // BLK

    def body(*refs):
        if bg is None:
            do_ref, o_ref, q_ref, k_ref, v_ref, dq_ref, dk_ref, dv_ref, dqa_ref, c_ref, cg_ref, ds_ref = refs
        else:
            do_ref, o_ref, q_ref, k_ref, v_ref, x_ref, dq_ref, dk_ref, dv_ref, g_ref = refs[:10]
            dqa_ref, c_ref, cg_ref, ds_ref = refs[10:14]
            start, wait = _xchg_ops(x_ref, g_ref, *refs[14:], True)
            first, last = _first_last(4, nq)
            pl.when(first)(start)
        i = pl.program_id(1)

        @pl.when(i == 0)
        def _():
            dk_ref[...] = jnp.zeros_like(dk_ref)
            dv_ref[...] = jnp.zeros_like(dv_ref)

        m_a = lax.broadcasted_iota(jnp.int32, (1, BLK), 1) < 64
        q = q_ref[...]
        zq = jnp.zeros_like(q)
        qs = (jnp.where(m_a, q, zq), jnp.where(m_a, zq, q))
        do = do_ref[...]
        zd = jnp.zeros_like(do)
        dos = (jnp.where(m_a, do, zd).astype(_MXU), jnp.where(m_a, zd, do).astype(_MXU))
        do_b = do.astype(_MXU)
        prod = do_b.astype(f32) * o_ref[...]
        ds_ref[0] = jnp.broadcast_to(jnp.sum(jnp.where(m_a, prod, 0.0), 1, keepdims=True), (qb, BLK))
        ds_ref[1] = jnp.broadcast_to(jnp.sum(jnp.where(m_a, 0.0, prod), 1, keepdims=True), (qb, BLK))
        u_gt = _tri_ones(True)
        u_ge = _tri_ones(False)
        dqa_ref[...] = jnp.zeros_like(dqa_ref)
        c_ref[...] = jnp.zeros_like(c_ref)
        cg_ref[...] = jnp.zeros_like(cg_ref)

        def chain(x, j, masked, both):
            off = _tile_off(j)
            k = k_ref[pl.ds(off, BLK), :]
            if x == 0:
                both["kk"] = _pair_rhs(k, m_a)
                both["z"] = _dot(q, both["kk"], NT)
                both["dw"] = _dot(do_b, _pair_rhs(v_ref[pl.ds(off, BLK), :], m_a), NT)
            mask = _sb_mask(i, j, qb) if masked else None
            o = {}
            yield from _sb_stages(both["z"][:, x * BLK:(x + 1) * BLK], mask, c_ref, x, u_gt, o)
            wb = o["w"]
            gr = wb.astype(f32) * both["dw"][:, x * BLK:(x + 1) * BLK]
            hi = gr.astype(_MXU)
            lo = (gr - hi.astype(f32)).astype(_MXU)
            yield
            eg = (jnp.dot(hi, u_ge, preferred_element_type=f32)
                  + jnp.dot(lo, u_ge, preferred_element_type=f32))
            yield
            cg = cg_ref[x]
            suffix = eg[:, 0:BLK] + cg
            cg_ref[x] = cg + eg[:, BLK:2 * BLK]
            dz = gr - jnp.exp(o["lb"]) * (gr + ds_ref[x] - suffix)
            if masked:
                dz = jnp.where(mask, dz, 0.0)
            dz = dz.astype(_MXU)
            yield
            if x == 0:
                both["dz"] = dz
            else:
                dqa_ref[0] += _dot(jnp.concatenate([both["dz"], dz], axis=1), both["kk"])
            dk_ref[pl.ds(off, BLK), :] += _dot(dz, qs[x], TN)
            dv_ref[pl.ds(off, BLK), :] += _dot(wb, dos[x], TN)

        def tiles(js, masked):
            shared = [{} for _ in js]
            _staggered([chain(x, j, masked, shared[t]) for t, j in enumerate(js) for x in range(2)])

        _sb_sweep(i, r, tiles, 4)
        dq_ref[...] = dqa_ref[0] * SB_SCALE
        if bg is not None:
            pl.when(last)(wait)

    blk = pl.BlockSpec((qb, 128), lambda p, i: (i, p))
    scr = pltpu.VMEM((2, qb, BLK), f32)
    in_specs = [blk, blk, blk, pl.BlockSpec((L, 128), lambda p, i: (0, 4 + p)),
                pl.BlockSpec((L, 128), lambda p, i: (0, 8 + p))]
    out_specs = (blk, pl.BlockSpec((L, 128), lambda p, i: (0, p)), pl.BlockSpec((L, 128), lambda p, i: (0, p)))
    out_shape = tuple(jax.ShapeDtypeStruct((L, 512), f32) for _ in range(3))
    if bg is None:
        return pl.pallas_call(body, name="sb_bwd", grid=(4, nq), in_specs=in_specs, out_specs=out_specs,
                              out_shape=out_shape, scratch_shapes=[scr, scr, scr, scr],
                              compiler_params=_cp(("parallel", "arbitrary")))(dmixed, out_a, sb, sb, sb)
    return pl.pallas_call(body, name="sb_bwd_scatter", grid=(4, nq), in_specs=in_specs + [_HBM],
                          out_specs=out_specs + (_HBM,), out_shape=out_shape + (_xchg_shape(bg, True),),
                          scratch_shapes=[scr, scr, scr, scr] + list(_XCHG_SEMS),
                          compiler_params=_cp(("arbitrary", "arbitrary")))(dmixed, out_a, sb, sb, sb, bg)


def _mla_mask(i, j, qb):
    row = i * qb + lax.broadcasted_iota(jnp.int32, (qb, BLK), 0)
    col = j * BLK + lax.broadcasted_iota(jnp.int32, (qb, BLK), 1)
    return (col <= row) & ((col >= N_PAD) | (col == row))


def _pair_mask2():
    return (lax.broadcasted_iota(jnp.int32, (1, 256), 1) % 128) < 64


def _mla_q2(qn_ref, qr_ref, mc_ref, ms_ref):
    qr = qr_ref[...]
    qr = qr * mc_ref[...] + _rot(qr, 16) * ms_ref[...]
    q2 = jnp.concatenate([qn_ref[...], qr], axis=1)
    m2 = _pair_mask2()
    z2 = jnp.zeros_like(q2)
    return (jnp.where(m2, q2, z2).astype(_MXU), jnp.where(m2, z2, q2).astype(_MXU)), q2.astype(_MXU)


def _mla_fwd(q, kv, kr2, mc, ms):
    L = q.shape[0]
    qb = _qrows(L)
    nq, r = L // qb, qb // BLK

    def body(qn_ref, qr_ref, mc_ref, ms_ref, kn_ref, v_ref, kr_ref, o_ref, lse_ref, acc_ref, m_ref):
        i = pl.program_id(1)
        m_a = lax.broadcasted_iota(jnp.int32, (1, BLK), 1) < 64
        _, q2 = _mla_q2(qn_ref, qr_ref, mc_ref, ms_ref)
        m2 = _pair_mask2()
        acc_ref[...] = jnp.zeros_like(acc_ref)
        m_ref[...] = jnp.full(m_ref.shape, -1e30, f32)
        ones = jnp.ones((BLK, BLK), _MXU)

        def chain(x, j, masked, both):
            off = _tile_off(j)
            if x == 0:
                k2 = jnp.concatenate([kn_ref[pl.ds(off, BLK), :], kr_ref[pl.ds(off, BLK), :]], axis=1)
                both["s"] = _dot(q2, _pair_rhs(k2, m2), NT)
            yield
            s = both["s"][:, x * BLK:(x + 1) * BLK] * MLA_SCALE
            if masked:
                mask = _mla_mask(i, j, qb)
                s = jnp.where(mask, s, -1e30)
            m_old = m_ref[x]
            m_new = jnp.maximum(m_old, jnp.max(s, 1, keepdims=True))
            a = jnp.exp(m_old - m_new)
            p = jnp.exp(s - m_new)
            if masked:
                p = jnp.where(mask, p, 0.0)
            m_ref[x] = m_new
            p = p.astype(_MXU)
            yield
            v1 = jnp.concatenate([v_ref[pl.ds(off, BLK), :], ones], axis=1)
            acc_ref[x] = jnp.concatenate([a, a], axis=1) * acc_ref[x] + _dot(p, v1)

        def tiles(js, masked):
            shared = [{} for _ in js]
            _staggered([chain(x, j, masked, shared[t]) for t, j in enumerate(js) for x in range(2)])

        _sb_sweep(i, r, tiles, 4)
        o_ref[...] = jnp.where(m_a, acc_ref[0, :, 0:BLK] / acc_ref[0, :, BLK:2 * BLK],
                               acc_ref[1, :, 0:BLK] / acc_ref[1, :, BLK:2 * BLK])
        for x in range(2):
            lse_ref[0, x] = m_ref[x] + jnp.log(acc_ref[x, :, BLK:2 * BLK])

    blk = lambda cb: pl.BlockSpec((qb, 128), lambda p, i: (i, cb + p))
    full = lambda cb: pl.BlockSpec((L, 128), lambda p, i: (0, cb + p))
    tab = pl.BlockSpec((qb, 128), lambda p, i: (i, 0))
    return pl.pallas_call(
        body, name="mla_fwd", grid=(4, nq),
        in_specs=[blk(0), blk(4), tab, tab, full(0), full(4), pl.BlockSpec((L, 128), lambda p, i: (0, 0))],
        out_specs=(blk(0), pl.BlockSpec((1, 2, qb, 128), lambda p, i: (p, 0, i, 0))),
        out_shape=(jax.ShapeDtypeStruct((L, 512), f32), jax.ShapeDtypeStruct((4, 2, L, 128), f32)),
        scratch_shapes=[pltpu.VMEM((2, qb, 2 * BLK), f32), pltpu.VMEM((2, qb, BLK), f32)],
        compiler_params=_cp(("parallel", "arbitrary")))(q, q, mc, ms, kv, kv, kr2)


def _mla_bwd(dmixed, q, kv, kr2, out_b, lse, mc, ms):
    L = q.shape[0]
    qb = _qrows(L)
    nq, r = L // qb, qb // BLK

    def body(do_ref, o_ref, lse_ref, qn_ref, qr_ref, mc_ref, ms_ref, kn_ref, v_ref, kr_ref,
             dqn_ref, dqr_ref, dkn_ref, dv_ref, dkr_ref, dqa_ref, ds_ref):
        i = pl.program_id(1)

        @pl.when(i == 0)
        def _():
            dkn_ref[...] = jnp.zeros_like(dkn_ref)
            dv_ref[...] = jnp.zeros_like(dv_ref)
            dkr_ref[...] = jnp.zeros_like(dkr_ref)

        m_a = lax.broadcasted_iota(jnp.int32, (1, BLK), 1) < 64
        qs, q2 = _mla_q2(qn_ref, qr_ref, mc_ref, ms_ref)
        m2 = _pair_mask2()
        do = do_ref[...]
        zd = jnp.zeros_like(do)
        dos = (jnp.where(m_a, do, zd).astype(_MXU), jnp.where(m_a, zd, do).astype(_MXU))
        do_b = do.astype(_MXU)
        prod = do * o_ref[...]
        ds_ref[0] = jnp.broadcast_to(jnp.sum(jnp.where(m_a, prod, 0.0), 1, keepdims=True), (qb, BLK))
        ds_ref[1] = jnp.broadcast_to(jnp.sum(jnp.where(m_a, 0.0, prod), 1, keepdims=True), (qb, BLK))
        dqa_ref[...] = jnp.zeros_like(dqa_ref)

        def chain(x, j, masked, both):
            off = _tile_off(j)
            if x == 0:
                k2 = jnp.concatenate([kn_ref[pl.ds(off, BLK), :], kr_ref[pl.ds(off, BLK), :]], axis=1)
                both["kk"] = _pair_rhs(k2, m2)
                both["s"] = _dot(q2, both["kk"], NT)
                both["dp"] = _dot(do_b, _pair_rhs(v_ref[pl.ds(off, BLK), :], m_a), NT)
            yield
            s = both["s"][:, x * BLK:(x + 1) * BLK] * MLA_SCALE
            dp = both["dp"][:, x * BLK:(x + 1) * BLK]
            if masked:
                mask = _mla_mask(i, j, qb)
                p = jnp.where(mask, jnp.exp(jnp.where(mask, s, 0.0) - lse_ref[0, x]), 0.0)
            else:
                p = jnp.exp(s - lse_ref[0, x])
            pb = p.astype(_MXU)
            ds = (p * (dp - ds_ref[x]) * MLA_SCALE).astype(_MXU)
            yield
            if x == 0:
                both["ds"] = ds
            else:
                dqa_ref[0] += _dot(jnp.concatenate([both["ds"], ds], axis=1), both["kk"])
            dk_t = _dot(ds, qs[x], TN)
            dkn_ref[pl.ds(off, BLK), :] += dk_t[:, 0:128]
            dkr_ref[0, pl.ds(off, BLK), :] += dk_t[:, 128:256]
            dv_ref[pl.ds(off, BLK), :] += _dot(pb, dos[x], TN)

        def tiles(js, masked):
            shared = [{} for _ in js]
            _staggered([chain(x, j, masked, shared[t]) for t, j in enumerate(js) for x in range(2)])

        _sb_sweep(i, r, tiles, 4)
        dq2 = dqa_ref[0]
        dqn_ref[...] = dq2[:, 0:128]
        dy = dq2[:, 128:256]
        dqr_ref[...] = dy * mc_ref[...] - _rot(dy * ms_ref[...], 16)

    blk = lambda cb: pl.BlockSpec((qb, 128), lambda p, i: (i, cb + p))
    full = lambda cb: pl.BlockSpec((L, 128), lambda p, i: (0, cb + p))
    tab = pl.BlockSpec((qb, 128), lambda p, i: (i, 0))
    o512 = jax.ShapeDtypeStruct((L, 512), f32)
    return pl.pallas_call(
        body, name="mla_bwd", grid=(4, nq),
        in_specs=[blk(4), blk(0), pl.BlockSpec((1, 2, qb, 128), lambda p, i: (p, 0, i, 0)), blk(0), blk(4), tab, tab,
                  full(0), full(4), pl.BlockSpec((L, 128), lambda p, i: (0, 0))],
        out_specs=(blk(0), blk(0), full(0), full(0), pl.BlockSpec((1, L, 128), lambda p, i: (p, 0, 0))),
        out_shape=(o512, o512, o512, o512, jax.ShapeDtypeStruct((4, L, 128), f32)),
        scratch_shapes=[pltpu.VMEM((2, qb, 2 * BLK), f32), pltpu.VMEM((2, qb, BLK), f32)],
        compiler_params=_cp(("parallel", "arbitrary")))(dmixed, out_b, lse, q, q, mc, ms, kv, kv, kr2)


def _ret_tables():
    log_g = jnp.log(jnp.array(RET_GAMMA, f32))
    idx = jnp.arange(BLK, dtype=f32)
    diff = idx[:, None] - idx[None, :]
    d_in = jnp.where(diff[None] >= 0, jnp.exp(jnp.maximum(diff, 0.0)[None] * log_g[:, None, None]), 0.0)
    q_dec = jnp.exp((idx[None, :] + 1.0) * log_g[:, None])
    k_dec = jnp.exp((BLK - 1.0 - idx[None, :]) * log_g[:, None])
    c_dec = jnp.exp(BLK * log_g)
    bc = lambda a: jnp.broadcast_to(a[:, :, None], (4, BLK, BLK))
    return d_in, bc(q_dec), bc(k_dec), jnp.broadcast_to(c_dec[:, None, None], (4, 8, BLK))


def _head_mask(x):
    lane = lax.broadcasted_iota(jnp.int32, (1, BLK), 1)
    return (lane < 64) if x == 0 else (lane >= 64)


def _ret_fwd(rqk, rv, proj, rtabs):
    L = rqk.shape[0]
    n = L // BLK
    d_in, q_dec, k_dec, c_dec = rtabs

    def body(q_ref, k_ref, v_ref, g_ref, din_ref, qd_ref, kd_ref, cd_ref, y_ref, o_ref, st_ref, s_scr):
        @pl.when(pl.program_id(1) == 0)
        def _():
            s_scr[...] = jnp.zeros_like(s_scr)

        q = q_ref[...]
        k = k_ref[...]
        zq = jnp.zeros_like(q)
        for x in range(2):
            hm = _head_mask(x)
            sl = slice(x * 128, (x + 1) * 128)
            qm = jnp.where(hm, q, zq)
            km = jnp.where(hm, k, zq)
            v = v_ref[:, sl]
            s_in = s_scr[x]
            st_ref[0, 0, x] = s_in
            inner = _dot(qm, km, NT) * din_ref[x]
            y = _dot(inner, v) + _dot(qm, s_in) * qd_ref[x]
            s_scr[x] = s_in * cd_ref[x, 0:1, :] + _dot(km.astype(f32) * kd_ref[x], v, TN)
            y_ref[:, sl] = y
            mu = jnp.mean(y, -1, keepdims=True)
            yc = y - mu
            yn = yc * lax.rsqrt(jnp.mean(jnp.square(yc), -1, keepdims=True) + LN_EPS)
            g = g_ref[:, sl]
            o_ref[:, sl] = g * jax.nn.sigmoid(g) * yn

    tab = pl.BlockSpec((2, BLK, BLK), lambda p, i: (p, 0, 0))
    return pl.pallas_call(
        body, name="ret_fwd", grid=(2, n),
        in_specs=[pl.BlockSpec((BLK, 128), lambda p, i: (i, p)), pl.BlockSpec((BLK, 128), lambda p, i: (i, 2 + p)),
                  pl.BlockSpec((BLK, 256), lambda p, i: (i, p)),
                  pl.BlockSpec((BLK, 256), lambda p, i: (i, C_RG // 256 + p)),
                  tab, tab, tab, pl.BlockSpec((2, 8, BLK), lambda p, i: (p, 0, 0))],
        out_specs=(pl.BlockSpec((BLK, 256), lambda p, i: (i, p)), pl.BlockSpec((BLK, 256), lambda p, i: (i, p)),
                   pl.BlockSpec((1, 1, 2, BLK, BLK), lambda p, i: (p, i, 0, 0, 0))),
        out_shape=(jax.ShapeDtypeStruct((L, 512), f32), jax.ShapeDtypeStruct((L, 512), f32),
                   jax.ShapeDtypeStruct((2, n, 2, BLK, BLK), f32)),
        scratch_shapes=[pltpu.VMEM((2, BLK, BLK), f32)],
        compiler_params=_cp(("parallel", "arbitrary")))(rqk, rqk, rv, proj, d_in, q_dec, k_dec, c_dec)


def _ret_bwd(dmixed, rqk, rv, proj, y, states, rtabs):
    L = rqk.shape[0]
    n = L // BLK
    d_in, q_dec, k_dec, c_dec = rtabs

    def body(do_ref, q_ref, k_ref, v_ref, g_ref, y_ref, st_ref, din_ref, qd_ref, kd_ref, cd_ref,
             dq_ref, dk_ref, dv_ref, dg_ref, ds_scr):
        @pl.when(pl.program_id(1) == 0)
        def _():
            ds_scr[...] = jnp.zeros_like(ds_scr)

        q = q_ref[...]
        k = k_ref[...]
        zq = jnp.zeros_like(q)
        dq_acc = jnp.zeros((BLK, BLK), f32)
        dk_acc = jnp.zeros((BLK, BLK), f32)
        for x in range(2):
            hm = _head_mask(x)
            sl = slice(x * 128, (x + 1) * 128)
            qm = jnp.where(hm, q, zq)
            km = jnp.where(hm, k, zq)
            v = v_ref[:, sl]
            yv = y_ref[:, sl]
            g = g_ref[:, sl]
            do = do_ref[:, sl]
            mu = jnp.mean(yv, -1, keepdims=True)
            yc = yv - mu
            rstd = lax.rsqrt(jnp.mean(jnp.square(yc), -1, keepdims=True) + LN_EPS)
            yn = yc * rstd
            sg = jax.nn.sigmoid(g)
            dg_ref[:, sl] = do * yn * sg * (1.0 + g * (1.0 - sg))
            dyn = do * g * sg
            dy = rstd * (dyn - jnp.mean(dyn, -1, keepdims=True) - yn * jnp.mean(dyn * yn, -1, keepdims=True))
            s_in = st_ref[0, 0, x]
            ds_out = ds_scr[x]
            kd = km.astype(f32) * kd_ref[x]
            a = _dot(qm, km, NT) * din_ref[x]
            da = _dot(dy, v, NT) * din_ref[x]
            dyq = dy * qd_ref[x]
            dq_acc += _dot(da, km) + _dot(dyq, s_in, NT)
            dk_acc += _dot(da, qm, TN) + _dot(v, ds_out, NT) * kd_ref[x]
            dv_ref[:, sl] = _dot(a, dy, TN) + _dot(kd, ds_out)
            ds_scr[x] = ds_out * cd_ref[x, 0:1, :] + _dot(qm, dyq, TN)
        dq_ref[...] = dq_acc
        dk_ref[...] = dk_acc

    rev = lambda w, cb: pl.BlockSpec((BLK, w), lambda p, i: (n - 1 - i, cb + p))
    tab = pl.BlockSpec((2, BLK, BLK), lambda p, i: (p, 0, 0))
    return pl.pallas_call(
        body, name="ret_bwd", grid=(2, n),
        in_specs=[rev(256, 4), rev(128, 0), rev(128, 2), rev(256, 0), rev(256, C_RG // 256), rev(256, 0),
                  pl.BlockSpec((1, 1, 2, BLK, BLK), lambda p, i: (p, n - 1 - i, 0, 0, 0)),
                  tab, tab, tab, pl.BlockSpec((2, 8, BLK), lambda p, i: (p, 0, 0))],
        out_specs=(rev(128, 0), rev(128, 0), rev(256, 0), rev(256, 0)),
        out_shape=(jax.ShapeDtypeStruct((L, 256), f32), jax.ShapeDtypeStruct((L, 256), f32),
                   jax.ShapeDtypeStruct((L, 512), f32), jax.ShapeDtypeStruct((L, 512), f32)),
        scratch_shapes=[pltpu.VMEM((2, BLK, BLK), f32)],
        compiler_params=_cp(("parallel", "arbitrary")))(dmixed, rqk, rqk, rv, proj, y, states, d_in, q_dec, k_dec, c_dec)


def _loss_head(h, target):
    L = h.shape[0]
    n = L // BLK

    def body(h_ref, t_ref, dy_ref, l_ref):
        i = pl.program_id(0)

        @pl.when(i == 0)
        def _():
            dy_ref[...] = jnp.zeros_like(dy_ref)
            l_ref[...] = jnp.zeros_like(l_ref)

        @pl.when(i > 0)
        def _():
            err = h_ref[...] - t_ref[...]
            dy_ref[...] = err * (1.0 / D)
            sq = jnp.sum(jnp.sum(jnp.square(err), 1, keepdims=True), 0, keepdims=True)
            l_ref[...] += (0.5 / D) * sq

    return pl.pallas_call(
        body, name="loss_head", grid=(n,),
        in_specs=[pl.BlockSpec((BLK, D), lambda i: (i, 0)),
                  pl.BlockSpec((BLK, D), lambda i: (jnp.maximum(i - 1, 0), 0))],
        out_specs=(pl.BlockSpec((BLK, D), lambda i: (i, 0)), pl.BlockSpec((8, 128), lambda i: (0, 0))),
        out_shape=(jax.ShapeDtypeStruct((L, D), f32), jax.ShapeDtypeStruct((8, 128), f32)),
        compiler_params=_cp(("arbitrary",)))(h, target)


def _adam_math(w, g, m, v):
    m = ADAM_B1 * m + (1.0 - ADAM_B1) * g
    v = ADAM_B2 * v + (1.0 - ADAM_B2) * jnp.square(g)
    m_hat = m / (1.0 - ADAM_B1 ** ADAM_STEP)
    v_hat = v / (1.0 - ADAM_B2 ** ADAM_STEP)
    delta = -ADAM_LR * (m_hat / (jnp.sqrt(v_hat) + ADAM_EPS) + ADAM_WD * w)
    return delta, m, v


def _adamw(parts, w, m, v, name):
    R, C = w.shape
    tr = _pick(R, (240, 192, 144, 96, 64, 48, 32, 16, 8))
    row = pl.BlockSpec((tr, C), lambda i: (i, 0))

    def body(p_ref, w_ref, m_ref, v_ref, g_ref, d_ref, nm_ref, nv_ref):
        g = p_ref[0].astype(f32)
        for k in range(1, N_DEV):
            g = g + p_ref[k].astype(f32)
        d, nm, nv = _adam_math(w_ref[...], g, m_ref[...], v_ref[...])
        g_ref[...] = g
        d_ref[...] = d
        nm_ref[...] = nm
        nv_ref[...] = nv

    o = jax.ShapeDtypeStruct((R, C), f32)
    return pl.pallas_call(
        body, name=name, grid=(R // tr,),
        in_specs=[pl.BlockSpec((N_DEV, tr, C), lambda i: (0, i, 0)), row, row, row],
        out_specs=(row, row, row, row), out_shape=(o, o, o, o),
        compiler_params=_cp(("parallel",)))(parts, w, m, v)


_XCHG_SEMS = [pltpu.SemaphoreType.DMA((N_DEV - 1,)), pltpu.SemaphoreType.DMA((N_DEV - 1,)), pltpu.SemaphoreType.DMA]
_HBM = pl.BlockSpec(memory_space=pltpu.HBM)


def _xchg_shape(x, all_to_all):
    return jax.ShapeDtypeStruct((N_DEV,) + tuple(x.shape[1:] if all_to_all else x.shape), x.dtype)


def _xchg_ops(x_ref, o_ref, send_sems, recv_sems, local_sem, all_to_all):
    mx, my, mc = lax.axis_index("x"), lax.axis_index("y"), lax.axis_index("c")
    me = 4 * mx + 2 * my + mc

    def peer(k):
        px = (1 - mx) if k & 4 else mx
        py = (1 - my) if k & 2 else my
        pc = (1 - mc) if k & 1 else mc
        return (px, py, pc), 4 * px + 2 * py + pc

    def copy(k):
        dev, idx = peer(k)
        src = x_ref.at[idx] if all_to_all else x_ref
        return pltpu.make_async_remote_copy(
            src_ref=src, dst_ref=o_ref.at[me], send_sem=send_sems.at[k - 1], recv_sem=recv_sems.at[k - 1],
            device_id=dev, device_id_type=pl.DeviceIdType.MESH)

    def start():
        pltpu.make_async_copy(x_ref.at[me] if all_to_all else x_ref, o_ref.at[me], local_sem).start()
        for k in range(1, N_DEV):
            copy(k).start()

    def wait():
        for k in range(1, N_DEV):
            dev, idx = peer(k)
            pltpu.make_async_remote_copy(
                src_ref=o_ref.at[idx], dst_ref=o_ref.at[idx], send_sem=send_sems.at[k - 1],
                recv_sem=recv_sems.at[k - 1], device_id=dev, device_id_type=pl.DeviceIdType.MESH).wait_recv()
        for k in range(1, N_DEV):
            copy(k).wait_send()
        pltpu.make_async_copy(x_ref.at[me] if all_to_all else x_ref, o_ref.at[me], local_sem).wait()

    return start, wait


def _exchange(x, all_to_all, name):
    def body(x_ref, o_ref, send_sems, recv_sems, local_sem):
        start, wait = _xchg_ops(x_ref, o_ref, send_sems, recv_sems, local_sem, all_to_all)
        start()
        wait()

    return pl.pallas_call(body, name=name, in_specs=[_HBM], out_specs=_HBM, out_shape=_xchg_shape(x, all_to_all),
                          scratch_shapes=list(_XCHG_SEMS))(x)


WC = 512
LAYER_ROWS = tuple(n // WC for n in (1024 * 468, 384 * 96, 256 * 128, 192 * 1024, 1024 * 512, 512 * 1024))


def _pack_layer(ws, l):
    return jnp.concatenate([w[l].reshape(-1, WC) for w in ws], axis=0)


def _unpack_layer(gathered):
    offs = np.cumsum((0,) + LAYER_ROWS)
    part = lambda k: gathered[:, offs[k]:offs[k + 1]]
    w_in = part(0).reshape(8, 1024, 468).transpose(1, 0, 2).reshape(1024, 3744)
    z32 = jnp.zeros((1024, 32), w_in.dtype)
    kr = w_in[:, 2176:2208]
    w_in = jnp.concatenate([w_in[:, 0:1536], w_in[:, 1920:2176], w_in[:, 2208:3744], w_in[:, 1536:1920],
                            kr, z32, kr, z32], axis=1)
    w_uq = part(1).reshape(8, 384, 96).transpose(1, 0, 2)
    rope = jnp.concatenate([w_uq[..., 64:96], jnp.zeros((384, 8, 32), w_uq.dtype)], axis=-1)
    w_uq = jnp.concatenate([w_uq[..., 0:64].reshape(384, 512), rope.reshape(384, 512)], axis=1)
    w_ukv = part(2).reshape(8, 256, 128).transpose(1, 0, 2)
    w_ukv = jnp.concatenate([w_ukv[..., 0:64].reshape(256, 512), w_ukv[..., 64:128].reshape(256, 512)], axis=1)
    w_out = part(3).reshape(1536, 1024)
    w_ff1 = part(4).reshape(8, 1024, 512).transpose(1, 0, 2).reshape(1024, 4096)
    w_ff2 = part(5).reshape(4096, 1024)
    return w_in, w_uq, w_ukv, w_out, w_ff1, w_ff2


def _pack_layer_grads(g_in, g_uq, g_ukv, g_out, g_ff1, g_ff2):
    kr = g_in[:, C_KR:C_KR + 32] + g_in[:, C_KR + 64:C_KR + 96]
    g_in = jnp.concatenate([g_in[:, 0:1536], g_in[:, C_CQ:C_CQ + 384], g_in[:, C_CKV:C_CKV + 256], kr,
                            g_in[:, C_RQ:C_CQ]], axis=1)
    g_in = g_in.reshape(1024, 8, 468).transpose(1, 0, 2)
    g_uq = jnp.concatenate([g_uq[:, 0:512].reshape(384, 8, 64), g_uq[:, 512:1024].reshape(384, 8, 64)[..., 0:32]],
                           axis=-1).transpose(1, 0, 2)
    g_ukv = jnp.concatenate([g_ukv[:, 0:512].reshape(256, 8, 64), g_ukv[:, 512:1024].reshape(256, 8, 64)],
                            axis=-1).transpose(1, 0, 2)
    g_ff1 = g_ff1.reshape(1024, 8, 512).transpose(1, 0, 2)
    return jnp.concatenate([g.reshape(8, -1, WC) for g in (g_in, g_uq, g_ukv, g_out, g_ff1, g_ff2)], axis=1)


def _rope_tables(L):
    pos = (jnp.arange(L) - N_PAD).astype(f32)

    def cs(half):
        inv = ROPE_THETA ** (-jnp.arange(half, dtype=f32) / half)
        ang = pos[:, None] * inv[None, :]
        return jnp.cos(ang), jnp.sin(ang)

    c, s = cs(32)
    rc, rs = jnp.tile(c, (1, 8)), jnp.tile(s, (1, 8))
    c, s = cs(16)
    z = jnp.zeros((L, 32), f32)
    mc, ms = jnp.concatenate([c, c, z, c, c, z], 1), jnp.concatenate([s, s, z, s, s, z], 1)
    return rc, rs, mc, ms


def _layer_fwd(h, wl, gq, gkv, g1, b1, g2, b2, tabs, rtabs, next_shard):
    w_in, w_uq, w_ukv, w_out, w_ff1, w_ff2 = wl
    proj = _mm(h, w_in, "nn", "mm_in")
    sb, cqn, ckvn, rqk, rv, kr2 = _prep_fwd(proj, gq, gkv, tabs)
    q = _mm(cqn, w_uq, "nn", "mm_uq")
    kv = _mm(ckvn, w_ukv, "nn", "mm_ukv", out_dtype=bf16)
    if next_shard is None:
        out_a, gathered = _sb_fwd(sb), None
    else:
        out_a, gathered = _sb_fwd(sb, next_shard)
    out_b, lse = _mla_fwd(q, kv, kr2, tabs[2], tabs[3])
    y, out_c, states = _ret_fwd(rqk, rv, proj, rtabs)
    mixed = jnp.concatenate([out_a, out_b, out_c], axis=1).astype(bf16)
    mix = _mm(mixed, w_out, "nn", "mm_out")
    h1 = _ln_fwd(h, mix, g1, b1, DN_ALPHA, "ln_fwd")
    u, a = _mm(h1, w_ff1, "nn", "mm_ff1", epi="relu2")
    ff = _mm(a, w_ff2, "nn", "mm_ff2")
    h2 = _ln_fwd(h1, ff, g2, b2, DN_ALPHA, "ln_fwd")
    saved = (h, proj, sb, cqn, ckvn, rqk, rv, kr2, q, kv, out_a, out_b, lse, y, states, mixed, mix, h1, u, a, ff)
    return h2, saved, gathered


def _layer_bwd(dh2, saved, wl, gq, gkv, g1, g2, tabs, rtabs, grads_above):
    w_in, w_uq, w_ukv, w_out, w_ff1, w_ff2 = wl
    h, proj, sb, cqn, ckvn, rqk, rv, kr2, q, kv, out_a, out_b, lse, y, states, mixed, mix, h1, u, a, ff = saved
    dz2, dg2, db2 = _ln_bwd(dh2, h1, ff, g2, DN_ALPHA, "ln_bwd")
    du = _mm(dz2, w_ff2, "nt", "mm_dff2", epi="mul_relu", extra=u, out_dtype=bf16)
    gw_ff2 = _mm(a, dz2, "tn", "mm_gff2")
    dh1 = _mm(du, w_ff1, "nt", "mm_dff1", epi="add", extra=dz2, alpha=DN_ALPHA)
    gw_ff1 = _mm(h1, du, "tn", "mm_gff1")
    dz1, dg1, db1 = _ln_bwd(dh1, h, mix, g1, DN_ALPHA, "ln_bwd")
    dmixed = _mm(dz1, w_out, "nt", "mm_dout")
    gw_out = _mm(mixed, dz1, "tn", "mm_gout")
    if grads_above is None:
        (dsq, dsk, dsv), parts_above = _sb_bwd(dmixed, sb, out_a), None
    else:
        dsq, dsk, dsv, parts_above = _sb_bwd(dmixed, sb, out_a, grads_above)
    dqn, dqr, dkn, dv, dkr_p = _mla_bwd(dmixed, q, kv, kr2, out_b, lse, tabs[2], tabs[3])
    dq = jnp.concatenate([dqn, dqr], axis=1).astype(bf16)
    dkv = jnp.concatenate([dkn, dv], axis=1).astype(bf16)
    dcqn = _mm(dq, w_uq, "nt", "mm_duq")
    gw_uq = _mm(cqn, dq, "tn", "mm_guq")
    dckvn = _mm(dkv, w_ukv, "nt", "mm_dukv")
    gw_ukv = _mm(ckvn, dkv, "tn", "mm_gukv")
    drq_r, drk_r, drv, drg = _ret_bwd(dmixed, rqk, rv, proj, y, states, rtabs)
    dkr_r = dkr_p[0] + dkr_p[1] + dkr_p[2] + dkr_p[3]
    dcq, dckv, drq, drk, dkr2, dgq, dgkv = _prep_bwd(proj, gq, gkv, tabs, dcqn, dckvn, drq_r, drk_r, dkr_r)
    dproj = jnp.concatenate([dsq, dsk, dsv, dckv, drq, drk, drv, drg, dcq, dkr2], axis=1).astype(bf16)
    dh = _mm(dproj, w_in, "nt", "mm_din", epi="add", extra=dz1, alpha=DN_ALPHA)
    gw_in = _mm(h, dproj, "tn", "mm_gin")
    return dh, (gw_in, gw_uq, gw_ukv, gw_out, gw_ff1, gw_ff2), (dgq, dgkv, dg1, db1, dg2, db2), parts_above


def kernel(x, meta_tokens, ln_emb_g, ln_emb_b, w_in, mla_q_norm, mla_kv_norm, w_uq, w_ukv, w_out, ln1_g, ln1_b, w_ff1, w_ff2, ln2_g, ln2_b, loss_target, m_meta_tokens, m_ln_emb_g, m_ln_emb_b, m_w_in, m_mla_q_norm, m_mla_kv_norm, m_w_uq, m_w_ukv, m_w_out, m_ln1_g, m_ln1_b, m_w_ff1, m_w_ff2, m_ln2_g, m_ln2_b, v_meta_tokens, v_ln_emb_g, v_ln_emb_b, v_w_in, v_mla_q_norm, v_mla_kv_norm, v_w_uq, v_w_ukv, v_w_out, v_ln1_g, v_ln1_b, v_w_ff1, v_w_ff2, v_ln2_g, v_ln2_b):
    depth = w_in.shape[0]
    S = x.shape[1]
    L = S + BLK
    me = 4 * lax.axis_index("x") + 2 * lax.axis_index("y") + lax.axis_index("c")
    big = (w_in, w_uq, w_ukv, w_out, w_ff1, w_ff2)
    big_m = (m_w_in, m_w_uq, m_w_ukv, m_w_out, m_w_ff1, m_w_ff2)
    big_v = (v_w_in, v_w_uq, v_w_ukv, v_w_out, v_w_ff1, v_w_ff2)
    small = (ln_emb_g, ln_emb_b, mla_q_norm, mla_kv_norm, ln1_g, ln1_b, ln2_g, ln2_b)
    small_m = (m_ln_emb_g, m_ln_emb_b, m_mla_q_norm, m_mla_kv_norm, m_ln1_g, m_ln1_b, m_ln2_g, m_ln2_b)
    small_v = (v_ln_emb_g, v_ln_emb_b, v_mla_q_norm, v_mla_kv_norm, v_ln1_g, v_ln1_b, v_ln2_g, v_ln2_b)

    shards = [_pack_layer(big, l) for l in range(depth)]
    gathered = _exchange(shards[0].astype(bf16), False, "gather_w0")
    meta_all = _exchange(meta_tokens, False, "gather_meta")
    meta_full = meta_all.transpose(1, 0, 2).reshape(N_META, D)

    tabs = _rope_tables(L)
    rtabs = _ret_tables()

    hcat = jnp.concatenate([jnp.zeros((N_PAD, D), f32), meta_full, x[0]], axis=0)
    h = _ln_fwd(hcat, None, ln_emb_g, ln_emb_b, 1.0, "ln_emb_fwd")
    saved, full = [], []
    for l in range(depth):
        full.append(_unpack_layer(gathered))
        nxt = shards[l + 1].astype(bf16) if l + 1 < depth else None
        h, sv, gathered = _layer_fwd(h, full[l], mla_q_norm[l], mla_kv_norm[l], ln1_g[l], ln1_b[l], ln2_g[l],
                                     ln2_b[l], tabs, rtabs, nxt)
        saved.append(sv)

    dh, loss_part = _loss_head(h, loss_target[0])
    gsmall, parts, pending = [None] * depth, [None] * depth, None
    for l in reversed(range(depth)):
        dh, gbig, gsmall[l], got = _layer_bwd(dh, saved[l], full[l], mla_q_norm[l], mla_kv_norm[l], ln1_g[l],
                                              ln2_g[l], tabs, rtabs, pending)
        if pending is not None:
            parts[l + 1] = got
        pending = _pack_layer_grads(*gbig).astype(bf16)
    parts[0] = _exchange(pending, True, "scatter_g0")
    dz0, dg_emb, db_emb = _ln_bwd(dh, hcat, None, ln_emb_g, 1.0, "ln_emb_bwd")
    grad_x = dz0[BLK:][None]
    dmeta = dz0[N_PAD:BLK]

    adam = [_adamw(parts[l], shards[l], _pack_layer(big_m, l), _pack_layer(big_v, l), "adamw_big")
            for l in range(depth)]

    st = lambda k: jnp.stack([gsmall[l][k] for l in range(depth)])
    g_small = (dg_emb, db_emb, st(0), st(1), st(2), st(3), st(4), st(5))
    n_small = sum(int(np.prod(a.shape)) for a in small)
    flat = jnp.concatenate([a.reshape(-1) for a in g_small] + [dmeta.reshape(-1), loss_part[0, 0:1]])
    rows = -(-(flat.shape[0]) // (8 * D)) * 8
    pad = rows * D - flat.shape[0]
    flat = jnp.concatenate([flat, jnp.zeros((pad,), f32)]).reshape(rows, D)
    parts_s = _exchange(flat, False, "gather_small")

    def pack_small(arrs, meta_shard):
        col = jnp.zeros((N_META, D), f32)
        col = lax.dynamic_update_slice(col, meta_shard, (0, me * 128))
        fl = jnp.concatenate([a.reshape(-1) for a in arrs] + [col.reshape(-1), jnp.zeros((1 + pad,), f32)])
        return fl.reshape(rows, D)

    g_s, d_s, m_s, v_s = _adamw(parts_s, pack_small(small, meta_tokens), pack_small(small_m, m_meta_tokens),
                                pack_small(small_v, v_meta_tokens), "adamw_small")
    loss = g_s.reshape(-1)[n_small + N_META * D]

    def unpack_big(which):
        outs, off = [], 0
        for w, r in zip(big, LAYER_ROWS):
            outs.append(jnp.stack([adam[l][which][off:off + r].reshape(w.shape[1:]) for l in range(depth)]))
            off += r
        return outs

    def unpack_small(flat_rows):
        fl = flat_rows.reshape(-1)
        outs, off = [], 0
        for a in small:
            n = int(np.prod(a.shape))
            outs.append(fl[off:off + n].reshape(a.shape))
            off += n
        meta = lax.dynamic_slice(fl[off:off + N_META * D].reshape(N_META, D), (0, me * 128), (N_META, 128))
        return meta, outs

    def assemble(which, small_rows_arr):
        b = unpack_big(which)
        meta, s = unpack_small(small_rows_arr)
        return [meta, s[0], s[1], b[0], s[2], s[3], b[1], b[2], b[3], s[4], s[5], b[4], b[5], s[6], s[7]]

    return (loss, grad_x, *assemble(0, g_s), *assemble(1, d_s), *assemble(2, m_s), *assemble(3, v_s))
```

```python
import functools
import math

import numpy as np
import jax
import jax.numpy as jnp
from jax import lax
from jax.experimental import pallas as pl
from jax.experimental.pallas import tpu as pltpu

f32 = jnp.float32
bf16 = jnp.bfloat16
_MXU = jnp.bfloat16

BLK = 128
N_META = 16
N_PAD = 112
D = 1024
N_DEV = 8
LN_EPS = 1e-5
DEPTH = 4
DN_ALPHA = (2 * DEPTH) ** 0.25
ROPE_THETA = 10000.0
MLA_SCALE = (64 + 32) ** -0.5
SB_SCALE = 0.125
RET_SCALE = 0.125
RET_GAMMA = tuple(1.0 - 2.0 ** (-5 - h) for h in range(4))

ADAM_LR, ADAM_B1, ADAM_B2, ADAM_EPS, ADAM_WD, ADAM_STEP = 0.001, 0.9, 0.999, 1e-08, 0.01, 10

C_SBQ, C_SBK, C_SBV, C_CKV, C_RQ, C_RK, C_RV, C_RG, C_CQ, C_KR, N_INP = (
    0, 512, 1024, 1536, 1792, 2048, 2304, 2816, 3328, 3712, 3840)

VMEM_LIMIT = 56 * 1024 * 1024


def _cp(sem):
    return pltpu.CompilerParams(dimension_semantics=sem, vmem_limit_bytes=VMEM_LIMIT)


def _pick(n, cands):
    for c in cands:
        if n % c == 0:
            return c
    return n


def _dot(a, b, dims=(((1,), (0,)), ((), ()))):
    return lax.dot_general(a.astype(_MXU), b.astype(_MXU), dims, preferred_element_type=f32)


NT = (((1,), (1,)), ((), ()))
TN = (((0,), (0,)), ((), ()))


def _dot3(x, u):
    hi = x.astype(_MXU)
    r1 = x - hi.astype(f32)
    mid = r1.astype(_MXU)
    lo = (r1 - mid.astype(f32)).astype(_MXU)
    return (jnp.dot(hi, u, preferred_element_type=f32) + jnp.dot(mid, u, preferred_element_type=f32)
            + jnp.dot(lo, u, preferred_element_type=f32))


def _rot(x, half):
    lane = lax.broadcasted_iota(jnp.int32, x.shape, 1)
    first = (lane % 64) < half
    return jnp.where(first, -pltpu.roll(x, 128 - half, 1), pltpu.roll(x, half, 1))


def _mm(a, b, mode, name, epi=None, extra=None, alpha=1.0, out_dtype=f32):
    if mode == "nn":
        (M, K), N = a.shape, b.shape[1]
    elif mode == "nt":
        (M, K), N = a.shape, b.shape[0]
    else:
        (K, M), N = a.shape, b.shape[1]
    tm = _pick(M, (1408, 1024, 768, 512, 384, 256, 128))
    tn = _pick(N, ((1920,) if mode == "tn" else ()) + (1024, 768, 512, 384, 256, 128))
    tk = _pick(K, (1024, 768, 512, 384, 256, 128))
    nk = K // tk
    if mode == "nn":
        a_spec = pl.BlockSpec((tm, tk), lambda i, j, k: (i, k))
        b_spec = pl.BlockSpec((tk, tn), lambda i, j, k: (k, j))
        dims = (((1,), (0,)), ((), ()))
    elif mode == "nt":
        a_spec = pl.BlockSpec((tm, tk), lambda i, j, k: (i, k))
        b_spec = pl.BlockSpec((tn, tk), lambda i, j, k: (j, k))
        dims = NT
    else:
        a_spec = pl.BlockSpec((tk, tm), lambda i, j, k: (k, i))
        b_spec = pl.BlockSpec((tk, tn), lambda i, j, k: (k, j))
        dims = TN
    o_spec = pl.BlockSpec((tm, tn), lambda i, j, k: (i, j))
    in_specs, args = [a_spec, b_spec], [a, b]
    if extra is not None:
        in_specs.append(o_spec)
        args.append(extra)
    if epi == "relu2":
        out_shape = (jax.ShapeDtypeStruct((M, N), f32), jax.ShapeDtypeStruct((M, N), bf16))
        out_specs = (o_spec, o_spec)
    else:
        out_shape = jax.ShapeDtypeStruct((M, N), out_dtype)
        out_specs = o_spec

    def body(*refs):
        a_ref, b_ref = refs[0], refs[1]
        acc = refs[-1]
        k = pl.program_id(2)

        @pl.when(k == 0)
        def _():
            acc[...] = jnp.zeros_like(acc)

        acc[...] += _dot(a_ref[...], b_ref[...], dims)

        @pl.when(k == nk - 1)
        def _():
            r = acc[...]
            if epi == "relu2":
                refs[2][...] = r
                refs[3][...] = jnp.square(jnp.maximum(r, 0.0)).astype(bf16)
            elif epi == "mul_relu":
                refs[3][...] = (r * (2.0 * jnp.maximum(refs[2][...], 0.0))).astype(out_dtype)
            elif epi == "add":
                refs[3][...] = r + alpha * refs[2][...]
            else:
                refs[2][...] = r.astype(out_dtype)

    return pl.pallas_call(
        body, name=name, grid=(M // tm, N // tn, nk), in_specs=in_specs, out_specs=out_specs,
        out_shape=out_shape, scratch_shapes=[pltpu.VMEM((tm, tn), f32)],
        compiler_params=_cp(("parallel", "parallel", "arbitrary")))(*args)


def _ln_fwd(h, m, g, b, alpha, name):
    L = h.shape[0]
    tm = _pick(L, (384, 256, 128))
    row = pl.BlockSpec((tm, D), lambda i: (i, 0))
    vec = pl.BlockSpec((1, D), lambda i: (0, 0))

    def body(*refs):
        if m is None:
            h_ref, g_ref, b_ref, o_ref = refs
            z = h_ref[...]
        else:
            h_ref, m_ref, g_ref, b_ref, o_ref = refs
            z = alpha * h_ref[...] + m_ref[...]
        mu = jnp.mean(z, -1, keepdims=True)
        var = jnp.mean(jnp.square(z - mu), -1, keepdims=True)
        o_ref[...] = (z - mu) * lax.rsqrt(var + LN_EPS) * g_ref[...] + b_ref[...]

    args = [h] + ([] if m is None else [m]) + [g.reshape(1, D), b.reshape(1, D)]
    specs = [row] + ([] if m is None else [row]) + [vec, vec]
    return pl.pallas_call(body, name=name, grid=(L // tm,), in_specs=specs, out_specs=row,
                          out_shape=jax.ShapeDtypeStruct((L, D), f32), compiler_params=_cp(("parallel",)))(*args)


def _ln_bwd(dy, h, m, g, alpha, name):
    L = h.shape[0]
    tm = _pick(L, (384, 256, 128))
    row = pl.BlockSpec((tm, D), lambda i: (i, 0))
    vec = pl.BlockSpec((1, D), lambda i: (0, 0))
    acc = pl.BlockSpec((8, D), lambda i: (0, 0))

    def body(*refs):
        if m is None:
            dy_ref, h_ref, g_ref, dz_ref, dg_ref, db_ref = refs
            z = h_ref[...]
        else:
            dy_ref, h_ref, m_ref, g_ref, dz_ref, dg_ref, db_ref = refs
            z = alpha * h_ref[...] + m_ref[...]

        @pl.when(pl.program_id(0) == 0)
        def _():
            dg_ref[...] = jnp.zeros_like(dg_ref)
            db_ref[...] = jnp.zeros_like(db_ref)

        dyv = dy_ref[...]
        mu = jnp.mean(z, -1, keepdims=True)
        zc = z - mu
        rstd = lax.rsqrt(jnp.mean(jnp.square(zc), -1, keepdims=True) + LN_EPS)
        xh = zc * rstd
        dxh = dyv * g_ref[...]
        dz_ref[...] = rstd * (dxh - jnp.mean(dxh, -1, keepdims=True) - xh * jnp.mean(dxh * xh, -1, keepdims=True))
        dg_ref[0:1, :] += jnp.sum(dyv * xh, 0, keepdims=True)
        db_ref[0:1, :] += jnp.sum(dyv, 0, keepdims=True)

    args = [dy, h] + ([] if m is None else [m]) + [g.reshape(1, D)]
    specs = [row, row] + ([] if m is None else [row]) + [vec]
    dz, dg, db = pl.pallas_call(
        body, name=name, grid=(L // tm,), in_specs=specs, out_specs=(row, acc, acc),
        out_shape=(jax.ShapeDtypeStruct((L, D), f32), jax.ShapeDtypeStruct((8, D), f32),
                   jax.ShapeDtypeStruct((8, D), f32)),
        compiler_params=_cp(("arbitrary",)))(*args)
    return dz, dg[0], db[0]


def _rms(x, g):
    r = lax.rsqrt(jnp.mean(jnp.square(x), -1, keepdims=True) + LN_EPS)
    return x * r * g


def _prep_fwd(proj, gq, gkv, tabs):
    L = proj.shape[0]
    tm = BLK
    rc, rs, mc, ms = tabs

    def body(p_ref, gq_ref, gkv_ref, rc_ref, rs_ref, mc_ref, ms_ref, sb_ref, cq_ref, ckv_ref, rqk_ref, rv_ref, kr_ref):
        i = pl.program_id(0)
        sb_ref[:, 0:512] = (p_ref[:, C_SBQ:C_SBQ + 512] * SB_SCALE).astype(bf16)
        sb_ref[:, 512:1536] = p_ref[:, C_SBK:C_SBK + 1024].astype(bf16)
        cq_ref[...] = _rms(p_ref[:, C_CQ:C_CQ + 384], gq_ref[...]).astype(bf16)
        ckv_ref[...] = _rms(p_ref[:, C_CKV:C_CKV + 256], gkv_ref[...]).astype(bf16)
        valid = (i * tm + lax.broadcasted_iota(jnp.int32, (tm, 128), 0)) >= N_PAD
        for c in range(2):
            sl = slice(c * 128, (c + 1) * 128)
            x = p_ref[:, C_RQ + c * 128:C_RQ + (c + 1) * 128]
            rqk_ref[:, sl] = (x * rc_ref[:, sl] + _rot(x, 32) * rs_ref[:, sl]).astype(bf16)
            x = p_ref[:, C_RK + c * 128:C_RK + (c + 1) * 128]
            kk = (x * rc_ref[:, sl] + _rot(x, 32) * rs_ref[:, sl]) * RET_SCALE
            rqk_ref[:, 256 + c * 128:256 + (c + 1) * 128] = jnp.where(valid, kk, 0.0).astype(bf16)
        rv_ref[...] = p_ref[:, C_RV:C_RV + 512].astype(bf16)
        x = p_ref[:, C_KR:C_KR + 128]
        kr_ref[...] = (x * mc_ref[...] + _rot(x, 16) * ms_ref[...]).astype(bf16)

    def row(w):
        return pl.BlockSpec((tm, w), lambda i: (i, 0))

    def vec(w):
        return pl.BlockSpec((1, w), lambda i: (0, 0))

    widths = (1536, 384, 256, 512, 512, 128)
    return pl.pallas_call(
        body, name="prep_fwd", grid=(L // tm,),
        in_specs=[row(N_INP), vec(384), vec(256), row(256), row(256), row(128), row(128)],
        out_specs=tuple(row(w) for w in widths),
        out_shape=tuple(jax.ShapeDtypeStruct((L, w), bf16) for w in widths),
        compiler_params=_cp(("parallel",)))(proj, gq.reshape(1, 384), gkv.reshape(1, 256), rc, rs, mc, ms)


def _rms_bwd(x, g, dy):
    r = lax.rsqrt(jnp.mean(jnp.square(x), -1, keepdims=True) + LN_EPS)
    u = dy * g
    dx = r * u - x * (r * r * r) * jnp.mean(x * u, -1, keepdims=True)
    return dx, jnp.sum(dy * x * r, 0, keepdims=True)


def _prep_bwd(proj, gq, gkv, tabs, dcqn, dckvn, drq_r, drk_r, dkr_r):
    L = proj.shape[0]
    tm = BLK
    rc, rs, mc, ms = tabs

    def body(p_ref, gq_ref, gkv_ref, rc_ref, rs_ref, mc_ref, ms_ref, dcqn_ref, dckvn_ref, drq_ref, drk_ref,
             dkr_ref, ocq_ref, ockv_ref, orq_ref, ork_ref, okr_ref, dgq_ref, dgkv_ref):
        i = pl.program_id(0)

        @pl.when(i == 0)
        def _():
            dgq_ref[...] = jnp.zeros_like(dgq_ref)
            dgkv_ref[...] = jnp.zeros_like(dgkv_ref)

        dx, dg = _rms_bwd(p_ref[:, C_CQ:C_CQ + 384], gq_ref[...], dcqn_ref[...])
        ocq_ref[...] = dx
        dgq_ref[0:1, :] += dg
        dx, dg = _rms_bwd(p_ref[:, C_CKV:C_CKV + 256], gkv_ref[...], dckvn_ref[...])
        ockv_ref[...] = dx
        dgkv_ref[0:1, :] += dg
        valid = (i * tm + lax.broadcasted_iota(jnp.int32, (tm, 128), 0)) >= N_PAD
        for c in range(2):
            sl = slice(c * 128, (c + 1) * 128)
            dy = drq_ref[:, sl]
            orq_ref[:, sl] = dy * rc_ref[:, sl] - _rot(dy * rs_ref[:, sl], 32)
            dy = jnp.where(valid, drk_ref[:, sl], 0.0) * RET_SCALE
            ork_ref[:, sl] = dy * rc_ref[:, sl] - _rot(dy * rs_ref[:, sl], 32)
        dy = dkr_ref[...]
        okr_ref[...] = dy * mc_ref[...] - _rot(dy * ms_ref[...], 16)

    def row(w):
        return pl.BlockSpec((tm, w), lambda i: (i, 0))

    def vec(w):
        return pl.BlockSpec((1, w), lambda i: (0, 0))

    def acc(w):
        return pl.BlockSpec((8, w), lambda i: (0, 0))

    widths = (384, 256, 256, 256, 128)
    outs = pl.pallas_call(
        body, name="prep_bwd", grid=(L // tm,),
        in_specs=[row(N_INP), vec(384), vec(256), row(256), row(256), row(128), row(128),
                  row(384), row(256), row(256), row(256), row(128)],
        out_specs=tuple(row(w) for w in widths) + (acc(384), acc(256)),
        out_shape=tuple(jax.ShapeDtypeStruct((L, w), f32) for w in widths)
        + (jax.ShapeDtypeStruct((8, 384), f32), jax.ShapeDtypeStruct((8, 256), f32)),
        compiler_params=_cp(("arbitrary",)))(
            proj, gq.reshape(1, 384), gkv.reshape(1, 256), rc, rs, mc, ms, dcqn, dckvn, drq_r, drk_r, dkr_r)
    return outs[:5] + (outs[5][0], outs[6][0])


def _qrows(L):
    return _pick(L, (384, 256, 128))


def _tri_ones(strict):
    r = lax.broadcasted_iota(jnp.int32, (BLK, 2 * BLK), 0)
    c = lax.broadcasted_iota(jnp.int32, (BLK, 2 * BLK), 1)
    tri = (r > c) if strict else (r >= c)
    return jnp.where(tri | (c >= BLK), 1.0, 0.0).astype(_MXU)


def _dot2(x, u):
    hi = x.astype(_MXU)
    lo = (x - hi.astype(f32)).astype(_MXU)
    return jnp.dot(hi, u, preferred_element_type=f32) + jnp.dot(lo, u, preferred_element_type=f32)


def _staggered(chains):
    live = list(chains)
    step = 0
    while live:
        for ci, g in enumerate(chains):
            if g in live and step >= ci and next(g, True):
                live.remove(g)
        step += 1


def _pair_rhs(t, m):
    zt = jnp.zeros_like(t)
    return jnp.concatenate([jnp.where(m, t, zt), jnp.where(m, zt, t)], axis=0)


def _sb_stages(z, mask, c_ref, x, u_gt, out):
    yield
    lb = jnp.minimum(z, 0.0) - jnp.log1p(jnp.exp(-jnp.abs(z)))
    lk = lb - z
    if mask is not None:
        lk = jnp.where(mask, lk, 0.0)
    hi = lk.astype(_MXU)
    lo = (lk - hi.astype(f32)).astype(_MXU)
    yield
    el = (jnp.dot(hi, u_gt, preferred_element_type=f32)
          + jnp.dot(lo, u_gt, preferred_element_type=f32))
    yield
    c = c_ref[x]
    w = jnp.exp(lb + el[:, 0:BLK] + c)
    if mask is not None:
        w = jnp.where(mask, w, 0.0)
    c_ref[x] = c + el[:, BLK:2 * BLK]
    out["w"], out["lb"] = w.astype(_MXU), lb


def _sb_sweep(i, r, tiles, per_pass):
    n_t = r * (i + 1)
    tiles([n_t - 1 - t for t in range(r)], True)
    n_bulk = jnp.maximum(r * i - 1, 0)
    n_full = n_bulk // per_pass
    lax.fori_loop(0, n_full,
                  lambda jj, c: tiles([r * i - 1 - per_pass * jj - t for t in range(per_pass)], False) or c, 0)
    rem = n_bulk - n_full * per_pass

    if per_pass == 4:
        @pl.when(rem >= 2)
        def _():
            tiles([rem, rem - 1], False)

    @pl.when(rem % 2 == 1)
    def _():
        tiles([1], False)

    @pl.when(i > 0)
    def _():
        tiles([0], True)


def _tile_off(j):
    return j * BLK if isinstance(j, int) else pl.multiple_of(j * BLK, BLK)


def _sb_mask(i, j, qb):
    row = i * qb + lax.broadcasted_iota(jnp.int32, (qb, BLK), 0)
    col = j * BLK + lax.broadcasted_iota(jnp.int32, (qb, BLK), 1)
    return (col < row) & (col >= N_PAD)


def _first_last(n0, n1):
    p, i = pl.program_id(0), pl.program_id(1)
    return (p == 0) & (i == 0), (p == n0 - 1) & (i == n1 - 1)


def _sb_fwd(sb, bg=None):
    L = sb.shape[0]
    qb = _qrows(L)
    nq, r = L // qb, qb // BLK

    def body(*refs):
        if bg is None:
            q_ref, k_ref, v_ref, o_ref, acc_ref, c_ref = refs
        else:
            q_ref, k_ref, v_ref, x_ref, o_ref, g_ref, acc_ref, c_ref = refs[:8]
            start, wait = _xchg_ops(x_ref, g_ref, *refs[8:], False)
            first, last = _first_last(4, nq)
            pl.when(first)(start)
        i = pl.program_id(1)
        m_a = lax.broadcasted_iota(jnp.int32, (1, BLK), 1) < 64
        q = q_ref[...]
        u_gt = _tri_ones(True)
        acc_ref[...] = jnp.zeros_like(acc_ref)
        c_ref[...] = jnp.zeros_like(c_ref)

        def chain(x, j, masked, both):
            off = _tile_off(j)
            if x == 0:
                both["z"] = _dot(q, _pair_rhs(k_ref[pl.ds(off, BLK), :], m_a), NT)
            mask = _sb_mask(i, j, qb) if masked else None
            o = {}
            yield from _sb_stages(both["z"][:, x * BLK:(x + 1) * BLK], mask, c_ref, x, u_gt, o)
            yield
            acc_ref[x] += _dot(o["w"], v_ref[pl.ds(off, BLK), :])

        def tiles(js, masked):
            shared = [{} for _ in js]
            _staggered([chain(x, j, masked, shared[t]) for t, j in enumerate(js) for x in range(2)])

        _sb_sweep(i, r, tiles, 4)
        o_ref[...] = jnp.where(m_a, acc_ref[0], acc_ref[1])
        if bg is not None:
            pl.when(last)(wait)

    in_specs = [pl.BlockSpec((qb, 128), lambda p, i: (i, p)),
                pl.BlockSpec((L, 128), lambda p, i: (0, 4 + p)),
                pl.BlockSpec((L, 128), lambda p, i: (0, 8 + p))]
    o_spec = pl.BlockSpec((qb, 128), lambda p, i: (i, p))
    o_shape = jax.ShapeDtypeStruct((L, 512), f32)
    scratch = [pltpu.VMEM((2, qb, BLK), f32), pltpu.VMEM((2, qb, BLK), f32)]
    if bg is None:
        return pl.pallas_call(body, name="sb_fwd", grid=(4, nq), in_specs=in_specs, out_specs=o_spec,
                              out_shape=o_shape, scratch_shapes=scratch,
                              compiler_params=_cp(("parallel", "arbitrary")))(sb, sb, sb)
    return pl.pallas_call(body, name="sb_fwd_gather", grid=(4, nq), in_specs=in_specs + [_HBM],
                          out_specs=(o_spec, _HBM), out_shape=(o_shape, _xchg_shape(bg, False)),
                          scratch_shapes=scratch + list(_XCHG_SEMS),
                          compiler_params=_cp(("arbitrary", "arbitrary")))(sb, sb, sb, bg)


def _sb_bwd(dmixed, sb, out_a, bg=None):
    L = sb.shape[0]
    qb = _qrows(L)
    nq, r = L // qb, qb // BLK

    def body(*refs):
        if bg is None:
            do_ref, o_ref, q_ref, k_ref, v_ref, dq_ref, dk_ref, dv_ref, dqa_ref, c_ref, cg_ref, ds_ref = refs
        else:
            do_ref, o_ref, q_ref, k_ref, v_ref, x_ref, dq_ref, dk_ref, dv_ref, g_ref = refs[:10]
            dqa_ref, c_ref, cg_ref, ds_ref = refs[10:14]
            start, wait = _xchg_ops(x_ref, g_ref, *refs[14:], True)
            first, last = _first_last(4, nq)
            pl.when(first)(start)
        i = pl.program_id(1)

        @pl.when(i == 0)
        def _():
            dk_ref[...] = jnp.zeros_like(dk_ref)
            dv_ref[...] = jnp.zeros_like(dv_ref)

        m_a = lax.broadcasted_iota(jnp.int32, (1, BLK), 1) < 64
        q = q_ref[...]
        zq = jnp.zeros_like(q)
        qs = (jnp.where(m_a, q, zq), jnp.where(m_a, zq, q))
        do = do_ref[...]
        zd = jnp.zeros_like(do)
        dos = (jnp.where(m_a, do, zd).astype(_MXU), jnp.where(m_a, zd, do).astype(_MXU))
        do_b = do.astype(_MXU)
        prod = do_b.astype(f32) * o_ref[...]
        ds_ref[0] = jnp.broadcast_to(jnp.sum(jnp.where(m_a, prod, 0.0), 1, keepdims=True), (qb, BLK))
        ds_ref[1] = jnp.broadcast_to(jnp.sum(jnp.where(m_a, 0.0, prod), 1, keepdims=True), (qb, BLK))
        u_gt = _tri_ones(True)
        u_ge = _tri_ones(False)
        dqa_ref[...] = jnp.zeros_like(dqa_ref)
        c_ref[...] = jnp.zeros_like(c_ref)
        cg_ref[...] = jnp.zeros_like(cg_ref)

        def chain(x, j, masked, both):
            off = _tile_off(j)
            k = k_ref[pl.ds(off, BLK), :]
            if x == 0:
                both["kk"] = _pair_rhs(k, m_a)
                both["z"] = _dot(q, both["kk"], NT)
                both["dw"] = _dot(do_b, _pair_rhs(v_ref[pl.ds(off, BLK), :], m_a), NT)
            mask = _sb_mask(i, j, qb) if masked else None
            o = {}
            yield from _sb_stages(both["z"][:, x * BLK:(x + 1) * BLK], mask, c_ref, x, u_gt, o)
            wb = o["w"]
            gr = wb.astype(f32) * both["dw"][:, x * BLK:(x + 1) * BLK]
            hi = gr.astype(_MXU)
            lo = (gr - hi.astype(f32)).astype(_MXU)
            yield
            eg = (jnp.dot(hi, u_ge, preferred_element_type=f32)
                  + jnp.dot(lo, u_ge, preferred_element_type=f32))
            yield
            cg = cg_ref[x]
            suffix = eg[:, 0:BLK] + cg
            cg_ref[x] = cg + eg[:, BLK:2 * BLK]
            dz = gr - jnp.exp(o["lb"]) * (gr + ds_ref[x] - suffix)
            if masked:
                dz = jnp.where(mask, dz, 0.0)
            dz = dz.astype(_MXU)
            yield
            if x == 0:
                both["dz"] = dz
            else:
                dqa_ref[0] += _dot(jnp.concatenate([both["dz"], dz], axis=1), both["kk"])
            dk_ref[pl.ds(off, BLK), :] += _dot(dz, qs[x], TN)
            dv_ref[pl.ds(off, BLK), :] += _dot(wb, dos[x], TN)

        def tiles(js, masked):
            shared = [{} for _ in js]
            _staggered([chain(x, j, masked, shared[t]) for t, j in enumerate(js) for x in range(2)])

        _sb_sweep(i, r, tiles, 4)
        dq_ref[...] = dqa_ref[0] * SB_SCALE
        if bg is not None:
            pl.when(last)(wait)

    blk = pl.BlockSpec((qb, 128), lambda p, i: (i, p))
    scr = pltpu.VMEM((2, qb, BLK), f32)
    in_specs = [blk, blk, blk, pl.BlockSpec((L, 128), lambda p, i: (0, 4 + p)),
                pl.BlockSpec((L, 128), lambda p, i: (0, 8 + p))]
    out_specs = (blk, pl.BlockSpec((L, 128), lambda p, i: (0, p)), pl.BlockSpec((L, 128), lambda p, i: (0, p)))
    out_shape = tuple(jax.ShapeDtypeStruct((L, 512), f32) for _ in range(3))
    if bg is None:
        return pl.pallas_call(body, name="sb_bwd", grid=(4, nq), in_specs=in_specs, out_specs=out_specs,
                              out_shape=out_shape, scratch_shapes=[scr, scr, scr, scr],
                              compiler_params=_cp(("parallel", "arbitrary")))(dmixed, out_a, sb, sb, sb)
    return pl.pallas_call(body, name="sb_bwd_scatter", grid=(4, nq), in_specs=in_specs + [_HBM],
                          out_specs=out_specs + (_HBM,), out_shape=out_shape + (_xchg_shape(bg, True),),
                          scratch_shapes=[scr, scr, scr, scr] + list(_XCHG_SEMS),
                          compiler_params=_cp(("arbitrary", "arbitrary")))(dmixed, out_a, sb, sb, sb, bg)


def _mla_mask(i, j, qb):
    row = i * qb + lax.broadcasted_iota(jnp.int32, (qb, BLK), 0)
    col = j * BLK + lax.broadcasted_iota(jnp.int32, (qb, BLK), 1)
    return (col <= row) & ((col >= N_PAD) | (col == row))


def _pair_mask2():
    return (lax.broadcasted_iota(jnp.int32, (1, 256), 1) % 128) < 64


def _mla_q2(qn_ref, qr_ref, mc_ref, ms_ref):
    qr = qr_ref[...]
    qr = qr * mc_ref[...] + _rot(qr, 16) * ms_ref[...]
    q2 = jnp.concatenate([qn_ref[...], qr], axis=1)
    m2 = _pair_mask2()
    z2 = jnp.zeros_like(q2)
    return (jnp.where(m2, q2, z2).astype(_MXU), jnp.where(m2, z2, q2).astype(_MXU)), q2.astype(_MXU)


def _mla_fwd(q, kv, kr2, mc, ms):
    L = q.shape[0]
    qb = _qrows(L)
    nq, r = L // qb, qb // BLK

    def body(qn_ref, qr_ref, mc_ref, ms_ref, kn_ref, v_ref, kr_ref, o_ref, lse_ref, acc_ref, m_ref):
        i = pl.program_id(1)
        m_a = lax.broadcasted_iota(jnp.int32, (1, BLK), 1) < 64
        _, q2 = _mla_q2(qn_ref, qr_ref, mc_ref, ms_ref)
        m2 = _pair_mask2()
        acc_ref[...] = jnp.zeros_like(acc_ref)
        m_ref[...] = jnp.full(m_ref.shape, -1e30, f32)
        ones = jnp.ones((BLK, BLK), _MXU)

        def chain(x, j, masked, both):
            off = _tile_off(j)
            if x == 0:
                k2 = jnp.concatenate([kn_ref[pl.ds(off, BLK), :], kr_ref[pl.ds(off, BLK), :]], axis=1)
                both["s"] = _dot(q2, _pair_rhs(k2, m2), NT)
            yield
            s = both["s"][:, x * BLK:(x + 1) * BLK] * MLA_SCALE
            if masked:
                mask = _mla_mask(i, j, qb)
                s = jnp.where(mask, s, -1e30)
            m_old = m_ref[x]
            m_new = jnp.maximum(m_old, jnp.max(s, 1, keepdims=True))
            a = jnp.exp(m_old - m_new)
            p = jnp.exp(s - m_new)
            if masked:
                p = jnp.where(mask, p, 0.0)
            m_ref[x] = m_new
            p = p.astype(_MXU)
            yield
            v1 = jnp.concatenate([v_ref[pl.ds(off, BLK), :], ones], axis=1)
            acc_ref[x] = jnp.concatenate([a, a], axis=1) * acc_ref[x] + _dot(p, v1)

        def tiles(js, masked):
            shared = [{} for _ in js]
            _staggered([chain(x, j, masked, shared[t]) for t, j in enumerate(js) for x in range(2)])

        _sb_sweep(i, r, tiles, 4)
        o_ref[...] = jnp.where(m_a, acc_ref[0, :, 0:BLK] / acc_ref[0, :, BLK:2 * BLK],
                               acc_ref[1, :, 0:BLK] / acc_ref[1, :, BLK:2 * BLK])
        for x in range(2):
            lse_ref[0, x] = m_ref[x] + jnp.log(acc_ref[x, :, BLK:2 * BLK])

    blk = lambda cb: pl.BlockSpec((qb, 128), lambda p, i: (i, cb + p))
    full = lambda cb: pl.BlockSpec((L, 128), lambda p, i: (0, cb + p))
    tab = pl.BlockSpec((qb, 128), lambda p, i: (i, 0))
    return pl.pallas_call(
        body, name="mla_fwd", grid=(4, nq),
        in_specs=[blk(0), blk(4), tab, tab, full(0), full(4), pl.BlockSpec((L, 128), lambda p, i: (0, 0))],
        out_specs=(blk(0), pl.BlockSpec((1, 2, qb, 128), lambda p, i: (p, 0, i, 0))),
        out_shape=(jax.ShapeDtypeStruct((L, 512), f32), jax.ShapeDtypeStruct((4, 2, L, 128), f32)),
        scratch_shapes=[pltpu.VMEM((2, qb, 2 * BLK), f32), pltpu.VMEM((2, qb, BLK), f32)],
        compiler_params=_cp(("parallel", "arbitrary")))(q, q, mc, ms, kv, kv, kr2)


def _mla_bwd(dmixed, q, kv, kr2, out_b, lse, mc, ms):
    L = q.shape[0]
    qb = _qrows(L)
    nq, r = L // qb, qb // BLK

    def body(do_ref, o_ref, lse_ref, qn_ref, qr_ref, mc_ref, ms_ref, kn_ref, v_ref, kr_ref,
             dqn_ref, dqr_ref, dkn_ref, dv_ref, dkr_ref, dqa_ref, ds_ref):
        i = pl.program_id(1)

        @pl.when(i == 0)
        def _():
            dkn_ref[...] = jnp.zeros_like(dkn_ref)
            dv_ref[...] = jnp.zeros_like(dv_ref)
            dkr_ref[...] = jnp.zeros_like(dkr_ref)

        m_a = lax.broadcasted_iota(jnp.int32, (1, BLK), 1) < 64
        qs, q2 = _mla_q2(qn_ref, qr_ref, mc_ref, ms_ref)
        m2 = _pair_mask2()
        do = do_ref[...]
        zd = jnp.zeros_like(do)
        dos = (jnp.where(m_a, do, zd).astype(_MXU), jnp.where(m_a, zd, do).astype(_MXU))
        do_b = do.astype(_MXU)
        prod = do * o_ref[...]
        ds_ref[0] = jnp.broadcast_to(jnp.sum(jnp.where(m_a, prod, 0.0), 1, keepdims=True), (qb, BLK))
        ds_ref[1] = jnp.broadcast_to(jnp.sum(jnp.where(m_a, 0.0, prod), 1, keepdims=True), (qb, BLK))
        dqa_ref[...] = jnp.zeros_like(dqa_ref)

        def chain(x, j, masked, both):
            off = _tile_off(j)
            if x == 0:
                k2 = jnp.concatenate([kn_ref[pl.ds(off, BLK), :], kr_ref[pl.ds(off, BLK), :]], axis=1)
                both["kk"] = _pair_rhs(k2, m2)
                both["s"] = _dot(q2, both["kk"], NT)
                both["dp"] = _dot(do_b, _pair_rhs(v_ref[pl.ds(off, BLK), :], m_a), NT)
            yield
            s = both["s"][:, x * BLK:(x + 1) * BLK] * MLA_SCALE
            dp = both["dp"][:, x * BLK:(x + 1) * BLK]
            if masked:
                mask = _mla_mask(i, j, qb)
                p = jnp.where(mask, jnp.exp(jnp.where(mask, s, 0.0) - lse_ref[0, x]), 0.0)
            else:
                p = jnp.exp(s - lse_ref[0, x])
            pb = p.astype(_MXU)
            ds = (p * (dp - ds_ref[x]) * MLA_SCALE).astype(_MXU)
            yield
            if x == 0:
                both["ds"] = ds
            else:
                dqa_ref[0] += _dot(jnp.concatenate([both["ds"], ds], axis=1), both["kk"])
            dk_t = _dot(ds, qs[x], TN)
            dkn_ref[pl.ds(off, BLK), :] += dk_t[:, 0:128]
            dkr_ref[0, pl.ds(off, BLK), :] += dk_t[:, 128:256]
            dv_ref[pl.ds(off, BLK), :] += _dot(pb, dos[x], TN)

        def tiles(js, masked):
            shared = [{} for _ in js]
            _staggered([chain(x, j, masked, shared[t]) for t, j in enumerate(js) for x in range(2)])

        _sb_sweep(i, r, tiles, 4)
        dq2 = dqa_ref[0]
        dqn_ref[...] = dq2[:, 0:128]
        dy = dq2[:, 128:256]
        dqr_ref[...] = dy * mc_ref[...] - _rot(dy * ms_ref[...], 16)

    blk = lambda cb: pl.BlockSpec((qb, 128), lambda p, i: (i, cb + p))
    full = lambda cb: pl.BlockSpec((L, 128), lambda p, i: (0, cb + p))
    tab = pl.BlockSpec((qb, 128), lambda p, i: (i, 0))
    o512 = jax.ShapeDtypeStruct((L, 512), f32)
    return pl.pallas_call(
        body, name="mla_bwd", grid=(4, nq),
        in_specs=[blk(4), blk(0), pl.BlockSpec((1, 2, qb, 128), lambda p, i: (p, 0, i, 0)), blk(0), blk(4), tab, tab,
                  full(0), full(4), pl.BlockSpec((L, 128), lambda p, i: (0, 0))],
        out_specs=(blk(0), blk(0), full(0), full(0), pl.BlockSpec((1, L, 128), lambda p, i: (p, 0, 0))),
        out_shape=(o512, o512, o512, o512, jax.ShapeDtypeStruct((4, L, 128), f32)),
        scratch_shapes=[pltpu.VMEM((2, qb, 2 * BLK), f32), pltpu.VMEM((2, qb, BLK), f32)],
        compiler_params=_cp(("parallel", "arbitrary")))(dmixed, out_b, lse, q, q, mc, ms, kv, kv, kr2)


def _ret_tables():
    log_g = jnp.log(jnp.array(RET_GAMMA, f32))
    idx = jnp.arange(BLK, dtype=f32)
    diff = idx[:, None] - idx[None, :]
    d_in = jnp.where(diff[None] >= 0, jnp.exp(jnp.maximum(diff, 0.0)[None] * log_g[:, None, None]), 0.0)
    q_dec = jnp.exp((idx[None, :] + 1.0) * log_g[:, None])
    k_dec = jnp.exp((BLK - 1.0 - idx[None, :]) * log_g[:, None])
    c_dec = jnp.exp(BLK * log_g)
    bc = lambda a: jnp.broadcast_to(a[:, :, None], (4, BLK, BLK))
    return d_in, bc(q_dec), bc(k_dec), jnp.broadcast_to(c_dec[:, None, None], (4, 8, BLK))


def _head_mask(x):
    lane = lax.broadcasted_iota(jnp.int32, (1, BLK), 1)
    return (lane < 64) if x == 0 else (lane >= 64)


def _ret_fwd(rqk, rv, proj, rtabs):
    L = rqk.shape[0]
    n = L // BLK
    d_in, q_dec, k_dec, c_dec = rtabs

    def body(q_ref, k_ref, v_ref, g_ref, din_ref, qd_ref, kd_ref, cd_ref, y_ref, o_ref, st_ref, s_scr):
        @pl.when(pl.program_id(1) == 0)
        def _():
            s_scr[...] = jnp.zeros_like(s_scr)

        q = q_ref[...]
        k = k_ref[...]
        zq = jnp.zeros_like(q)
        for x in range(2):
            hm = _head_mask(x)
            sl = slice(x * 128, (x + 1) * 128)
            qm = jnp.where(hm, q, zq)
            km = jnp.where(hm, k, zq)
            v = v_ref[:, sl]
            s_in = s_scr[x]
            st_ref[0, 0, x] = s_in
            inner = _dot(qm, km, NT) * din_ref[x]
            y = _dot(inner, v) + _dot(qm, s_in) * qd_ref[x]
            s_scr[x] = s_in * cd_ref[x, 0:1, :] + _dot(km.astype(f32) * kd_ref[x], v, TN)
            y_ref[:, sl] = y
            mu = jnp.mean(y, -1, keepdims=True)
            yc = y - mu
            yn = yc * lax.rsqrt(jnp.mean(jnp.square(yc), -1, keepdims=True) + LN_EPS)
            g = g_ref[:, sl]
            o_ref[:, sl] = g * jax.nn.sigmoid(g) * yn

    tab = pl.BlockSpec((2, BLK, BLK), lambda p, i: (p, 0, 0))
    return pl.pallas_call(
        body, name="ret_fwd", grid=(2, n),
        in_specs=[pl.BlockSpec((BLK, 128), lambda p, i: (i, p)), pl.BlockSpec((BLK, 128), lambda p, i: (i, 2 + p)),
                  pl.BlockSpec((BLK, 256), lambda p, i: (i, p)),
                  pl.BlockSpec((BLK, 256), lambda p, i: (i, C_RG // 256 + p)),
                  tab, tab, tab, pl.BlockSpec((2, 8, BLK), lambda p, i: (p, 0, 0))],
        out_specs=(pl.BlockSpec((BLK, 256), lambda p, i: (i, p)), pl.BlockSpec((BLK, 256), lambda p, i: (i, p)),
                   pl.BlockSpec((1, 1, 2, BLK, BLK), lambda p, i: (p, i, 0, 0, 0))),
        out_shape=(jax.ShapeDtypeStruct((L, 512), f32), jax.ShapeDtypeStruct((L, 512), f32),
                   jax.ShapeDtypeStruct((2, n, 2, BLK, BLK), f32)),
        scratch_shapes=[pltpu.VMEM((2, BLK, BLK), f32)],
        compiler_params=_cp(("parallel", "arbitrary")))(rqk, rqk, rv, proj, d_in, q_dec, k_dec, c_dec)


def _ret_bwd(dmixed, rqk, rv, proj, y, states, rtabs):
    L = rqk.shape[0]
    n = L // BLK
    d_in, q_dec, k_dec, c_dec = rtabs

    def body(do_ref, q_ref, k_ref, v_ref, g_ref, y_ref, st_ref, din_ref, qd_ref, kd_ref, cd_ref,
             dq_ref, dk_ref, dv_ref, dg_ref, ds_scr):
        @pl.when(pl.program_id(1) == 0)
        def _():
            ds_scr[...] = jnp.zeros_like(ds_scr)

        q = q_ref[...]
        k = k_ref[...]
        zq = jnp.zeros_like(q)
        dq_acc = jnp.zeros((BLK, BLK), f32)
        dk_acc = jnp.zeros((BLK, BLK), f32)
        for x in range(2):
            hm = _head_mask(x)
            sl = slice(x * 128, (x + 1) * 128)
            qm = jnp.where(hm, q, zq)
            km = jnp.where(hm, k, zq)
            v = v_ref[:, sl]
            yv = y_ref[:, sl]
            g = g_ref[:, sl]
            do = do_ref[:, sl]
            mu = jnp.mean(yv, -1, keepdims=True)
            yc = yv - mu
            rstd = lax.rsqrt(jnp.mean(jnp.square(yc), -1, keepdims=True) + LN_EPS)
            yn = yc * rstd
            sg = jax.nn.sigmoid(g)
            dg_ref[:, sl] = do * yn * sg * (1.0 + g * (1.0 - sg))
            dyn = do * g * sg
            dy = rstd * (dyn - jnp.mean(dyn, -1, keepdims=True) - yn * jnp.mean(dyn * yn, -1, keepdims=True))
            s_in = st_ref[0, 0, x]
            ds_out = ds_scr[x]
            kd = km.astype(f32) * kd_ref[x]
            a = _dot(qm, km, NT) * din_ref[x]
            da = _dot(dy, v, NT) * din_ref[x]
            dyq = dy * qd_ref[x]
            dq_acc += _dot(da, km) + _dot(dyq, s_in, NT)
            dk_acc += _dot(da, qm, TN) + _dot(v, ds_out, NT) * kd_ref[x]
            dv_ref[:, sl] = _dot(a, dy, TN) + _dot(kd, ds_out)
            ds_scr[x] = ds_out * cd_ref[x, 0:1, :] + _dot(qm, dyq, TN)
        dq_ref[...] = dq_acc
        dk_ref[...] = dk_acc

    rev = lambda w, cb: pl.BlockSpec((BLK, w), lambda p, i: (n - 1 - i, cb + p))
    tab = pl.BlockSpec((2, BLK, BLK), lambda p, i: (p, 0, 0))
    return pl.pallas_call(
        body, name="ret_bwd", grid=(2, n),
        in_specs=[rev(256, 4), rev(128, 0), rev(128, 2), rev(256, 0), rev(256, C_RG // 256), rev(256, 0),
                  pl.BlockSpec((1, 1, 2, BLK, BLK), lambda p, i: (p, n - 1 - i, 0, 0, 0)),
                  tab, tab, tab, pl.BlockSpec((2, 8, BLK), lambda p, i: (p, 0, 0))],
        out_specs=(rev(128, 0), rev(128, 0), rev(256, 0), rev(256, 0)),
        out_shape=(jax.ShapeDtypeStruct((L, 256), f32), jax.ShapeDtypeStruct((L, 256), f32),
                   jax.ShapeDtypeStruct((L, 512), f32), jax.ShapeDtypeStruct((L, 512), f32)),
        scratch_shapes=[pltpu.VMEM((2, BLK, BLK), f32)],
        compiler_params=_cp(("parallel", "arbitrary")))(dmixed, rqk, rqk, rv, proj, y, states, d_in, q_dec, k_dec, c_dec)


def _loss_head(h, target):
    L = h.shape[0]
    n = L // BLK

    def body(h_ref, t_ref, dy_ref, l_ref):
        i = pl.program_id(0)

        @pl.when(i == 0)
        def _():
            dy_ref[...] = jnp.zeros_like(dy_ref)
            l_ref[...] = jnp.zeros_like(l_ref)

        @pl.when(i > 0)
        def _():
            err = h_ref[...] - t_ref[...]
            dy_ref[...] = err * (1.0 / D)
            sq = jnp.sum(jnp.sum(jnp.square(err), 1, keepdims=True), 0, keepdims=True)
            l_ref[...] += (0.5 / D) * sq

    return pl.pallas_call(
        body, name="loss_head", grid=(n,),
        in_specs=[pl.BlockSpec((BLK, D), lambda i: (i, 0)),
                  pl.BlockSpec((BLK, D), lambda i: (jnp.maximum(i - 1, 0), 0))],
        out_specs=(pl.BlockSpec((BLK, D), lambda i: (i, 0)), pl.BlockSpec((8, 128), lambda i: (0, 0))),
        out_shape=(jax.ShapeDtypeStruct((L, D), f32), jax.ShapeDtypeStruct((8, 128), f32)),
        compiler_params=_cp(("arbitrary",)))(h, target)


def _adam_math(w, g, m, v):
    m = ADAM_B1 * m + (1.0 - ADAM_B1) * g
    v = ADAM_B2 * v + (1.0 - ADAM_B2) * jnp.square(g)
    m_hat = m / (1.0 - ADAM_B1 ** ADAM_STEP)
    v_hat = v / (1.0 - ADAM_B2 ** ADAM_STEP)
    delta = -ADAM_LR * (m_hat / (jnp.sqrt(v_hat) + ADAM_EPS) + ADAM_WD * w)
    return delta, m, v


def _adamw(parts, w, m, v, name):
    R, C = w.shape
    tr = _pick(R, (240, 192, 144, 96, 64, 48, 32, 16, 8))
    row = pl.BlockSpec((tr, C), lambda i: (i, 0))

    def body(p_ref, w_ref, m_ref, v_ref, g_ref, d_ref, nm_ref, nv_ref):
        g = p_ref[0].astype(f32)
        for k in range(1, N_DEV):
            g = g + p_ref[k].astype(f32)
        d, nm, nv = _adam_math(w_ref[...], g, m_ref[...], v_ref[...])
        g_ref[...] = g
        d_ref[...] = d
        nm_ref[...] = nm
        nv_ref[...] = nv

    o = jax.ShapeDtypeStruct((R, C), f32)
    return pl.pallas_call(
        body, name=name, grid=(R // tr,),
        in_specs=[pl.BlockSpec((N_DEV, tr, C), lambda i: (0, i, 0)), row, row, row],
        out_specs=(row, row, row, row), out_shape=(o, o, o, o),
        compiler_params=_cp(("parallel",)))(parts, w, m, v)


_XCHG_SEMS = [pltpu.SemaphoreType.DMA((N_DEV - 1,)), pltpu.SemaphoreType.DMA((N_DEV - 1,)), pltpu.SemaphoreType.DMA]
_HBM = pl.BlockSpec(memory_space=pltpu.HBM)


def _xchg_shape(x, all_to_all):
    return jax.ShapeDtypeStruct((N_DEV,) + tuple(x.shape[1:] if all_to_all else x.shape), x.dtype)


def _xchg_ops(x_ref, o_ref, send_sems, recv_sems, local_sem, all_to_all):
    mx, my, mc = lax.axis_index("x"), lax.axis_index("y"), lax.axis_index("c")
    me = 4 * mx + 2 * my + mc

    def peer(k):
        px = (1 - mx) if k & 4 else mx
        py = (1 - my) if k & 2 else my
        pc = (1 - mc) if k & 1 else mc
        return (px, py, pc), 4 * px + 2 * py + pc

    def copy(k):
        dev, idx = peer(k)
        src = x_ref.at[idx] if all_to_all else x_ref
        return pltpu.make_async_remote_copy(
            src_ref=src, dst_ref=o_ref.at[me], send_sem=send_sems.at[k - 1], recv_sem=recv_sems.at[k - 1],
            device_id=dev, device_id_type=pl.DeviceIdType.MESH)

    def start():
        pltpu.make_async_copy(x_ref.at[me] if all_to_all else x_ref, o_ref.at[me], local_sem).start()
        for k in range(1, N_DEV):
            copy(k).start()

    def wait():
        for k in range(1, N_DEV):
            dev, idx = peer(k)
            pltpu.make_async_remote_copy(
                src_ref=o_ref.at[idx], dst_ref=o_ref.at[idx], send_sem=send_sems.at[k - 1],
                recv_sem=recv_sems.at[k - 1], device_id=dev, device_id_type=pl.DeviceIdType.MESH).wait_recv()
        for k in range(1, N_DEV):
            copy(k).wait_send()
        pltpu.make_async_copy(x_ref.at[me] if all_to_all else x_ref, o_ref.at[me], local_sem).wait()

    return start, wait


def _exchange(x, all_to_all, name):
    def body(x_ref, o_ref, send_sems, recv_sems, local_sem):
        start, wait = _xchg_ops(x_ref, o_ref, send_sems, recv_sems, local_sem, all_to_all)
        start()
        wait()

    return pl.pallas_call(body, name=name, in_specs=[_HBM], out_specs=_HBM, out_shape=_xchg_shape(x, all_to_all),
                          scratch_shapes=list(_XCHG_SEMS))(x)


WC = 512
LAYER_ROWS = tuple(n // WC for n in (1024 * 468, 384 * 96, 256 * 128, 192 * 1024, 1024 * 512, 512 * 1024))


def _pack_layer(ws, l):
    return jnp.concatenate([w[l].reshape(-1, WC) for w in ws], axis=0)


def _unpack_layer(gathered):
    offs = np.cumsum((0,) + LAYER_ROWS)
    part = lambda k: gathered[:, offs[k]:offs[k + 1]]
    w_in = part(0).reshape(8, 1024, 468).transpose(1, 0, 2).reshape(1024, 3744)
    z32 = jnp.zeros((1024, 32), w_in.dtype)
    kr = w_in[:, 2176:2208]
    w_in = jnp.concatenate([w_in[:, 0:1536], w_in[:, 1920:2176], w_in[:, 2208:3744], w_in[:, 1536:1920],
                            kr, z32, kr, z32], axis=1)
    w_uq = part(1).reshape(8, 384, 96).transpose(1, 0, 2)
    rope = jnp.concatenate([w_uq[..., 64:96], jnp.zeros((384, 8, 32), w_uq.dtype)], axis=-1)
    w_uq = jnp.concatenate([w_uq[..., 0:64].reshape(384, 512), rope.reshape(384, 512)], axis=1)
    w_ukv = part(2).reshape(8, 256, 128).transpose(1, 0, 2)
    w_ukv = jnp.concatenate([w_ukv[..., 0:64].reshape(256, 512), w_ukv[..., 64:128].reshape(256, 512)], axis=1)
    w_out = part(3).reshape(1536, 1024)
    w_ff1 = part(4).reshape(8, 1024, 512).transpose(1, 0, 2).reshape(1024, 4096)
    w_ff2 = part(5).reshape(4096, 1024)
    return w_in, w_uq, w_ukv, w_out, w_ff1, w_ff2


def _pack_layer_grads(g_in, g_uq, g_ukv, g_out, g_ff1, g_ff2):
    kr = g_in[:, C_KR:C_KR + 32] + g_in[:, C_KR + 64:C_KR + 96]
    g_in = jnp.concatenate([g_in[:, 0:1536], g_in[:, C_CQ:C_CQ + 384], g_in[:, C_CKV:C_CKV + 256], kr,
                            g_in[:, C_RQ:C_CQ]], axis=1)
    g_in = g_in.reshape(1024, 8, 468).transpose(1, 0, 2)
    g_uq = jnp.concatenate([g_uq[:, 0:512].reshape(384, 8, 64), g_uq[:, 512:1024].reshape(384, 8, 64)[..., 0:32]],
                           axis=-1).transpose(1, 0, 2)
    g_ukv = jnp.concatenate([g_ukv[:, 0:512].reshape(256, 8, 64), g_ukv[:, 512:1024].reshape(256, 8, 64)],
                            axis=-1).transpose(1, 0, 2)
    g_ff1 = g_ff1.reshape(1024, 8, 512).transpose(1, 0, 2)
    return jnp.concatenate([g.reshape(8, -1, WC) for g in (g_in, g_uq, g_ukv, g_out, g_ff1, g_ff2)], axis=1)


def _rope_tables(L):
    pos = (jnp.arange(L) - N_PAD).astype(f32)

    def cs(half):
        inv = ROPE_THETA ** (-jnp.arange(half, dtype=f32) / half)
        ang = pos[:, None] * inv[None, :]
        return jnp.cos(ang), jnp.sin(ang)

    c, s = cs(32)
    rc, rs = jnp.tile(c, (1, 8)), jnp.tile(s, (1, 8))
    c, s = cs(16)
    z = jnp.zeros((L, 32), f32)
    mc, ms = jnp.concatenate([c, c, z, c, c, z], 1), jnp.concatenate([s, s, z, s, s, z], 1)
    return rc, rs, mc, ms


def _layer_fwd(h, wl, gq, gkv, g1, b1, g2, b2, tabs, rtabs, next_shard):
    w_in, w_uq, w_ukv, w_out, w_ff1, w_ff2 = wl
    proj = _mm(h, w_in, "nn", "mm_in")
    sb, cqn, ckvn, rqk, rv, kr2 = _prep_fwd(proj, gq, gkv, tabs)
    q = _mm(cqn, w_uq, "nn", "mm_uq")
    kv = _mm(ckvn, w_ukv, "nn", "mm_ukv", out_dtype=bf16)
    if next_shard is None:
        out_a, gathered = _sb_fwd(sb), None
    else:
        out_a, gathered = _sb_fwd(sb, next_shard)
    out_b, lse = _mla_fwd(q, kv, kr2, tabs[2], tabs[3])
    y, out_c, states = _ret_fwd(rqk, rv, proj, rtabs)
    mixed = jnp.concatenate([out_a, out_b, out_c], axis=1).astype(bf16)
    mix = _mm(mixed, w_out, "nn", "mm_out")
    h1 = _ln_fwd(h, mix, g1, b1, DN_ALPHA, "ln_fwd")
    u, a = _mm(h1, w_ff1, "nn", "mm_ff1", epi="relu2")
    ff = _mm(a, w_ff2, "nn", "mm_ff2")
    h2 = _ln_fwd(h1, ff, g2, b2, DN_ALPHA, "ln_fwd")
    saved = (h, proj, sb, cqn, ckvn, rqk, rv, kr2, q, kv, out_a, out_b, lse, y, states, mixed, mix, h1, u, a, ff)
    return h2, saved, gathered


def _layer_bwd(dh2, saved, wl, gq, gkv, g1, g2, tabs, rtabs, grads_above):
    w_in, w_uq, w_ukv, w_out, w_ff1, w_ff2 = wl
    h, proj, sb, cqn, ckvn, rqk, rv, kr2, q, kv, out_a, out_b, lse, y, states, mixed, mix, h1, u, a, ff = saved
    dz2, dg2, db2 = _ln_bwd(dh2, h1, ff, g2, DN_ALPHA, "ln_bwd")
    du = _mm(dz2, w_ff2, "nt", "mm_dff2", epi="mul_relu", extra=u, out_dtype=bf16)
    gw_ff2 = _mm(a, dz2, "tn", "mm_gff2")
    dh1 = _mm(du, w_ff1, "nt", "mm_dff1", epi="add", extra=dz2, alpha=DN_ALPHA)
    gw_ff1 = _mm(h1, du, "tn", "mm_gff1")
    dz1, dg1, db1 = _ln_bwd(dh1, h, mix, g1, DN_ALPHA, "ln_bwd")
    dmixed = _mm(dz1, w_out, "nt", "mm_dout")
    gw_out = _mm(mixed, dz1, "tn", "mm_gout")
    if grads_above is None:
        (dsq, dsk, dsv), parts_above = _sb_bwd(dmixed, sb, out_a), None
    else:
        dsq, dsk, dsv, parts_above = _sb_bwd(dmixed, sb, out_a, grads_above)
    dqn, dqr, dkn, dv, dkr_p = _mla_bwd(dmixed, q, kv, kr2, out_b, lse, tabs[2], tabs[3])
    dq = jnp.concatenate([dqn, dqr], axis=1).astype(bf16)
    dkv = jnp.concatenate([dkn, dv], axis=1).astype(bf16)
    dcqn = _mm(dq, w_uq, "nt", "mm_duq")
    gw_uq = _mm(cqn, dq, "tn", "mm_guq")
    dckvn = _mm(dkv, w_ukv, "nt", "mm_dukv")
    gw_ukv = _mm(ckvn, dkv, "tn", "mm_gukv")
    drq_r, drk_r, drv, drg = _ret_bwd(dmixed, rqk, rv, proj, y, states, rtabs)
    dkr_r = dkr_p[0] + dkr_p[1] + dkr_p[2] + dkr_p[3]
    dcq, dckv, drq, drk, dkr2, dgq, dgkv = _prep_bwd(proj, gq, gkv, tabs, dcqn, dckvn, drq_r, drk_r, dkr_r)
    dproj = jnp.concatenate([dsq, dsk, dsv, dckv, drq, drk, drv, drg, dcq, dkr2], axis=1).astype(bf16)
    dh = _mm(dproj, w_in, "nt", "mm_din", epi="add", extra=dz1, alpha=DN_ALPHA)
    gw_in = _mm(h, dproj, "tn", "mm_gin")
    return dh, (gw_in, gw_uq, gw_ukv, gw_out, gw_ff1, gw_ff2), (dgq, dgkv, dg1, db1, dg2, db2), parts_above


def kernel(x, meta_tokens, ln_emb_g, ln_emb_b, w_in, mla_q_norm, mla_kv_norm, w_uq, w_ukv, w_out, ln1_g, ln1_b, w_ff1, w_ff2, ln2_g, ln2_b, loss_target, m_meta_tokens, m_ln_emb_g, m_ln_emb_b, m_w_in, m_mla_q_norm, m_mla_kv_norm, m_w_uq, m_w_ukv, m_w_out, m_ln1_g, m_ln1_b, m_w_ff1, m_w_ff2, m_ln2_g, m_ln2_b, v_meta_tokens, v_ln_emb_g, v_ln_emb_b, v_w_in, v_mla_q_norm, v_mla_kv_norm, v_w_uq, v_w_ukv, v_w_out, v_ln1_g, v_ln1_b, v_w_ff1, v_w_ff2, v_ln2_g, v_ln2_b):
    depth = w_in.shape[0]
    S = x.shape[1]
    L = S + BLK
    me = 4 * lax.axis_index("x") + 2 * lax.axis_index("y") + lax.axis_index("c")
    big = (w_in, w_uq, w_ukv, w_out, w_ff1, w_ff2)
    big_m = (m_w_in, m_w_uq, m_w_ukv, m_w_out, m_w_ff1, m_w_ff2)
    big_v = (v_w_in, v_w_uq, v_w_ukv, v_w_out, v_w_ff1, v_w_ff2)
    small = (ln_emb_g, ln_emb_b, mla_q_norm, mla_kv_norm, ln1_g, ln1_b, ln2_g, ln2_b)
    small_m = (m_ln_emb_g, m_ln_emb_b, m_mla_q_norm, m_mla_kv_norm, m_ln1_g, m_ln1_b, m_ln2_g, m_ln2_b)
    small_v = (v_ln_emb_g, v_ln_emb_b, v_mla_q_norm, v_mla_kv_norm, v_ln1_g, v_ln1_b, v_ln2_g, v_ln2_b)

    shards = [_pack_layer(big, l) for l in range(depth)]
    gathered = _exchange(shards[0].astype(bf16), False, "gather_w0")
    meta_all = _exchange(meta_tokens, False, "gather_meta")
    meta_full = meta_all.transpose(1, 0, 2).reshape(N_META, D)

    tabs = _rope_tables(L)
    rtabs = _ret_tables()

    hcat = jnp.concatenate([jnp.zeros((N_PAD, D), f32), meta_full, x[0]], axis=0)
    h = _ln_fwd(hcat, None, ln_emb_g, ln_emb_b, 1.0, "ln_emb_fwd")
    saved, full = [], []
    for l in range(depth):
        full.append(_unpack_layer(gathered))
        nxt = shards[l + 1].astype(bf16) if l + 1 < depth else None
        h, sv, gathered = _layer_fwd(h, full[l], mla_q_norm[l], mla_kv_norm[l], ln1_g[l], ln1_b[l], ln2_g[l],
                                     ln2_b[l], tabs, rtabs, nxt)
        saved.append(sv)

    dh, loss_part = _loss_head(h, loss_target[0])
    gsmall, parts, pending = [None] * depth, [None] * depth, None
    for l in reversed(range(depth)):
        dh, gbig, gsmall[l], got = _layer_bwd(dh, saved[l], full[l], mla_q_norm[l], mla_kv_norm[l], ln1_g[l],
                                              ln2_g[l], tabs, rtabs, pending)
        if pending is not None:
            parts[l + 1] = got
        pending = _pack_layer_grads(*gbig).astype(bf16)
    parts[0] = _exchange(pending, True, "scatter_g0")
    dz0, dg_emb, db_emb = _ln_bwd(dh, hcat, None, ln_emb_g, 1.0, "ln_emb_bwd")
    grad_x = dz0[BLK:][None]
    dmeta = dz0[N_PAD:BLK]

    adam = [_adamw(parts[l], shards[l], _pack_layer(big_m, l), _pack_layer(big_v, l), "adamw_big")
            for l in range(depth)]

    st = lambda k: jnp.stack([gsmall[l][k] for l in range(depth)])
    g_small = (dg_emb, db_emb, st(0), st(1), st(2), st(3), st(4), st(5))
    n_small = sum(int(np.prod(a.shape)) for a in small)
    flat = jnp.concatenate([a.reshape(-1) for a in g_small] + [dmeta.reshape(-1), loss_part[0, 0:1]])
    rows = -(-(flat.shape[0]) // (8 * D)) * 8
    pad = rows * D - flat.shape[0]
    flat = jnp.concatenate([flat, jnp.zeros((pad,), f32)]).reshape(rows, D)
    parts_s = _exchange(flat, False, "gather_small")

    def pack_small(arrs, meta_shard):
        col = jnp.zeros((N_META, D), f32)
        col = lax.dynamic_update_slice(col, meta_shard, (0, me * 128))
        fl = jnp.concatenate([a.reshape(-1) for a in arrs] + [col.reshape(-1), jnp.zeros((1 + pad,), f32)])
        return fl.reshape(rows, D)

    g_s, d_s, m_s, v_s = _adamw(parts_s, pack_small(small, meta_tokens), pack_small(small_m, m_meta_tokens),
                                pack_small(small_v, v_meta_tokens), "adamw_small")
    loss = g_s.reshape(-1)[n_small + N_META * D]

    def unpack_big(which):
        outs, off = [], 0
        for w, r in zip(big, LAYER_ROWS):
            outs.append(jnp.stack([adam[l][which][off:off + r].reshape(w.shape[1:]) for l in range(depth)]))
            off += r
        return outs

    def unpack_small(flat_rows):
        fl = flat_rows.reshape(-1)
        outs, off = [], 0
        for a in small:
            n = int(np.prod(a.shape))
            outs.append(fl[off:off + n].reshape(a.shape))
            off += n
        meta = lax.dynamic_slice(fl[off:off + N_META * D].reshape(N_META, D), (0, me * 128), (N_META, 128))
        return meta, outs

    def assemble(which, small_rows_arr):
        b = unpack_big(which)
        meta, s = unpack_small(small_rows_arr)
        return [meta, s[0], s[1], b[0], s[2], s[3], b[1], b[2], b[3], s[4], s[5], b[4], b[5], s[6], s[7]]

    return (loss, grad_x, *assemble(0, g_s), *assemble(1, d_s), *assemble(2, m_s), *assemble(3, v_s))
```

```python
import functools
import math

import numpy as np
import jax
import jax.numpy as jnp
from jax import lax
from jax.experimental import pallas as pl
from jax.experimental.pallas import tpu as pltpu

f32 = jnp.float32
bf16 = jnp.bfloat16
_MXU = jnp.bfloat16

BLK = 128
N_META = 16
N_PAD = 112
D = 1024
N_DEV = 8
LN_EPS = 1e-5
DEPTH = 4
DN_ALPHA = (2 * DEPTH) ** 0.25
ROPE_THETA = 10000.0
MLA_SCALE = (64 + 32) ** -0.5
SB_SCALE = 0.125
RET_SCALE = 0.125
RET_GAMMA = tuple(1.0 - 2.0 ** (-5 - h) for h in range(4))

ADAM_LR, ADAM_B1, ADAM_B2, ADAM_EPS, ADAM_WD, ADAM_STEP = 0.001, 0.9, 0.999, 1e-08, 0.01, 10

C_SBQ, C_SBK, C_SBV, C_CKV, C_RQ, C_RK, C_RV, C_RG, C_CQ, C_KR, N_INP = (
    0, 512, 1024, 1536, 1792, 2048, 2304, 2816, 3328, 3712, 3840)

VMEM_LIMIT = 56 * 1024 * 1024


def _cp(sem):
    return pltpu.CompilerParams(dimension_semantics=sem, vmem_limit_bytes=VMEM_LIMIT)


def _pick(n, cands):
    for c in cands:
        if n % c == 0:
            return c
    return n


def _dot(a, b, dims=(((1,), (0,)), ((), ()))):
    return lax.dot_general(a.astype(_MXU), b.astype(_MXU), dims, preferred_element_type=f32)


NT = (((1,), (1,)), ((), ()))
TN = (((0,), (0,)), ((), ()))


def _dot3(x, u):
    hi = x.astype(_MXU)
    r1 = x - hi.astype(f32)
    mid = r1.astype(_MXU)
    lo = (r1 - mid.astype(f32)).astype(_MXU)
    return (jnp.dot(hi, u, preferred_element_type=f32) + jnp.dot(mid, u, preferred_element_type=f32)
            + jnp.dot(lo, u, preferred_element_type=f32))


def _rot(x, half):
    lane = lax.broadcasted_iota(jnp.int32, x.shape, 1)
    first = (lane % 64) < half
    return jnp.where(first, -pltpu.roll(x, 128 - half, 1), pltpu.roll(x, half, 1))


def _mm(a, b, mode, name, epi=None, extra=None, alpha=1.0, out_dtype=f32):
    if mode == "nn":
        (M, K), N = a.shape, b.shape[1]
    elif mode == "nt":
        (M, K), N = a.shape, b.shape[0]
    else:
        (K, M), N = a.shape, b.shape[1]
    tm = _pick(M, (1408, 1024, 768, 512, 384, 256, 128))
    tn = _pick(N, ((1920,) if mode == "tn" else ()) + (1024, 768, 512, 384, 256, 128))
    tk = _pick(K, (1024, 768, 512, 384, 256, 128))
    nk = K // tk
    if mode == "nn":
        a_spec = pl.BlockSpec((tm, tk), lambda i, j, k: (i, k))
        b_spec = pl.BlockSpec((tk, tn), lambda i, j, k: (k, j))
        dims = (((1,), (0,)), ((), ()))
    elif mode == "nt":
        a_spec = pl.BlockSpec((tm, tk), lambda i, j, k: (i, k))
        b_spec = pl.BlockSpec((tn, tk), lambda i, j, k: (j, k))
        dims = NT
    else:
        a_spec = pl.BlockSpec((tk, tm), lambda i, j, k: (k, i))
        b_spec = pl.BlockSpec((tk, tn), lambda i, j, k: (k, j))
        dims = TN
    o_spec = pl.BlockSpec((tm, tn), lambda i, j, k: (i, j))
    in_specs, args = [a_spec, b_spec], [a, b]
    if extra is not None:
        in_specs.append(o_spec)
        args.append(extra)
    if epi == "relu2":
        out_shape = (jax.ShapeDtypeStruct((M, N), f32), jax.ShapeDtypeStruct((M, N), bf16))
        out_specs = (o_spec, o_spec)
    else:
        out_shape = jax.ShapeDtypeStruct((M, N), out_dtype)
        out_specs = o_spec

    def body(*refs):
        a_ref, b_ref = refs[0], refs[1]
        acc = refs[-1]
        k = pl.program_id(2)

        @pl.when(k == 0)
        def _():
            acc[...] = jnp.zeros_like(acc)

        acc[...] += _dot(a_ref[...], b_ref[...], dims)

        @pl.when(k == nk - 1)
        def _():
            r = acc[...]
            if epi == "relu2":
                refs[2][...] = r
                refs[3][...] = jnp.square(jnp.maximum(r, 0.0)).astype(bf16)
            elif epi == "mul_relu":
                refs[3][...] = (r * (2.0 * jnp.maximum(refs[2][...], 0.0))).astype(out_dtype)
            elif epi == "add":
                refs[3][...] = r + alpha * refs[2][...]
            else:
                refs[2][...] = r.astype(out_dtype)

    return pl.pallas_call(
        body, name=name, grid=(M // tm, N // tn, nk), in_specs=in_specs, out_specs=out_specs,
        out_shape=out_shape, scratch_shapes=[pltpu.VMEM((tm, tn), f32)],
        compiler_params=_cp(("parallel", "parallel", "arbitrary")))(*args)


def _ln_fwd(h, m, g, b, alpha, name):
    L = h.shape[0]
    tm = _pick(L, (384, 256, 128))
    row = pl.BlockSpec((tm, D), lambda i: (i, 0))
    vec = pl.BlockSpec((1, D), lambda i: (0, 0))

    def body(*refs):
        if m is None:
            h_ref, g_ref, b_ref, o_ref, ob_ref = refs
            z = h_ref[...]
        else:
            h_ref, m_ref, g_ref, b_ref, o_ref, ob_ref = refs
            z = alpha * h_ref[...] + m_ref[...]
        mu = jnp.mean(z, -1, keepdims=True)
        var = jnp.mean(jnp.square(z - mu), -1, keepdims=True)
        y = (z - mu) * lax.rsqrt(var + LN_EPS) * g_ref[...] + b_ref[...]
        o_ref[...] = y
        ob_ref[...] = y.astype(bf16)

    args = [h] + ([] if m is None else [m]) + [g.reshape(1, D), b.reshape(1, D)]
    specs = [row] + ([] if m is None else [row]) + [vec, vec]
    return pl.pallas_call(body, name=name, grid=(L // tm,), in_specs=specs, out_specs=(row, row),
                          out_shape=(jax.ShapeDtypeStruct((L, D), f32), jax.ShapeDtypeStruct((L, D), bf16)),
                          compiler_params=_cp(("parallel",)))(*args)


def _ln_bwd(dy, h, m, g, alpha, name):
    L = h.shape[0]
    tm = _pick(L, (384, 256, 128))
    row = pl.BlockSpec((tm, D), lambda i: (i, 0))
    vec = pl.BlockSpec((1, D), lambda i: (0, 0))
    acc = pl.BlockSpec((8, D), lambda i: (0, 0))

    def body(*refs):
        if m is None:
            dy_ref, h_ref, g_ref, dz_ref, dg_ref, db_ref = refs
            z = h_ref[...]
        else:
            dy_ref, h_ref, m_ref, g_ref, dz_ref, dg_ref, db_ref = refs
            z = alpha * h_ref[...] + m_ref[...]

        @pl.when(pl.program_id(0) == 0)
        def _():
            dg_ref[...] = jnp.zeros_like(dg_ref)
            db_ref[...] = jnp.zeros_like(db_ref)

        dyv = dy_ref[...]
        mu = jnp.mean(z, -1, keepdims=True)
        zc = z - mu
        rstd = lax.rsqrt(jnp.mean(jnp.square(zc), -1, keepdims=True) + LN_EPS)
        xh = zc * rstd
        dxh = dyv * g_ref[...]
        dz_ref[...] = rstd * (dxh - jnp.mean(dxh, -1, keepdims=True) - xh * jnp.mean(dxh * xh, -1, keepdims=True))
        dg_ref[0:1, :] += jnp.sum(dyv * xh, 0, keepdims=True)
        db_ref[0:1, :] += jnp.sum(dyv, 0, keepdims=True)

    args = [dy, h] + ([] if m is None else [m]) + [g.reshape(1, D)]
    specs = [row, row] + ([] if m is None else [row]) + [vec]
    dz, dg, db = pl.pallas_call(
        body, name=name, grid=(L // tm,), in_specs=specs, out_specs=(row, acc, acc),
        out_shape=(jax.ShapeDtypeStruct((L, D), f32), jax.ShapeDtypeStruct((8, D), f32),
                   jax.ShapeDtypeStruct((8, D), f32)),
        compiler_params=_cp(("arbitrary",)))(*args)
    return dz, dg[0], db[0]


def _rms(x, g):
    r = lax.rsqrt(jnp.mean(jnp.square(x), -1, keepdims=True) + LN_EPS)
    return x * r * g


def _prep_fwd(proj, gq, gkv, tabs):
    L = proj.shape[0]
    tm = BLK
    rc, rs, mc, ms = tabs

    def body(p_ref, gq_ref, gkv_ref, rc_ref, rs_ref, mc_ref, ms_ref, sb_ref, cq_ref, ckv_ref, rqk_ref, rv_ref, kr_ref):
        i = pl.program_id(0)
        sb_ref[:, 0:512] = (p_ref[:, C_SBQ:C_SBQ + 512] * SB_SCALE).astype(bf16)
        sb_ref[:, 512:1536] = p_ref[:, C_SBK:C_SBK + 1024].astype(bf16)
        cq_ref[...] = _rms(p_ref[:, C_CQ:C_CQ + 384], gq_ref[...]).astype(bf16)
        ckv_ref[...] = _rms(p_ref[:, C_CKV:C_CKV + 256], gkv_ref[...]).astype(bf16)
        valid = (i * tm + lax.broadcasted_iota(jnp.int32, (tm, 128), 0)) >= N_PAD
        for c in range(2):
            sl = slice(c * 128, (c + 1) * 128)
            x = p_ref[:, C_RQ + c * 128:C_RQ + (c + 1) * 128]
            rqk_ref[:, sl] = (x * rc_ref[:, sl] + _rot(x, 32) * rs_ref[:, sl]).astype(bf16)
            x = p_ref[:, C_RK + c * 128:C_RK + (c + 1) * 128]
            kk = (x * rc_ref[:, sl] + _rot(x, 32) * rs_ref[:, sl]) * RET_SCALE
            rqk_ref[:, 256 + c * 128:256 + (c + 1) * 128] = jnp.where(valid, kk, 0.0).astype(bf16)
        rv_ref[...] = p_ref[:, C_RV:C_RV + 512].astype(bf16)
        x = p_ref[:, C_KR:C_KR + 128]
        kr_ref[...] = (x * mc_ref[...] + _rot(x, 16) * ms_ref[...]).astype(bf16)

    def row(w):
        return pl.BlockSpec((tm, w), lambda i: (i, 0))

    def vec(w):
        return pl.BlockSpec((1, w), lambda i: (0, 0))

    widths = (1536, 384, 256, 512, 512, 128)
    return pl.pallas_call(
        body, name="prep_fwd", grid=(L // tm,),
        in_specs=[row(N_INP), vec(384), vec(256), row(256), row(256), row(128), row(128)],
        out_specs=tuple(row(w) for w in widths),
        out_shape=tuple(jax.ShapeDtypeStruct((L, w), bf16) for w in widths),
        compiler_params=_cp(("parallel",)))(proj, gq.reshape(1, 384), gkv.reshape(1, 256), rc, rs, mc, ms)


def _rms_bwd(x, g, dy):
    r = lax.rsqrt(jnp.mean(jnp.square(x), -1, keepdims=True) + LN_EPS)
    u = dy * g
    dx = r * u - x * (r * r * r) * jnp.mean(x * u, -1, keepdims=True)
    return dx, jnp.sum(dy * x * r, 0, keepdims=True)


def _prep_bwd(proj, gq, gkv, tabs, dcqn, dckvn, drq_r, drk_r, dkr_r):
    L = proj.shape[0]
    tm = BLK
    rc, rs, mc, ms = tabs

    def body(p_ref, gq_ref, gkv_ref, rc_ref, rs_ref, mc_ref, ms_ref, dcqn_ref, dckvn_ref, drq_ref, drk_ref,
             dkr_ref, ocq_ref, ockv_ref, orq_ref, ork_ref, okr_ref, dgq_ref, dgkv_ref):
        i = pl.program_id(0)

        @pl.when(i == 0)
        def _():
            dgq_ref[...] = jnp.zeros_like(dgq_ref)
            dgkv_ref[...] = jnp.zeros_like(dgkv_ref)

        dx, dg = _rms_bwd(p_ref[:, C_CQ:C_CQ + 384], gq_ref[...], dcqn_ref[...])
        ocq_ref[...] = dx
        dgq_ref[0:1, :] += dg
        dx, dg = _rms_bwd(p_ref[:, C_CKV:C_CKV + 256], gkv_ref[...], dckvn_ref[...])
        ockv_ref[...] = dx
        dgkv_ref[0:1, :] += dg
        valid = (i * tm + lax.broadcasted_iota(jnp.int32, (tm, 128), 0)) >= N_PAD
        for c in range(2):
            sl = slice(c * 128, (c + 1) * 128)
            dy = drq_ref[:, sl]
            orq_ref[:, sl] = dy * rc_ref[:, sl] - _rot(dy * rs_ref[:, sl], 32)
            dy = jnp.where(valid, drk_ref[:, sl], 0.0) * RET_SCALE
            ork_ref[:, sl] = dy * rc_ref[:, sl] - _rot(dy * rs_ref[:, sl], 32)
        dy = dkr_ref[...]
        okr_ref[...] = dy * mc_ref[...] - _rot(dy * ms_ref[...], 16)

    def row(w):
        return pl.BlockSpec((tm, w), lambda i: (i, 0))

    def vec(w):
        return pl.BlockSpec((1, w), lambda i: (0, 0))

    def acc(w):
        return pl.BlockSpec((8, w), lambda i: (0, 0))

    widths = (384, 256, 256, 256, 128)
    outs = pl.pallas_call(
        body, name="prep_bwd", grid=(L // tm,),
        in_specs=[row(N_INP), vec(384), vec(256), row(256), row(256), row(128), row(128),
                  row(384), row(256), row(256), row(256), row(128)],
        out_specs=tuple(row(w) for w in widths) + (acc(384), acc(256)),
        out_shape=tuple(jax.ShapeDtypeStruct((L, w), f32) for w in widths)
        + (jax.ShapeDtypeStruct((8, 384), f32), jax.ShapeDtypeStruct((8, 256), f32)),
        compiler_params=_cp(("arbitrary",)))(
            proj, gq.reshape(1, 384), gkv.reshape(1, 256), rc, rs, mc, ms, dcqn, dckvn, drq_r, drk_r, dkr_r)
    return outs[:5] + (outs[5][0], outs[6][0])


def _qrows(L):
    return _pick(L, (384, 256, 128))


def _tri_ones(strict):
    r = lax.broadcasted_iota(jnp.int32, (BLK, 2 * BLK), 0)
    c = lax.broadcasted_iota(jnp.int32, (BLK, 2 * BLK), 1)
    tri = (r > c) if strict else (r >= c)
    return jnp.where(tri | (c >= BLK), 1.0, 0.0).astype(_MXU)


def _dot2(x, u):
    hi = x.astype(_MXU)
    lo = (x - hi.astype(f32)).astype(_MXU)
    return jnp.dot(hi, u, preferred_element_type=f32) + jnp.dot(lo, u, preferred_element_type=f32)


def _staggered(chains):
    live = list(chains)
    step = 0
    while live:
        for ci, g in enumerate(chains):
            if g in live and step >= ci and next(g, True):
                live.remove(g)
        step += 1


def _pair_rhs(t, m):
    zt = jnp.zeros_like(t)
    return jnp.concatenate([jnp.where(m, t, zt), jnp.where(m, zt, t)], axis=0)


def _sb_stages(z, mask, c_ref, x, u_gt, out):
    yield
    lb = jnp.minimum(z, 0.0) - jnp.log1p(jnp.exp(-jnp.abs(z)))
    lk = lb - z
    if mask is not None:
        lk = jnp.where(mask, lk, 0.0)
    hi = lk.astype(_MXU)
    lo = (lk - hi.astype(f32)).astype(_MXU)
    yield
    el = (jnp.dot(hi, u_gt, preferred_element_type=f32)
          + jnp.dot(lo, u_gt, preferred_element_type=f32))
    yield
    c = c_ref[x]
    w = jnp.exp(lb + el[:, 0:BLK] + c)
    if mask is not None:
        w = jnp.where(mask, w, 0.0)
    c_ref[x] = c + el[:, BLK:2 * BLK]
    out["w"], out["lb"] = w.astype(_MXU), lb


def _sb_sweep(i, r, tiles, per_pass):
    n_t = r * (i + 1)
    tiles([n_t - 1 - t for t in range(r)], True)
    n_bulk = jnp.maximum(r * i - 1, 0)
    n_full = n_bulk // per_pass
    lax.fori_loop(0, n_full,
                  lambda jj, c: tiles([r * i - 1 - per_pass * jj - t for t in range(per_pass)], False) or c, 0)
    rem = n_bulk - n_full * per_pass

    if per_pass == 4:
        @pl.when(rem >= 2)
        def _():
            tiles([rem, rem - 1], False)

    @pl.when(rem % 2 == 1)
    def _():
        tiles([1], False)

    @pl.when(i > 0)
    def _():
        tiles([0], True)


def _tile_off(j):
    return j * BLK if isinstance(j, int) else pl.multiple_of(j * BLK, BLK)


def _sb_mask(i, j, qb):
    row = i * qb + lax.broadcasted_iota(jnp.int32, (qb, BLK), 0)
    col = j * BLK + lax.broadcasted_iota(jnp.int32, (qb, BLK), 1)
    return (col < row) & (col >= N_PAD)


def _first_last(n0, n1):
    p, i = pl.program_id(0), pl.program_id(1)
    return (p == 0) & (i == 0), (p == n0 - 1) & (i == n1 - 1)


def _sb_fwd(sb, bg=None):
    L = sb.shape[0]
    qb = _qrows(L)
    nq, r = L // qb, qb // BLK

    def body(*refs):
        if bg is None:
            q_ref, k_ref, v_ref, o_ref, acc_ref, c_ref = refs
        else:
            q_ref, k_ref, v_ref, x_ref, o_ref, g_ref, acc_ref, c_ref = refs[:8]
            start, wait = _xchg_ops(x_ref, g_ref, *refs[8:], False)
            first, last = _first_last(4, nq)
            pl.when(first)(start)
        i = pl.program_id(1)
        m_a = lax.broadcasted_iota(jnp.int32, (1, BLK), 1) < 64
        q = q_ref[...]
        u_gt = _tri_ones(True)
        acc_ref[...] = jnp.zeros_like(acc_ref)
        c_ref[...] = jnp.zeros_like(c_ref)

        def chain(x, j, masked, both):
            off = _tile_off(j)
            if x == 0:
                both["z"] = _dot(q, _pair_rhs(k_ref[pl.ds(off, BLK), :], m_a), NT)
            mask = _sb_mask(i, j, qb) if masked else None
            o = {}
            yield from _sb_stages(both["z"][:, x * BLK:(x + 1) * BLK], mask, c_ref, x, u_gt, o)
            yield
            acc_ref[x] += _dot(o["w"], v_ref[pl.ds(off, BLK), :])

        def tiles(js, masked):
            shared = [{} for _ in js]
            _staggered([chain(x, j, masked, shared[t]) for t, j in enumerate(js) for x in range(2)])

        _sb_sweep(i, r, tiles, 4)
        o_ref[...] = jnp.where(m_a, acc_ref[0], acc_ref[1])
        if bg is not None:
            pl.when(last)(wait)

    in_specs = [pl.BlockSpec((qb, 128), lambda p, i: (i, p)),
                pl.BlockSpec((L, 128), lambda p, i: (0, 4 + p)),
                pl.BlockSpec((L, 128), lambda p, i: (0, 8 + p))]
    o_spec = pl.BlockSpec((qb, 128), lambda p, i: (i, p))
    o_shape = jax.ShapeDtypeStruct((L, 512), f32)
    scratch = [pltpu.VMEM((2, qb, BLK), f32), pltpu.VMEM((2, qb, BLK), f32)]
    if bg is None:
        return pl.pallas_call(body, name="sb_fwd", grid=(4, nq), in_specs=in_specs, out_specs=o_spec,
                              out_shape=o_shape, scratch_shapes=scratch,
                              compiler_params=_cp(("parallel", "arbitrary")))(sb, sb, sb)
    return pl.pallas_call(body, name="sb_fwd_gather", grid=(4, nq), in_specs=in_specs + [_HBM],
                          out_specs=(o_spec, _HBM), out_shape=(o_shape, _xchg_shape(bg, False)),
                          scratch_shapes=scratch + list(_XCHG_SEMS),
                          compiler_params=_cp(("arbitrary", "arbitrary")))(sb, sb, sb, bg)


def _sb_bwd(dmixed, sb, out_a, bg=None):
    L = sb.shape[0]
    qb = _qrows(L)
    nq, r = L // qb, qb // BLK

    def body(*refs):
        if bg is None:
            do_ref, o_ref, q_ref, k_ref, v_ref, dq_ref, dk_ref, dv_ref, dqa_ref, c_ref, cg_ref, ds_ref = refs
        else:
            do_ref, o_ref, q_ref, k_ref, v_ref, x_ref, dq_ref, dk_ref, dv_ref, g_ref = refs[:10]
            dqa_ref, c_ref, cg_ref, ds_ref = refs[10:14]
            start, wait = _xchg_ops(x_ref, g_ref, *refs[14:], True)
            first, last = _first_last(4, nq)
            pl.when(first)(start)
        i = pl.program_id(1)

        @pl.when(i == 0)
        def _():
            dk_ref[...] = jnp.zeros_like(dk_ref)
            dv_ref[...] = jnp.zeros_like(dv_ref)

        m_a = lax.broadcasted_iota(jnp.int32, (1, BLK), 1) < 64
        q = q_ref[...]
        zq = jnp.zeros_like(q)
        qs = (jnp.where(m_a, q, zq), jnp.where(m_a, zq, q))
        do = do_ref[...]
        zd = jnp.zeros_like(do)
        dos = (jnp.where(m_a, do, zd).astype(_MXU), jnp.where(m_a, zd, do).astype(_MXU))
        do_b = do.astype(_MXU)
        prod = do_b.astype(f32) * o_ref[...]
        ds_ref[0] = jnp.broadcast_to(jnp.sum(jnp.where(m_a, prod, 0.0), 1, keepdims=True), (qb, BLK))
        ds_ref[1] = jnp.broadcast_to(jnp.sum(jnp.where(m_a, 0.0, prod), 1, keepdims=True), (qb, BLK))
        u_gt = _tri_ones(True)
        u_ge = _tri_ones(False)
        dqa_ref[...] = jnp.zeros_like(dqa_ref)
        c_ref[...] = jnp.zeros_like(c_ref)
        cg_ref[...] = jnp.zeros_like(cg_ref)

        def chain(x, j, masked, both):
            off = _tile_off(j)
            k = k_ref[pl.ds(off, BLK), :]
            if x == 0:
                both["kk"] = _pair_rhs(k, m_a)
                both["z"] = _dot(q, both["kk"], NT)
                both["dw"] = _dot(do_b, _pair_rhs(v_ref[pl.ds(off, BLK), :], m_a), NT)
            mask = _sb_mask(i, j, qb) if masked else None
            o = {}
            yield from _sb_stages(both["z"][:, x * BLK:(x + 1) * BLK], mask, c_ref, x, u_gt, o)
            wb = o["w"]
            gr = wb.astype(f32) * both["dw"][:, x * BLK:(x + 1) * BLK]
            hi = gr.astype(_MXU)
            lo = (gr - hi.astype(f32)).astype(_MXU)
            yield
            eg = (jnp.dot(hi, u_ge, preferred_element_type=f32)
                  + jnp.dot(lo, u_ge, preferred_element_type=f32))
            yield
            cg = cg_ref[x]
            suffix = eg[:, 0:BLK] + cg
            cg_ref[x] = cg + eg[:, BLK:2 * BLK]
            dz = gr - jnp.exp(o["lb"]) * (gr + ds_ref[x] - suffix)
            if masked:
                dz = jnp.where(mask, dz, 0.0)
            dz = dz.astype(_MXU)
            yield
            if x == 0:
                both["dz"] = dz
            else:
                dqa_ref[0] += _dot(jnp.concatenate([both["dz"], dz], axis=1), both["kk"])
            dk_ref[pl.ds(off, BLK), :] += _dot(dz, qs[x], TN)
            dv_ref[pl.ds(off, BLK), :] += _dot(wb, dos[x], TN)

        def tiles(js, masked):
            shared = [{} for _ in js]
            _staggered([chain(x, j, masked, shared[t]) for t, j in enumerate(js) for x in range(2)])

        _sb_sweep(i, r, tiles, 4)
        dq_ref[...] = dqa_ref[0] * SB_SCALE
        if bg is not None:
            pl.when(last)(wait)

    blk = pl.BlockSpec((qb, 128), lambda p, i: (i, p))
    scr = pltpu.VMEM((2, qb, BLK), f32)
    in_specs = [blk, blk, blk, pl.BlockSpec((L, 128), lambda p, i: (0, 4 + p)),
                pl.BlockSpec((L, 128), lambda p, i: (0, 8 + p))]
    out_specs = (blk, pl.BlockSpec((L, 128), lambda p, i: (0, p)), pl.BlockSpec((L, 128), lambda p, i: (0, p)))
    out_shape = tuple(jax.ShapeDtypeStruct((L, 512), f32) for _ in range(3))
    if bg is None:
        return pl.pallas_call(body, name="sb_bwd", grid=(4, nq), in_specs=in_specs, out_specs=out_specs,
                              out_shape=out_shape, scratch_shapes=[scr, scr, scr, scr],
                              compiler_params=_cp(("parallel", "arbitrary")))(dmixed, out_a, sb, sb, sb)
    return pl.pallas_call(body, name="sb_bwd_scatter", grid=(4, nq), in_specs=in_specs + [_HBM],
                          out_specs=out_specs + (_HBM,), out_shape=out_shape + (_xchg_shape(bg, True),),
                          scratch_shapes=[scr, scr, scr, scr] + list(_XCHG_SEMS),
                          compiler_params=_cp(("arbitrary", "arbitrary")))(dmixed, out_a, sb, sb, sb, bg)


def _mla_mask(i, j, qb):
    row = i * qb + lax.broadcasted_iota(jnp.int32, (qb, BLK), 0)
    col = j * BLK + lax.broadcasted_iota(jnp.int32, (qb, BLK), 1)
    return (col <= row) & ((col >= N_PAD) | (col == row))


def _pair_mask2():
    return (lax.broadcasted_iota(jnp.int32, (1, 256), 1) % 128) < 64


def _mla_q2(qn_ref, qr_ref, mc_ref, ms_ref):
    qr = qr_ref[...]
    qr = qr * mc_ref[...] + _rot(qr, 16) * ms_ref[...]
    q2 = jnp.concatenate([qn_ref[...], qr], axis=1)
    m2 = _pair_mask2()
    z2 = jnp.zeros_like(q2)
    return (jnp.where(m2, q2, z2).astype(_MXU), jnp.where(m2, z2, q2).astype(_MXU)), q2.astype(_MXU)


def _mla_fwd(q, kv, kr2, mc, ms):
    L = q.shape[0]
    qb = _qrows(L)
    nq, r = L // qb, qb // BLK

    def body(qn_ref, qr_ref, mc_ref, ms_ref, kn_ref, v_ref, kr_ref, o_ref, lse_ref, acc_ref, m_ref):
        i = pl.program_id(1)
        m_a = lax.broadcasted_iota(jnp.int32, (1, BLK), 1) < 64
        _, q2 = _mla_q2(qn_ref, qr_ref, mc_ref, ms_ref)
        m2 = _pair_mask2()
        acc_ref[...] = jnp.zeros_like(acc_ref)
        m_ref[...] = jnp.full(m_ref.shape, -1e30, f32)
        ones = jnp.ones((BLK, BLK), _MXU)

        def chain(x, j, masked, both):
            off = _tile_off(j)
            if x == 0:
                k2 = jnp.concatenate([kn_ref[pl.ds(off, BLK), :], kr_ref[pl.ds(off, BLK), :]], axis=1)
                both["s"] = _dot(q2, _pair_rhs(k2, m2), NT)
            yield
            s = both["s"][:, x * BLK:(x + 1) * BLK] * MLA_SCALE
            if masked:
                mask = _mla_mask(i, j, qb)
                s = jnp.where(mask, s, -1e30)
            m_old = m_ref[x]
            m_new = jnp.maximum(m_old, jnp.max(s, 1, keepdims=True))
            a = jnp.exp(m_old - m_new)
            p = jnp.exp(s - m_new)
            if masked:
                p = jnp.where(mask, p, 0.0)
            m_ref[x] = m_new
            p = p.astype(_MXU)
            yield
            v1 = jnp.concatenate([v_ref[pl.ds(off, BLK), :], ones], axis=1)
            acc_ref[x] = jnp.concatenate([a, a], axis=1) * acc_ref[x] + _dot(p, v1)

        def tiles(js, masked):
            shared = [{} for _ in js]
            _staggered([chain(x, j, masked, shared[t]) for t, j in enumerate(js) for x in range(2)])

        _sb_sweep(i, r, tiles, 4)
        o_ref[...] = jnp.where(m_a, acc_ref[0, :, 0:BLK] / acc_ref[0, :, BLK:2 * BLK],
                               acc_ref[1, :, 0:BLK] / acc_ref[1, :, BLK:2 * BLK])
        for x in range(2):
            lse_ref[0, x] = m_ref[x] + jnp.log(acc_ref[x, :, BLK:2 * BLK])

    blk = lambda cb: pl.BlockSpec((qb, 128), lambda p, i: (i, cb + p))
    full = lambda cb: pl.BlockSpec((L, 128), lambda p, i: (0, cb + p))
    tab = pl.BlockSpec((qb, 128), lambda p, i: (i, 0))
    return pl.pallas_call(
        body, name="mla_fwd", grid=(4, nq),
        in_specs=[blk(0), blk(4), tab, tab, full(0), full(4), pl.BlockSpec((L, 128), lambda p, i: (0, 0))],
        out_specs=(blk(0), pl.BlockSpec((1, 2, qb, 128), lambda p, i: (p, 0, i, 0))),
        out_shape=(jax.ShapeDtypeStruct((L, 512), f32), jax.ShapeDtypeStruct((4, 2, L, 128), f32)),
        scratch_shapes=[pltpu.VMEM((2, qb, 2 * BLK), f32), pltpu.VMEM((2, qb, BLK), f32)],
        compiler_params=_cp(("parallel", "arbitrary")))(q, q, mc, ms, kv, kv, kr2)


def _mla_bwd(dmixed, q, kv, kr2, out_b, lse, mc, ms):
    L = q.shape[0]
    qb = _qrows(L)
    nq, r = L // qb, qb // BLK

    def body(do_ref, o_ref, lse_ref, qn_ref, qr_ref, mc_ref, ms_ref, kn_ref, v_ref, kr_ref,
             dqn_ref, dqr_ref, dkn_ref, dv_ref, dkr_ref, dqa_ref, ds_ref):
        i = pl.program_id(1)

        @pl.when(i == 0)
        def _():
            dkn_ref[...] = jnp.zeros_like(dkn_ref)
            dv_ref[...] = jnp.zeros_like(dv_ref)
            dkr_ref[...] = jnp.zeros_like(dkr_ref)

        m_a = lax.broadcasted_iota(jnp.int32, (1, BLK), 1) < 64
        qs, q2 = _mla_q2(qn_ref, qr_ref, mc_ref, ms_ref)
        m2 = _pair_mask2()
        do = do_ref[...]
        zd = jnp.zeros_like(do)
        dos = (jnp.where(m_a, do, zd).astype(_MXU), jnp.where(m_a, zd, do).astype(_MXU))
        do_b = do.astype(_MXU)
        prod = do * o_ref[...]
        ds_ref[0] = jnp.broadcast_to(jnp.sum(jnp.where(m_a, prod, 0.0), 1, keepdims=True), (qb, BLK))
        ds_ref[1] = jnp.broadcast_to(jnp.sum(jnp.where(m_a, 0.0, prod), 1, keepdims=True), (qb, BLK))
        dqa_ref[...] = jnp.zeros_like(dqa_ref)

        def chain(x, j, masked, both):
            off = _tile_off(j)
            if x == 0:
                k2 = jnp.concatenate([kn_ref[pl.ds(off, BLK), :], kr_ref[pl.ds(off, BLK), :]], axis=1)
                both["kk"] = _pair_rhs(k2, m2)
                both["s"] = _dot(q2, both["kk"], NT)
                both["dp"] = _dot(do_b, _pair_rhs(v_ref[pl.ds(off, BLK), :], m_a), NT)
            yield
            s = both["s"][:, x * BLK:(x + 1) * BLK] * MLA_SCALE
            dp = both["dp"][:, x * BLK:(x + 1) * BLK]
            if masked:
                mask = _mla_mask(i, j, qb)
                p = jnp.where(mask, jnp.exp(jnp.where(mask, s, 0.0) - lse_ref[0, x]), 0.0)
            else:
                p = jnp.exp(s - lse_ref[0, x])
            pb = p.astype(_MXU)
            ds = (p * (dp - ds_ref[x]) * MLA_SCALE).astype(_MXU)
            yield
            if x == 0:
                both["ds"] = ds
            else:
                dqa_ref[0] += _dot(jnp.concatenate([both["ds"], ds], axis=1), both["kk"])
            dk_t = _dot(ds, qs[x], TN)
            dkn_ref[pl.ds(off, BLK), :] += dk_t[:, 0:128]
            dkr_ref[0, pl.ds(off, BLK), :] += dk_t[:, 128:256]
            dv_ref[pl.ds(off, BLK), :] += _dot(pb, dos[x], TN)

        def tiles(js, masked):
            shared = [{} for _ in js]
            _staggered([chain(x, j, masked, shared[t]) for t, j in enumerate(js) for x in range(2)])

        _sb_sweep(i, r, tiles, 4)
        dq2 = dqa_ref[0]
        dqn_ref[...] = dq2[:, 0:128]
        dy = dq2[:, 128:256]
        dqr_ref[...] = dy * mc_ref[...] - _rot(dy * ms_ref[...], 16)

    blk = lambda cb: pl.BlockSpec((qb, 128), lambda p, i: (i, cb + p))
    full = lambda cb: pl.BlockSpec((L, 128), lambda p, i: (0, cb + p))
    tab = pl.BlockSpec((qb, 128), lambda p, i: (i, 0))
    o512 = jax.ShapeDtypeStruct((L, 512), f32)
    return pl.pallas_call(
        body, name="mla_bwd", grid=(4, nq),
        in_specs=[blk(4), blk(0), pl.BlockSpec((1, 2, qb, 128), lambda p, i: (p, 0, i, 0)), blk(0), blk(4), tab, tab,
                  full(0), full(4), pl.BlockSpec((L, 128), lambda p, i: (0, 0))],
        out_specs=(blk(0), blk(0), full(0), full(0), pl.BlockSpec((1, L, 128), lambda p, i: (p, 0, 0))),
        out_shape=(o512, o512, o512, o512, jax.ShapeDtypeStruct((4, L, 128), f32)),
        scratch_shapes=[pltpu.VMEM((2, qb, 2 * BLK), f32), pltpu.VMEM((2, qb, BLK), f32)],
        compiler_params=_cp(("parallel", "arbitrary")))(dmixed, out_b, lse, q, q, mc, ms, kv, kv, kr2)


def _ret_tables():
    log_g = jnp.log(jnp.array(RET_GAMMA, f32))
    idx = jnp.arange(BLK, dtype=f32)
    diff = idx[:, None] - idx[None, :]
    d_in = jnp.where(diff[None] >= 0, jnp.exp(jnp.maximum(diff, 0.0)[None] * log_g[:, None, None]), 0.0)
    q_dec = jnp.exp((idx[None, :] + 1.0) * log_g[:, None])
    k_dec = jnp.exp((BLK - 1.0 - idx[None, :]) * log_g[:, None])
    c_dec = jnp.exp(BLK * log_g)
    bc = lambda a: jnp.broadcast_to(a[:, :, None], (4, BLK, BLK))
    return d_in, bc(q_dec), bc(k_dec), jnp.broadcast_to(c_dec[:, None, None], (4, 8, BLK))


def _head_mask(x):
    lane = lax.broadcasted_iota(jnp.int32, (1, BLK), 1)
    return (lane < 64) if x == 0 else (lane >= 64)


def _ret_fwd(rqk, rv, proj, rtabs):
    L = rqk.shape[0]
    n = L // BLK
    d_in, q_dec, k_dec, c_dec = rtabs

    def body(q_ref, k_ref, v_ref, g_ref, din_ref, qd_ref, kd_ref, cd_ref, y_ref, o_ref, st_ref, s_scr):
        @pl.when(pl.program_id(1) == 0)
        def _():
            s_scr[...] = jnp.zeros_like(s_scr)

        q = q_ref[...]
        k = k_ref[...]
        zq = jnp.zeros_like(q)
        for x in range(2):
            hm = _head_mask(x)
            sl = slice(x * 128, (x + 1) * 128)
            qm = jnp.where(hm, q, zq)
            km = jnp.where(hm, k, zq)
            v = v_ref[:, sl]
            s_in = s_scr[x]
            st_ref[0, 0, x] = s_in
            inner = _dot(qm, km, NT) * din_ref[x]
            y = _dot(inner, v) + _dot(qm, s_in) * qd_ref[x]
            s_scr[x] = s_in * cd_ref[x, 0:1, :] + _dot(km.astype(f32) * kd_ref[x], v, TN)
            y_ref[:, sl] = y
            mu = jnp.mean(y, -1, keepdims=True)
            yc = y - mu
            yn = yc * lax.rsqrt(jnp.mean(jnp.square(yc), -1, keepdims=True) + LN_EPS)
            g = g_ref[:, sl]
            o_ref[:, sl] = g * jax.nn.sigmoid(g) * yn

    tab = pl.BlockSpec((2, BLK, BLK), lambda p, i: (p, 0, 0))
    return pl.pallas_call(
        body, name="ret_fwd", grid=(2, n),
        in_specs=[pl.BlockSpec((BLK, 128), lambda p, i: (i, p)), pl.BlockSpec((BLK, 128), lambda p, i: (i, 2 + p)),
                  pl.BlockSpec((BLK, 256), lambda p, i: (i, p)),
                  pl.BlockSpec((BLK, 256), lambda p, i: (i, C_RG // 256 + p)),
                  tab, tab, tab, pl.BlockSpec((2, 8, BLK), lambda p, i: (p, 0, 0))],
        out_specs=(pl.BlockSpec((BLK, 256), lambda p, i: (i, p)), pl.BlockSpec((BLK, 256), lambda p, i: (i, p)),
                   pl.BlockSpec((1, 1, 2, BLK, BLK), lambda p, i: (p, i, 0, 0, 0))),
        out_shape=(jax.ShapeDtypeStruct((L, 512), f32), jax.ShapeDtypeStruct((L, 512), f32),
                   jax.ShapeDtypeStruct((2, n, 2, BLK, BLK), f32)),
        scratch_shapes=[pltpu.VMEM((2, BLK, BLK), f32)],
        compiler_params=_cp(("parallel", "arbitrary")))(rqk, rqk, rv, proj, d_in, q_dec, k_dec, c_dec)


def _ret_bwd(dmixed, rqk, rv, proj, y, states, rtabs):
    L = rqk.shape[0]
    n = L // BLK
    d_in, q_dec, k_dec, c_dec = rtabs

    def body(do_ref, q_ref, k_ref, v_ref, g_ref, y_ref, st_ref, din_ref, qd_ref, kd_ref, cd_ref,
             dq_ref, dk_ref, dv_ref, dg_ref, ds_scr):
        @pl.when(pl.program_id(1) == 0)
        def _():
            ds_scr[...] = jnp.zeros_like(ds_scr)

        q = q_ref[...]
        k = k_ref[...]
        zq = jnp.zeros_like(q)
        dq_acc = jnp.zeros((BLK, BLK), f32)
        dk_acc = jnp.zeros((BLK, BLK), f32)
        for x in range(2):
            hm = _head_mask(x)
            sl = slice(x * 128, (x + 1) * 128)
            qm = jnp.where(hm, q, zq)
            km = jnp.where(hm, k, zq)
            v = v_ref[:, sl]
            yv = y_ref[:, sl]
            g = g_ref[:, sl]
            do = do_ref[:, sl]
            mu = jnp.mean(yv, -1, keepdims=True)
            yc = yv - mu
            rstd = lax.rsqrt(jnp.mean(jnp.square(yc), -1, keepdims=True) + LN_EPS)
            yn = yc * rstd
            sg = jax.nn.sigmoid(g)
            dg_ref[:, sl] = do * yn * sg * (1.0 + g * (1.0 - sg))
            dyn = do * g * sg
            dy = rstd * (dyn - jnp.mean(dyn, -1, keepdims=True) - yn * jnp.mean(dyn * yn, -1, keepdims=True))
            s_in = st_ref[0, 0, x]
            ds_out = ds_scr[x]
            kd = km.astype(f32) * kd_ref[x]
            a = _dot(qm, km, NT) * din_ref[x]
            da = _dot(dy, v, NT) * din_ref[x]
            dyq = dy * qd_ref[x]
            dq_acc += _dot(da, km) + _dot(dyq, s_in, NT)
            dk_acc += _dot(da, qm, TN) + _dot(v, ds_out, NT) * kd_ref[x]
            dv_ref[:, sl] = _dot(a, dy, TN) + _dot(kd, ds_out)
            ds_scr[x] = ds_out * cd_ref[x, 0:1, :] + _dot(qm, dyq, TN)
        dq_ref[...] = dq_acc
        dk_ref[...] = dk_acc

    rev = lambda w, cb: pl.BlockSpec((BLK, w), lambda p, i: (n - 1 - i, cb + p))
    tab = pl.BlockSpec((2, BLK, BLK), lambda p, i: (p, 0, 0))
    return pl.pallas_call(
        body, name="ret_bwd", grid=(2, n),
        in_specs=[rev(256, 4), rev(128, 0), rev(128, 2), rev(256, 0), rev(256, C_RG // 256), rev(256, 0),
                  pl.BlockSpec((1, 1, 2, BLK, BLK), lambda p, i: (p, n - 1 - i, 0, 0, 0)),
                  tab, tab, tab, pl.BlockSpec((2, 8, BLK), lambda p, i: (p, 0, 0))],
        out_specs=(rev(128, 0), rev(128, 0), rev(256, 0), rev(256, 0)),
        out_shape=(jax.ShapeDtypeStruct((L, 256), f32), jax.ShapeDtypeStruct((L, 256), f32),
                   jax.ShapeDtypeStruct((L, 512), f32), jax.ShapeDtypeStruct((L, 512), f32)),
        scratch_shapes=[pltpu.VMEM((2, BLK, BLK), f32)],
        compiler_params=_cp(("parallel", "arbitrary")))(dmixed, rqk, rqk, rv, proj, y, states, d_in, q_dec, k_dec, c_dec)


def _loss_head(h, target):
    L = h.shape[0]
    n = L // BLK

    def body(h_ref, t_ref, dy_ref, l_ref):
        i = pl.program_id(0)

        @pl.when(i == 0)
        def _():
            dy_ref[...] = jnp.zeros_like(dy_ref)
            l_ref[...] = jnp.zeros_like(l_ref)

        @pl.when(i > 0)
        def _():
            err = h_ref[...] - t_ref[...]
            dy_ref[...] = err * (1.0 / D)
            sq = jnp.sum(jnp.sum(jnp.square(err), 1, keepdims=True), 0, keepdims=True)
            l_ref[...] += (0.5 / D) * sq

    return pl.pallas_call(
        body, name="loss_head", grid=(n,),
        in_specs=[pl.BlockSpec((BLK, D), lambda i: (i, 0)),
                  pl.BlockSpec((BLK, D), lambda i: (jnp.maximum(i - 1, 0), 0))],
        out_specs=(pl.BlockSpec((BLK, D), lambda i: (i, 0)), pl.BlockSpec((8, 128), lambda i: (0, 0))),
        out_shape=(jax.ShapeDtypeStruct((L, D), f32), jax.ShapeDtypeStruct((8, 128), f32)),
        compiler_params=_cp(("arbitrary",)))(h, target)


def _adam_math(w, g, m, v):
    m = ADAM_B1 * m + (1.0 - ADAM_B1) * g
    v = ADAM_B2 * v + (1.0 - ADAM_B2) * jnp.square(g)
    m_hat = m / (1.0 - ADAM_B1 ** ADAM_STEP)
    v_hat = v / (1.0 - ADAM_B2 ** ADAM_STEP)
    delta = -ADAM_LR * (m_hat / (jnp.sqrt(v_hat) + ADAM_EPS) + ADAM_WD * w)
    return delta, m, v


def _adamw(parts, w, m, v, name):
    R, C = w.shape
    tr = _pick(R, (240, 192, 144, 96, 64, 48, 32, 16, 8))
    row = pl.BlockSpec((tr, C), lambda i: (i, 0))

    def body(p_ref, w_ref, m_ref, v_ref, g_ref, d_ref, nm_ref, nv_ref):
        g = p_ref[0].astype(f32)
        for k in range(1, N_DEV):
            g = g + p_ref[k].astype(f32)
        d, nm, nv = _adam_math(w_ref[...], g, m_ref[...], v_ref[...])
        g_ref[...] = g
        d_ref[...] = d
        nm_ref[...] = nm
        nv_ref[...] = nv

    o = jax.ShapeDtypeStruct((R, C), f32)
    return pl.pallas_call(
        body, name=name, grid=(R // tr,),
        in_specs=[pl.BlockSpec((N_DEV, tr, C), lambda i: (0, i, 0)), row, row, row],
        out_specs=(row, row, row, row), out_shape=(o, o, o, o),
        compiler_params=_cp(("parallel",)))(parts, w, m, v)


_XCHG_SEMS = [pltpu.SemaphoreType.DMA((N_DEV - 1,)), pltpu.SemaphoreType.DMA((N_DEV - 1,)), pltpu.SemaphoreType.DMA]
_HBM = pl.BlockSpec(memory_space=pltpu.HBM)


def _xchg_shape(x, all_to_all):
    return jax.ShapeDtypeStruct((N_DEV,) + tuple(x.shape[1:] if all_to_all else x.shape), x.dtype)


def _xchg_ops(x_ref, o_ref, send_sems, recv_sems, local_sem, all_to_all):
    mx, my, mc = lax.axis_index("x"), lax.axis_index("y"), lax.axis_index("c")
    me = 4 * mx + 2 * my + mc

    def peer(k):
        px = (1 - mx) if k & 4 else mx
        py = (1 - my) if k & 2 else my
        pc = (1 - mc) if k & 1 else mc
        return (px, py, pc), 4 * px + 2 * py + pc

    def copy(k):
        dev, idx = peer(k)
        src = x_ref.at[idx] if all_to_all else x_ref
        return pltpu.make_async_remote_copy(
            src_ref=src, dst_ref=o_ref.at[me], send_sem=send_sems.at[k - 1], recv_sem=recv_sems.at[k - 1],
            device_id=dev, device_id_type=pl.DeviceIdType.MESH)

    def start():
        pltpu.make_async_copy(x_ref.at[me] if all_to_all else x_ref, o_ref.at[me], local_sem).start()
        for k in range(1, N_DEV):
            copy(k).start()

    def wait():
        for k in range(1, N_DEV):
            dev, idx = peer(k)
            pltpu.make_async_remote_copy(
                src_ref=o_ref.at[idx], dst_ref=o_ref.at[idx], send_sem=send_sems.at[k - 1],
                recv_sem=recv_sems.at[k - 1], device_id=dev, device_id_type=pl.DeviceIdType.MESH).wait_recv()
        for k in range(1, N_DEV):
            copy(k).wait_send()
        pltpu.make_async_copy(x_ref.at[me] if all_to_all else x_ref, o_ref.at[me], local_sem).wait()

    return start, wait


def _exchange(x, all_to_all, name):
    def body(x_ref, o_ref, send_sems, recv_sems, local_sem):
        start, wait = _xchg_ops(x_ref, o_ref, send_sems, recv_sems, local_sem, all_to_all)
        start()
        wait()

    return pl.pallas_call(body, name=name, in_specs=[_HBM], out_specs=_HBM, out_shape=_xchg_shape(x, all_to_all),
                          scratch_shapes=list(_XCHG_SEMS))(x)


WC = 512
LAYER_ROWS = tuple(n // WC for n in (1024 * 468, 384 * 96, 256 * 128, 192 * 1024, 1024 * 512, 512 * 1024))


def _pack_layer(ws, l):
    return jnp.concatenate([w[l].reshape(-1, WC) for w in ws], axis=0)


def _unpack_layer(gathered):
    offs = np.cumsum((0,) + LAYER_ROWS)
    part = lambda k: gathered[:, offs[k]:offs[k + 1]]
    w_in = part(0).reshape(8, 1024, 468).transpose(1, 0, 2).reshape(1024, 3744)
    z32 = jnp.zeros((1024, 32), w_in.dtype)
    kr = w_in[:, 2176:2208]
    w_in = jnp.concatenate([w_in[:, 0:1536], w_in[:, 1920:2176], w_in[:, 2208:3744], w_in[:, 1536:1920],
                            kr, z32, kr, z32], axis=1)
    w_uq = part(1).reshape(8, 384, 96).transpose(1, 0, 2)
    rope = jnp.concatenate([w_uq[..., 64:96], jnp.zeros((384, 8, 32), w_uq.dtype)], axis=-1)
    w_uq = jnp.concatenate([w_uq[..., 0:64].reshape(384, 512), rope.reshape(384, 512)], axis=1)
    w_ukv = part(2).reshape(8, 256, 128).transpose(1, 0, 2)
    w_ukv = jnp.concatenate([w_ukv[..., 0:64].reshape(256, 512), w_ukv[..., 64:128].reshape(256, 512)], axis=1)
    w_out = part(3).reshape(1536, 1024)
    w_ff1 = part(4).reshape(8, 1024, 512).transpose(1, 0, 2).reshape(1024, 4096)
    w_ff2 = part(5).reshape(4096, 1024)
    return w_in, w_uq, w_ukv, w_out, w_ff1, w_ff2


def _pack_layer_grads(g_in, g_uq, g_ukv, g_out, g_ff1, g_ff2):
    kr = g_in[:, C_KR:C_KR + 32] + g_in[:, C_KR + 64:C_KR + 96]
    g_in = jnp.concatenate([g_in[:, 0:1536], g_in[:, C_CQ:C_CQ + 384], g_in[:, C_CKV:C_CKV + 256], kr,
                            g_in[:, C_RQ:C_CQ]], axis=1)
    g_in = g_in.reshape(1024, 8, 468).transpose(1, 0, 2)
    g_uq = jnp.concatenate([g_uq[:, 0:512].reshape(384, 8, 64), g_uq[:, 512:1024].reshape(384, 8, 64)[..., 0:32]],
                           axis=-1).transpose(1, 0, 2)
    g_ukv = jnp.concatenate([g_ukv[:, 0:512].reshape(256, 8, 64), g_ukv[:, 512:1024].reshape(256, 8, 64)],
                            axis=-1).transpose(1, 0, 2)
    g_ff1 = g_ff1.reshape(1024, 8, 512).transpose(1, 0, 2)
    return jnp.concatenate([g.reshape(8, -1, WC) for g in (g_in, g_uq, g_ukv, g_out, g_ff1, g_ff2)], axis=1)


def _rope_tables(L):
    pos = (jnp.arange(L) - N_PAD).astype(f32)

    def cs(half):
        inv = ROPE_THETA ** (-jnp.arange(half, dtype=f32) / half)
        ang = pos[:, None] * inv[None, :]
        return jnp.cos(ang), jnp.sin(ang)

    c, s = cs(32)
    rc, rs = jnp.tile(c, (1, 8)), jnp.tile(s, (1, 8))
    c, s = cs(16)
    z = jnp.zeros((L, 32), f32)
    mc, ms = jnp.concatenate([c, c, z, c, c, z], 1), jnp.concatenate([s, s, z, s, s, z], 1)
    return rc, rs, mc, ms


def _layer_fwd(h, hb, wl, gq, gkv, g1, b1, g2, b2, tabs, rtabs, next_shard):
    w_in, w_uq, w_ukv, w_out, w_ff1, w_ff2 = wl
    proj = _mm(hb, w_in, "nn", "mm_in")
    sb, cqn, ckvn, rqk, rv, kr2 = _prep_fwd(proj, gq, gkv, tabs)
    q = _mm(cqn, w_uq, "nn", "mm_uq")
    kv = _mm(ckvn, w_ukv, "nn", "mm_ukv", out_dtype=bf16)
    if next_shard is None:
        out_a, gathered = _sb_fwd(sb), None
    else:
        out_a, gathered = _sb_fwd(sb, next_shard)
    out_b, lse = _mla_fwd(q, kv, kr2, tabs[2], tabs[3])
    y, out_c, states = _ret_fwd(rqk, rv, proj, rtabs)
    mixed = jnp.concatenate([out_a, out_b, out_c], axis=1).astype(bf16)
    mix = _mm(mixed, w_out, "nn", "mm_out")
    h1, h1b = _ln_fwd(h, mix, g1, b1, DN_ALPHA, "ln_fwd")
    u, a = _mm(h1b, w_ff1, "nn", "mm_ff1", epi="relu2")
    ff = _mm(a, w_ff2, "nn", "mm_ff2")
    h2, h2b = _ln_fwd(h1, ff, g2, b2, DN_ALPHA, "ln_fwd")
    saved = (h, proj, sb, cqn, ckvn, rqk, rv, kr2, q, kv, out_a, out_b, lse, y, states, mixed, mix, h1, u, a, ff,
             hb, h1b)
    return h2, h2b, saved, gathered


def _layer_bwd(dh2, saved, wl, gq, gkv, g1, g2, tabs, rtabs, grads_above):
    w_in, w_uq, w_ukv, w_out, w_ff1, w_ff2 = wl
    (h, proj, sb, cqn, ckvn, rqk, rv, kr2, q, kv, out_a, out_b, lse, y, states, mixed, mix, h1, u, a, ff,
     hb, h1b) = saved
    dz2, dg2, db2 = _ln_bwd(dh2, h1, ff, g2, DN_ALPHA, "ln_bwd")
    du = _mm(dz2, w_ff2, "nt", "mm_dff2", epi="mul_relu", extra=u, out_dtype=bf16)
    gw_ff2 = _mm(a, dz2, "tn", "mm_gff2")
    dh1 = _mm(du, w_ff1, "nt", "mm_dff1", epi="add", extra=dz2, alpha=DN_ALPHA)
    gw_ff1 = _mm(h1b, du, "tn", "mm_gff1")
    dz1, dg1, db1 = _ln_bwd(dh1, h, mix, g1, DN_ALPHA, "ln_bwd")
    dmixed = _mm(dz1, w_out, "nt", "mm_dout")
    gw_out = _mm(mixed, dz1, "tn", "mm_gout")
    if grads_above is None:
        (dsq, dsk, dsv), parts_above = _sb_bwd(dmixed, sb, out_a), None
    else:
        dsq, dsk, dsv, parts_above = _sb_bwd(dmixed, sb, out_a, grads_above)
    dqn, dqr, dkn, dv, dkr_p = _mla_bwd(dmixed, q, kv, kr2, out_b, lse, tabs[2], tabs[3])
    dq = jnp.concatenate([dqn, dqr], axis=1).astype(bf16)
    dkv = jnp.concatenate([dkn, dv], axis=1).astype(bf16)
    dcqn = _mm(dq, w_uq, "nt", "mm_duq")
    gw_uq = _mm(cqn, dq, "tn", "mm_guq")
    dckvn = _mm(dkv, w_ukv, "nt", "mm_dukv")
    gw_ukv = _mm(ckvn, dkv, "tn", "mm_gukv")
    drq_r, drk_r, drv, drg = _ret_bwd(dmixed, rqk, rv, proj, y, states, rtabs)
    dkr_r = dkr_p[0] + dkr_p[1] + dkr_p[2] + dkr_p[3]
    dcq, dckv, drq, drk, dkr2, dgq, dgkv = _prep_bwd(proj, gq, gkv, tabs, dcqn, dckvn, drq_r, drk_r, dkr_r)
    dproj = jnp.concatenate([dsq, dsk, dsv, dckv, drq, drk, drv, drg, dcq, dkr2], axis=1).astype(bf16)
    dh = _mm(dproj, w_in, "nt", "mm_din", epi="add", extra=dz1, alpha=DN_ALPHA)
    gw_in = _mm(hb, dproj, "tn", "mm_gin")
    return dh, (gw_in, gw_uq, gw_ukv, gw_out, gw_ff1, gw_ff2), (dgq, dgkv, dg1, db1, dg2, db2), parts_above


def kernel(x, meta_tokens, ln_emb_g, ln_emb_b, w_in, mla_q_norm, mla_kv_norm, w_uq, w_ukv, w_out, ln1_g, ln1_b, w_ff1, w_ff2, ln2_g, ln2_b, loss_target, m_meta_tokens, m_ln_emb_g, m_ln_emb_b, m_w_in, m_mla_q_norm, m_mla_kv_norm, m_w_uq, m_w_ukv, m_w_out, m_ln1_g, m_ln1_b, m_w_ff1, m_w_ff2, m_ln2_g, m_ln2_b, v_meta_tokens, v_ln_emb_g, v_ln_emb_b, v_w_in, v_mla_q_norm, v_mla_kv_norm, v_w_uq, v_w_ukv, v_w_out, v_ln1_g, v_ln1_b, v_w_ff1, v_w_ff2, v_ln2_g, v_ln2_b):
    depth = w_in.shape[0]
    S = x.shape[1]
    L = S + BLK
    me = 4 * lax.axis_index("x") + 2 * lax.axis_index("y") + lax.axis_index("c")
    big = (w_in, w_uq, w_ukv, w_out, w_ff1, w_ff2)
    big_m = (m_w_in, m_w_uq, m_w_ukv, m_w_out, m_w_ff1, m_w_ff2)
    big_v = (v_w_in, v_w_uq, v_w_ukv, v_w_out, v_w_ff1, v_w_ff2)
    small = (ln_emb_g, ln_emb_b, mla_q_norm, mla_kv_norm, ln1_g, ln1_b, ln2_g, ln2_b)
    small_m = (m_ln_emb_g, m_ln_emb_b, m_mla_q_norm, m_mla_kv_norm, m_ln1_g, m_ln1_b, m_ln2_g, m_ln2_b)
    small_v = (v_ln_emb_g, v_ln_emb_b, v_mla_q_norm, v_mla_kv_norm, v_ln1_g, v_ln1_b, v_ln2_g, v_ln2_b)

    shards = [_pack_layer(big, l) for l in range(depth)]
    gathered = _exchange(shards[0].astype(bf16), False, "gather_w0")
    meta_all = _exchange(meta_tokens, False, "gather_meta")
    meta_full = meta_all.transpose(1, 0, 2).reshape(N_META, D)

    tabs = _rope_tables(L)
    rtabs = _ret_tables()

    hcat = jnp.concatenate([jnp.zeros((N_PAD, D), f32), meta_full, x[0]], axis=0)
    h, hb = _ln_fwd(hcat, None, ln_emb_g, ln_emb_b, 1.0, "ln_emb_fwd")
    saved, full = [], []
    for l in range(depth):
        full.append(_unpack_layer(gathered))
        nxt = shards[l + 1].astype(bf16) if l + 1 < depth else None
        h, hb, sv, gathered = _layer_fwd(h, hb, full[l], mla_q_norm[l], mla_kv_norm[l], ln1_g[l], ln1_b[l], ln2_g[l],
                                         ln2_b[l], tabs, rtabs, nxt)
        saved.append(sv)

    dh, loss_part = _loss_head(h, loss_target[0])
    gsmall, parts, pending = [None] * depth, [None] * depth, None
    for l in reversed(range(depth)):
        dh, gbig, gsmall[l], got = _layer_bwd(dh, saved[l], full[l], mla_q_norm[l], mla_kv_norm[l], ln1_g[l],
                                              ln2_g[l], tabs, rtabs, pending)
        if pending is not None:
            parts[l + 1] = got
        pending = _pack_layer_grads(*gbig).astype(bf16)
    parts[0] = _exchange(pending, True, "scatter_g0")
    dz0, dg_emb, db_emb = _ln_bwd(dh, hcat, None, ln_emb_g, 1.0, "ln_emb_bwd")
    grad_x = dz0[BLK:][None]
    dmeta = dz0[N_PAD:BLK]

    adam = [_adamw(parts[l], shards[l], _pack_layer(big_m, l), _pack_layer(big_v, l), "adamw_big")
            for l in range(depth)]

    st = lambda k: jnp.stack([gsmall[l][k] for l in range(depth)])
    g_small = (dg_emb, db_emb, st(0), st(1), st(2), st(3), st(4), st(5))
    n_small = sum(int(np.prod(a.shape)) for a in small)
    flat = jnp.concatenate([a.reshape(-1) for a in g_small] + [dmeta.reshape(-1), loss_part[0, 0:1]])
    rows = -(-(flat.shape[0]) // (8 * D)) * 8
    pad = rows * D - flat.shape[0]
    flat = jnp.concatenate([flat, jnp.zeros((pad,), f32)]).reshape(rows, D)
    parts_s = _exchange(flat, False, "gather_small")

    def pack_small(arrs, meta_shard):
        col = jnp.zeros((N_META, D), f32)
        col = lax.dynamic_update_slice(col, meta_shard, (0, me * 128))
        fl = jnp.concatenate([a.reshape(-1) for a in arrs] + [col.reshape(-1), jnp.zeros((1 + pad,), f32)])
        return fl.reshape(rows, D)

    g_s, d_s, m_s, v_s = _adamw(parts_s, pack_small(small, meta_tokens), pack_small(small_m, m_meta_tokens),
                                pack_small(small_v, v_meta_tokens), "adamw_small")
    loss = g_s.reshape(-1)[n_small + N_META * D]

    def unpack_big(which):
        outs, off = [], 0
        for w, r in zip(big, LAYER_ROWS):
            outs.append(jnp.stack([adam[l][which][off:off + r].reshape(w.shape[1:]) for l in range(depth)]))
            off += r
        return outs

    def unpack_small(flat_rows):
        fl = flat_rows.reshape(-1)
        outs, off = [], 0
        for a in small:
            n = int(np.prod(a.shape))
            outs.append(fl[off:off + n].reshape(a.shape))
            off += n
        meta = lax.dynamic_slice(fl[off:off + N_META * D].reshape(N_META, D), (0, me * 128), (N_META, 128))
        return meta, outs

    def assemble(which, small_rows_arr):
        b = unpack_big(which)
        meta, s = unpack_small(small_rows_arr)
        return [meta, s[0], s[1], b[0], s[2], s[3], b[1], b[2], b[3], s[4], s[5], b[4], b[5], s[6], s[7]]

    return (loss, grad_x, *assemble(0, g_s), *assemble(1, d_s), *assemble(2, m_s), *assemble(3, v_s))
```
